```python
import jax, jax.numpy as jnp
from jax import lax
import numpy as np

D_MODEL = 1024
BATCH = 1
SEQ = 16384
DEPTH = 4

N_MIXERS = 3
HEAD_DIM = 64
N_HEADS = D_MODEL // HEAD_DIM
Q_DIM = N_HEADS * HEAD_DIM
Q_BLOCK = 128
ALIBI_MAX = 8.0
NORM_EPS = 1e-6
NEG_BIG = -1e30
FORCE_SCORE = 1e4

NSA_KV_HEADS = 4
NSA_GROUP = N_HEADS // NSA_KV_HEADS
NSA_KV_DIM = NSA_KV_HEADS * HEAD_DIM
CMP_BLOCK = 32
CMP_STRIDE = 16
CMP_HIDDEN = 256
SEL_BLOCK = 64
N_SELECT = 16
NSA_WINDOW = 512
NSA_IN = Q_DIM + 6 * NSA_KV_DIM + 3 * N_HEADS

SWA_KV_HEADS = 2
SWA_GROUP = N_HEADS // SWA_KV_HEADS
SWA_KV_DIM = SWA_KV_HEADS * HEAD_DIM
SWA_WINDOW = 128
SWA_IN = Q_DIM + 2 * SWA_KV_DIM

FOX_HEADS = N_HEADS
FOX_IN = 3 * Q_DIM + FOX_HEADS

D_FF = 2816
CONV_WIDTH = 3

N_NSA_LAYERS = (DEPTH - 0 + N_MIXERS - 1) // N_MIXERS
N_SWA_LAYERS = (DEPTH - 1 + N_MIXERS - 1) // N_MIXERS
N_FOX_LAYERS = (DEPTH - 2 + N_MIXERS - 1) // N_MIXERS

kernel_name = 'hybrid_nsa_swa_fox_convglu'


def rmsnorm(x, g):
    xf = x.astype(jnp.float32)
    y = xf * lax.rsqrt(jnp.mean(xf * xf, axis=-1, keepdims=True) + NORM_EPS)
    return (y * g.astype(jnp.float32)).astype(x.dtype)


def alibi_slopes(n_heads):
    return jnp.asarray(2.0 ** (-ALIBI_MAX * np.arange(1, n_heads + 1) / n_heads), dtype=jnp.float32)


def masked_softmax(s, mask):
    p = jax.nn.softmax(jnp.where(mask, s, NEG_BIG), axis=-1)
    return jnp.where(mask, p, 0.0)


def nsa_mixer(h, w_in, cmp_pos, cmp_w1, cmp_w2, w_out):
    B, T, _ = h.shape
    G, R, Dh = NSA_KV_HEADS, NSA_GROUP, HEAD_DIM
    f32 = jnp.float32
    scale = Dh ** -0.5
    splits = [Q_DIM + i * NSA_KV_DIM for i in range(7)]
    q, kc, vc, ks, vs, kw, vw, gate_logit = jnp.split(h @ w_in, splits, axis=-1)
    q = q.reshape(B, T, G, R, Dh)
    kc, vc, ks, vs, kw, vw = [a.reshape(B, T, G, Dh) for a in (kc, vc, ks, vs, kw, vw)]
    gates = jax.nn.sigmoid(gate_logit.astype(f32)).reshape(B, T, G, R, 3)
    slopes = alibi_slopes(N_HEADS).reshape(G, R)

    n_cmp = (T - CMP_BLOCK) // CMP_STRIDE + 1
    cmp_start = np.arange(n_cmp) * CMP_STRIDE
    cmp_idx = cmp_start[:, None] + np.arange(CMP_BLOCK)[None, :]

    def compress(a, pos, w1, w2):
        blk = a[:, cmp_idx] + pos[None, None, :, None, :].astype(a.dtype)
        blk = blk.transpose(0, 1, 3, 2, 4).reshape(B, n_cmp, G, CMP_BLOCK * Dh)
        return jax.nn.gelu(blk @ w1) @ w2

    k_cmp = compress(kc, cmp_pos[0], cmp_w1[0], cmp_w2[0])
    v_cmp = compress(vc, cmp_pos[1], cmp_w1[1], cmp_w2[1]).astype(f32)
    cmp_end = jnp.asarray(cmp_start + CMP_BLOCK - 1, dtype=jnp.int32)

    n_sel = T // SEL_BLOCK
    sel_start = np.arange(n_sel) * SEL_BLOCK
    overlap = np.clip(np.minimum(cmp_start[:, None] + CMP_BLOCK, sel_start[None, :] + SEL_BLOCK)
                      - np.maximum(cmp_start[:, None], sel_start[None, :]), 0, None) / CMP_BLOCK
    overlap = jnp.asarray(overlap, dtype=f32)
    k_top = min(N_SELECT, n_sel)
    ks_blk = ks.reshape(B, n_sel, SEL_BLOCK, G, Dh).transpose(0, 3, 1, 2, 4)
    vs_blk = vs.reshape(B, n_sel, SEL_BLOCK, G, Dh).transpose(0, 3, 1, 2, 4)
    b_idx = jnp.arange(B)[:, None, None, None]
    g_idx = jnp.arange(G)[None, :, None, None]
    blk_ids = jnp.arange(n_sel, dtype=jnp.int32)

    kw_pad = jnp.pad(kw, ((0, 0), (NSA_WINDOW, 0), (0, 0), (0, 0)))
    vw_pad = jnp.pad(vw, ((0, 0), (NSA_WINDOW, 0), (0, 0), (0, 0)))
    win_off = jnp.arange(Q_BLOCK + NSA_WINDOW, dtype=jnp.int32) - NSA_WINDOW

    def query_block(c):
        t0 = c * Q_BLOCK
        tq = t0 + jnp.arange(Q_BLOCK, dtype=jnp.int32)
        qc = lax.dynamic_slice_in_dim(q, t0, Q_BLOCK, axis=1)
        gc = lax.dynamic_slice_in_dim(gates, t0, Q_BLOCK, axis=1)

        d_cmp = tq[:, None] - cmp_end[None, :]
        s = jnp.einsum('bqgrd,bngd->bgrqn', qc, k_cmp).astype(f32) * scale
        s = s - slopes[None, :, :, None, None] * d_cmp.astype(f32)
        p_cmp = masked_softmax(s, d_cmp >= 0)
        o_cmp = jnp.einsum('bgrqn,bngd->bqgrd', p_cmp, v_cmp)

        imp = jnp.einsum('bgrqn,nj->bgqj', p_cmp, overlap)
        cur = (tq // SEL_BLOCK)[:, None]
        forced = (blk_ids[None, :] == 0) | (blk_ids[None, :] == cur) | (blk_ids[None, :] == cur - 1)
        imp = jnp.where(forced, FORCE_SCORE, jnp.where(blk_ids[None, :] <= cur, imp, -1.0))
        _, sel = lax.top_k(imp, k_top)
        k_sel = ks_blk[b_idx, g_idx, sel].reshape(B, G, Q_BLOCK, k_top * SEL_BLOCK, Dh)
        v_sel = vs_blk[b_idx, g_idx, sel].reshape(B, G, Q_BLOCK, k_top * SEL_BLOCK, Dh).astype(f32)
        pos = (sel[..., None] * SEL_BLOCK + jnp.arange(SEL_BLOCK, dtype=jnp.int32)).reshape(B, G, Q_BLOCK, k_top * SEL_BLOCK)
        d_sel = (tq[None, None, :, None] - pos)[:, :, None]
        s = jnp.einsum('bqgrd,bgqxd->bgrqx', qc, k_sel).astype(f32) * scale
        s = s - slopes[None, :, :, None, None] * d_sel.astype(f32)
        p_sel = masked_softmax(s, d_sel >= 0)
        o_sel = jnp.einsum('bgrqx,bgqxd->bqgrd', p_sel, v_sel)

        kwc = lax.dynamic_slice_in_dim(kw_pad, t0, Q_BLOCK + NSA_WINDOW, axis=1)
        vwc = lax.dynamic_slice_in_dim(vw_pad, t0, Q_BLOCK + NSA_WINDOW, axis=1).astype(f32)
        sk = t0 + win_off
        d_win = tq[:, None] - sk[None, :]
        win_mask = (d_win >= 0) & (d_win < NSA_WINDOW) & (sk[None, :] >= 0)
        s = jnp.einsum('bqgrd,bkgd->bgrqk', qc, kwc).astype(f32) * scale
        s = s - slopes[None, :, :, None, None] * d_win.astype(f32)
        p_win = masked_softmax(s, win_mask)
        o_win = jnp.einsum('bgrqk,bkgd->bqgrd', p_win, vwc)

        o = gc[..., 0:1] * o_cmp + gc[..., 1:2] * o_sel + gc[..., 2:3] * o_win
        return o.astype(h.dtype)

    o = lax.map(query_block, jnp.arange(T // Q_BLOCK))
    o = jnp.moveaxis(o, 0, 1).reshape(B, T, Q_DIM)
    return o @ w_out


def swa_sink_mixer(h, w_in, sinks, w_out):
    B, T, _ = h.shape
    G, R, Dh = SWA_KV_HEADS, SWA_GROUP, HEAD_DIM
    f32 = jnp.float32
    nb = T // Q_BLOCK
    q, k, v = jnp.split(h @ w_in, [Q_DIM, Q_DIM + SWA_KV_DIM], axis=-1)
    q = q.reshape(B, nb, Q_BLOCK, G, R, Dh)
    k = k.reshape(B, nb, Q_BLOCK, G, Dh)
    v = v.reshape(B, nb, Q_BLOCK, G, Dh)

    def with_previous_block(a):
        prev = jnp.pad(a[:, :-1], ((0, 0), (1, 0), (0, 0), (0, 0), (0, 0)))
        return jnp.concatenate([prev, a], axis=2)

    kk, vv = with_previous_block(k), with_previous_block(v).astype(f32)
    rel_q = jnp.arange(Q_BLOCK, dtype=jnp.int32)
    rel_k = jnp.arange(2 * Q_BLOCK, dtype=jnp.int32) - Q_BLOCK
    dist = rel_q[:, None] - rel_k[None, :]
    abs_k = (jnp.arange(nb, dtype=jnp.int32) * Q_BLOCK)[:, None] + rel_k[None, :]
    mask = ((dist >= 0) & (dist < SWA_WINDOW))[None] & (abs_k >= 0)[:, None, :]
    slopes = alibi_slopes(N_HEADS).reshape(G, R)
    s = jnp.einsum('bnqgrd,bnkgd->bngrqk', q, kk).astype(f32) * Dh ** -0.5
    s = s - slopes[:, :, None, None] * dist.astype(f32)
    s = jnp.where(mask[None, :, None, None], s, NEG_BIG)
    sink = jnp.broadcast_to(sinks.astype(f32).reshape(G, R)[:, :, None, None], s.shape[:-1] + (1,))
    p = jax.nn.softmax(jnp.concatenate([s, sink], axis=-1), axis=-1)[..., :-1]
    o = jnp.einsum('bngrqk,bnkgd->bnqgrd', p, vv)
    return o.reshape(B, T, Q_DIM).astype(h.dtype) @ w_out


def fox_mixer(h, w_in, f_bias, w_out):
    B, T, _ = h.shape
    H, Dh = FOX_HEADS, HEAD_DIM
    f32 = jnp.float32
    q, k, v, f_logit = jnp.split(h @ w_in, [Q_DIM, 2 * Q_DIM, 3 * Q_DIM], axis=-1)
    q = q.reshape(B, T, H, Dh)
    k = k.reshape(B, T, H, Dh)
    v = v.reshape(B, T, H, Dh).astype(f32)
    log_f = jax.nn.log_sigmoid(f_logit.astype(f32) + f_bias.astype(f32))
    cum = jnp.cumsum(log_f, axis=1).transpose(0, 2, 1)
    t_k = jnp.arange(T, dtype=jnp.int32)

    def query_block(c):
        t0 = c * Q_BLOCK
        tq = t0 + jnp.arange(Q_BLOCK, dtype=jnp.int32)
        qc = lax.dynamic_slice_in_dim(q, t0, Q_BLOCK, axis=1)
        cq = lax.dynamic_slice_in_dim(cum, t0, Q_BLOCK, axis=2)
        s = jnp.einsum('bqhd,bkhd->bhqk', qc, k).astype(f32) * Dh ** -0.5
        s = s + cq[..., :, None] - cum[..., None, :]
        p = masked_softmax(s, t_k[None, :] <= tq[:, None])
        return jnp.einsum('bhqk,bkhd->bqhd', p, v).astype(h.dtype)

    o = lax.map(query_block, jnp.arange(T // Q_BLOCK))
    o = jnp.moveaxis(o, 0, 1).reshape(B, T, Q_DIM)
    return o @ w_out


def conv_glu_mlp(h, w_up, conv_w, conv_b, w_down):
    T = h.shape[1]
    u = h @ w_up
    u_pad = jnp.pad(u, ((0, 0), (CONV_WIDTH - 1, 0), (0, 0)))
    w = conv_w.astype(u.dtype)
    u = sum(w[i] * u_pad[:, i:i + T] for i in range(CONV_WIDTH)) + conv_b.astype(u.dtype)
    a, g = jnp.split(u, 2, axis=-1)
    return (jax.nn.silu(g) * a) @ w_down


def setup_inputs(seed: int = 0) -> dict:
    key = jax.random.key(seed)
    k = jax.random.split(key, 20)

    def dense(kk, shape, fan_in):
        return jax.random.normal(kk, shape, jnp.float32) * fan_in ** -0.5

    def gain(kk, shape):
        return 1.0 + 0.05 * jax.random.normal(kk, shape, jnp.float32)

    return {
        'x': jax.random.normal(k[0], (BATCH, SEQ, D_MODEL), jnp.float32),
        'attn_norm': gain(k[1], (DEPTH, D_MODEL)),
        'mlp_norm': gain(k[2], (DEPTH, D_MODEL)),
        'final_norm': gain(k[3], (D_MODEL,)),
        'nsa_w_in': dense(k[4], (N_NSA_LAYERS, D_MODEL, NSA_IN), D_MODEL),
        'nsa_cmp_pos': 0.1 * jax.random.normal(k[5], (N_NSA_LAYERS, 2, CMP_BLOCK, HEAD_DIM), jnp.float32),
        'nsa_cmp_w1': dense(k[6], (N_NSA_LAYERS, 2, CMP_BLOCK * HEAD_DIM, CMP_HIDDEN), CMP_BLOCK * HEAD_DIM),
        'nsa_cmp_w2': dense(k[7], (N_NSA_LAYERS, 2, CMP_HIDDEN, HEAD_DIM), CMP_HIDDEN),
        'nsa_w_out': dense(k[8], (N_NSA_LAYERS, Q_DIM, D_MODEL), Q_DIM),
        'swa_w_in': dense(k[9], (N_SWA_LAYERS, D_MODEL, SWA_IN), D_MODEL),
        'swa_sinks': 0.5 * jax.random.normal(k[10], (N_SWA_LAYERS, N_HEADS), jnp.float32),
        'swa_w_out': dense(k[11], (N_SWA_LAYERS, Q_DIM, D_MODEL), Q_DIM),
        'fox_w_in': dense(k[12], (N_FOX_LAYERS, D_MODEL, FOX_IN), D_MODEL),
        'fox_f_bias': jax.random.uniform(k[13], (N_FOX_LAYERS, FOX_HEADS), jnp.float32, minval=2.0, maxval=5.0),
        'fox_w_out': dense(k[14], (N_FOX_LAYERS, Q_DIM, D_MODEL), Q_DIM),
        'mlp_w_up': dense(k[15], (DEPTH, D_MODEL, 2 * D_FF), D_MODEL),
        'mlp_conv_w': dense(k[16], (DEPTH, CONV_WIDTH, 2 * D_FF), CONV_WIDTH),
        'mlp_conv_b': 0.02 * jax.random.normal(k[17], (DEPTH, 2 * D_FF), jnp.float32),
        'mlp_w_down': dense(k[18], (DEPTH, D_FF, D_MODEL), D_FF),
    }


def reference(x, attn_norm, mlp_norm, final_norm, nsa_w_in, nsa_cmp_pos, nsa_cmp_w1, nsa_cmp_w2, nsa_w_out,
              swa_w_in, swa_sinks, swa_w_out, fox_w_in, fox_f_bias, fox_w_out,
              mlp_w_up, mlp_conv_w, mlp_conv_b, mlp_w_down):
    for i in range(DEPTH):
        kind, j = i % N_MIXERS, i // N_MIXERS
        h = rmsnorm(x, attn_norm[i])
        if kind == 0:
            y = nsa_mixer(h, nsa_w_in[j], nsa_cmp_pos[j], nsa_cmp_w1[j], nsa_cmp_w2[j], nsa_w_out[j])
        elif kind == 1:
            y = swa_sink_mixer(h, swa_w_in[j], swa_sinks[j], swa_w_out[j])
        else:
            y = fox_mixer(h, fox_w_in[j], fox_f_bias[j], fox_w_out[j])
        x = x + y
        x = x + conv_glu_mlp(rmsnorm(x, mlp_norm[i]), mlp_w_up[i], mlp_conv_w[i], mlp_conv_b[i], mlp_w_down[i])
    return rmsnorm(x, final_norm)
```

```python
import functools

import numpy as np
import jax
import jax.numpy as jnp
from jax import lax
from jax.experimental import pallas as pl
from jax.experimental.pallas import tpu as pltpu

F32 = jnp.float32
BF16 = jnp.bfloat16

D_MODEL = 1024
DEPTH = 4
N_MIXERS = 3
HEAD_DIM = 64
N_HEADS = 16
Q_DIM = N_HEADS * HEAD_DIM
ALIBI_MAX = 8.0
NORM_EPS = 1e-6
FORCE_SCORE = 1e4
MASKED = -2e30
M_INIT = -1e30

NSA_KV_HEADS = 4
NSA_GROUP = 4
NSA_KV_DIM = NSA_KV_HEADS * HEAD_DIM
CMP_BLOCK = 32
CMP_STRIDE = 16
CMP_HIDDEN = 256
SEL_BLOCK = 64
N_SELECT = 16
NSA_WINDOW = 512
NSA_TQ = 128
NSA_TK = 512

SWA_KV_HEADS = 2
SWA_GROUP = 8
SWA_WINDOW = 128
SWA_TQ = 128

FOX_T = 512

D_FF = 2816
LANES = 128
VMEM_LIMIT = 56 * 1024 * 1024


def _cparams(semantics, vmem=VMEM_LIMIT):
    return pltpu.CompilerParams(dimension_semantics=semantics, vmem_limit_bytes=vmem)


def _alibi_slopes():
    return np.asarray(2.0 ** (-ALIBI_MAX * np.arange(1, N_HEADS + 1) / N_HEADS), dtype=np.float32)


def _rms(x, g):
    ms = jnp.mean(x * x, axis=-1, keepdims=True)
    return x * lax.rsqrt(ms + NORM_EPS) * g


def _norm_proj_kernel(x_ref, g_ref, w_ref, o_ref, h_ref):
    @pl.when(pl.program_id(1) == 0)
    def _():
        h_ref[...] = _rms(x_ref[...], g_ref[...]).astype(BF16)

    o_ref[...] = jnp.dot(h_ref[...], w_ref[...], preferred_element_type=F32).astype(o_ref.dtype)


def _pick_tile(n, candidates):
    for c in candidates:
        if n % c == 0:
            return c
    raise ValueError(f"no tile for {n}")


def _norm_proj(x, g, w, out_dtype):
    t, d = x.shape
    n = w.shape[1]
    tm = _pick_tile(t, (1024, 512, 256, 128))
    tn = _pick_tile(n, (1024, 768, 512, 256, 128))
    return pl.pallas_call(
        _norm_proj_kernel,
        grid=(t // tm, n // tn),
        in_specs=[pl.BlockSpec((tm, d), lambda i, j: (i, 0)),
                  pl.BlockSpec((1, d), lambda i, j: (0, 0)),
                  pl.BlockSpec((d, tn), lambda i, j: (0, j))],
        out_specs=pl.BlockSpec((tm, tn), lambda i, j: (i, j)),
        out_shape=jax.ShapeDtypeStruct((t, n), out_dtype),
        scratch_shapes=[pltpu.VMEM((tm, d), BF16)],
        compiler_params=_cparams(("parallel", "arbitrary")),
        name="norm_proj",
    )(x, g.reshape(1, d), w)


def _out_proj_kernel(x_ref, o_ref, w_ref, y_ref):
    y_ref[...] = x_ref[...] + jnp.dot(o_ref[...], w_ref[...], preferred_element_type=F32)


def _out_proj(x, o, w):
    t, d = x.shape
    tm = _pick_tile(t, (1024, 512, 256, 128))
    return pl.pallas_call(
        _out_proj_kernel,
        grid=(t // tm,),
        in_specs=[pl.BlockSpec((tm, d), lambda i: (i, 0)),
                  pl.BlockSpec((tm, o.shape[1]), lambda i: (i, 0)),
                  pl.BlockSpec(w.shape, lambda i: (0, 0))],
        out_specs=pl.BlockSpec((tm, d), lambda i: (i, 0)),
        out_shape=jax.ShapeDtypeStruct((t, d), F32),
        compiler_params=_cparams(("parallel",)),
        name="out_proj",
    )(x, o, w)


MLP_HALO = 16


def _mlp_kernel(x_ref, xh_ref, g_ref, wa_ref, wg_ref, cwa_ref, cwg_ref, cba_ref, cbg_ref, wd_ref,
                y_ref, h_ref, acc_ref):
    i, j = pl.program_id(0), pl.program_id(1)
    tm = x_ref.shape[0]

    @pl.when(j == 0)
    def _():
        halo = _rms(xh_ref[...], g_ref[...])
        h_ref[0:MLP_HALO, :] = jnp.where(i > 0, halo, 0.0).astype(BF16)
        h_ref[MLP_HALO:, :] = _rms(x_ref[...], g_ref[...]).astype(BF16)
        acc_ref[...] = jnp.zeros_like(acc_ref)

    h = h_ref[...]

    def conv(w_ref, cw_ref, cb_ref):
        u = jnp.dot(h, w_ref[...], preferred_element_type=F32)
        cw = cw_ref[...]
        return (cw[0:1] * u[MLP_HALO - 2:MLP_HALO - 2 + tm] + cw[1:2] * u[MLP_HALO - 1:MLP_HALO - 1 + tm]
                + cw[2:3] * u[MLP_HALO:MLP_HALO + tm] + cb_ref[...])

    a = conv(wa_ref, cwa_ref, cba_ref)
    gt = conv(wg_ref, cwg_ref, cbg_ref)
    act = (jax.nn.silu(gt) * a).astype(BF16)
    acc_ref[...] += jnp.dot(act, wd_ref[...], preferred_element_type=F32)

    @pl.when(j == pl.num_programs(1) - 1)
    def _():
        y_ref[...] = x_ref[...] + acc_ref[...]


def _mlp(x, g, w_up, conv_w, conv_b, w_down):
    t, d = x.shape
    f = w_down.shape[0]
    tm = _pick_tile(t, (1024, 512, 256, 128))
    tf = _pick_tile(f, (256, 128))
    nf = f // tf
    hb = tm // MLP_HALO
    return pl.pallas_call(
        _mlp_kernel,
        grid=(t // tm, nf),
        in_specs=[pl.BlockSpec((tm, d), lambda i, j: (i, 0)),
                  pl.BlockSpec((MLP_HALO, d), lambda i, j: (jnp.maximum(i * hb - 1, 0), 0)),
                  pl.BlockSpec((1, d), lambda i, j: (0, 0)),
                  pl.BlockSpec((d, tf), lambda i, j: (0, j)),
                  pl.BlockSpec((d, tf), lambda i, j: (0, nf + j)),
                  pl.BlockSpec((3, tf), lambda i, j: (0, j)),
                  pl.BlockSpec((3, tf), lambda i, j: (0, nf + j)),
                  pl.BlockSpec((1, tf), lambda i, j: (0, j)),
                  pl.BlockSpec((1, tf), lambda i, j: (0, nf + j)),
                  pl.BlockSpec((tf, d), lambda i, j: (j, 0))],
        out_specs=pl.BlockSpec((tm, d), lambda i, j: (i, 0)),
        out_shape=jax.ShapeDtypeStruct((t, d), F32),
        scratch_shapes=[pltpu.VMEM((tm + MLP_HALO, d), BF16), pltpu.VMEM((tm, d), F32)],
        compiler_params=_cparams(("parallel", "arbitrary")),
        name="conv_glu_mlp",
    )(x, x, g.reshape(1, d), w_up, w_up, conv_w, conv_w, conv_b, conv_b, w_down)


def _rmsnorm_kernel(x_ref, g_ref, y_ref):
    y_ref[...] = _rms(x_ref[...], g_ref[...])


def _rmsnorm(x, g):
    t, d = x.shape
    tm = _pick_tile(t, (1024, 512, 256, 128))
    return pl.pallas_call(
        _rmsnorm_kernel,
        grid=(t // tm,),
        in_specs=[pl.BlockSpec((tm, d), lambda i: (i, 0)), pl.BlockSpec((1, d), lambda i: (0, 0))],
        out_specs=pl.BlockSpec((tm, d), lambda i: (i, 0)),
        out_shape=jax.ShapeDtypeStruct((t, d), F32),
        compiler_params=_cparams(("parallel",)),
        name="final_rmsnorm",
    )(x, g.reshape(1, d))


def _swap_halves(q_pair):
    return pltpu.roll(q_pair.astype(F32), HEAD_DIM, 1).astype(BF16)


def _lane_half(shape):
    return lax.broadcasted_iota(jnp.int32, shape, 1) // HEAD_DIM


def _rows_to_heads(o_t, n_heads, tq):
    stacked = jnp.concatenate([o_t[:, r * tq:(r + 1) * tq] for r in range(n_heads)], axis=0)
    return stacked.T


def _swa_kernel(q_ref, kp_ref, kc_ref, vp_ref, vc_ref, bias_ref, sink_ref, o_ref):
    tq = q_ref.shape[0]
    rows = SWA_GROUP * tq
    k_ext = jnp.concatenate([kp_ref[...], kc_ref[...]], axis=0)
    v_ext = jnp.concatenate([vp_ref[...], vc_ref[...]], axis=1)
    half = _lane_half((tq, LANES))
    outs = []
    for g in range(SWA_KV_HEADS):
        parts = []
        for r in range(SWA_GROUP):
            hd = g * SWA_GROUP + r
            q_pair = q_ref[:, (hd // 2) * LANES:(hd // 2 + 1) * LANES]
            src = q_pair if hd % 2 == g else _swap_halves(q_pair)
            parts.append(jnp.where(half == g, src, jnp.zeros_like(src)))
        qp = jnp.concatenate(parts, axis=0)
        s = lax.dot_general(k_ext, qp, (((1,), (1,)), ((), ())), preferred_element_type=F32)
        s = s * (HEAD_DIM ** -0.5) + bias_ref[0, g]
        sink = sink_ref[g]
        m = jnp.maximum(jnp.max(s, axis=0, keepdims=True), sink)
        e = jnp.exp(s - m)
        l = jnp.sum(e, axis=0, keepdims=True) + jnp.exp(sink - m)
        o_t = jnp.dot(v_ext[g * HEAD_DIM:(g + 1) * HEAD_DIM], e.astype(BF16), preferred_element_type=F32)
        outs.append(_rows_to_heads(o_t / l, SWA_GROUP, tq))
    o_ref[...] = jnp.concatenate(outs, axis=1).astype(o_ref.dtype)


def _swa_bias():
    tq = SWA_TQ
    slopes = _alibi_slopes().reshape(SWA_KV_HEADS, SWA_GROUP)
    x = np.arange(2 * tq)[:, None] - tq
    q = np.arange(tq)[None, :]
    dist = q - x
    valid = (dist >= 0) & (dist < SWA_WINDOW)
    out = np.empty((2, SWA_KV_HEADS, 2 * tq, SWA_GROUP * tq), np.float32)
    for var in range(2):
        v = valid & ((x >= 0) | (var == 1))
        for g in range(SWA_KV_HEADS):
            for r in range(SWA_GROUP):
                out[var, g, :, r * tq:(r + 1) * tq] = np.where(v, -slopes[g, r] * dist, MASKED)
    return out


def _swa_attention(qkv, v_t, sinks):
    t = qkv.shape[0]
    tq = SWA_TQ
    kcol = Q_DIM // LANES
    bias = jnp.asarray(_swa_bias())
    sink_rows = jnp.repeat(sinks.astype(F32).reshape(SWA_KV_HEADS, 1, SWA_GROUP), tq, axis=2)
    prev = lambda i: jnp.maximum(i - 1, 0)
    return pl.pallas_call(
        _swa_kernel,
        grid=(t // tq,),
        in_specs=[pl.BlockSpec((tq, Q_DIM), lambda i: (i, 0)),
                  pl.BlockSpec((tq, LANES), lambda i: (prev(i), kcol)),
                  pl.BlockSpec((tq, LANES), lambda i: (i, kcol)),
                  pl.BlockSpec((LANES, tq), lambda i: (0, prev(i))),
                  pl.BlockSpec((LANES, tq), lambda i: (0, i)),
                  pl.BlockSpec((1,) + bias.shape[1:], lambda i: (jnp.minimum(i, 1), 0, 0, 0)),
                  pl.BlockSpec(sink_rows.shape, lambda i: (0, 0, 0))],
        out_specs=pl.BlockSpec((tq, Q_DIM), lambda i: (i, 0)),
        out_shape=jax.ShapeDtypeStruct((t, Q_DIM), BF16),
        compiler_params=_cparams(("parallel",)),
        name="swa_attention",
    )(qkv, qkv, qkv, v_t, v_t, bias, sink_rows)


def _swa_layer(x, norm_g, w_in, sinks, w_out):
    qkv = _norm_proj(x, norm_g, w_in.astype(BF16), BF16)
    v_t = qkv[:, Q_DIM + SWA_KV_HEADS * HEAD_DIM:].T
    o = _swa_attention(qkv, v_t, sinks)
    return _out_proj(x, o, w_out.astype(BF16))


FOX_PARTS = 3


def _split_bf16(v):
    parts, rest = [], v
    for _ in range(FOX_PARTS):
        p = rest.astype(BF16)
        parts.append(p)
        rest = rest - p.astype(F32)
    return parts


def _fox_prep_kernel(f_ref, b_ref, aug_ref, cref_ref, carry_ref):
    tm = f_ref.shape[0]

    @pl.when(pl.program_id(0) == 0)
    def _():
        carry_ref[...] = jnp.zeros_like(carry_ref)

    logf = jax.nn.log_sigmoid(f_ref[...] + b_ref[...])
    row = lax.broadcasted_iota(jnp.int32, (tm, tm), 0)
    col = lax.broadcasted_iota(jnp.int32, (tm, tm), 1)
    tri = jnp.where(row >= col, 1.0, 0.0).astype(BF16)
    local = sum(jnp.dot(tri, p, preferred_element_type=F32) for p in _split_bf16(logf))
    cum = local + carry_ref[...]
    carry_ref[...] = cum[tm - 1:tm, :]
    first = cum[0:1, :]
    cref_ref[0] = jnp.broadcast_to(first, cref_ref.shape[1:])
    hi, mid, lo = _split_bf16(first - cum)
    lane = lax.broadcasted_iota(jnp.int32, (tm, LANES), 1)
    zero = jnp.zeros_like(hi)
    aug_ref[...] = jnp.where(lane < N_HEADS, hi, jnp.where(lane < 2 * N_HEADS, mid,
                                                            jnp.where(lane < 3 * N_HEADS, lo, zero)))


def _fox_prep(f_logit3, f_bias3):
    t = f_logit3.shape[0]
    nt = t // FOX_T
    return pl.pallas_call(
        _fox_prep_kernel,
        grid=(nt,),
        in_specs=[pl.BlockSpec((FOX_T, LANES), lambda i: (i, 0)), pl.BlockSpec((1, LANES), lambda i: (0, 0))],
        out_specs=[pl.BlockSpec((FOX_T, LANES), lambda i: (i, 0)), pl.BlockSpec((1, 8, LANES), lambda i: (i, 0, 0))],
        out_shape=[jax.ShapeDtypeStruct((t, LANES), BF16), jax.ShapeDtypeStruct((nt, 8, LANES), F32)],
        scratch_shapes=[pltpu.VMEM((1, LANES), F32)],
        compiler_params=_cparams(("arbitrary",)),
        name="fox_prep",
    )(f_logit3, f_bias3)


def _fox_kernel(cref_ref, q_ref, k_ref, aug_ref, vt_ref, o_ref, m_ref, l_ref, acc_ref):
    pair, qi = pl.program_id(0), pl.program_id(1)
    tq = q_ref.shape[0]
    tk = FOX_T
    lane = lax.broadcasted_iota(jnp.int32, (tq, LANES), 1)
    q2 = q_ref[...] * jnp.asarray(HEAD_DIM ** -0.5, BF16)
    q_ext = []
    for hh in range(2):
        head = 2 * pair + hh
        pick = (lane % N_HEADS == head) & (lane < FOX_PARTS * N_HEADS)
        q_ext.append(jnp.concatenate([jnp.where(lane // HEAD_DIM == hh, q2, jnp.zeros_like(q2)),
                                      jnp.where(pick, 1.0, 0.0).astype(BF16)], axis=1))
    m_ref[...] = jnp.full_like(m_ref, M_INIT)
    l_ref[...] = jnp.zeros_like(l_ref)
    acc_ref[...] = jnp.zeros_like(acc_ref)

    def step(kj, diagonal):
        start = pl.multiple_of(kj * tk, tk)
        k_ext = jnp.concatenate([k_ref[pl.ds(start, tk), :], aug_ref[pl.ds(start, tk), :]], axis=1)
        for hh in range(2):
            head = 2 * pair + hh
            s = lax.dot_general(k_ext, q_ext[hh], (((1,), (1,)), ((), ())), preferred_element_type=F32)
            if diagonal:
                key = lax.broadcasted_iota(jnp.int32, (tk, tq), 0)
                qry = lax.broadcasted_iota(jnp.int32, (tk, tq), 1)
                s = jnp.where(key <= qry, s, MASKED)
            off = cref_ref[qi, head] - cref_ref[kj, head]
            m_old = m_ref[hh]
            m_new = jnp.maximum(m_old, jnp.max(s, axis=0, keepdims=True) + off)
            e = jnp.exp(s - (m_new - off))
            alpha = jnp.exp(m_old - m_new)
            m_ref[hh] = m_new
            l_ref[hh] = alpha * l_ref[hh] + jnp.sum(e, axis=0, keepdims=True)
            v_t = vt_ref[kj, hh * HEAD_DIM:(hh + 1) * HEAD_DIM, :]
            acc_ref[hh] = alpha * acc_ref[hh] + jnp.dot(v_t, e.astype(BF16), preferred_element_type=F32)

    def body(kj, carry):
        step(kj, False)
        return carry

    lax.fori_loop(0, qi, body, 0)
    step(qi, True)
    o_t = jnp.concatenate([acc_ref[hh] / l_ref[hh] for hh in range(2)], axis=0)
    o_ref[...] = o_t.T.astype(o_ref.dtype)


def _fox_attention(qkv, aug, v_t3, cref):
    t = qkv.shape[0]
    tq = FOX_T
    nk = t // FOX_T
    kcol = Q_DIM // LANES
    return pl.pallas_call(
        _fox_kernel,
        grid=(N_HEADS // 2, t // tq),
        in_specs=[pl.BlockSpec(memory_space=pltpu.SMEM),
                  pl.BlockSpec((tq, LANES), lambda p, i: (i, p)),
                  pl.BlockSpec((t, LANES), lambda p, i: (0, kcol + p)),
                  pl.BlockSpec((t, LANES), lambda p, i: (0, 0)),
                  pl.BlockSpec((nk, LANES, FOX_T), lambda p, i: (0, p, 0))],
        out_specs=pl.BlockSpec((tq, LANES), lambda p, i: (i, p)),
        out_shape=jax.ShapeDtypeStruct((t, Q_DIM), BF16),
        scratch_shapes=[pltpu.VMEM((2, 1, tq), F32), pltpu.VMEM((2, 1, tq), F32),
                        pltpu.VMEM((2, HEAD_DIM, tq), F32)],
        compiler_params=_cparams(("parallel", "arbitrary")),
        name="fox_attention",
    )(cref, qkv, qkv, aug, v_t3)


def _fox_layer(x, norm_g, w_in, f_bias, w_out):
    t = x.shape[0]
    qkv = _norm_proj(x, norm_g, w_in[:, :3 * Q_DIM].astype(BF16), BF16)
    w_f = w_in[:, 3 * Q_DIM:]
    pad = jnp.zeros((D_MODEL, LANES - FOX_PARTS * N_HEADS), w_f.dtype)
    w_f3 = jnp.concatenate([w_f] * FOX_PARTS + [pad], axis=1).astype(BF16)
    b3 = jnp.concatenate([f_bias.astype(F32)] * FOX_PARTS + [jnp.zeros((LANES - FOX_PARTS * N_HEADS,), F32)])
    f_logit3 = _norm_proj(x, norm_g, w_f3, F32)
    aug, cref = _fox_prep(f_logit3, b3.reshape(1, LANES))
    v_t3 = qkv[:, 2 * Q_DIM:].reshape(t // FOX_T, FOX_T, Q_DIM).transpose(0, 2, 1)
    o = _fox_attention(qkv, aug, v_t3, cref[:, 0, :N_HEADS])
    return _out_proj(x, o, w_out.astype(BF16))


def _nsa_compress_kernel(a_ref, pos_ref, w1_ref, w2_ref, o_ref):
    tc = a_ref.shape[2]
    half = CMP_STRIDE * HEAD_DIM
    pos = pos_ref[0]
    out = jnp.zeros(o_ref.shape[1:], F32)
    for gg in range(2):
        a = a_ref[0, gg]
        top = jnp.dot((a + pos[:, :half]).astype(BF16), w1_ref[0, :half], preferred_element_type=F32)
        bot = jnp.dot((a + pos[:, half:]).astype(BF16), w1_ref[0, half:], preferred_element_type=F32)
        hid = top + pltpu.roll(bot, tc - 1, 0)
        out = out + jnp.dot(jax.nn.gelu(hid).astype(BF16), w2_ref[0, gg], preferred_element_type=F32)
    o_ref[0] = out


def _nsa_compress(a, pos, w1, w2):
    _, g, tc, width = a.shape
    return pl.pallas_call(
        _nsa_compress_kernel,
        grid=(2, g // 2),
        in_specs=[pl.BlockSpec((1, 2, tc, width), lambda kv, gp: (kv, gp, 0, 0)),
                  pl.BlockSpec((1, 1, 2 * width), lambda kv, gp: (kv, 0, 0)),
                  pl.BlockSpec((1, 2 * width, CMP_HIDDEN), lambda kv, gp: (kv, 0, 0)),
                  pl.BlockSpec((1, 2, CMP_HIDDEN, LANES), lambda kv, gp: (kv, 0, 0, 0))],
        out_specs=pl.BlockSpec((1, tc, LANES), lambda kv, gp: (kv, 0, gp)),
        out_shape=jax.ShapeDtypeStruct((2, tc, g * HEAD_DIM), F32),
        compiler_params=_cparams(("parallel", "parallel")),
        name="nsa_compress",
    )(a, pos, w1, w2)


NSA_WIN_TILES = NSA_WINDOW // NSA_TQ + 1


def _nsa_kernel(slopes_ref, q_ref, ks_ref, kw4_ref, kw3_ref, kw2_ref, kw1_ref, kw0_ref, vst_ref,
                vw4_ref, vw3_ref, vw2_ref, vw1_ref, vw0_ref, kcmp_ref, vcmpt_ref, ovt_ref, gates_ref,
                o_ref, selt_ref, m_ref, l_ref, acc_ref):
    g, qi = pl.program_id(0), pl.program_id(1)
    tq = q_ref.shape[0]
    tc = kcmp_ref.shape[0]
    n_sel = ovt_ref.shape[0]
    rr = NSA_GROUP
    t0 = qi * tq
    hg = g % 2
    nt = (((1,), (1,)), ((), ()))
    slope = [slopes_ref[g * rr + r] for r in range(rr)]

    half = _lane_half((tq, LANES))
    parts = []
    for r in range(rr):
        q_pair = q_ref[:, (r // 2) * LANES:(r // 2 + 1) * LANES]
        src = jnp.where(hg == r % 2, q_pair, _swap_halves(q_pair))
        parts.append(jnp.where(half == hg, src, jnp.zeros_like(src)) * jnp.asarray(HEAD_DIM ** -0.5, BF16))
    qp = jnp.concatenate(parts, axis=0)

    def lanes_of(s, r):
        return s[:, r * tq:(r + 1) * tq]

    s_all = lax.dot_general(kcmp_ref[...], qp, nt, preferred_element_type=F32)
    n_id = lax.broadcasted_iota(jnp.int32, (tc, tq), 0)
    q_id = lax.broadcasted_iota(jnp.int32, (tc, tq), 1)
    neg_d = (CMP_STRIDE * n_id + (CMP_BLOCK - 1) - q_id - t0).astype(F32)
    valid = neg_d <= 0.0
    o_cmp, p_sum = [], jnp.zeros((tc, tq), F32)
    for r in range(rr):
        s = jnp.where(valid, lanes_of(s_all, r) + slope[r] * neg_d, MASKED)
        m = jnp.maximum(jnp.max(s, axis=0, keepdims=True), M_INIT)
        e = jnp.exp(s - m)
        l = jnp.sum(e, axis=0, keepdims=True)
        p = e * jnp.where(l > 0.0, 1.0 / l, 0.0)
        p_sum = p_sum + p
        o_cmp.append(jnp.dot(vcmpt_ref[...], p.astype(BF16), preferred_element_type=F32))

    imp = jnp.dot(ovt_ref[...], p_sum.astype(BF16), preferred_element_type=F32)
    blk = lax.broadcasted_iota(jnp.int32, (n_sel, tq), 0)
    cur = (t0 + lax.broadcasted_iota(jnp.int32, (n_sel, tq), 1)) // SEL_BLOCK
    blk_f = blk.astype(F32)
    score = jnp.where(blk == 0, FORCE_SCORE,
                      jnp.where(blk == cur, FORCE_SCORE,
                                jnp.where(blk == cur - 1, FORCE_SCORE, jnp.where(blk <= cur, imp, -1.0))))

    def pick(_, carry):
        sc, chosen = carry
        best = jnp.max(sc, axis=0, keepdims=True)
        first = jnp.min(jnp.where(sc == best, blk_f, float(n_sel)), axis=0, keepdims=True)
        hit = blk_f == first
        return jnp.where(hit, -3e38, sc), jnp.where(hit, 1.0, chosen)

    _, chosen = lax.fori_loop(0, N_SELECT, pick, (score, jnp.zeros((n_sel, tq), F32)))
    selt_ref[...] = jnp.where(blk <= cur, chosen, 0.0)

    m_ref[...] = jnp.full_like(m_ref, M_INIT)
    l_ref[...] = jnp.zeros_like(l_ref)
    acc_ref[...] = jnp.zeros_like(acc_ref)
    x_id = lax.broadcasted_iota(jnp.int32, (NSA_TK, tq), 0)
    xq = (x_id - lax.broadcasted_iota(jnp.int32, (NSA_TK, tq), 1)).astype(F32)
    blocks_per_chunk = NSA_TK // SEL_BLOCK

    def sel_chunk(c, carry):
        start = pl.multiple_of(c * NSA_TK, NSA_TK)
        s_c = lax.dot_general(ks_ref[pl.ds(start, NSA_TK), :], qp, nt, preferred_element_type=F32)
        rows = selt_ref[pl.ds(pl.multiple_of(c * blocks_per_chunk, blocks_per_chunk), blocks_per_chunk), :]
        picked = jnp.concatenate([jnp.broadcast_to(rows[b:b + 1], (SEL_BLOCK, tq))
                                  for b in range(blocks_per_chunk)], axis=0)
        neg = xq + (start - t0).astype(F32)
        ok = jnp.where(neg <= 0.0, picked, 0.0) > 0.5
        v_t = vst_ref[c]
        for r in range(rr):
            s = jnp.where(ok, lanes_of(s_c, r) + slope[r] * neg, MASKED)
            m_old = m_ref[r]
            m_new = jnp.maximum(m_old, jnp.max(s, axis=0, keepdims=True))
            e = jnp.exp(s - m_new)
            alpha = jnp.exp(m_old - m_new)
            m_ref[r] = m_new
            l_ref[r] = alpha * l_ref[r] + jnp.sum(e, axis=0, keepdims=True)
            acc_ref[r] = alpha * acc_ref[r] + jnp.dot(v_t, e.astype(BF16), preferred_element_type=F32)
        return carry

    lax.fori_loop(0, t0 // NSA_TK + 1, sel_chunk, 0)

    kw_refs = (kw4_ref, kw3_ref, kw2_ref, kw1_ref, kw0_ref)
    vw_refs = (vw4_ref, vw3_ref, vw2_ref, vw1_ref, vw0_ref)
    xw = lax.broadcasted_iota(jnp.int32, (tq, tq), 0)
    qw = lax.broadcasted_iota(jnp.int32, (tq, tq), 1)
    s_win = []
    for idx in range(NSA_WIN_TILES):
        back = NSA_WIN_TILES - 1 - idx
        s_t = lax.dot_general(kw_refs[idx][...], qp, nt, preferred_element_type=F32)
        dist = (qw - xw + back * tq).astype(F32)
        inside = (dist >= 0.0) & (dist < float(NSA_WINDOW))
        tile_ok = qi >= back
        per_head = []
        for r in range(rr):
            s = lanes_of(s_t, r) - slope[r] * dist
            per_head.append(jnp.where(inside, jnp.where(tile_ok, s, MASKED), MASKED))
        s_win.append(per_head)
    o_win = []
    for r in range(rr):
        m = functools.reduce(jnp.maximum, [jnp.max(s_win[idx][r], axis=0, keepdims=True)
                                           for idx in range(NSA_WIN_TILES)])
        l = jnp.zeros((1, tq), F32)
        acc = jnp.zeros((HEAD_DIM, tq), F32)
        for idx in range(NSA_WIN_TILES):
            e = jnp.exp(s_win[idx][r] - m)
            l = l + jnp.sum(e, axis=0, keepdims=True)
            acc = acc + jnp.dot(vw_refs[idx][...], e.astype(BF16), preferred_element_type=F32)
        o_win.append(acc / l)

    gate = jax.nn.sigmoid(gates_ref[0])
    merged = []
    for r in range(rr):
        o_sel = acc_ref[r] / l_ref[r]
        merged.append(gate[3 * r:3 * r + 1] * o_cmp[r] + gate[3 * r + 1:3 * r + 2] * o_sel
                      + gate[3 * r + 2:3 * r + 3] * o_win[r])
    o_ref[...] = jnp.concatenate(merged, axis=0).T.astype(o_ref.dtype)


def _nsa_overlap_t(n_sel, tc):
    cmp_start = np.arange(tc - 1) * CMP_STRIDE
    sel_start = np.arange(n_sel) * SEL_BLOCK
    ov = np.clip(np.minimum(cmp_start[None, :] + CMP_BLOCK, sel_start[:, None] + SEL_BLOCK)
                 - np.maximum(cmp_start[None, :], sel_start[:, None]), 0, None) / CMP_BLOCK
    return np.concatenate([ov, np.zeros((n_sel, 1))], axis=1).astype(np.float32)


def _nsa_attention(main, vs_t3, vw_t, k_cmp, v_cmp_t, gates_t):
    t = main.shape[0]
    tq = NSA_TQ
    tc = k_cmp.shape[0]
    n_sel = t // SEL_BLOCK
    nc = t // NSA_TK
    ks_col = Q_DIM // LANES
    kw_col = ks_col + NSA_KV_DIM // LANES
    ov_t = jnp.asarray(_nsa_overlap_t(n_sel, tc), BF16)
    slopes = jnp.asarray(_alibi_slopes())

    def kw_spec(back):
        return pl.BlockSpec((tq, LANES), lambda g, i: (jnp.maximum(i - back, 0), kw_col + g // 2))

    def vw_spec(back):
        return pl.BlockSpec((HEAD_DIM, tq), lambda g, i: (g, jnp.maximum(i - back, 0)))

    backs = list(range(NSA_WIN_TILES - 1, -1, -1))
    return pl.pallas_call(
        _nsa_kernel,
        grid=(NSA_KV_HEADS, t // tq),
        in_specs=[pl.BlockSpec(memory_space=pltpu.SMEM),
                  pl.BlockSpec((tq, NSA_GROUP * HEAD_DIM), lambda g, i: (i, g)),
                  pl.BlockSpec((t, LANES), lambda g, i: (0, ks_col + g // 2))]
                 + [kw_spec(b) for b in backs]
                 + [pl.BlockSpec((nc, HEAD_DIM, NSA_TK), lambda g, i: (0, g, 0))]
                 + [vw_spec(b) for b in backs]
                 + [pl.BlockSpec((tc, LANES), lambda g, i: (0, g // 2)),
                    pl.BlockSpec((HEAD_DIM, tc), lambda g, i: (g, 0)),
                    pl.BlockSpec((n_sel, tc), lambda g, i: (0, 0)),
                    pl.BlockSpec((1, 16, tq), lambda g, i: (g, 0, i))],
        out_specs=pl.BlockSpec((tq, NSA_GROUP * HEAD_DIM), lambda g, i: (i, g)),
        out_shape=jax.ShapeDtypeStruct((t, Q_DIM), BF16),
        scratch_shapes=[pltpu.VMEM((n_sel, tq), F32), pltpu.VMEM((NSA_GROUP, 1, tq), F32),
                        pltpu.VMEM((NSA_GROUP, 1, tq), F32), pltpu.VMEM((NSA_GROUP, HEAD_DIM, tq), F32)],
        compiler_params=_cparams(("parallel", "arbitrary")),
        name="nsa_attention",
    )(slopes, main, main, *([main] * NSA_WIN_TILES), vs_t3, *([vw_t] * NSA_WIN_TILES), k_cmp, v_cmp_t, ov_t, gates_t)


def _nsa_layer(x, norm_g, w_in, cmp_pos, cmp_w1, cmp_w2, w_out):
    t = x.shape[0]
    kv = NSA_KV_DIM
    col = lambda i: w_in[:, Q_DIM + i * kv:Q_DIM + (i + 1) * kv]
    w_main = jnp.concatenate([w_in[:, :Q_DIM], col(2), col(4), col(3), col(5)], axis=1).astype(BF16)
    w_cmp = jnp.concatenate([col(0), col(1)], axis=1).astype(BF16)
    n_gate = 3 * N_HEADS
    w_gate = jnp.concatenate([w_in[:, Q_DIM + 6 * kv:], jnp.zeros((D_MODEL, LANES - n_gate), w_in.dtype)],
                             axis=1).astype(BF16)
    main = _norm_proj(x, norm_g, w_main, BF16)
    kcvc = _norm_proj(x, norm_g, w_cmp, F32)
    gate_logit = _norm_proj(x, norm_g, w_gate, F32)

    tc = t // CMP_STRIDE
    a = kcvc.reshape(tc, CMP_STRIDE, 2, NSA_KV_HEADS, HEAD_DIM).transpose(2, 3, 0, 1, 4)
    a = a.reshape(2, NSA_KV_HEADS, tc, CMP_STRIDE * HEAD_DIM)
    zeros = jnp.zeros_like(cmp_w2)
    w2 = jnp.stack([jnp.concatenate([cmp_w2, zeros], axis=2), jnp.concatenate([zeros, cmp_w2], axis=2)], axis=1)
    cmp = _nsa_compress(a, cmp_pos.reshape(2, 1, CMP_BLOCK * HEAD_DIM), cmp_w1.astype(BF16), w2.astype(BF16))
    k_cmp = cmp[0].astype(BF16)
    v_cmp_t = cmp[1].T.astype(BF16)

    off_vs = Q_DIM + 2 * kv
    vs_t3 = main[:, off_vs:off_vs + kv].reshape(t // NSA_TK, NSA_TK, kv).transpose(0, 2, 1)
    vw_t = main[:, off_vs + kv:].T
    gates_t = gate_logit[:, :n_gate].reshape(t, NSA_KV_HEADS, 3 * NSA_GROUP).transpose(1, 2, 0)
    gates_t = jnp.pad(gates_t, ((0, 0), (0, 16 - 3 * NSA_GROUP), (0, 0)))
    o = _nsa_attention(main, vs_t3, vw_t, k_cmp, v_cmp_t, gates_t)
    return _out_proj(x, o, w_out.astype(BF16))


def kernel(x, attn_norm, mlp_norm, final_norm, nsa_w_in, nsa_cmp_pos, nsa_cmp_w1, nsa_cmp_w2, nsa_w_out,
           swa_w_in, swa_sinks, swa_w_out, fox_w_in, fox_f_bias, fox_w_out,
           mlp_w_up, mlp_conv_w, mlp_conv_b, mlp_w_down):
    assert x.shape[0] == 1, "the trunk is written for batch 1"
    h = x[0]
    for i in range(DEPTH):
        kind, j = i % N_MIXERS, i // N_MIXERS
        if kind == 0:
            h = _nsa_layer(h, attn_norm[i], nsa_w_in[j], nsa_cmp_pos[j], nsa_cmp_w1[j], nsa_cmp_w2[j], nsa_w_out[j])
        elif kind == 1:
            h = _swa_layer(h, attn_norm[i], swa_w_in[j], swa_sinks[j], swa_w_out[j])
        else:
            h = _fox_layer(h, attn_norm[i], fox_w_in[j], fox_f_bias[j], fox_w_out[j])
        h = _mlp(h, mlp_norm[i], mlp_w_up[i].astype(BF16), mlp_conv_w[i], mlp_conv_b[i].reshape(1, -1),
                 mlp_w_down[i].astype(BF16))
    return _rmsnorm(h, final_norm)[None]
```

```python
import functools

import numpy as np
import jax
import jax.numpy as jnp
from jax import lax
from jax.experimental import pallas as pl
from jax.experimental.pallas import tpu as pltpu

F32 = jnp.float32
BF16 = jnp.bfloat16

D_MODEL = 1024
DEPTH = 4
N_MIXERS = 3
HEAD_DIM = 64
N_HEADS = 16
Q_DIM = N_HEADS * HEAD_DIM
ALIBI_MAX = 8.0
NORM_EPS = 1e-6
FORCE_SCORE = 1e4
MASKED = -2e30
M_INIT = -1e30

NSA_KV_HEADS = 4
NSA_GROUP = 4
NSA_KV_DIM = NSA_KV_HEADS * HEAD_DIM
CMP_BLOCK = 32
CMP_STRIDE = 16
CMP_HIDDEN = 256
SEL_BLOCK = 64
N_SELECT = 16
NSA_WINDOW = 512
NSA_TQ = 128
NSA_TK = 512

SWA_KV_HEADS = 2
SWA_GROUP = 8
SWA_WINDOW = 128
SWA_TQ = 128

FOX_T = 512

D_FF = 2816
LANES = 128
VMEM_LIMIT = 56 * 1024 * 1024


def _cparams(semantics, vmem=VMEM_LIMIT):
    return pltpu.CompilerParams(dimension_semantics=semantics, vmem_limit_bytes=vmem)


def _alibi_slopes():
    return np.asarray(2.0 ** (-ALIBI_MAX * np.arange(1, N_HEADS + 1) / N_HEADS), dtype=np.float32)


def _rms(x, g):
    ms = jnp.mean(x * x, axis=-1, keepdims=True)
    return x * lax.rsqrt(ms + NORM_EPS) * g


def _norm_proj_kernel(x_ref, g_ref, w_ref, o_ref, h_ref):
    @pl.when(pl.program_id(1) == 0)
    def _():
        h_ref[...] = _rms(x_ref[...], g_ref[...]).astype(BF16)

    o_ref[...] = jnp.dot(h_ref[...], w_ref[...], preferred_element_type=F32).astype(o_ref.dtype)


def _pick_tile(n, candidates):
    for c in candidates:
        if n % c == 0:
            return c
    raise ValueError(f"no tile for {n}")


def _norm_proj(x, g, w, out_dtype):
    t, d = x.shape
    n = w.shape[1]
    tm = _pick_tile(t, (1024, 512, 256, 128))
    tn = _pick_tile(n, (1024, 768, 512, 256, 128))
    return pl.pallas_call(
        _norm_proj_kernel,
        grid=(t // tm, n // tn),
        in_specs=[pl.BlockSpec((tm, d), lambda i, j: (i, 0)),
                  pl.BlockSpec((1, d), lambda i, j: (0, 0)),
                  pl.BlockSpec((d, tn), lambda i, j: (0, j))],
        out_specs=pl.BlockSpec((tm, tn), lambda i, j: (i, j)),
        out_shape=jax.ShapeDtypeStruct((t, n), out_dtype),
        scratch_shapes=[pltpu.VMEM((tm, d), BF16)],
        compiler_params=_cparams(("parallel", "arbitrary")),
        name="norm_proj",
    )(x, g.reshape(1, d), w)


def _out_proj_kernel(x_ref, o_ref, w_ref, y_ref):
    y_ref[...] = x_ref[...] + jnp.dot(o_ref[...], w_ref[...], preferred_element_type=F32)


def _out_proj(x, o, w):
    t, d = x.shape
    tm = _pick_tile(t, (1024, 512, 256, 128))
    return pl.pallas_call(
        _out_proj_kernel,
        grid=(t // tm,),
        in_specs=[pl.BlockSpec((tm, d), lambda i: (i, 0)),
                  pl.BlockSpec((tm, o.shape[1]), lambda i: (i, 0)),
                  pl.BlockSpec(w.shape, lambda i: (0, 0))],
        out_specs=pl.BlockSpec((tm, d), lambda i: (i, 0)),
        out_shape=jax.ShapeDtypeStruct((t, d), F32),
        compiler_params=_cparams(("parallel",)),
        name="out_proj",
    )(x, o, w)


MLP_HALO = 16


def _mlp_kernel(x_ref, xh_ref, g_ref, wa_ref, wg_ref, cwa_ref, cwg_ref, cba_ref, cbg_ref, wd_ref,
                y_ref, h_ref, acc_ref):
    i, j = pl.program_id(0), pl.program_id(1)
    tm = x_ref.shape[0]

    @pl.when(j == 0)
    def _():
        halo = _rms(xh_ref[...], g_ref[...])
        h_ref[0:MLP_HALO, :] = jnp.where(i > 0, halo, 0.0).astype(BF16)
        h_ref[MLP_HALO:, :] = _rms(x_ref[...], g_ref[...]).astype(BF16)
        acc_ref[...] = jnp.zeros_like(acc_ref)

    h = h_ref[...]

    def conv(w_ref, cw_ref, cb_ref):
        u = jnp.dot(h, w_ref[...], preferred_element_type=F32)
        cw = cw_ref[...]
        return (cw[0:1] * u[MLP_HALO - 2:MLP_HALO - 2 + tm] + cw[1:2] * u[MLP_HALO - 1:MLP_HALO - 1 + tm]
                + cw[2:3] * u[MLP_HALO:MLP_HALO + tm] + cb_ref[...])

    a = conv(wa_ref, cwa_ref, cba_ref)
    gt = conv(wg_ref, cwg_ref, cbg_ref)
    act = (jax.nn.silu(gt) * a).astype(BF16)
    acc_ref[...] += jnp.dot(act, wd_ref[...], preferred_element_type=F32)

    @pl.when(j == pl.num_programs(1) - 1)
    def _():
        y_ref[...] = x_ref[...] + acc_ref[...]


def _mlp(x, g, w_up, conv_w, conv_b, w_down):
    t, d = x.shape
    f = w_down.shape[0]
    tm = _pick_tile(t, (1024, 512, 256, 128))
    tf = _pick_tile(f, (256, 128))
    nf = f // tf
    hb = tm // MLP_HALO
    return pl.pallas_call(
        _mlp_kernel,
        grid=(t // tm, nf),
        in_specs=[pl.BlockSpec((tm, d), lambda i, j: (i, 0)),
                  pl.BlockSpec((MLP_HALO, d), lambda i, j: (jnp.maximum(i * hb - 1, 0), 0)),
                  pl.BlockSpec((1, d), lambda i, j: (0, 0)),
                  pl.BlockSpec((d, tf), lambda i, j: (0, j)),
                  pl.BlockSpec((d, tf), lambda i, j: (0, nf + j)),
                  pl.BlockSpec((3, tf), lambda i, j: (0, j)),
                  pl.BlockSpec((3, tf), lambda i, j: (0, nf + j)),
                  pl.BlockSpec((1, tf), lambda i, j: (0, j)),
                  pl.BlockSpec((1, tf), lambda i, j: (0, nf + j)),
                  pl.BlockSpec((tf, d), lambda i, j: (j, 0))],
        out_specs=pl.BlockSpec((tm, d), lambda i, j: (i, 0)),
        out_shape=jax.ShapeDtypeStruct((t, d), F32),
        scratch_shapes=[pltpu.VMEM((tm + MLP_HALO, d), BF16), pltpu.VMEM((tm, d), F32)],
        compiler_params=_cparams(("parallel", "arbitrary")),
        name="conv_glu_mlp",
    )(x, x, g.reshape(1, d), w_up, w_up, conv_w, conv_w, conv_b, conv_b, w_down)


def _rmsnorm_kernel(x_ref, g_ref, y_ref):
    y_ref[...] = _rms(x_ref[...], g_ref[...])


def _rmsnorm(x, g):
    t, d = x.shape
    tm = _pick_tile(t, (1024, 512, 256, 128))
    return pl.pallas_call(
        _rmsnorm_kernel,
        grid=(t // tm,),
        in_specs=[pl.BlockSpec((tm, d), lambda i: (i, 0)), pl.BlockSpec((1, d), lambda i: (0, 0))],
        out_specs=pl.BlockSpec((tm, d), lambda i: (i, 0)),
        out_shape=jax.ShapeDtypeStruct((t, d), F32),
        compiler_params=_cparams(("parallel",)),
        name="final_rmsnorm",
    )(x, g.reshape(1, d))


def _swap_halves(q_pair):
    return pltpu.roll(q_pair.astype(F32), HEAD_DIM, 1).astype(BF16)


def _lane_half(shape):
    return lax.broadcasted_iota(jnp.int32, shape, 1) // HEAD_DIM


def _rows_to_heads(o_t, n_heads, tq):
    stacked = jnp.concatenate([o_t[:, r * tq:(r + 1) * tq] for r in range(n_heads)], axis=0)
    return stacked.T


def _swa_kernel(q_ref, kp_ref, kc_ref, vp_ref, vc_ref, bias_ref, sink_ref, o_ref):
    tq = q_ref.shape[0]
    rows = SWA_GROUP * tq
    k_ext = jnp.concatenate([kp_ref[...], kc_ref[...]], axis=0)
    v_ext = jnp.concatenate([vp_ref[...], vc_ref[...]], axis=1)
    half = _lane_half((tq, LANES))
    outs = []
    for g in range(SWA_KV_HEADS):
        parts = []
        for r in range(SWA_GROUP):
            hd = g * SWA_GROUP + r
            q_pair = q_ref[:, (hd // 2) * LANES:(hd // 2 + 1) * LANES]
            src = q_pair if hd % 2 == g else _swap_halves(q_pair)
            parts.append(jnp.where(half == g, src, jnp.zeros_like(src)))
        qp = jnp.concatenate(parts, axis=0)
        s = lax.dot_general(k_ext, qp, (((1,), (1,)), ((), ())), preferred_element_type=F32)
        s = s * (HEAD_DIM ** -0.5) + bias_ref[0, g]
        sink = sink_ref[g]
        m = jnp.maximum(jnp.max(s, axis=0, keepdims=True), sink)
        e = jnp.exp(s - m)
        l = jnp.sum(e, axis=0, keepdims=True) + jnp.exp(sink - m)
        o_t = jnp.dot(v_ext[g * HEAD_DIM:(g + 1) * HEAD_DIM], e.astype(BF16), preferred_element_type=F32)
        outs.append(_rows_to_heads(o_t / l, SWA_GROUP, tq))
    o_ref[...] = jnp.concatenate(outs, axis=1).astype(o_ref.dtype)


def _swa_bias():
    tq = SWA_TQ
    slopes = _alibi_slopes().reshape(SWA_KV_HEADS, SWA_GROUP)
    x = np.arange(2 * tq)[:, None] - tq
    q = np.arange(tq)[None, :]
    dist = q - x
    valid = (dist >= 0) & (dist < SWA_WINDOW)
    out = np.empty((2, SWA_KV_HEADS, 2 * tq, SWA_GROUP * tq), np.float32)
    for var in range(2):
        v = valid & ((x >= 0) | (var == 1))
        for g in range(SWA_KV_HEADS):
            for r in range(SWA_GROUP):
                out[var, g, :, r * tq:(r + 1) * tq] = np.where(v, -slopes[g, r] * dist, MASKED)
    return out


def _swa_attention(qkv, v_t, sinks):
    t = qkv.shape[0]
    tq = SWA_TQ
    kcol = Q_DIM // LANES
    bias = jnp.asarray(_swa_bias())
    sink_rows = jnp.repeat(sinks.astype(F32).reshape(SWA_KV_HEADS, 1, SWA_GROUP), tq, axis=2)
    prev = lambda i: jnp.maximum(i - 1, 0)
    return pl.pallas_call(
        _swa_kernel,
        grid=(t // tq,),
        in_specs=[pl.BlockSpec((tq, Q_DIM), lambda i: (i, 0)),
                  pl.BlockSpec((tq, LANES), lambda i: (prev(i), kcol)),
                  pl.BlockSpec((tq, LANES), lambda i: (i, kcol)),
                  pl.BlockSpec((LANES, tq), lambda i: (0, prev(i))),
                  pl.BlockSpec((LANES, tq), lambda i: (0, i)),
                  pl.BlockSpec((1,) + bias.shape[1:], lambda i: (jnp.minimum(i, 1), 0, 0, 0)),
                  pl.BlockSpec(sink_rows.shape, lambda i: (0, 0, 0))],
        out_specs=pl.BlockSpec((tq, Q_DIM), lambda i: (i, 0)),
        out_shape=jax.ShapeDtypeStruct((t, Q_DIM), BF16),
        compiler_params=_cparams(("parallel",)),
        name="swa_attention",
    )(qkv, qkv, qkv, v_t, v_t, bias, sink_rows)


def _swa_layer(x, norm_g, w_in, sinks, w_out):
    qkv = _norm_proj(x, norm_g, w_in.astype(BF16), BF16)
    v_t = qkv[:, Q_DIM + SWA_KV_HEADS * HEAD_DIM:].T
    o = _swa_attention(qkv, v_t, sinks)
    return _out_proj(x, o, w_out.astype(BF16))


FOX_PARTS = 3


def _split_bf16(v):
    parts, rest = [], v
    for _ in range(FOX_PARTS):
        p = rest.astype(BF16)
        parts.append(p)
        rest = rest - p.astype(F32)
    return parts


def _fox_prep_kernel(f_ref, b_ref, aug_ref, cref_ref, carry_ref):
    tm = f_ref.shape[0]

    @pl.when(pl.program_id(0) == 0)
    def _():
        carry_ref[...] = jnp.zeros_like(carry_ref)

    logf = jax.nn.log_sigmoid(f_ref[...] + b_ref[...])
    row = lax.broadcasted_iota(jnp.int32, (tm, tm), 0)
    col = lax.broadcasted_iota(jnp.int32, (tm, tm), 1)
    tri = jnp.where(row >= col, 1.0, 0.0).astype(BF16)
    local = sum(jnp.dot(tri, p, preferred_element_type=F32) for p in _split_bf16(logf))
    cum = local + carry_ref[...]
    carry_ref[...] = cum[tm - 1:tm, :]
    first = cum[0:1, :]
    cref_ref[0] = jnp.broadcast_to(first, cref_ref.shape[1:])
    hi, mid, lo = _split_bf16(first - cum)
    lane = lax.broadcasted_iota(jnp.int32, (tm, LANES), 1)
    zero = jnp.zeros_like(hi)
    aug_ref[...] = jnp.where(lane < N_HEADS, hi, jnp.where(lane < 2 * N_HEADS, mid,
                                                            jnp.where(lane < 3 * N_HEADS, lo, zero)))


def _fox_prep(f_logit3, f_bias3):
    t = f_logit3.shape[0]
    nt = t // FOX_T
    return pl.pallas_call(
        _fox_prep_kernel,
        grid=(nt,),
        in_specs=[pl.BlockSpec((FOX_T, LANES), lambda i: (i, 0)), pl.BlockSpec((1, LANES), lambda i: (0, 0))],
        out_specs=[pl.BlockSpec((FOX_T, LANES), lambda i: (i, 0)), pl.BlockSpec((1, 8, LANES), lambda i: (i, 0, 0))],
        out_shape=[jax.ShapeDtypeStruct((t, LANES), BF16), jax.ShapeDtypeStruct((nt, 8, LANES), F32)],
        scratch_shapes=[pltpu.VMEM((1, LANES), F32)],
        compiler_params=_cparams(("arbitrary",)),
        name="fox_prep",
    )(f_logit3, f_bias3)


def _fox_kernel(cref_ref, q_ref, k_ref, aug_ref, vt_ref, o_ref, m_ref, l_ref, acc_ref):
    pair, qi = pl.program_id(0), pl.program_id(1)
    tq = q_ref.shape[0]
    tk = FOX_T
    lane = lax.broadcasted_iota(jnp.int32, (tq, LANES), 1)
    q2 = q_ref[...] * jnp.asarray(HEAD_DIM ** -0.5, BF16)
    q_ext = []
    for hh in range(2):
        head = 2 * pair + hh
        pick = (lane % N_HEADS == head) & (lane < FOX_PARTS * N_HEADS)
        q_ext.append(jnp.concatenate([jnp.where(lane // HEAD_DIM == hh, q2, jnp.zeros_like(q2)),
                                      jnp.where(pick, 1.0, 0.0).astype(BF16)], axis=1))
    m_ref[...] = jnp.full_like(m_ref, M_INIT)
    l_ref[...] = jnp.zeros_like(l_ref)
    acc_ref[...] = jnp.zeros_like(acc_ref)

    def step(kj, diagonal):
        start = pl.multiple_of(kj * tk, tk)
        k_ext = jnp.concatenate([k_ref[pl.ds(start, tk), :], aug_ref[pl.ds(start, tk), :]], axis=1)
        for hh in range(2):
            head = 2 * pair + hh
            s = lax.dot_general(k_ext, q_ext[hh], (((1,), (1,)), ((), ())), preferred_element_type=F32)
            if diagonal:
                key = lax.broadcasted_iota(jnp.int32, (tk, tq), 0)
                qry = lax.broadcasted_iota(jnp.int32, (tk, tq), 1)
                s = jnp.where(key <= qry, s, MASKED)
            off = cref_ref[qi, head] - cref_ref[kj, head]
            m_old = m_ref[hh]
            m_new = jnp.maximum(m_old, jnp.max(s, axis=0, keepdims=True) + off)
            e = jnp.exp(s - (m_new - off))
            alpha = jnp.exp(m_old - m_new)
            m_ref[hh] = m_new
            l_ref[hh] = alpha * l_ref[hh] + jnp.sum(e, axis=0, keepdims=True)
            v_t = vt_ref[kj, hh * HEAD_DIM:(hh + 1) * HEAD_DIM, :]
            acc_ref[hh] = alpha * acc_ref[hh] + jnp.dot(v_t, e.astype(BF16), preferred_element_type=F32)

    def body(kj, carry):
        step(kj, False)
        return carry

    lax.fori_loop(0, qi, body, 0)
    step(qi, True)
    o_t = jnp.concatenate([acc_ref[hh] / l_ref[hh] for hh in range(2)], axis=0)
    o_ref[...] = o_t.T.astype(o_ref.dtype)


def _fox_attention(qkv, aug, v_t3, cref):
    t = qkv.shape[0]
    tq = FOX_T
    nk = t // FOX_T
    kcol = Q_DIM // LANES
    return pl.pallas_call(
        _fox_kernel,
        grid=(N_HEADS // 2, t // tq),
        in_specs=[pl.BlockSpec(memory_space=pltpu.SMEM),
                  pl.BlockSpec((tq, LANES), lambda p, i: (i, p)),
                  pl.BlockSpec((t, LANES), lambda p, i: (0, kcol + p)),
                  pl.BlockSpec((t, LANES), lambda p, i: (0, 0)),
                  pl.BlockSpec((nk, LANES, FOX_T), lambda p, i: (0, p, 0))],
        out_specs=pl.BlockSpec((tq, LANES), lambda p, i: (i, p)),
        out_shape=jax.ShapeDtypeStruct((t, Q_DIM), BF16),
        scratch_shapes=[pltpu.VMEM((2, 1, tq), F32), pltpu.VMEM((2, 1, tq), F32),
                        pltpu.VMEM((2, HEAD_DIM, tq), F32)],
        compiler_params=_cparams(("parallel", "arbitrary")),
        name="fox_attention",
    )(cref, qkv, qkv, aug, v_t3)


def _fox_layer(x, norm_g, w_in, f_bias, w_out):
    t = x.shape[0]
    qkv = _norm_proj(x, norm_g, w_in[:, :3 * Q_DIM].astype(BF16), BF16)
    w_f = w_in[:, 3 * Q_DIM:]
    pad = jnp.zeros((D_MODEL, LANES - FOX_PARTS * N_HEADS), w_f.dtype)
    w_f3 = jnp.concatenate([w_f] * FOX_PARTS + [pad], axis=1).astype(BF16)
    b3 = jnp.concatenate([f_bias.astype(F32)] * FOX_PARTS + [jnp.zeros((LANES - FOX_PARTS * N_HEADS,), F32)])
    f_logit3 = _norm_proj(x, norm_g, w_f3, F32)
    aug, cref = _fox_prep(f_logit3, b3.reshape(1, LANES))
    v_t3 = qkv[:, 2 * Q_DIM:].reshape(t // FOX_T, FOX_T, Q_DIM).transpose(0, 2, 1)
    o = _fox_attention(qkv, aug, v_t3, cref[:, 0, :N_HEADS])
    return _out_proj(x, o, w_out.astype(BF16))


def _nsa_compress_kernel(a_ref, pos_ref, w1_ref, w2_ref, o_ref):
    tc = a_ref.shape[2]
    half = CMP_STRIDE * HEAD_DIM
    pos = pos_ref[0]
    out = jnp.zeros(o_ref.shape[1:], F32)
    for gg in range(2):
        a = a_ref[0, gg]
        top = jnp.dot((a + pos[:, :half]).astype(BF16), w1_ref[0, :half], preferred_element_type=F32)
        bot = jnp.dot((a + pos[:, half:]).astype(BF16), w1_ref[0, half:], preferred_element_type=F32)
        hid = top + pltpu.roll(bot, tc - 1, 0)
        out = out + jnp.dot(jax.nn.gelu(hid).astype(BF16), w2_ref[0, gg], preferred_element_type=F32)
    o_ref[0] = out


def _nsa_compress(a, pos, w1, w2):
    _, g, tc, width = a.shape
    return pl.pallas_call(
        _nsa_compress_kernel,
        grid=(2, g // 2),
        in_specs=[pl.BlockSpec((1, 2, tc, width), lambda kv, gp: (kv, gp, 0, 0)),
                  pl.BlockSpec((1, 1, 2 * width), lambda kv, gp: (kv, 0, 0)),
                  pl.BlockSpec((1, 2 * width, CMP_HIDDEN), lambda kv, gp: (kv, 0, 0)),
                  pl.BlockSpec((1, 2, CMP_HIDDEN, LANES), lambda kv, gp: (kv, 0, 0, 0))],
        out_specs=pl.BlockSpec((1, tc, LANES), lambda kv, gp: (kv, 0, gp)),
        out_shape=jax.ShapeDtypeStruct((2, tc, g * HEAD_DIM), F32),
        compiler_params=_cparams(("parallel", "parallel")),
        name="nsa_compress",
    )(a, pos, w1, w2)


NSA_WIN_TILES = NSA_WINDOW // NSA_TQ + 1


def _nsa_kernel(slopes_ref, q_ref, ks_ref, kw4_ref, kw3_ref, kw2_ref, kw1_ref, kw0_ref, vst_ref,
                vw4_ref, vw3_ref, vw2_ref, vw1_ref, vw0_ref, kcmp_ref, vcmpt_ref, ovt_ref, gates_ref,
                o_ref, selt_ref, m_ref, l_ref, acc_ref, active_ref):
    g, qi = pl.program_id(0), pl.program_id(1)
    tq = q_ref.shape[0]
    tc = kcmp_ref.shape[0]
    n_sel = ovt_ref.shape[0]
    rr = NSA_GROUP
    t0 = qi * tq
    hg = g % 2
    nt = (((1,), (1,)), ((), ()))
    slope = [slopes_ref[g * rr + r] for r in range(rr)]

    half = _lane_half((tq, LANES))
    parts = []
    for r in range(rr):
        q_pair = q_ref[:, (r // 2) * LANES:(r // 2 + 1) * LANES]
        src = jnp.where(hg == r % 2, q_pair, _swap_halves(q_pair))
        parts.append(jnp.where(half == hg, src, jnp.zeros_like(src)) * jnp.asarray(HEAD_DIM ** -0.5, BF16))
    qp = jnp.concatenate(parts, axis=0)

    def lanes_of(s, r):
        return s[:, r * tq:(r + 1) * tq]

    s_all = lax.dot_general(kcmp_ref[...], qp, nt, preferred_element_type=F32)
    n_id = lax.broadcasted_iota(jnp.int32, (tc, tq), 0)
    q_id = lax.broadcasted_iota(jnp.int32, (tc, tq), 1)
    neg_d = (CMP_STRIDE * n_id + (CMP_BLOCK - 1) - q_id - t0).astype(F32)
    valid = neg_d <= 0.0
    o_cmp, p_sum = [], jnp.zeros((tc, tq), F32)
    for r in range(rr):
        s = jnp.where(valid, lanes_of(s_all, r) + slope[r] * neg_d, MASKED)
        m = jnp.maximum(jnp.max(s, axis=0, keepdims=True), M_INIT)
        e = jnp.exp(s - m)
        l = jnp.sum(e, axis=0, keepdims=True)
        p = e * jnp.where(l > 0.0, 1.0 / l, 0.0)
        p_sum = p_sum + p
        o_cmp.append(jnp.dot(vcmpt_ref[...], p.astype(BF16), preferred_element_type=F32))

    imp = jnp.dot(ovt_ref[...], p_sum.astype(BF16), preferred_element_type=F32)
    blk = lax.broadcasted_iota(jnp.int32, (n_sel, tq), 0)
    cur = (t0 + lax.broadcasted_iota(jnp.int32, (n_sel, tq), 1)) // SEL_BLOCK
    blk_f = blk.astype(F32)
    score = jnp.where(blk == 0, FORCE_SCORE,
                      jnp.where(blk == cur, FORCE_SCORE,
                                jnp.where(blk == cur - 1, FORCE_SCORE, jnp.where(blk <= cur, imp, -1.0))))

    def pick(_, carry):
        sc, chosen = carry
        best = jnp.max(sc, axis=0, keepdims=True)
        first = jnp.min(jnp.where(sc == best, blk_f, float(n_sel)), axis=0, keepdims=True)
        hit = blk_f == first
        return jnp.where(hit, -3e38, sc), jnp.where(hit, 1.0, chosen)

    _, chosen = lax.fori_loop(0, N_SELECT, pick, (score, jnp.zeros((n_sel, tq), F32)))
    selt_ref[...] = jnp.where(blk <= cur, chosen, 0.0)

    m_ref[...] = jnp.full_like(m_ref, M_INIT)
    l_ref[...] = jnp.zeros_like(l_ref)
    acc_ref[...] = jnp.zeros_like(acc_ref)
    x_id = lax.broadcasted_iota(jnp.int32, (NSA_TK, tq), 0)
    xq = (x_id - lax.broadcasted_iota(jnp.int32, (NSA_TK, tq), 1)).astype(F32)
    blocks_per_chunk = NSA_TK // SEL_BLOCK

    for c in range(n_sel // blocks_per_chunk):
        active_ref[c] = jnp.max(selt_ref[c * blocks_per_chunk:(c + 1) * blocks_per_chunk, :])

    def sel_chunk(c, carry):
        @pl.when(active_ref[c] > 0.0)
        def _():
            sel_chunk_body(c)
        return carry

    def sel_chunk_body(c):
        start = pl.multiple_of(c * NSA_TK, NSA_TK)
        s_c = lax.dot_general(ks_ref[pl.ds(start, NSA_TK), :], qp, nt, preferred_element_type=F32)
        rows = selt_ref[pl.ds(pl.multiple_of(c * blocks_per_chunk, blocks_per_chunk), blocks_per_chunk), :]
        picked = jnp.concatenate([jnp.broadcast_to(rows[b:b + 1], (SEL_BLOCK, tq))
                                  for b in range(blocks_per_chunk)], axis=0)
        neg = xq + (start - t0).astype(F32)
        ok = jnp.where(neg <= 0.0, picked, 0.0) > 0.5
        v_t = vst_ref[c]
        for r in range(rr):
            s = jnp.where(ok, lanes_of(s_c, r) + slope[r] * neg, MASKED)
            m_old = m_ref[r]
            m_new = jnp.maximum(m_old, jnp.max(s, axis=0, keepdims=True))
            e = jnp.exp(s - m_new)
            alpha = jnp.exp(m_old - m_new)
            m_ref[r] = m_new
            l_ref[r] = alpha * l_ref[r] + jnp.sum(e, axis=0, keepdims=True)
            acc_ref[r] = alpha * acc_ref[r] + jnp.dot(v_t, e.astype(BF16), preferred_element_type=F32)

    lax.fori_loop(0, t0 // NSA_TK + 1, sel_chunk, 0)

    kw_refs = (kw4_ref, kw3_ref, kw2_ref, kw1_ref, kw0_ref)
    vw_refs = (vw4_ref, vw3_ref, vw2_ref, vw1_ref, vw0_ref)
    xw = lax.broadcasted_iota(jnp.int32, (tq, tq), 0)
    qw = lax.broadcasted_iota(jnp.int32, (tq, tq), 1)
    s_win = []
    for idx in range(NSA_WIN_TILES):
        back = NSA_WIN_TILES - 1 - idx
        s_t = lax.dot_general(kw_refs[idx][...], qp, nt, preferred_element_type=F32)
        dist = (qw - xw + back * tq).astype(F32)
        inside = (dist >= 0.0) & (dist < float(NSA_WINDOW))
        tile_ok = qi >= back
        per_head = []
        for r in range(rr):
            s = lanes_of(s_t, r) - slope[r] * dist
            per_head.append(jnp.where(inside, jnp.where(tile_ok, s, MASKED), MASKED))
        s_win.append(per_head)
    o_win = []
    for r in range(rr):
        m = functools.reduce(jnp.maximum, [jnp.max(s_win[idx][r], axis=0, keepdims=True)
                                           for idx in range(NSA_WIN_TILES)])
        l = jnp.zeros((1, tq), F32)
        acc = jnp.zeros((HEAD_DIM, tq), F32)
        for idx in range(NSA_WIN_TILES):
            e = jnp.exp(s_win[idx][r] - m)
            l = l + jnp.sum(e, axis=0, keepdims=True)
            acc = acc + jnp.dot(vw_refs[idx][...], e.astype(BF16), preferred_element_type=F32)
        o_win.append(acc / l)

    gate = jax.nn.sigmoid(gates_ref[0])
    merged = []
    for r in range(rr):
        o_sel = acc_ref[r] / l_ref[r]
        merged.append(gate[3 * r:3 * r + 1] * o_cmp[r] + gate[3 * r + 1:3 * r + 2] * o_sel
                      + gate[3 * r + 2:3 * r + 3] * o_win[r])
    o_ref[...] = jnp.concatenate(merged, axis=0).T.astype(o_ref.dtype)


def _nsa_overlap_t(n_sel, tc):
    cmp_start = np.arange(tc - 1) * CMP_STRIDE
    sel_start = np.arange(n_sel) * SEL_BLOCK
    ov = np.clip(np.minimum(cmp_start[None, :] + CMP_BLOCK, sel_start[:, None] + SEL_BLOCK)
                 - np.maximum(cmp_start[None, :], sel_start[:, None]), 0, None) / CMP_BLOCK
    return np.concatenate([ov, np.zeros((n_sel, 1))], axis=1).astype(np.float32)


def _nsa_attention(main, vs_t3, vw_t, k_cmp, v_cmp_t, gates_t):
    t = main.shape[0]
    tq = NSA_TQ
    tc = k_cmp.shape[0]
    n_sel = t // SEL_BLOCK
    nc = t // NSA_TK
    ks_col = Q_DIM // LANES
    kw_col = ks_col + NSA_KV_DIM // LANES
    ov_t = jnp.asarray(_nsa_overlap_t(n_sel, tc), BF16)
    slopes = jnp.asarray(_alibi_slopes())

    def kw_spec(back):
        return pl.BlockSpec((tq, LANES), lambda g, i: (jnp.maximum(i - back, 0), kw_col + g // 2))

    def vw_spec(back):
        return pl.BlockSpec((HEAD_DIM, tq), lambda g, i: (g, jnp.maximum(i - back, 0)))

    backs = list(range(NSA_WIN_TILES - 1, -1, -1))
    return pl.pallas_call(
        _nsa_kernel,
        grid=(NSA_KV_HEADS, t // tq),
        in_specs=[pl.BlockSpec(memory_space=pltpu.SMEM),
                  pl.BlockSpec((tq, NSA_GROUP * HEAD_DIM), lambda g, i: (i, g)),
                  pl.BlockSpec((t, LANES), lambda g, i: (0, ks_col + g // 2))]
                 + [kw_spec(b) for b in backs]
                 + [pl.BlockSpec((nc, HEAD_DIM, NSA_TK), lambda g, i: (0, g, 0))]
                 + [vw_spec(b) for b in backs]
                 + [pl.BlockSpec((tc, LANES), lambda g, i: (0, g // 2)),
                    pl.BlockSpec((HEAD_DIM, tc), lambda g, i: (g, 0)),
                    pl.BlockSpec((n_sel, tc), lambda g, i: (0, 0)),
                    pl.BlockSpec((1, 16, tq), lambda g, i: (g, 0, i))],
        out_specs=pl.BlockSpec((tq, NSA_GROUP * HEAD_DIM), lambda g, i: (i, g)),
        out_shape=jax.ShapeDtypeStruct((t, Q_DIM), BF16),
        scratch_shapes=[pltpu.VMEM((n_sel, tq), F32), pltpu.VMEM((NSA_GROUP, 1, tq), F32),
                        pltpu.VMEM((NSA_GROUP, 1, tq), F32), pltpu.VMEM((NSA_GROUP, HEAD_DIM, tq), F32),
                        pltpu.SMEM((n_sel * SEL_BLOCK // NSA_TK,), F32)],
        compiler_params=_cparams(("parallel", "arbitrary")),
        name="nsa_attention",
    )(slopes, main, main, *([main] * NSA_WIN_TILES), vs_t3, *([vw_t] * NSA_WIN_TILES), k_cmp, v_cmp_t, ov_t, gates_t)


def _nsa_layer(x, norm_g, w_in, cmp_pos, cmp_w1, cmp_w2, w_out):
    t = x.shape[0]
    kv = NSA_KV_DIM
    col = lambda i: w_in[:, Q_DIM + i * kv:Q_DIM + (i + 1) * kv]
    w_main = jnp.concatenate([w_in[:, :Q_DIM], col(2), col(4), col(3), col(5)], axis=1).astype(BF16)
    w_cmp = jnp.concatenate([col(0), col(1)], axis=1).astype(BF16)
    n_gate = 3 * N_HEADS
    w_gate = jnp.concatenate([w_in[:, Q_DIM + 6 * kv:], jnp.zeros((D_MODEL, LANES - n_gate), w_in.dtype)],
                             axis=1).astype(BF16)
    main = _norm_proj(x, norm_g, w_main, BF16)
    kcvc = _norm_proj(x, norm_g, w_cmp, F32)
    gate_logit = _norm_proj(x, norm_g, w_gate, F32)

    tc = t // CMP_STRIDE
    a = kcvc.reshape(tc, CMP_STRIDE, 2, NSA_KV_HEADS, HEAD_DIM).transpose(2, 3, 0, 1, 4)
    a = a.reshape(2, NSA_KV_HEADS, tc, CMP_STRIDE * HEAD_DIM)
    zeros = jnp.zeros_like(cmp_w2)
    w2 = jnp.stack([jnp.concatenate([cmp_w2, zeros], axis=2), jnp.concatenate([zeros, cmp_w2], axis=2)], axis=1)
    cmp = _nsa_compress(a, cmp_pos.reshape(2, 1, CMP_BLOCK * HEAD_DIM), cmp_w1.astype(BF16), w2.astype(BF16))
    k_cmp = cmp[0].astype(BF16)
    v_cmp_t = cmp[1].T.astype(BF16)

    off_vs = Q_DIM + 2 * kv
    vs_t3 = main[:, off_vs:off_vs + kv].reshape(t // NSA_TK, NSA_TK, kv).transpose(0, 2, 1)
    vw_t = main[:, off_vs + kv:].T
    gates_t = gate_logit[:, :n_gate].reshape(t, NSA_KV_HEADS, 3 * NSA_GROUP).transpose(1, 2, 0)
    gates_t = jnp.pad(gates_t, ((0, 0), (0, 16 - 3 * NSA_GROUP), (0, 0)))
    o = _nsa_attention(main, vs_t3, vw_t, k_cmp, v_cmp_t, gates_t)
    return _out_proj(x, o, w_out.astype(BF16))


def kernel(x, attn_norm, mlp_norm, final_norm, nsa_w_in, nsa_cmp_pos, nsa_cmp_w1, nsa_cmp_w2, nsa_w_out,
           swa_w_in, swa_sinks, swa_w_out, fox_w_in, fox_f_bias, fox_w_out,
           mlp_w_up, mlp_conv_w, mlp_conv_b, mlp_w_down):
    assert x.shape[0] == 1, "the trunk is written for batch 1"
    h = x[0]
    for i in range(DEPTH):
        kind, j = i % N_MIXERS, i // N_MIXERS
        if kind == 0:
            h = _nsa_layer(h, attn_norm[i], nsa_w_in[j], nsa_cmp_pos[j], nsa_cmp_w1[j], nsa_cmp_w2[j], nsa_w_out[j])
        elif kind == 1:
            h = _swa_layer(h, attn_norm[i], swa_w_in[j], swa_sinks[j], swa_w_out[j])
        else:
            h = _fox_layer(h, attn_norm[i], fox_w_in[j], fox_f_bias[j], fox_w_out[j])
        h = _mlp(h, mlp_norm[i], mlp_w_up[i].astype(BF16), mlp_conv_w[i], mlp_conv_b[i].reshape(1, -1),
                 mlp_w_down[i].astype(BF16))
    return _rmsnorm(h, final_norm)[None]
```

```python
import functools

import numpy as np
import jax
import jax.numpy as jnp
from jax import lax
from jax.experimental import pallas as pl
from jax.experimental.pallas import tpu as pltpu

F32 = jnp.float32
BF16 = jnp.bfloat16

D_MODEL = 1024
DEPTH = 4
N_MIXERS = 3
HEAD_DIM = 64
N_HEADS = 16
Q_DIM = N_HEADS * HEAD_DIM
ALIBI_MAX = 8.0
NORM_EPS = 1e-6
FORCE_SCORE = 1e4
MASKED = -2e30
M_INIT = -1e30

NSA_KV_HEADS = 4
NSA_GROUP = 4
NSA_KV_DIM = NSA_KV_HEADS * HEAD_DIM
CMP_BLOCK = 32
CMP_STRIDE = 16
CMP_HIDDEN = 256
SEL_BLOCK = 64
N_SELECT = 16
NSA_WINDOW = 512
NSA_TQ = 128
NSA_TK = 512

SWA_KV_HEADS = 2
SWA_GROUP = 8
SWA_WINDOW = 128
SWA_TQ = 128

FOX_TQ = 512
FOX_T = 256

D_FF = 2816
LANES = 128
VMEM_LIMIT = 56 * 1024 * 1024


def _cparams(semantics, vmem=VMEM_LIMIT):
    return pltpu.CompilerParams(dimension_semantics=semantics, vmem_limit_bytes=vmem)


def _alibi_slopes():
    return np.asarray(2.0 ** (-ALIBI_MAX * np.arange(1, N_HEADS + 1) / N_HEADS), dtype=np.float32)


def _rms(x, g):
    ms = jnp.mean(x * x, axis=-1, keepdims=True)
    return x * lax.rsqrt(ms + NORM_EPS) * g


def _norm_proj_kernel(x_ref, g_ref, w_ref, o_ref, h_ref):
    @pl.when(pl.program_id(1) == 0)
    def _():
        h_ref[...] = _rms(x_ref[...], g_ref[...]).astype(BF16)

    o_ref[...] = jnp.dot(h_ref[...], w_ref[...], preferred_element_type=F32).astype(o_ref.dtype)


def _pick_tile(n, candidates):
    for c in candidates:
        if n % c == 0:
            return c
    raise ValueError(f"no tile for {n}")


def _norm_proj(x, g, w, out_dtype):
    t, d = x.shape
    n = w.shape[1]
    tm = _pick_tile(t, (1024, 512, 256, 128))
    tn = _pick_tile(n, (1024, 768, 512, 256, 128))
    return pl.pallas_call(
        _norm_proj_kernel,
        grid=(t // tm, n // tn),
        in_specs=[pl.BlockSpec((tm, d), lambda i, j: (i, 0)),
                  pl.BlockSpec((1, d), lambda i, j: (0, 0)),
                  pl.BlockSpec((d, tn), lambda i, j: (0, j))],
        out_specs=pl.BlockSpec((tm, tn), lambda i, j: (i, j)),
        out_shape=jax.ShapeDtypeStruct((t, n), out_dtype),
        scratch_shapes=[pltpu.VMEM((tm, d), BF16)],
        compiler_params=_cparams(("parallel", "arbitrary")),
        name="norm_proj",
    )(x, g.reshape(1, d), w)


def _out_proj_kernel(x_ref, o_ref, w_ref, y_ref):
    y_ref[...] = x_ref[...] + jnp.dot(o_ref[...], w_ref[...], preferred_element_type=F32)


def _out_proj(x, o, w):
    t, d = x.shape
    tm = _pick_tile(t, (1024, 512, 256, 128))
    return pl.pallas_call(
        _out_proj_kernel,
        grid=(t // tm,),
        in_specs=[pl.BlockSpec((tm, d), lambda i: (i, 0)),
                  pl.BlockSpec((tm, o.shape[1]), lambda i: (i, 0)),
                  pl.BlockSpec(w.shape, lambda i: (0, 0))],
        out_specs=pl.BlockSpec((tm, d), lambda i: (i, 0)),
        out_shape=jax.ShapeDtypeStruct((t, d), F32),
        compiler_params=_cparams(("parallel",)),
        name="out_proj",
    )(x, o, w)


MLP_HALO = 16


def _mlp_kernel(x_ref, xh_ref, g_ref, wa_ref, wg_ref, cwa_ref, cwg_ref, cba_ref, cbg_ref, wd_ref,
                y_ref, h_ref, acc_ref):
    i, j = pl.program_id(0), pl.program_id(1)
    tm = x_ref.shape[0]

    @pl.when(j == 0)
    def _():
        halo = _rms(xh_ref[...], g_ref[...])
        h_ref[0:MLP_HALO, :] = jnp.where(i > 0, halo, 0.0).astype(BF16)
        h_ref[MLP_HALO:, :] = _rms(x_ref[...], g_ref[...]).astype(BF16)
        acc_ref[...] = jnp.zeros_like(acc_ref)

    h = h_ref[...]

    def conv(w_ref, cw_ref, cb_ref):
        u = jnp.dot(h, w_ref[...], preferred_element_type=F32)
        cw = cw_ref[...]
        return (cw[0:1] * u[MLP_HALO - 2:MLP_HALO - 2 + tm] + cw[1:2] * u[MLP_HALO - 1:MLP_HALO - 1 + tm]
                + cw[2:3] * u[MLP_HALO:MLP_HALO + tm] + cb_ref[...])

    a = conv(wa_ref, cwa_ref, cba_ref)
    gt = conv(wg_ref, cwg_ref, cbg_ref)
    act = (jax.nn.silu(gt) * a).astype(BF16)
    acc_ref[...] += jnp.dot(act, wd_ref[...], preferred_element_type=F32)

    @pl.when(j == pl.num_programs(1) - 1)
    def _():
        y_ref[...] = x_ref[...] + acc_ref[...]


def _mlp(x, g, w_up, conv_w, conv_b, w_down):
    t, d = x.shape
    f = w_down.shape[0]
    tm = _pick_tile(t, (1024, 512, 256, 128))
    tf = _pick_tile(f, (256, 128))
    nf = f // tf
    hb = tm // MLP_HALO
    return pl.pallas_call(
        _mlp_kernel,
        grid=(t // tm, nf),
        in_specs=[pl.BlockSpec((tm, d), lambda i, j: (i, 0)),
                  pl.BlockSpec((MLP_HALO, d), lambda i, j: (jnp.maximum(i * hb - 1, 0), 0)),
                  pl.BlockSpec((1, d), lambda i, j: (0, 0)),
                  pl.BlockSpec((d, tf), lambda i, j: (0, j)),
                  pl.BlockSpec((d, tf), lambda i, j: (0, nf + j)),
                  pl.BlockSpec((3, tf), lambda i, j: (0, j)),
                  pl.BlockSpec((3, tf), lambda i, j: (0, nf + j)),
                  pl.BlockSpec((1, tf), lambda i, j: (0, j)),
                  pl.BlockSpec((1, tf), lambda i, j: (0, nf + j)),
                  pl.BlockSpec((tf, d), lambda i, j: (j, 0))],
        out_specs=pl.BlockSpec((tm, d), lambda i, j: (i, 0)),
        out_shape=jax.ShapeDtypeStruct((t, d), F32),
        scratch_shapes=[pltpu.VMEM((tm + MLP_HALO, d), BF16), pltpu.VMEM((tm, d), F32)],
        compiler_params=_cparams(("parallel", "arbitrary")),
        name="conv_glu_mlp",
    )(x, x, g.reshape(1, d), w_up, w_up, conv_w, conv_w, conv_b, conv_b, w_down)


def _rmsnorm_kernel(x_ref, g_ref, y_ref):
    y_ref[...] = _rms(x_ref[...], g_ref[...])


def _rmsnorm(x, g):
    t, d = x.shape
    tm = _pick_tile(t, (1024, 512, 256, 128))
    return pl.pallas_call(
        _rmsnorm_kernel,
        grid=(t // tm,),
        in_specs=[pl.BlockSpec((tm, d), lambda i: (i, 0)), pl.BlockSpec((1, d), lambda i: (0, 0))],
        out_specs=pl.BlockSpec((tm, d), lambda i: (i, 0)),
        out_shape=jax.ShapeDtypeStruct((t, d), F32),
        compiler_params=_cparams(("parallel",)),
        name="final_rmsnorm",
    )(x, g.reshape(1, d))


def _swap_halves(q_pair):
    return pltpu.roll(q_pair.astype(F32), HEAD_DIM, 1).astype(BF16)


def _lane_half(shape):
    return lax.broadcasted_iota(jnp.int32, shape, 1) // HEAD_DIM


def _rows_to_heads(o_t, n_heads, tq):
    stacked = jnp.concatenate([o_t[:, r * tq:(r + 1) * tq] for r in range(n_heads)], axis=0)
    return stacked.T


def _swa_kernel(q_ref, kp_ref, kc_ref, vp_ref, vc_ref, bias_ref, sink_ref, o_ref):
    tq = q_ref.shape[0]
    rows = SWA_GROUP * tq
    k_ext = jnp.concatenate([kp_ref[...], kc_ref[...]], axis=0)
    v_ext = jnp.concatenate([vp_ref[...], vc_ref[...]], axis=1)
    half = _lane_half((tq, LANES))
    outs = []
    for g in range(SWA_KV_HEADS):
        parts = []
        for r in range(SWA_GROUP):
            hd = g * SWA_GROUP + r
            q_pair = q_ref[:, (hd // 2) * LANES:(hd // 2 + 1) * LANES]
            src = q_pair if hd % 2 == g else _swap_halves(q_pair)
            parts.append(jnp.where(half == g, src, jnp.zeros_like(src)))
        qp = jnp.concatenate(parts, axis=0)
        s = lax.dot_general(k_ext, qp, (((1,), (1,)), ((), ())), preferred_element_type=F32)
        s = s * (HEAD_DIM ** -0.5) + bias_ref[0, g]
        sink = sink_ref[g]
        m = jnp.maximum(jnp.max(s, axis=0, keepdims=True), sink)
        e = jnp.exp(s - m)
        l = jnp.sum(e, axis=0, keepdims=True) + jnp.exp(sink - m)
        o_t = jnp.dot(v_ext[g * HEAD_DIM:(g + 1) * HEAD_DIM], e.astype(BF16), preferred_element_type=F32)
        outs.append(_rows_to_heads(o_t / l, SWA_GROUP, tq))
    o_ref[...] = jnp.concatenate(outs, axis=1).astype(o_ref.dtype)


def _swa_bias():
    tq = SWA_TQ
    slopes = _alibi_slopes().reshape(SWA_KV_HEADS, SWA_GROUP)
    x = np.arange(2 * tq)[:, None] - tq
    q = np.arange(tq)[None, :]
    dist = q - x
    valid = (dist >= 0) & (dist < SWA_WINDOW)
    out = np.empty((2, SWA_KV_HEADS, 2 * tq, SWA_GROUP * tq), np.float32)
    for var in range(2):
        v = valid & ((x >= 0) | (var == 1))
        for g in range(SWA_KV_HEADS):
            for r in range(SWA_GROUP):
                out[var, g, :, r * tq:(r + 1) * tq] = np.where(v, -slopes[g, r] * dist, MASKED)
    return out


def _swa_attention(qkv, v_t, sinks):
    t = qkv.shape[0]
    tq = SWA_TQ
    kcol = Q_DIM // LANES
    bias = jnp.asarray(_swa_bias())
    sink_rows = jnp.repeat(sinks.astype(F32).reshape(SWA_KV_HEADS, 1, SWA_GROUP), tq, axis=2)
    prev = lambda i: jnp.maximum(i - 1, 0)
    return pl.pallas_call(
        _swa_kernel,
        grid=(t // tq,),
        in_specs=[pl.BlockSpec((tq, Q_DIM), lambda i: (i, 0)),
                  pl.BlockSpec((tq, LANES), lambda i: (prev(i), kcol)),
                  pl.BlockSpec((tq, LANES), lambda i: (i, kcol)),
                  pl.BlockSpec((LANES, tq), lambda i: (0, prev(i))),
                  pl.BlockSpec((LANES, tq), lambda i: (0, i)),
                  pl.BlockSpec((1,) + bias.shape[1:], lambda i: (jnp.minimum(i, 1), 0, 0, 0)),
                  pl.BlockSpec(sink_rows.shape, lambda i: (0, 0, 0))],
        out_specs=pl.BlockSpec((tq, Q_DIM), lambda i: (i, 0)),
        out_shape=jax.ShapeDtypeStruct((t, Q_DIM), BF16),
        compiler_params=_cparams(("parallel",)),
        name="swa_attention",
    )(qkv, qkv, qkv, v_t, v_t, bias, sink_rows)


def _swa_layer(x, norm_g, w_in, sinks, w_out):
    qkv = _norm_proj(x, norm_g, w_in.astype(BF16), BF16)
    v_t = qkv[:, Q_DIM + SWA_KV_HEADS * HEAD_DIM:].T
    o = _swa_attention(qkv, v_t, sinks)
    return _out_proj(x, o, w_out.astype(BF16))


FOX_PARTS = 3


def _split_bf16(v):
    parts, rest = [], v
    for _ in range(FOX_PARTS):
        p = rest.astype(BF16)
        parts.append(p)
        rest = rest - p.astype(F32)
    return parts


def _fox_prep_kernel(f_ref, b_ref, aug_ref, cref_ref, carry_ref):
    tm = f_ref.shape[0]

    @pl.when(pl.program_id(0) == 0)
    def _():
        carry_ref[...] = jnp.zeros_like(carry_ref)

    logf = jax.nn.log_sigmoid(f_ref[...] + b_ref[...])
    row = lax.broadcasted_iota(jnp.int32, (tm, tm), 0)
    col = lax.broadcasted_iota(jnp.int32, (tm, tm), 1)
    tri = jnp.where(row >= col, 1.0, 0.0).astype(BF16)
    local = sum(jnp.dot(tri, p, preferred_element_type=F32) for p in _split_bf16(logf))
    cum = local + carry_ref[...]
    carry_ref[...] = cum[tm - 1:tm, :]
    first = cum[0:1, :]
    cref_ref[0] = jnp.broadcast_to(first, cref_ref.shape[1:])
    hi, mid, lo = _split_bf16(first - cum)
    lane = lax.broadcasted_iota(jnp.int32, (tm, LANES), 1)
    zero = jnp.zeros_like(hi)
    aug_ref[...] = jnp.where(lane < N_HEADS, hi, jnp.where(lane < 2 * N_HEADS, mid,
                                                            jnp.where(lane < 3 * N_HEADS, lo, zero)))


def _fox_prep(f_logit3, f_bias3):
    t = f_logit3.shape[0]
    nt = t // FOX_T
    return pl.pallas_call(
        _fox_prep_kernel,
        grid=(nt,),
        in_specs=[pl.BlockSpec((FOX_T, LANES), lambda i: (i, 0)), pl.BlockSpec((1, LANES), lambda i: (0, 0))],
        out_specs=[pl.BlockSpec((FOX_T, LANES), lambda i: (i, 0)), pl.BlockSpec((1, 8, LANES), lambda i: (i, 0, 0))],
        out_shape=[jax.ShapeDtypeStruct((t, LANES), BF16), jax.ShapeDtypeStruct((nt, 8, LANES), F32)],
        scratch_shapes=[pltpu.VMEM((1, LANES), F32)],
        compiler_params=_cparams(("arbitrary",)),
        name="fox_prep",
    )(f_logit3, f_bias3)


def _fox_kernel(cref_ref, q_ref, k_ref, aug_ref, vt_ref, o_ref, m_ref, l_ref, acc_ref, qt_ref, s_ref):
    pair, qi = pl.program_id(0), pl.program_id(1)
    tq = q_ref.shape[0]
    tk = FOX_T
    lane = lax.broadcasted_iota(jnp.int32, (tq, LANES), 1)
    q2 = q_ref[...] * jnp.asarray(HEAD_DIM ** -0.5, BF16)
    for hh in range(2):
        head = 2 * pair + hh
        pick = (lane % N_HEADS == head) & (lane < FOX_PARTS * N_HEADS)
        q_ext = jnp.concatenate([jnp.where(lane // HEAD_DIM == hh, q2, jnp.zeros_like(q2)).astype(F32),
                                 jnp.where(pick, 1.0, 0.0)], axis=1)
        qt_ref[hh] = q_ext.T.astype(BF16)
    m_ref[...] = jnp.full_like(m_ref, M_INIT)
    l_ref[...] = jnp.zeros_like(l_ref)
    acc_ref[...] = jnp.zeros_like(acc_ref)
    ratio = tq // tk

    def scores(kj, hh):
        start = pl.multiple_of(kj * tk, tk)
        k_ext = jnp.concatenate([k_ref[pl.ds(start, tk), :], aug_ref[pl.ds(start, tk), :]], axis=1)
        s_ref[hh] = jnp.dot(k_ext, qt_ref[hh], preferred_element_type=F32)

    def consume(kj, hh, key_offset):
        head = 2 * pair + hh
        s = s_ref[hh]
        if key_offset is not None:
            key = lax.broadcasted_iota(jnp.int32, (tk, tq), 0) + key_offset
            qry = lax.broadcasted_iota(jnp.int32, (tk, tq), 1)
            s = jnp.where(key <= qry, s, MASKED)
        off = cref_ref[qi * ratio, head] - cref_ref[kj, head]
        m_old = m_ref[hh]
        m_new = jnp.maximum(m_old, jnp.max(s, axis=0, keepdims=True) + off)
        e = jnp.exp(s - (m_new - off))
        alpha = jnp.exp(m_old - m_new)
        m_ref[hh] = m_new
        l_ref[hh] = alpha * l_ref[hh] + jnp.sum(e, axis=0, keepdims=True)
        v_t = vt_ref[kj, hh * HEAD_DIM:(hh + 1) * HEAD_DIM, :]
        acc_ref[hh] = alpha * acc_ref[hh] + jnp.dot(v_t, e.astype(BF16), preferred_element_type=F32)

    n_full = qi * ratio
    scores(0, 0)

    def body(j, carry):
        for d in range(ratio):
            kj = j * ratio + d
            scores(kj, 1)
            consume(kj, 0, None)
            scores(kj + 1, 0)
            consume(kj, 1, None)
        return carry

    lax.fori_loop(0, qi, body, 0)
    for d in range(ratio):
        kj = n_full + d
        scores(kj, 1)
        consume(kj, 0, d * tk)
        if d + 1 < ratio:
            scores(kj + 1, 0)
        consume(kj, 1, d * tk)
    o_t = jnp.concatenate([acc_ref[hh] / l_ref[hh] for hh in range(2)], axis=0)
    o_ref[...] = o_t.T.astype(o_ref.dtype)


def _fox_attention(qkv, aug, v_t3, cref):
    t = qkv.shape[0]
    tq = FOX_TQ
    nk = t // FOX_T
    kcol = Q_DIM // LANES
    return pl.pallas_call(
        _fox_kernel,
        grid=(N_HEADS // 2, t // tq),
        in_specs=[pl.BlockSpec(memory_space=pltpu.SMEM),
                  pl.BlockSpec((tq, LANES), lambda p, i: (i, p)),
                  pl.BlockSpec((t, LANES), lambda p, i: (0, kcol + p)),
                  pl.BlockSpec((t, LANES), lambda p, i: (0, 0)),
                  pl.BlockSpec((nk, LANES, FOX_T), lambda p, i: (0, p, 0))],
        out_specs=pl.BlockSpec((tq, LANES), lambda p, i: (i, p)),
        out_shape=jax.ShapeDtypeStruct((t, Q_DIM), BF16),
        scratch_shapes=[pltpu.VMEM((2, 1, tq), F32), pltpu.VMEM((2, 1, tq), F32),
                        pltpu.VMEM((2, HEAD_DIM, tq), F32), pltpu.VMEM((2, 2 * LANES, tq), BF16), pltpu.VMEM((2, FOX_T, tq), F32)],
        compiler_params=_cparams(("parallel", "arbitrary")),
        name="fox_attention",
    )(cref, qkv, qkv, aug, v_t3)


def _fox_layer(x, norm_g, w_in, f_bias, w_out):
    t = x.shape[0]
    qkv = _norm_proj(x, norm_g, w_in[:, :3 * Q_DIM].astype(BF16), BF16)
    w_f = w_in[:, 3 * Q_DIM:]
    pad = jnp.zeros((D_MODEL, LANES - FOX_PARTS * N_HEADS), w_f.dtype)
    w_f3 = jnp.concatenate([w_f] * FOX_PARTS + [pad], axis=1).astype(BF16)
    b3 = jnp.concatenate([f_bias.astype(F32)] * FOX_PARTS + [jnp.zeros((LANES - FOX_PARTS * N_HEADS,), F32)])
    f_logit3 = _norm_proj(x, norm_g, w_f3, F32)
    aug, cref = _fox_prep(f_logit3, b3.reshape(1, LANES))
    v_t3 = qkv[:, 2 * Q_DIM:].reshape(t // FOX_T, FOX_T, Q_DIM).transpose(0, 2, 1)
    o = _fox_attention(qkv, aug, v_t3, cref[:, 0, :N_HEADS])
    return _out_proj(x, o, w_out.astype(BF16))


def _nsa_compress_kernel(a_ref, pos_ref, w1_ref, w2_ref, o_ref):
    tc = a_ref.shape[2]
    half = CMP_STRIDE * HEAD_DIM
    pos = pos_ref[0]
    out = jnp.zeros(o_ref.shape[1:], F32)
    for gg in range(2):
        a = a_ref[0, gg]
        top = jnp.dot((a + pos[:, :half]).astype(BF16), w1_ref[0, :half], preferred_element_type=F32)
        bot = jnp.dot((a + pos[:, half:]).astype(BF16), w1_ref[0, half:], preferred_element_type=F32)
        hid = top + pltpu.roll(bot, tc - 1, 0)
        out = out + jnp.dot(jax.nn.gelu(hid).astype(BF16), w2_ref[0, gg], preferred_element_type=F32)
    o_ref[0] = out


def _nsa_compress(a, pos, w1, w2):
    _, g, tc, width = a.shape
    return pl.pallas_call(
        _nsa_compress_kernel,
        grid=(2, g // 2),
        in_specs=[pl.BlockSpec((1, 2, tc, width), lambda kv, gp: (kv, gp, 0, 0)),
                  pl.BlockSpec((1, 1, 2 * width), lambda kv, gp: (kv, 0, 0)),
                  pl.BlockSpec((1, 2 * width, CMP_HIDDEN), lambda kv, gp: (kv, 0, 0)),
                  pl.BlockSpec((1, 2, CMP_HIDDEN, LANES), lambda kv, gp: (kv, 0, 0, 0))],
        out_specs=pl.BlockSpec((1, tc, LANES), lambda kv, gp: (kv, 0, gp)),
        out_shape=jax.ShapeDtypeStruct((2, tc, g * HEAD_DIM), F32),
        compiler_params=_cparams(("parallel", "parallel")),
        name="nsa_compress",
    )(a, pos, w1, w2)


NSA_WIN_TILES = NSA_WINDOW // NSA_TQ + 1


def _nsa_kernel(slopes_ref, q_ref, ks_ref, kw4_ref, kw3_ref, kw2_ref, kw1_ref, kw0_ref, vst_ref,
                vw4_ref, vw3_ref, vw2_ref, vw1_ref, vw0_ref, kcmp_ref, vcmpt_ref, ovt_ref, gates_ref,
                o_ref, selt_ref, m_ref, l_ref, acc_ref, active_ref):
    g, qi = pl.program_id(0), pl.program_id(1)
    tq = q_ref.shape[0]
    tc = kcmp_ref.shape[0]
    n_sel = ovt_ref.shape[0]
    rr = NSA_GROUP
    t0 = qi * tq
    hg = g % 2
    nt = (((1,), (1,)), ((), ()))
    slope = [slopes_ref[g * rr + r] for r in range(rr)]

    half = _lane_half((tq, LANES))
    parts = []
    for r in range(rr):
        q_pair = q_ref[:, (r // 2) * LANES:(r // 2 + 1) * LANES]
        src = jnp.where(hg == r % 2, q_pair, _swap_halves(q_pair))
        parts.append(jnp.where(half == hg, src, jnp.zeros_like(src)) * jnp.asarray(HEAD_DIM ** -0.5, BF16))
    qp = jnp.concatenate(parts, axis=0)

    def lanes_of(s, r):
        return s[:, r * tq:(r + 1) * tq]

    s_all = lax.dot_general(kcmp_ref[...], qp, nt, preferred_element_type=F32)
    n_id = lax.broadcasted_iota(jnp.int32, (tc, tq), 0)
    q_id = lax.broadcasted_iota(jnp.int32, (tc, tq), 1)
    neg_d = (CMP_STRIDE * n_id + (CMP_BLOCK - 1) - q_id - t0).astype(F32)
    valid = neg_d <= 0.0
    o_cmp, p_sum = [], jnp.zeros((tc, tq), F32)
    for r in range(rr):
        s = jnp.where(valid, lanes_of(s_all, r) + slope[r] * neg_d, MASKED)
        m = jnp.maximum(jnp.max(s, axis=0, keepdims=True), M_INIT)
        e = jnp.exp(s - m)
        l = jnp.sum(e, axis=0, keepdims=True)
        p = e * jnp.where(l > 0.0, 1.0 / l, 0.0)
        p_sum = p_sum + p
        o_cmp.append(jnp.dot(vcmpt_ref[...], p.astype(BF16), preferred_element_type=F32))

    imp = jnp.dot(ovt_ref[...], p_sum.astype(BF16), preferred_element_type=F32)
    blk = lax.broadcasted_iota(jnp.int32, (n_sel, tq), 0)
    cur = (t0 + lax.broadcasted_iota(jnp.int32, (n_sel, tq), 1)) // SEL_BLOCK
    blk_f = blk.astype(F32)
    score = jnp.where(blk == 0, FORCE_SCORE,
                      jnp.where(blk == cur, FORCE_SCORE,
                                jnp.where(blk == cur - 1, FORCE_SCORE, jnp.where(blk <= cur, imp, -1.0))))

    def pick(_, carry):
        sc, chosen = carry
        best = jnp.max(sc, axis=0, keepdims=True)
        first = jnp.min(jnp.where(sc == best, blk_f, float(n_sel)), axis=0, keepdims=True)
        hit = blk_f == first
        return jnp.where(hit, -3e38, sc), jnp.where(hit, 1.0, chosen)

    _, chosen = lax.fori_loop(0, N_SELECT, pick, (score, jnp.zeros((n_sel, tq), F32)))
    selt_ref[...] = jnp.where(blk <= cur, chosen, 0.0)

    m_ref[...] = jnp.full_like(m_ref, M_INIT)
    l_ref[...] = jnp.zeros_like(l_ref)
    acc_ref[...] = jnp.zeros_like(acc_ref)
    x_id = lax.broadcasted_iota(jnp.int32, (NSA_TK, tq), 0)
    xq = (x_id - lax.broadcasted_iota(jnp.int32, (NSA_TK, tq), 1)).astype(F32)
    blocks_per_chunk = NSA_TK // SEL_BLOCK

    for c in range(n_sel // blocks_per_chunk):
        active_ref[c] = jnp.max(selt_ref[c * blocks_per_chunk:(c + 1) * blocks_per_chunk, :])

    def sel_chunk(c, carry):
        @pl.when(active_ref[c] > 0.0)
        def _():
            sel_chunk_body(c)
        return carry

    def sel_chunk_body(c):
        start = pl.multiple_of(c * NSA_TK, NSA_TK)
        s_c = lax.dot_general(ks_ref[pl.ds(start, NSA_TK), :], qp, nt, preferred_element_type=F32)
        rows = selt_ref[pl.ds(pl.multiple_of(c * blocks_per_chunk, blocks_per_chunk), blocks_per_chunk), :]
        picked = jnp.concatenate([jnp.broadcast_to(rows[b:b + 1], (SEL_BLOCK, tq))
                                  for b in range(blocks_per_chunk)], axis=0)
        neg = xq + (start - t0).astype(F32)
        ok = jnp.where(neg <= 0.0, picked, 0.0) > 0.5
        v_t = vst_ref[c]
        for r in range(rr):
            s = jnp.where(ok, lanes_of(s_c, r) + slope[r] * neg, MASKED)
            m_old = m_ref[r]
            m_new = jnp.maximum(m_old, jnp.max(s, axis=0, keepdims=True))
            e = jnp.exp(s - m_new)
            alpha = jnp.exp(m_old - m_new)
            m_ref[r] = m_new
            l_ref[r] = alpha * l_ref[r] + jnp.sum(e, axis=0, keepdims=True)
            acc_ref[r] = alpha * acc_ref[r] + jnp.dot(v_t, e.astype(BF16), preferred_element_type=F32)

    lax.fori_loop(0, t0 // NSA_TK + 1, sel_chunk, 0)

    kw_refs = (kw4_ref, kw3_ref, kw2_ref, kw1_ref, kw0_ref)
    vw_refs = (vw4_ref, vw3_ref, vw2_ref, vw1_ref, vw0_ref)
    xw = lax.broadcasted_iota(jnp.int32, (tq, tq), 0)
    qw = lax.broadcasted_iota(jnp.int32, (tq, tq), 1)
    s_win = []
    for idx in range(NSA_WIN_TILES):
        back = NSA_WIN_TILES - 1 - idx
        s_t = lax.dot_general(kw_refs[idx][...], qp, nt, preferred_element_type=F32)
        dist = (qw - xw + back * tq).astype(F32)
        inside = (dist >= 0.0) & (dist < float(NSA_WINDOW))
        tile_ok = qi >= back
        per_head = []
        for r in range(rr):
            s = lanes_of(s_t, r) - slope[r] * dist
            per_head.append(jnp.where(inside, jnp.where(tile_ok, s, MASKED), MASKED))
        s_win.append(per_head)
    o_win = []
    for r in range(rr):
        m = functools.reduce(jnp.maximum, [jnp.max(s_win[idx][r], axis=0, keepdims=True)
                                           for idx in range(NSA_WIN_TILES)])
        l = jnp.zeros((1, tq), F32)
        acc = jnp.zeros((HEAD_DIM, tq), F32)
        for idx in range(NSA_WIN_TILES):
            e = jnp.exp(s_win[idx][r] - m)
            l = l + jnp.sum(e, axis=0, keepdims=True)
            acc = acc + jnp.dot(vw_refs[idx][...], e.astype(BF16), preferred_element_type=F32)
        o_win.append(acc / l)

    gate = jax.nn.sigmoid(gates_ref[0])
    merged = []
    for r in range(rr):
        o_sel = acc_ref[r] / l_ref[r]
        merged.append(gate[3 * r:3 * r + 1] * o_cmp[r] + gate[3 * r + 1:3 * r + 2] * o_sel
                      + gate[3 * r + 2:3 * r + 3] * o_win[r])
    o_ref[...] = jnp.concatenate(merged, axis=0).T.astype(o_ref.dtype)


def _nsa_overlap_t(n_sel, tc):
    cmp_start = np.arange(tc - 1) * CMP_STRIDE
    sel_start = np.arange(n_sel) * SEL_BLOCK
    ov = np.clip(np.minimum(cmp_start[None, :] + CMP_BLOCK, sel_start[:, None] + SEL_BLOCK)
                 - np.maximum(cmp_start[None, :], sel_start[:, None]), 0, None) / CMP_BLOCK
    return np.concatenate([ov, np.zeros((n_sel, 1))], axis=1).astype(np.float32)


def _nsa_attention(main, vs_t3, vw_t, k_cmp, v_cmp_t, gates_t):
    t = main.shape[0]
    tq = NSA_TQ
    tc = k_cmp.shape[0]
    n_sel = t // SEL_BLOCK
    nc = t // NSA_TK
    ks_col = Q_DIM // LANES
    kw_col = ks_col + NSA_KV_DIM // LANES
    ov_t = jnp.asarray(_nsa_overlap_t(n_sel, tc), BF16)
    slopes = jnp.asarray(_alibi_slopes())

    def kw_spec(back):
        return pl.BlockSpec((tq, LANES), lambda g, i: (jnp.maximum(i - back, 0), kw_col + g // 2))

    def vw_spec(back):
        return pl.BlockSpec((HEAD_DIM, tq), lambda g, i: (g, jnp.maximum(i - back, 0)))

    backs = list(range(NSA_WIN_TILES - 1, -1, -1))
    return pl.pallas_call(
        _nsa_kernel,
        grid=(NSA_KV_HEADS, t // tq),
        in_specs=[pl.BlockSpec(memory_space=pltpu.SMEM),
                  pl.BlockSpec((tq, NSA_GROUP * HEAD_DIM), lambda g, i: (i, g)),
                  pl.BlockSpec((t, LANES), lambda g, i: (0, ks_col + g // 2))]
                 + [kw_spec(b) for b in backs]
                 + [pl.BlockSpec((nc, HEAD_DIM, NSA_TK), lambda g, i: (0, g, 0))]
                 + [vw_spec(b) for b in backs]
                 + [pl.BlockSpec((tc, LANES), lambda g, i: (0, g // 2)),
                    pl.BlockSpec((HEAD_DIM, tc), lambda g, i: (g, 0)),
                    pl.BlockSpec((n_sel, tc), lambda g, i: (0, 0)),
                    pl.BlockSpec((1, 16, tq), lambda g, i: (g, 0, i))],
        out_specs=pl.BlockSpec((tq, NSA_GROUP * HEAD_DIM), lambda g, i: (i, g)),
        out_shape=jax.ShapeDtypeStruct((t, Q_DIM), BF16),
        scratch_shapes=[pltpu.VMEM((n_sel, tq), F32), pltpu.VMEM((NSA_GROUP, 1, tq), F32),
                        pltpu.VMEM((NSA_GROUP, 1, tq), F32), pltpu.VMEM((NSA_GROUP, HEAD_DIM, tq), F32),
                        pltpu.SMEM((n_sel * SEL_BLOCK // NSA_TK,), F32)],
        compiler_params=_cparams(("parallel", "arbitrary")),
        name="nsa_attention",
    )(slopes, main, main, *([main] * NSA_WIN_TILES), vs_t3, *([vw_t] * NSA_WIN_TILES), k_cmp, v_cmp_t, ov_t, gates_t)


def _nsa_layer(x, norm_g, w_in, cmp_pos, cmp_w1, cmp_w2, w_out):
    t = x.shape[0]
    kv = NSA_KV_DIM
    col = lambda i: w_in[:, Q_DIM + i * kv:Q_DIM + (i + 1) * kv]
    w_main = jnp.concatenate([w_in[:, :Q_DIM], col(2), col(4), col(3), col(5)], axis=1).astype(BF16)
    w_cmp = jnp.concatenate([col(0), col(1)], axis=1).astype(BF16)
    n_gate = 3 * N_HEADS
    w_gate = jnp.concatenate([w_in[:, Q_DIM + 6 * kv:], jnp.zeros((D_MODEL, LANES - n_gate), w_in.dtype)],
                             axis=1).astype(BF16)
    main = _norm_proj(x, norm_g, w_main, BF16)
    kcvc = _norm_proj(x, norm_g, w_cmp, F32)
    gate_logit = _norm_proj(x, norm_g, w_gate, F32)

    tc = t // CMP_STRIDE
    a = kcvc.reshape(tc, CMP_STRIDE, 2, NSA_KV_HEADS, HEAD_DIM).transpose(2, 3, 0, 1, 4)
    a = a.reshape(2, NSA_KV_HEADS, tc, CMP_STRIDE * HEAD_DIM)
    zeros = jnp.zeros_like(cmp_w2)
    w2 = jnp.stack([jnp.concatenate([cmp_w2, zeros], axis=2), jnp.concatenate([zeros, cmp_w2], axis=2)], axis=1)
    cmp = _nsa_compress(a, cmp_pos.reshape(2, 1, CMP_BLOCK * HEAD_DIM), cmp_w1.astype(BF16), w2.astype(BF16))
    k_cmp = cmp[0].astype(BF16)
    v_cmp_t = cmp[1].T.astype(BF16)

    off_vs = Q_DIM + 2 * kv
    vs_t3 = main[:, off_vs:off_vs + kv].reshape(t // NSA_TK, NSA_TK, kv).transpose(0, 2, 1)
    vw_t = main[:, off_vs + kv:].T
    gates_t = gate_logit[:, :n_gate].reshape(t, NSA_KV_HEADS, 3 * NSA_GROUP).transpose(1, 2, 0)
    gates_t = jnp.pad(gates_t, ((0, 0), (0, 16 - 3 * NSA_GROUP), (0, 0)))
    o = _nsa_attention(main, vs_t3, vw_t, k_cmp, v_cmp_t, gates_t)
    return _out_proj(x, o, w_out.astype(BF16))


def kernel(x, attn_norm, mlp_norm, final_norm, nsa_w_in, nsa_cmp_pos, nsa_cmp_w1, nsa_cmp_w2, nsa_w_out,
           swa_w_in, swa_sinks, swa_w_out, fox_w_in, fox_f_bias, fox_w_out,
           mlp_w_up, mlp_conv_w, mlp_conv_b, mlp_w_down):
    assert x.shape[0] == 1, "the trunk is written for batch 1"
    h = x[0]
    for i in range(DEPTH):
        kind, j = i % N_MIXERS, i // N_MIXERS
        if kind == 0:
            h = _nsa_layer(h, attn_norm[i], nsa_w_in[j], nsa_cmp_pos[j], nsa_cmp_w1[j], nsa_cmp_w2[j], nsa_w_out[j])
        elif kind == 1:
            h = _swa_layer(h, attn_norm[i], swa_w_in[j], swa_sinks[j], swa_w_out[j])
        else:
            h = _fox_layer(h, attn_norm[i], fox_w_in[j], fox_f_bias[j], fox_w_out[j])
        h = _mlp(h, mlp_norm[i], mlp_w_up[i].astype(BF16), mlp_conv_w[i], mlp_conv_b[i].reshape(1, -1),
                 mlp_w_down[i].astype(BF16))
    return _rmsnorm(h, final_norm)[None]
```

```python
import functools

import numpy as np
import jax
import jax.numpy as jnp
from jax import lax
from jax.experimental import pallas as pl
from jax.experimental.pallas import tpu as pltpu

F32 = jnp.float32
BF16 = jnp.bfloat16

D_MODEL = 1024
DEPTH = 4
N_MIXERS = 3
HEAD_DIM = 64
N_HEADS = 16
Q_DIM = N_HEADS * HEAD_DIM
ALIBI_MAX = 8.0
NORM_EPS = 1e-6
FORCE_SCORE = 1e4
MASKED = -2e30
M_INIT = -1e30

NSA_KV_HEADS = 4
NSA_GROUP = 4
NSA_KV_DIM = NSA_KV_HEADS * HEAD_DIM
CMP_BLOCK = 32
CMP_STRIDE = 16
CMP_HIDDEN = 256
SEL_BLOCK = 64
N_SELECT = 16
NSA_WINDOW = 512
NSA_TQ = 128
NSA_TK = 512

SWA_KV_HEADS = 2
SWA_GROUP = 8
SWA_WINDOW = 128
SWA_TQ = 128

FOX_TQ = 512
FOX_T = 256

D_FF = 2816
LANES = 128
VMEM_LIMIT = 56 * 1024 * 1024


def _cparams(semantics, vmem=VMEM_LIMIT):
    return pltpu.CompilerParams(dimension_semantics=semantics, vmem_limit_bytes=vmem)


def _alibi_slopes():
    return np.asarray(2.0 ** (-ALIBI_MAX * np.arange(1, N_HEADS + 1) / N_HEADS), dtype=np.float32)


def _rms(x, g):
    ms = jnp.mean(x * x, axis=-1, keepdims=True)
    return x * lax.rsqrt(ms + NORM_EPS) * g


def _norm_proj_kernel(x_ref, g_ref, w_ref, o_ref, h_ref):
    @pl.when(pl.program_id(1) == 0)
    def _():
        h_ref[...] = _rms(x_ref[...], g_ref[...]).astype(BF16)

    o_ref[...] = jnp.dot(h_ref[...], w_ref[...], preferred_element_type=F32).astype(o_ref.dtype)


def _pick_tile(n, candidates):
    for c in candidates:
        if n % c == 0:
            return c
    raise ValueError(f"no tile for {n}")


def _norm_proj(x, g, w, out_dtype):
    t, d = x.shape
    n = w.shape[1]
    tm = _pick_tile(t, (1024, 512, 256, 128))
    tn = _pick_tile(n, (1024, 768, 512, 256, 128))
    return pl.pallas_call(
        _norm_proj_kernel,
        grid=(t // tm, n // tn),
        in_specs=[pl.BlockSpec((tm, d), lambda i, j: (i, 0)),
                  pl.BlockSpec((1, d), lambda i, j: (0, 0)),
                  pl.BlockSpec((d, tn), lambda i, j: (0, j))],
        out_specs=pl.BlockSpec((tm, tn), lambda i, j: (i, j)),
        out_shape=jax.ShapeDtypeStruct((t, n), out_dtype),
        scratch_shapes=[pltpu.VMEM((tm, d), BF16)],
        compiler_params=_cparams(("parallel", "arbitrary")),
        name="norm_proj",
    )(x, g.reshape(1, d), w)


def _out_proj_kernel(x_ref, o_ref, w_ref, y_ref):
    y_ref[...] = x_ref[...] + jnp.dot(o_ref[...], w_ref[...], preferred_element_type=F32)


def _out_proj(x, o, w):
    t, d = x.shape
    tm = _pick_tile(t, (1024, 512, 256, 128))
    return pl.pallas_call(
        _out_proj_kernel,
        grid=(t // tm,),
        in_specs=[pl.BlockSpec((tm, d), lambda i: (i, 0)),
                  pl.BlockSpec((tm, o.shape[1]), lambda i: (i, 0)),
                  pl.BlockSpec(w.shape, lambda i: (0, 0))],
        out_specs=pl.BlockSpec((tm, d), lambda i: (i, 0)),
        out_shape=jax.ShapeDtypeStruct((t, d), F32),
        compiler_params=_cparams(("parallel",)),
        name="out_proj",
    )(x, o, w)


MLP_HALO = 16


def _mlp_kernel(x_ref, xh_ref, g_ref, wa_ref, wg_ref, cwa_ref, cwg_ref, cba_ref, cbg_ref, wd_ref,
                y_ref, h_ref, acc_ref):
    i, j = pl.program_id(0), pl.program_id(1)
    tm = x_ref.shape[0]

    @pl.when(j == 0)
    def _():
        halo = _rms(xh_ref[...], g_ref[...])
        h_ref[0:MLP_HALO, :] = jnp.where(i > 0, halo, 0.0).astype(BF16)
        h_ref[MLP_HALO:, :] = _rms(x_ref[...], g_ref[...]).astype(BF16)
        acc_ref[...] = jnp.zeros_like(acc_ref)

    h = h_ref[...]

    def conv(w_ref, cw_ref, cb_ref):
        u = jnp.dot(h, w_ref[...], preferred_element_type=F32)
        cw = cw_ref[...]
        return (cw[0:1] * u[MLP_HALO - 2:MLP_HALO - 2 + tm] + cw[1:2] * u[MLP_HALO - 1:MLP_HALO - 1 + tm]
                + cw[2:3] * u[MLP_HALO:MLP_HALO + tm] + cb_ref[...])

    a = conv(wa_ref, cwa_ref, cba_ref)
    gt = conv(wg_ref, cwg_ref, cbg_ref)
    act = (jax.nn.silu(gt) * a).astype(BF16)
    acc_ref[...] += jnp.dot(act, wd_ref[...], preferred_element_type=F32)

    @pl.when(j == pl.num_programs(1) - 1)
    def _():
        y_ref[...] = x_ref[...] + acc_ref[...]


def _mlp(x, g, w_up, conv_w, conv_b, w_down):
    t, d = x.shape
    f = w_down.shape[0]
    tm = _pick_tile(t, (1024, 512, 256, 128))
    tf = _pick_tile(f, (256, 128))
    nf = f // tf
    hb = tm // MLP_HALO
    return pl.pallas_call(
        _mlp_kernel,
        grid=(t // tm, nf),
        in_specs=[pl.BlockSpec((tm, d), lambda i, j: (i, 0)),
                  pl.BlockSpec((MLP_HALO, d), lambda i, j: (jnp.maximum(i * hb - 1, 0), 0)),
                  pl.BlockSpec((1, d), lambda i, j: (0, 0)),
                  pl.BlockSpec((d, tf), lambda i, j: (0, j)),
                  pl.BlockSpec((d, tf), lambda i, j: (0, nf + j)),
                  pl.BlockSpec((3, tf), lambda i, j: (0, j)),
                  pl.BlockSpec((3, tf), lambda i, j: (0, nf + j)),
                  pl.BlockSpec((1, tf), lambda i, j: (0, j)),
                  pl.BlockSpec((1, tf), lambda i, j: (0, nf + j)),
                  pl.BlockSpec((tf, d), lambda i, j: (j, 0))],
        out_specs=pl.BlockSpec((tm, d), lambda i, j: (i, 0)),
        out_shape=jax.ShapeDtypeStruct((t, d), F32),
        scratch_shapes=[pltpu.VMEM((tm + MLP_HALO, d), BF16), pltpu.VMEM((tm, d), F32)],
        compiler_params=_cparams(("parallel", "arbitrary")),
        name="conv_glu_mlp",
    )(x, x, g.reshape(1, d), w_up, w_up, conv_w, conv_w, conv_b, conv_b, w_down)


def _rmsnorm_kernel(x_ref, g_ref, y_ref):
    y_ref[...] = _rms(x_ref[...], g_ref[...])


def _rmsnorm(x, g):
    t, d = x.shape
    tm = _pick_tile(t, (1024, 512, 256, 128))
    return pl.pallas_call(
        _rmsnorm_kernel,
        grid=(t // tm,),
        in_specs=[pl.BlockSpec((tm, d), lambda i: (i, 0)), pl.BlockSpec((1, d), lambda i: (0, 0))],
        out_specs=pl.BlockSpec((tm, d), lambda i: (i, 0)),
        out_shape=jax.ShapeDtypeStruct((t, d), F32),
        compiler_params=_cparams(("parallel",)),
        name="final_rmsnorm",
    )(x, g.reshape(1, d))


def _swap_halves(q_pair):
    return pltpu.roll(q_pair.astype(F32), HEAD_DIM, 1).astype(BF16)


def _lane_half(shape):
    return lax.broadcasted_iota(jnp.int32, shape, 1) // HEAD_DIM


def _rows_to_heads(o_t, n_heads, tq):
    stacked = jnp.concatenate([o_t[:, r * tq:(r + 1) * tq] for r in range(n_heads)], axis=0)
    return stacked.T


def _swa_kernel(q_ref, kp_ref, kc_ref, vp_ref, vc_ref, bias_ref, sink_ref, o_ref):
    tq = q_ref.shape[0]
    rows = SWA_GROUP * tq
    k_ext = jnp.concatenate([kp_ref[...], kc_ref[...]], axis=0)
    v_ext = jnp.concatenate([vp_ref[...], vc_ref[...]], axis=1)
    half = _lane_half((tq, LANES))
    outs = []
    for g in range(SWA_KV_HEADS):
        parts = []
        for r in range(SWA_GROUP):
            hd = g * SWA_GROUP + r
            q_pair = q_ref[:, (hd // 2) * LANES:(hd // 2 + 1) * LANES]
            src = q_pair if hd % 2 == g else _swap_halves(q_pair)
            parts.append(jnp.where(half == g, src, jnp.zeros_like(src)))
        qp = jnp.concatenate(parts, axis=0)
        s = lax.dot_general(k_ext, qp, (((1,), (1,)), ((), ())), preferred_element_type=F32)
        s = s * (HEAD_DIM ** -0.5) + bias_ref[0, g]
        sink = sink_ref[g]
        m = jnp.maximum(jnp.max(s, axis=0, keepdims=True), sink)
        e = jnp.exp(s - m)
        l = jnp.sum(e, axis=0, keepdims=True) + jnp.exp(sink - m)
        o_t = jnp.dot(v_ext[g * HEAD_DIM:(g + 1) * HEAD_DIM], e.astype(BF16), preferred_element_type=F32)
        outs.append(_rows_to_heads(o_t / l, SWA_GROUP, tq))
    o_ref[...] = jnp.concatenate(outs, axis=1).astype(o_ref.dtype)


def _swa_bias():
    tq = SWA_TQ
    slopes = _alibi_slopes().reshape(SWA_KV_HEADS, SWA_GROUP)
    x = np.arange(2 * tq)[:, None] - tq
    q = np.arange(tq)[None, :]
    dist = q - x
    valid = (dist >= 0) & (dist < SWA_WINDOW)
    out = np.empty((2, SWA_KV_HEADS, 2 * tq, SWA_GROUP * tq), np.float32)
    for var in range(2):
        v = valid & ((x >= 0) | (var == 1))
        for g in range(SWA_KV_HEADS):
            for r in range(SWA_GROUP):
                out[var, g, :, r * tq:(r + 1) * tq] = np.where(v, -slopes[g, r] * dist, MASKED)
    return out


def _swa_attention(qkv, v_t, sinks):
    t = qkv.shape[0]
    tq = SWA_TQ
    kcol = Q_DIM // LANES
    bias = jnp.asarray(_swa_bias())
    sink_rows = jnp.repeat(sinks.astype(F32).reshape(SWA_KV_HEADS, 1, SWA_GROUP), tq, axis=2)
    prev = lambda i: jnp.maximum(i - 1, 0)
    return pl.pallas_call(
        _swa_kernel,
        grid=(t // tq,),
        in_specs=[pl.BlockSpec((tq, Q_DIM), lambda i: (i, 0)),
                  pl.BlockSpec((tq, LANES), lambda i: (prev(i), kcol)),
                  pl.BlockSpec((tq, LANES), lambda i: (i, kcol)),
                  pl.BlockSpec((LANES, tq), lambda i: (0, prev(i))),
                  pl.BlockSpec((LANES, tq), lambda i: (0, i)),
                  pl.BlockSpec((1,) + bias.shape[1:], lambda i: (jnp.minimum(i, 1), 0, 0, 0)),
                  pl.BlockSpec(sink_rows.shape, lambda i: (0, 0, 0))],
        out_specs=pl.BlockSpec((tq, Q_DIM), lambda i: (i, 0)),
        out_shape=jax.ShapeDtypeStruct((t, Q_DIM), BF16),
        compiler_params=_cparams(("parallel",)),
        name="swa_attention",
    )(qkv, qkv, qkv, v_t, v_t, bias, sink_rows)


def _swa_layer(x, norm_g, w_in, sinks, w_out):
    qkv = _norm_proj(x, norm_g, w_in.astype(BF16), BF16)
    v_t = qkv[:, Q_DIM + SWA_KV_HEADS * HEAD_DIM:].T
    o = _swa_attention(qkv, v_t, sinks)
    return _out_proj(x, o, w_out.astype(BF16))


FOX_PARTS = 3


def _split_bf16(v):
    parts, rest = [], v
    for _ in range(FOX_PARTS):
        p = rest.astype(BF16)
        parts.append(p)
        rest = rest - p.astype(F32)
    return parts


def _fox_prep_kernel(f_ref, b_ref, aug_ref, cref_ref, carry_ref):
    tm = f_ref.shape[0]

    @pl.when(pl.program_id(0) == 0)
    def _():
        carry_ref[...] = jnp.zeros_like(carry_ref)

    logf = jax.nn.log_sigmoid(f_ref[...] + b_ref[...])
    row = lax.broadcasted_iota(jnp.int32, (tm, tm), 0)
    col = lax.broadcasted_iota(jnp.int32, (tm, tm), 1)
    tri = jnp.where(row >= col, 1.0, 0.0).astype(BF16)
    local = sum(jnp.dot(tri, p, preferred_element_type=F32) for p in _split_bf16(logf))
    cum = local + carry_ref[...]
    carry_ref[...] = cum[tm - 1:tm, :]
    first = cum[0:1, :]
    cref_ref[0] = jnp.broadcast_to(first, cref_ref.shape[1:])
    hi, mid, lo = _split_bf16(first - cum)
    lane = lax.broadcasted_iota(jnp.int32, (tm, LANES), 1)
    zero = jnp.zeros_like(hi)
    aug_ref[...] = jnp.where(lane < N_HEADS, hi, jnp.where(lane < 2 * N_HEADS, mid,
                                                            jnp.where(lane < 3 * N_HEADS, lo, zero)))


def _fox_prep(f_logit3, f_bias3):
    t = f_logit3.shape[0]
    nt = t // FOX_T
    return pl.pallas_call(
        _fox_prep_kernel,
        grid=(nt,),
        in_specs=[pl.BlockSpec((FOX_T, LANES), lambda i: (i, 0)), pl.BlockSpec((1, LANES), lambda i: (0, 0))],
        out_specs=[pl.BlockSpec((FOX_T, LANES), lambda i: (i, 0)), pl.BlockSpec((1, 8, LANES), lambda i: (i, 0, 0))],
        out_shape=[jax.ShapeDtypeStruct((t, LANES), BF16), jax.ShapeDtypeStruct((nt, 8, LANES), F32)],
        scratch_shapes=[pltpu.VMEM((1, LANES), F32)],
        compiler_params=_cparams(("arbitrary",)),
        name="fox_prep",
    )(f_logit3, f_bias3)


def _fox_kernel(cref_ref, q_ref, k_ref, aug_ref, vt_ref, o_ref, m_ref, l_ref, acc_ref, qt_ref, s_ref):
    pair, qi = pl.program_id(0), pl.program_id(1)
    tq = q_ref.shape[0]
    tk = FOX_T
    lane = lax.broadcasted_iota(jnp.int32, (tq, LANES), 1)
    q2 = q_ref[...] * jnp.asarray(HEAD_DIM ** -0.5, BF16)
    for hh in range(2):
        head = 2 * pair + hh
        pick = (lane % N_HEADS == head) & (lane < FOX_PARTS * N_HEADS)
        q_ext = jnp.concatenate([jnp.where(lane // HEAD_DIM == hh, q2, jnp.zeros_like(q2)).astype(F32),
                                 jnp.where(pick, 1.0, 0.0)], axis=1)
        qt_ref[hh] = q_ext.T.astype(BF16)
    m_ref[...] = jnp.full_like(m_ref, M_INIT)
    l_ref[...] = jnp.zeros_like(l_ref)
    acc_ref[...] = jnp.zeros_like(acc_ref)
    ratio = tq // tk

    def scores(kj, hh):
        start = pl.multiple_of(kj * tk, tk)
        k_ext = jnp.concatenate([k_ref[pl.ds(start, tk), :], aug_ref[pl.ds(start, tk), :]], axis=1)
        s_ref[hh] = jnp.dot(k_ext, qt_ref[hh], preferred_element_type=F32)

    def consume(kj, hh, key_offset):
        head = 2 * pair + hh
        s = s_ref[hh]
        if key_offset is not None:
            key = lax.broadcasted_iota(jnp.int32, (tk, tq), 0) + key_offset
            qry = lax.broadcasted_iota(jnp.int32, (tk, tq), 1)
            s = jnp.where(key <= qry, s, MASKED)
        off = cref_ref[qi * ratio, head] - cref_ref[kj, head]
        m_old = m_ref[hh]
        m_new = jnp.maximum(m_old, jnp.max(s, axis=0, keepdims=True) + off)
        e = jnp.exp(s - (m_new - off))
        alpha = jnp.exp(m_old - m_new)
        m_ref[hh] = m_new
        l_ref[hh] = alpha * l_ref[hh] + jnp.sum(e, axis=0, keepdims=True)
        v_t = vt_ref[kj, hh * HEAD_DIM:(hh + 1) * HEAD_DIM, :]
        acc_ref[hh] = alpha * acc_ref[hh] + jnp.dot(v_t, e.astype(BF16), preferred_element_type=F32)

    n_full = qi * ratio
    scores(0, 0)

    def body(j, carry):
        for d in range(ratio):
            kj = j * ratio + d
            scores(kj, 1)
            consume(kj, 0, None)
            scores(kj + 1, 0)
            consume(kj, 1, None)
        return carry

    lax.fori_loop(0, qi, body, 0)
    for d in range(ratio):
        kj = n_full + d
        scores(kj, 1)
        consume(kj, 0, d * tk)
        if d + 1 < ratio:
            scores(kj + 1, 0)
        consume(kj, 1, d * tk)
    o_t = jnp.concatenate([acc_ref[hh] / l_ref[hh] for hh in range(2)], axis=0)
    o_ref[...] = o_t.T.astype(o_ref.dtype)


def _fox_attention(qkv, aug, v_t3, cref):
    t = qkv.shape[0]
    tq = FOX_TQ
    nk = t // FOX_T
    kcol = Q_DIM // LANES
    return pl.pallas_call(
        _fox_kernel,
        grid=(N_HEADS // 2, t // tq),
        in_specs=[pl.BlockSpec(memory_space=pltpu.SMEM),
                  pl.BlockSpec((tq, LANES), lambda p, i: (i, p)),
                  pl.BlockSpec((t, LANES), lambda p, i: (0, kcol + p)),
                  pl.BlockSpec((t, LANES), lambda p, i: (0, 0)),
                  pl.BlockSpec((nk, LANES, FOX_T), lambda p, i: (0, p, 0))],
        out_specs=pl.BlockSpec((tq, LANES), lambda p, i: (i, p)),
        out_shape=jax.ShapeDtypeStruct((t, Q_DIM), BF16),
        scratch_shapes=[pltpu.VMEM((2, 1, tq), F32), pltpu.VMEM((2, 1, tq), F32),
                        pltpu.VMEM((2, HEAD_DIM, tq), F32), pltpu.VMEM((2, 2 * LANES, tq), BF16), pltpu.VMEM((2, FOX_T, tq), F32)],
        compiler_params=_cparams(("parallel", "arbitrary")),
        name="fox_attention",
    )(cref, qkv, qkv, aug, v_t3)


def _fox_layer(x, norm_g, w_in, f_bias, w_out):
    t = x.shape[0]
    qkv = _norm_proj(x, norm_g, w_in[:, :3 * Q_DIM].astype(BF16), BF16)
    w_f = w_in[:, 3 * Q_DIM:]
    pad = jnp.zeros((D_MODEL, LANES - FOX_PARTS * N_HEADS), w_f.dtype)
    w_f3 = jnp.concatenate([w_f] * FOX_PARTS + [pad], axis=1).astype(BF16)
    b3 = jnp.concatenate([f_bias.astype(F32)] * FOX_PARTS + [jnp.zeros((LANES - FOX_PARTS * N_HEADS,), F32)])
    f_logit3 = _norm_proj(x, norm_g, w_f3, F32)
    aug, cref = _fox_prep(f_logit3, b3.reshape(1, LANES))
    v_t3 = qkv[:, 2 * Q_DIM:].reshape(t // FOX_T, FOX_T, Q_DIM).transpose(0, 2, 1)
    o = _fox_attention(qkv, aug, v_t3, cref[:, 0, :N_HEADS])
    return _out_proj(x, o, w_out.astype(BF16))


def _nsa_compress_kernel(a_ref, pos_ref, w1_ref, w2_ref, o_ref):
    tc = a_ref.shape[2]
    half = CMP_STRIDE * HEAD_DIM
    pos = pos_ref[0]
    out = jnp.zeros(o_ref.shape[1:], F32)
    for gg in range(2):
        a = a_ref[0, gg]
        top = jnp.dot((a + pos[:, :half]).astype(BF16), w1_ref[0, :half], preferred_element_type=F32)
        bot = jnp.dot((a + pos[:, half:]).astype(BF16), w1_ref[0, half:], preferred_element_type=F32)
        hid = top + pltpu.roll(bot, tc - 1, 0)
        out = out + jnp.dot(jax.nn.gelu(hid).astype(BF16), w2_ref[0, gg], preferred_element_type=F32)
    o_ref[0] = out


def _nsa_compress(a, pos, w1, w2):
    _, g, tc, width = a.shape
    return pl.pallas_call(
        _nsa_compress_kernel,
        grid=(2, g // 2),
        in_specs=[pl.BlockSpec((1, 2, tc, width), lambda kv, gp: (kv, gp, 0, 0)),
                  pl.BlockSpec((1, 1, 2 * width), lambda kv, gp: (kv, 0, 0)),
                  pl.BlockSpec((1, 2 * width, CMP_HIDDEN), lambda kv, gp: (kv, 0, 0)),
                  pl.BlockSpec((1, 2, CMP_HIDDEN, LANES), lambda kv, gp: (kv, 0, 0, 0))],
        out_specs=pl.BlockSpec((1, tc, LANES), lambda kv, gp: (kv, 0, gp)),
        out_shape=jax.ShapeDtypeStruct((2, tc, g * HEAD_DIM), F32),
        compiler_params=_cparams(("parallel", "parallel")),
        name="nsa_compress",
    )(a, pos, w1, w2)


NSA_WIN_TILES = NSA_WINDOW // NSA_TQ + 1


def _nsa_kernel(slopes_ref, q_ref, ks_ref, *refs):
    kw_refs, refs = refs[:NSA_WIN_TILES], refs[NSA_WIN_TILES:]
    vst_ref, refs = refs[0], refs[1:]
    vw_refs, refs = refs[:NSA_WIN_TILES], refs[NSA_WIN_TILES:]
    (kcmp_ref, vcmpt_ref, ovt_ref, gates_ref, ksel_ext_ref, kwin_ext_ref, qext_ref,
     o_ref, selt_ref, m_ref, l_ref, acc_ref, qt_ref, s_ref, active_ref, list_ref) = refs
    g, qi = pl.program_id(0), pl.program_id(1)
    tq = q_ref.shape[0]
    tc = kcmp_ref.shape[0]
    n_sel = ovt_ref.shape[0]
    rr = NSA_GROUP
    t0 = qi * tq
    hg = g % 2
    slope = [slopes_ref[g * rr + r] for r in range(rr)]

    half = _lane_half((tq, LANES))
    parts = []
    for r in range(rr):
        q_pair = q_ref[:, (r // 2) * LANES:(r // 2 + 1) * LANES]
        src = jnp.where(hg == r % 2, q_pair, _swap_halves(q_pair))
        parts.append(jnp.where(half == hg, src, jnp.zeros_like(src)) * jnp.asarray(HEAD_DIM ** -0.5, BF16))
    qp = jnp.concatenate(parts, axis=0)
    qt_ref[0:LANES, :] = qp.astype(F32).T.astype(BF16)

    @pl.when(qi == 0)
    def _():
        qt_ref[LANES:, :] = qext_ref[0]

    def lanes_of(s, r):
        return s[:, r * tq:(r + 1) * tq]

    s_all = jnp.dot(kcmp_ref[...], qt_ref[0:LANES, :], preferred_element_type=F32)
    n_id = lax.broadcasted_iota(jnp.int32, (tc, tq), 0)
    q_id = lax.broadcasted_iota(jnp.int32, (tc, tq), 1)
    neg_d = (CMP_STRIDE * n_id + (CMP_BLOCK - 1) - q_id - t0).astype(F32)
    valid = neg_d <= 0.0
    p_all, p_sum = [], jnp.zeros((tc, tq), F32)
    for r in range(rr):
        s = jnp.where(valid, lanes_of(s_all, r) + slope[r] * neg_d, MASKED)
        m = jnp.maximum(jnp.max(s, axis=0, keepdims=True), M_INIT)
        e = jnp.exp(s - m)
        l = jnp.sum(e, axis=0, keepdims=True)
        p = e * jnp.where(l > 0.0, 1.0 / l, 0.0)
        p_sum = p_sum + p
        p_all.append(p.astype(BF16))
    o_cmp_all = jnp.dot(vcmpt_ref[...], jnp.concatenate(p_all, axis=1), preferred_element_type=F32)
    o_cmp = [lanes_of(o_cmp_all, r) for r in range(rr)]

    imp = jnp.dot(ovt_ref[...], p_sum.astype(BF16), preferred_element_type=F32)
    blk = lax.broadcasted_iota(jnp.int32, (n_sel, tq), 0)
    cur = (t0 + lax.broadcasted_iota(jnp.int32, (n_sel, tq), 1)) // SEL_BLOCK
    blk_f = blk.astype(F32)
    forced = jnp.where(blk == 0, 1.0, jnp.where(blk == cur, 1.0, jnp.where(blk == cur - 1, 1.0, 0.0)))
    score = jnp.where(forced > 0.0, -3e38, jnp.where(blk <= cur, imp, -1.0))

    def pick(_, carry):
        sc, chosen = carry
        best = jnp.max(sc, axis=0, keepdims=True)
        first = jnp.min(jnp.where(sc == best, blk_f, float(n_sel)), axis=0, keepdims=True)
        hit = blk_f == first
        return jnp.where(hit, -3e38, sc), jnp.where(hit, 1.0, chosen)

    _, chosen = lax.fori_loop(0, N_SELECT - 3, pick, (score, forced))
    selt_ref[...] = jnp.where(blk <= cur, chosen, 0.0)

    m_ref[...] = jnp.full_like(m_ref, M_INIT)
    l_ref[...] = jnp.zeros_like(l_ref)
    acc_ref[...] = jnp.zeros_like(acc_ref)
    blocks_per_chunk = NSA_TK // SEL_BLOCK

    for c in range(n_sel // blocks_per_chunk):
        active_ref[c] = jnp.max(selt_ref[c * blocks_per_chunk:(c + 1) * blocks_per_chunk, :])

    def compact(c, n):
        list_ref[n] = c
        return n + (active_ref[c] > 0.0).astype(jnp.int32)

    n_active = lax.fori_loop(0, t0 // NSA_TK + 1, compact, 0)

    def sel_scores(c, buf):
        start = pl.multiple_of(c * NSA_TK, NSA_TK)
        rows = selt_ref[pl.ds(pl.multiple_of(c * blocks_per_chunk, blocks_per_chunk), blocks_per_chunk), :]
        mask_rows = jnp.concatenate([jnp.where(rows > 0.5, 0.0, MASKED), jnp.zeros_like(rows)], axis=0)
        first = LANES + buf * NSA_EXT_MASK_ROWS
        qt_ref[first:first + NSA_EXT_MASK_ROWS, :] = jnp.concatenate([mask_rows] * rr, axis=1).astype(BF16)
        k_full = jnp.concatenate([ks_ref[pl.ds(start, NSA_TK), :], ksel_ext_ref[buf]], axis=1)
        s_ref[buf] = jnp.dot(k_full, qt_ref[...], preferred_element_type=F32)

    def sel_consume(c, buf, diagonal):
        shift = (c * NSA_TK - t0).astype(F32)
        if diagonal:
            key = lax.broadcasted_iota(jnp.int32, (NSA_TK, tq), 0) + c * NSA_TK
            visible = key <= lax.broadcasted_iota(jnp.int32, (NSA_TK, tq), 1) + t0
        e_all, alpha_all = [], []
        for r in range(rr):
            off = slope[r] * shift
            s = s_ref[buf][:, r * tq:(r + 1) * tq]
            if diagonal:
                s = jnp.where(visible, s, MASKED)
            m_old = m_ref[r]
            m_new = jnp.maximum(m_old, jnp.max(s, axis=0, keepdims=True) + off)
            e = jnp.exp(s - (m_new - off))
            alpha = jnp.exp(m_old - m_new)
            m_ref[r] = m_new
            l_ref[r] = alpha * l_ref[r] + jnp.sum(e, axis=0, keepdims=True)
            e_all.append(e.astype(BF16))
            alpha_all.append(alpha)
        pv = jnp.dot(vst_ref[c], jnp.concatenate(e_all, axis=1), preferred_element_type=F32)
        acc_ref[...] = jnp.concatenate(alpha_all, axis=1) * acc_ref[...] + pv

    n_before = n_active - 1
    sel_scores(list_ref[0], 0)

    def sel_pair(j, carry):
        c0, c1, c2 = list_ref[2 * j], list_ref[2 * j + 1], list_ref[2 * j + 2]
        sel_scores(c1, 1)
        sel_consume(c0, 0, False)
        sel_scores(c2, 0)
        sel_consume(c1, 1, False)
        return carry

    lax.fori_loop(0, n_before // 2, sel_pair, 0)

    @pl.when(n_before % 2 == 1)
    def _():
        c0, c_last = list_ref[n_before - 1], list_ref[n_before]
        sel_scores(c_last, 1)
        sel_consume(c0, 0, False)
        sel_consume(c_last, 1, True)

    @pl.when(n_before % 2 == 0)
    def _():
        sel_consume(list_ref[n_before], 0, True)

    xw = lax.broadcasted_iota(jnp.int32, (tq, tq), 0)
    qw = lax.broadcasted_iota(jnp.int32, (tq, tq), 1)
    s_win, off_win = [], []
    for idx in range(NSA_WIN_TILES):
        back = NSA_WIN_TILES - 1 - idx
        k_full = jnp.concatenate([kw_refs[idx][...], kwin_ext_ref[...]], axis=1)
        s_t = jnp.dot(k_full, qt_ref[...], preferred_element_type=F32)
        per_head = [lanes_of(s_t, r) for r in range(rr)]
        if back == NSA_WIN_TILES - 1:
            per_head = [jnp.where(xw > qw, s, MASKED) for s in per_head]
        if back == 0:
            per_head = [jnp.where(xw <= qw, s, MASKED) for s in per_head]
        s_win.append(per_head)
        off_win.append([jnp.where(qi >= back, -slope[r] * float(back * tq), MASKED) for r in range(rr)])
    e_win, l_win = [], []
    for r in range(rr):
        m = functools.reduce(jnp.maximum, [jnp.max(s_win[idx][r], axis=0, keepdims=True) + off_win[idx][r]
                                           for idx in range(NSA_WIN_TILES)])
        e_tiles = [jnp.exp(s_win[idx][r] - (m - off_win[idx][r])) for idx in range(NSA_WIN_TILES)]
        l_win.append(functools.reduce(jnp.add, [jnp.sum(e, axis=0, keepdims=True) for e in e_tiles]))
        e_win.append(jnp.concatenate([e.astype(BF16) for e in e_tiles], axis=0))
    v_win = jnp.concatenate([vw_refs[idx][...] for idx in range(NSA_WIN_TILES)], axis=1)
    o_win_all = jnp.dot(v_win, jnp.concatenate(e_win, axis=1), preferred_element_type=F32)
    o_win = [lanes_of(o_win_all, r) / l_win[r] for r in range(rr)]

    gate = jax.nn.sigmoid(gates_ref[0])
    merged = []
    for r in range(rr):
        o_sel = lanes_of(acc_ref[...], r) / l_ref[r]
        merged.append(gate[3 * r:3 * r + 1] * o_cmp[r] + gate[3 * r + 1:3 * r + 2] * o_sel
                      + gate[3 * r + 2:3 * r + 3] * o_win[r])
    o_ref[...] = jnp.concatenate(merged, axis=0).T.astype(o_ref.dtype)


def _nsa_overlap_t(n_sel, tc):
    cmp_start = np.arange(tc - 1) * CMP_STRIDE
    sel_start = np.arange(n_sel) * SEL_BLOCK
    ov = np.clip(np.minimum(cmp_start[None, :] + CMP_BLOCK, sel_start[:, None] + SEL_BLOCK)
                 - np.maximum(cmp_start[None, :], sel_start[:, None]), 0, None) / CMP_BLOCK
    return np.concatenate([ov, np.zeros((n_sel, 1))], axis=1).astype(np.float32)


NSA_EXT_MASK_ROWS = 16
NSA_EXT_SEL = 32
NSA_EXT_WIN = 48
SLOPE_PIECES = 3


def _bf16_pieces(v):
    out, rest = [], np.asarray(v, np.float32)
    for _ in range(SLOPE_PIECES):
        p = rest.astype(BF16).astype(np.float32)
        out.append(p)
        rest = rest - p
    return out


def _nsa_extensions(tq):
    ksel = np.zeros((2, NSA_TK, LANES), np.float32)
    x = np.arange(NSA_TK)
    for buf in range(2):
        ksel[buf, x, buf * NSA_EXT_MASK_ROWS + x // SEL_BLOCK] = 1.0
    for j in range(SLOPE_PIECES):
        ksel[:, :, NSA_EXT_SEL + j] = 256 * (x // 256)
        ksel[:, :, NSA_EXT_SEL + SLOPE_PIECES + j] = x % 256
    kwin = np.zeros((tq, LANES), np.float32)
    for j in range(SLOPE_PIECES):
        kwin[:, NSA_EXT_WIN + j] = np.arange(tq)
    assert tq <= 256, "a window key offset must be exact in bf16"
    pieces = _bf16_pieces(_alibi_slopes())
    qext = np.zeros((NSA_KV_HEADS, LANES, NSA_GROUP * tq), np.float32)
    for g in range(NSA_KV_HEADS):
        for r in range(NSA_GROUP):
            cols = slice(r * tq, (r + 1) * tq)
            for j in range(SLOPE_PIECES):
                p = pieces[j][g * NSA_GROUP + r]
                qext[g, NSA_EXT_SEL + j, cols] = p
                qext[g, NSA_EXT_SEL + SLOPE_PIECES + j, cols] = p
                qext[g, NSA_EXT_WIN + j, cols] = p
    return jnp.asarray(ksel, BF16), jnp.asarray(kwin, BF16), jnp.asarray(qext, BF16)


def _nsa_attention(main, vs_t3, vw_t, k_cmp, v_cmp_t, gates_t):
    t = main.shape[0]
    tq = NSA_TQ
    tc = k_cmp.shape[0]
    n_sel = t // SEL_BLOCK
    nc = t // NSA_TK
    ks_col = Q_DIM // LANES
    kw_col = ks_col + NSA_KV_DIM // LANES
    ov_t = jnp.asarray(_nsa_overlap_t(n_sel, tc), BF16)
    slopes = jnp.asarray(_alibi_slopes())
    ksel_ext, kwin_ext, qext = _nsa_extensions(tq)

    def kw_spec(back):
        return pl.BlockSpec((tq, LANES), lambda g, i: (jnp.maximum(i - back, 0), kw_col + g // 2))

    def vw_spec(back):
        return pl.BlockSpec((HEAD_DIM, tq), lambda g, i: (g, jnp.maximum(i - back, 0)))

    backs = list(range(NSA_WIN_TILES - 1, -1, -1))
    return pl.pallas_call(
        _nsa_kernel,
        grid=(NSA_KV_HEADS, t // tq),
        in_specs=[pl.BlockSpec(memory_space=pltpu.SMEM),
                  pl.BlockSpec((tq, NSA_GROUP * HEAD_DIM), lambda g, i: (i, g)),
                  pl.BlockSpec((t, LANES), lambda g, i: (0, ks_col + g // 2))]
                 + [kw_spec(b) for b in backs]
                 + [pl.BlockSpec((nc, HEAD_DIM, NSA_TK), lambda g, i: (0, g, 0))]
                 + [vw_spec(b) for b in backs]
                 + [pl.BlockSpec((tc, LANES), lambda g, i: (0, g // 2)),
                    pl.BlockSpec((HEAD_DIM, tc), lambda g, i: (g, 0)),
                    pl.BlockSpec((n_sel, tc), lambda g, i: (0, 0)),
                    pl.BlockSpec((1, 16, tq), lambda g, i: (g, 0, i)),
                    pl.BlockSpec(ksel_ext.shape, lambda g, i: (0, 0, 0)),
                    pl.BlockSpec(kwin_ext.shape, lambda g, i: (0, 0)),
                    pl.BlockSpec((1,) + qext.shape[1:], lambda g, i: (g, 0, 0))],
        out_specs=pl.BlockSpec((tq, NSA_GROUP * HEAD_DIM), lambda g, i: (i, g)),
        out_shape=jax.ShapeDtypeStruct((t, Q_DIM), BF16),
        scratch_shapes=[pltpu.VMEM((n_sel, tq), F32), pltpu.VMEM((NSA_GROUP, 1, tq), F32),
                        pltpu.VMEM((NSA_GROUP, 1, tq), F32), pltpu.VMEM((HEAD_DIM, NSA_GROUP * tq), F32),
                        pltpu.VMEM((2 * LANES, NSA_GROUP * tq), BF16),
                        pltpu.VMEM((2, NSA_TK, NSA_GROUP * tq), F32),
                        pltpu.SMEM((nc,), F32), pltpu.SMEM((nc,), jnp.int32)],
        compiler_params=_cparams(("parallel", "arbitrary")),
        name="nsa_attention",
    )(slopes, main, main, *([main] * NSA_WIN_TILES), vs_t3, *([vw_t] * NSA_WIN_TILES), k_cmp, v_cmp_t, ov_t, gates_t,
      ksel_ext, kwin_ext, qext)


def _nsa_layer(x, norm_g, w_in, cmp_pos, cmp_w1, cmp_w2, w_out):
    t = x.shape[0]
    kv = NSA_KV_DIM
    col = lambda i: w_in[:, Q_DIM + i * kv:Q_DIM + (i + 1) * kv]
    w_main = jnp.concatenate([w_in[:, :Q_DIM], col(2), col(4), col(3), col(5)], axis=1).astype(BF16)
    w_cmp = jnp.concatenate([col(0), col(1)], axis=1).astype(BF16)
    n_gate = 3 * N_HEADS
    w_gate = jnp.concatenate([w_in[:, Q_DIM + 6 * kv:], jnp.zeros((D_MODEL, LANES - n_gate), w_in.dtype)],
                             axis=1).astype(BF16)
    main = _norm_proj(x, norm_g, w_main, BF16)
    kcvc = _norm_proj(x, norm_g, w_cmp, F32)
    gate_logit = _norm_proj(x, norm_g, w_gate, F32)

    tc = t // CMP_STRIDE
    a = kcvc.reshape(tc, CMP_STRIDE, 2, NSA_KV_HEADS, HEAD_DIM).transpose(2, 3, 0, 1, 4)
    a = a.reshape(2, NSA_KV_HEADS, tc, CMP_STRIDE * HEAD_DIM)
    zeros = jnp.zeros_like(cmp_w2)
    w2 = jnp.stack([jnp.concatenate([cmp_w2, zeros], axis=2), jnp.concatenate([zeros, cmp_w2], axis=2)], axis=1)
    cmp = _nsa_compress(a, cmp_pos.reshape(2, 1, CMP_BLOCK * HEAD_DIM), cmp_w1.astype(BF16), w2.astype(BF16))
    k_cmp = cmp[0].astype(BF16)
    v_cmp_t = cmp[1].T.astype(BF16)

    off_vs = Q_DIM + 2 * kv
    vs_t3 = main[:, off_vs:off_vs + kv].reshape(t // NSA_TK, NSA_TK, kv).transpose(0, 2, 1)
    vw_t = main[:, off_vs + kv:].T
    gates_t = gate_logit[:, :n_gate].reshape(t, NSA_KV_HEADS, 3 * NSA_GROUP).transpose(1, 2, 0)
    gates_t = jnp.pad(gates_t, ((0, 0), (0, 16 - 3 * NSA_GROUP), (0, 0)))
    o = _nsa_attention(main, vs_t3, vw_t, k_cmp, v_cmp_t, gates_t)
    return _out_proj(x, o, w_out.astype(BF16))


def kernel(x, attn_norm, mlp_norm, final_norm, nsa_w_in, nsa_cmp_pos, nsa_cmp_w1, nsa_cmp_w2, nsa_w_out,
           swa_w_in, swa_sinks, swa_w_out, fox_w_in, fox_f_bias, fox_w_out,
           mlp_w_up, mlp_conv_w, mlp_conv_b, mlp_w_down):
    assert x.shape[0] == 1, "the trunk is written for batch 1"
    h = x[0]
    for i in range(DEPTH):
        kind, j = i % N_MIXERS, i // N_MIXERS
        if kind == 0:
            h = _nsa_layer(h, attn_norm[i], nsa_w_in[j], nsa_cmp_pos[j], nsa_cmp_w1[j], nsa_cmp_w2[j], nsa_w_out[j])
        elif kind == 1:
            h = _swa_layer(h, attn_norm[i], swa_w_in[j], swa_sinks[j], swa_w_out[j])
        else:
            h = _fox_layer(h, attn_norm[i], fox_w_in[j], fox_f_bias[j], fox_w_out[j])
        h = _mlp(h, mlp_norm[i], mlp_w_up[i].astype(BF16), mlp_conv_w[i], mlp_conv_b[i].reshape(1, -1),
                 mlp_w_down[i].astype(BF16))
    return _rmsnorm(h, final_norm)[None]
```

```python
import functools

import numpy as np
import jax
import jax.numpy as jnp
from jax import lax
from jax.experimental import pallas as pl
from jax.experimental.pallas import tpu as pltpu

F32 = jnp.float32
BF16 = jnp.bfloat16

D_MODEL = 1024
DEPTH = 4
N_MIXERS = 3
HEAD_DIM = 64
N_HEADS = 16
Q_DIM = N_HEADS * HEAD_DIM
ALIBI_MAX = 8.0
NORM_EPS = 1e-6
FORCE_SCORE = 1e4
MASKED = -2e30
M_INIT = -1e30
TAKEN = -3e38

NSA_KV_HEADS = 4
NSA_GROUP = 4
NSA_KV_DIM = NSA_KV_HEADS * HEAD_DIM
CMP_BLOCK = 32
CMP_STRIDE = 16
CMP_HIDDEN = 256
SEL_BLOCK = 64
N_SELECT = 16
NSA_WINDOW = 512
NSA_TQ = 256
NSA_TK = 512

SWA_KV_HEADS = 2
SWA_GROUP = 8
SWA_WINDOW = 128
SWA_TQ = 128

FOX_TQ = 512
FOX_T = 256

D_FF = 2816
LANES = 128
VMEM_LIMIT = 56 * 1024 * 1024


def _cparams(semantics, vmem=VMEM_LIMIT):
    return pltpu.CompilerParams(dimension_semantics=semantics, vmem_limit_bytes=vmem)


def _alibi_slopes():
    return np.asarray(2.0 ** (-ALIBI_MAX * np.arange(1, N_HEADS + 1) / N_HEADS), dtype=np.float32)


def _rms(x, g):
    ms = jnp.mean(x * x, axis=-1, keepdims=True)
    return x * lax.rsqrt(ms + NORM_EPS) * g


def _norm_proj_kernel(x_ref, g_ref, w_ref, o_ref, h_ref):
    @pl.when(pl.program_id(1) == 0)
    def _():
        h_ref[...] = _rms(x_ref[...], g_ref[...]).astype(BF16)

    o_ref[...] = jnp.dot(h_ref[...], w_ref[...], preferred_element_type=F32).astype(o_ref.dtype)


def _pick_tile(n, candidates):
    for c in candidates:
        if n % c == 0:
            return c
    raise ValueError(f"no tile for {n}")


def _norm_proj(x, g, w, out_dtype):
    t, d = x.shape
    n = w.shape[1]
    tm = _pick_tile(t, (1024, 512, 256, 128))
    tn = _pick_tile(n, (1024, 768, 512, 256, 128))
    return pl.pallas_call(
        _norm_proj_kernel,
        grid=(t // tm, n // tn),
        in_specs=[pl.BlockSpec((tm, d), lambda i, j: (i, 0)),
                  pl.BlockSpec((1, d), lambda i, j: (0, 0)),
                  pl.BlockSpec((d, tn), lambda i, j: (0, j))],
        out_specs=pl.BlockSpec((tm, tn), lambda i, j: (i, j)),
        out_shape=jax.ShapeDtypeStruct((t, n), out_dtype),
        scratch_shapes=[pltpu.VMEM((tm, d), BF16)],
        compiler_params=_cparams(("parallel", "arbitrary")),
        name="norm_proj",
    )(x, g.reshape(1, d), w)


def _out_proj_kernel(x_ref, o_ref, w_ref, y_ref):
    y_ref[...] = x_ref[...] + jnp.dot(o_ref[...], w_ref[...], preferred_element_type=F32)


def _out_proj(x, o, w):
    t, d = x.shape
    tm = _pick_tile(t, (1024, 512, 256, 128))
    return pl.pallas_call(
        _out_proj_kernel,
        grid=(t // tm,),
        in_specs=[pl.BlockSpec((tm, d), lambda i: (i, 0)),
                  pl.BlockSpec((tm, o.shape[1]), lambda i: (i, 0)),
                  pl.BlockSpec(w.shape, lambda i: (0, 0))],
        out_specs=pl.BlockSpec((tm, d), lambda i: (i, 0)),
        out_shape=jax.ShapeDtypeStruct((t, d), F32),
        compiler_params=_cparams(("parallel",)),
        name="out_proj",
    )(x, o, w)


MLP_HALO = 16


def _mlp_kernel(x_ref, xh_ref, g_ref, wa_ref, wg_ref, cwa_ref, cwg_ref, cba_ref, cbg_ref, wd_ref,
                y_ref, h_ref, acc_ref, ua_ref, ug_ref):
    i, j = pl.program_id(0), pl.program_id(1)
    tm = x_ref.shape[0]

    @pl.when(j == 0)
    def _():
        halo = _rms(xh_ref[...], g_ref[...])
        h_ref[0:MLP_HALO, :] = jnp.where(i > 0, halo, 0.0).astype(BF16)
        h_ref[MLP_HALO:, :] = _rms(x_ref[...], g_ref[...]).astype(BF16)
        acc_ref[...] = jnp.zeros_like(acc_ref)

    h = h_ref[...]

    def conv(w_ref, cw_ref, cb_ref, u_ref):
        u_ref[...] = jnp.dot(h, w_ref[...], preferred_element_type=F32)
        cw = cw_ref[...]
        return (cw[0:1] * u_ref[MLP_HALO - 2:MLP_HALO - 2 + tm, :] + cw[1:2] * u_ref[MLP_HALO - 1:MLP_HALO - 1 + tm, :]
                + cw[2:3] * u_ref[MLP_HALO:MLP_HALO + tm, :] + cb_ref[...])

    a = conv(wa_ref, cwa_ref, cba_ref, ua_ref)
    gt = conv(wg_ref, cwg_ref, cbg_ref, ug_ref)
    act = (jax.nn.silu(gt) * a).astype(BF16)
    acc_ref[...] += jnp.dot(act, wd_ref[...], preferred_element_type=F32)

    @pl.when(j == pl.num_programs(1) - 1)
    def _():
        y_ref[...] = x_ref[...] + acc_ref[...]


def _mlp(x, g, w_up, conv_w, conv_b, w_down):
    t, d = x.shape
    f = w_down.shape[0]
    tm = _pick_tile(t, (512, 256, 128))
    tf = _pick_tile(f, (1408, 256, 128))
    nf = f // tf
    hb = tm // MLP_HALO
    return pl.pallas_call(
        _mlp_kernel,
        grid=(t // tm, nf),
        in_specs=[pl.BlockSpec((tm, d), lambda i, j: (i, 0)),
                  pl.BlockSpec((MLP_HALO, d), lambda i, j: (jnp.maximum(i * hb - 1, 0), 0)),
                  pl.BlockSpec((1, d), lambda i, j: (0, 0)),
                  pl.BlockSpec((d, tf), lambda i, j: (0, j)),
                  pl.BlockSpec((d, tf), lambda i, j: (0, nf + j)),
                  pl.BlockSpec((3, tf), lambda i, j: (0, j)),
                  pl.BlockSpec((3, tf), lambda i, j: (0, nf + j)),
                  pl.BlockSpec((1, tf), lambda i, j: (0, j)),
                  pl.BlockSpec((1, tf), lambda i, j: (0, nf + j)),
                  pl.BlockSpec((tf, d), lambda i, j: (j, 0))],
        out_specs=pl.BlockSpec((tm, d), lambda i, j: (i, 0)),
        out_shape=jax.ShapeDtypeStruct((t, d), F32),
        scratch_shapes=[pltpu.VMEM((tm + MLP_HALO, d), BF16), pltpu.VMEM((tm, d), F32),
                        pltpu.VMEM((tm + MLP_HALO, tf), F32), pltpu.VMEM((tm + MLP_HALO, tf), F32)],
        compiler_params=_cparams(("parallel", "arbitrary")),
        name="conv_glu_mlp",
    )(x, x, g.reshape(1, d), w_up, w_up, conv_w, conv_w, conv_b, conv_b, w_down)


def _rmsnorm_kernel(x_ref, g_ref, y_ref):
    y_ref[...] = _rms(x_ref[...], g_ref[...])


def _rmsnorm(x, g):
    t, d = x.shape
    tm = _pick_tile(t, (1024, 512, 256, 128))
    return pl.pallas_call(
        _rmsnorm_kernel,
        grid=(t // tm,),
        in_specs=[pl.BlockSpec((tm, d), lambda i: (i, 0)), pl.BlockSpec((1, d), lambda i: (0, 0))],
        out_specs=pl.BlockSpec((tm, d), lambda i: (i, 0)),
        out_shape=jax.ShapeDtypeStruct((t, d), F32),
        compiler_params=_cparams(("parallel",)),
        name="final_rmsnorm",
    )(x, g.reshape(1, d))


def _swap_halves(q_pair):
    return pltpu.roll(q_pair.astype(F32), HEAD_DIM, 1).astype(BF16)


def _lane_half(shape):
    return lax.broadcasted_iota(jnp.int32, shape, 1) // HEAD_DIM


def _rows_to_heads(o_t, n_heads, tq):
    stacked = jnp.concatenate([o_t[:, r * tq:(r + 1) * tq] for r in range(n_heads)], axis=0)
    return stacked.T


def _swa_kernel(q_ref, kp_ref, kc_ref, vp_ref, vc_ref, bias_ref, sink_ref, o_ref):
    tq = q_ref.shape[0]
    rows = SWA_GROUP * tq
    k_ext = jnp.concatenate([kp_ref[...], kc_ref[...]], axis=0)
    v_ext = jnp.concatenate([vp_ref[...], vc_ref[...]], axis=1)
    half = _lane_half((tq, LANES))
    outs = []
    for g in range(SWA_KV_HEADS):
        parts = []
        for r in range(SWA_GROUP):
            hd = g * SWA_GROUP + r
            q_pair = q_ref[:, (hd // 2) * LANES:(hd // 2 + 1) * LANES]
            src = q_pair if hd % 2 == g else _swap_halves(q_pair)
            parts.append(jnp.where(half == g, src, jnp.zeros_like(src)))
        qp = jnp.concatenate(parts, axis=0)
        s = lax.dot_general(k_ext, qp, (((1,), (1,)), ((), ())), preferred_element_type=F32)
        s = s * (HEAD_DIM ** -0.5) + bias_ref[0, g]
        sink = sink_ref[g]
        m = jnp.maximum(jnp.max(s, axis=0, keepdims=True), sink)
        e = jnp.exp(s - m)
        l = jnp.sum(e, axis=0, keepdims=True) + jnp.exp(sink - m)
        o_t = jnp.dot(v_ext[g * HEAD_DIM:(g + 1) * HEAD_DIM], e.astype(BF16), preferred_element_type=F32)
        outs.append(_rows_to_heads(o_t / l, SWA_GROUP, tq))
    o_ref[...] = jnp.concatenate(outs, axis=1).astype(o_ref.dtype)


def _swa_bias():
    tq = SWA_TQ
    slopes = _alibi_slopes().reshape(SWA_KV_HEADS, SWA_GROUP)
    x = np.arange(2 * tq)[:, None] - tq
    q = np.arange(tq)[None, :]
    dist = q - x
    valid = (dist >= 0) & (dist < SWA_WINDOW)
    out = np.empty((2, SWA_KV_HEADS, 2 * tq, SWA_GROUP * tq), np.float32)
    for var in range(2):
        v = valid & ((x >= 0) | (var == 1))
        for g in range(SWA_KV_HEADS):
            for r in range(SWA_GROUP):
                out[var, g, :, r * tq:(r + 1) * tq] = np.where(v, -slopes[g, r] * dist, MASKED)
    return out


def _swa_attention(qkv, v_t, sinks):
    t = qkv.shape[0]
    tq = SWA_TQ
    kcol = Q_DIM // LANES
    bias = jnp.asarray(_swa_bias())
    sink_rows = jnp.repeat(sinks.astype(F32).reshape(SWA_KV_HEADS, 1, SWA_GROUP), tq, axis=2)
    prev = lambda i: jnp.maximum(i - 1, 0)
    return pl.pallas_call(
        _swa_kernel,
        grid=(t // tq,),
        in_specs=[pl.BlockSpec((tq, Q_DIM), lambda i: (i, 0)),
                  pl.BlockSpec((tq, LANES), lambda i: (prev(i), kcol)),
                  pl.BlockSpec((tq, LANES), lambda i: (i, kcol)),
                  pl.BlockSpec((LANES, tq), lambda i: (0, prev(i))),
                  pl.BlockSpec((LANES, tq), lambda i: (0, i)),
                  pl.BlockSpec((1,) + bias.shape[1:], lambda i: (jnp.minimum(i, 1), 0, 0, 0)),
                  pl.BlockSpec(sink_rows.shape, lambda i: (0, 0, 0))],
        out_specs=pl.BlockSpec((tq, Q_DIM), lambda i: (i, 0)),
        out_shape=jax.ShapeDtypeStruct((t, Q_DIM), BF16),
        compiler_params=_cparams(("parallel",)),
        name="swa_attention",
    )(qkv, qkv, qkv, v_t, v_t, bias, sink_rows)


def _swa_layer(x, norm_g, w_in, sinks, w_out):
    qkv = _norm_proj(x, norm_g, w_in.astype(BF16), BF16)
    v_t = qkv[:, Q_DIM + SWA_KV_HEADS * HEAD_DIM:].T
    o = _swa_attention(qkv, v_t, sinks)
    return _out_proj(x, o, w_out.astype(BF16))


FOX_PARTS = 3


def _split_bf16(v):
    parts, rest = [], v
    for _ in range(FOX_PARTS):
        p = rest.astype(BF16)
        parts.append(p)
        rest = rest - p.astype(F32)
    return parts


def _fox_prep_kernel(f_ref, b_ref, aug_ref, cref_ref, carry_ref):
    tm = f_ref.shape[0]

    @pl.when(pl.program_id(0) == 0)
    def _():
        carry_ref[...] = jnp.zeros_like(carry_ref)

    logf = jax.nn.log_sigmoid(f_ref[...] + b_ref[...])
    row = lax.broadcasted_iota(jnp.int32, (tm, tm), 0)
    col = lax.broadcasted_iota(jnp.int32, (tm, tm), 1)
    tri = jnp.where(row >= col, 1.0, 0.0).astype(BF16)
    local = sum(jnp.dot(tri, p, preferred_element_type=F32) for p in _split_bf16(logf))
    cum = local + carry_ref[...]
    carry_ref[...] = cum[tm - 1:tm, :]
    first = cum[0:1, :]
    cref_ref[0] = jnp.broadcast_to(first, cref_ref.shape[1:])
    hi, mid, lo = _split_bf16(first - cum)
    lane = lax.broadcasted_iota(jnp.int32, (tm, LANES), 1)
    zero = jnp.zeros_like(hi)
    aug_ref[...] = jnp.where(lane < N_HEADS, hi, jnp.where(lane < 2 * N_HEADS, mid,
                                                            jnp.where(lane < 3 * N_HEADS, lo, zero)))


def _fox_prep(f_logit3, f_bias3):
    t = f_logit3.shape[0]
    nt = t // FOX_T
    return pl.pallas_call(
        _fox_prep_kernel,
        grid=(nt,),
        in_specs=[pl.BlockSpec((FOX_T, LANES), lambda i: (i, 0)), pl.BlockSpec((1, LANES), lambda i: (0, 0))],
        out_specs=[pl.BlockSpec((FOX_T, LANES), lambda i: (i, 0)), pl.BlockSpec((1, 8, LANES), lambda i: (i, 0, 0))],
        out_shape=[jax.ShapeDtypeStruct((t, LANES), BF16), jax.ShapeDtypeStruct((nt, 8, LANES), F32)],
        scratch_shapes=[pltpu.VMEM((1, LANES), F32)],
        compiler_params=_cparams(("arbitrary",)),
        name="fox_prep",
    )(f_logit3, f_bias3)


def _fox_kernel(cref_ref, q_ref, k_ref, aug_ref, vt_ref, o_ref, m_ref, l_ref, acc_ref, qt_ref, s_ref):
    pair, qi = pl.program_id(0), pl.program_id(1)
    tq = q_ref.shape[0]
    tk = FOX_T
    lane = lax.broadcasted_iota(jnp.int32, (tq, LANES), 1)
    q2 = q_ref[...] * jnp.asarray(HEAD_DIM ** -0.5, BF16)
    for hh in range(2):
        head = 2 * pair + hh
        pick = (lane % N_HEADS == head) & (lane < FOX_PARTS * N_HEADS)
        q_ext = jnp.concatenate([jnp.where(lane // HEAD_DIM == hh, q2, jnp.zeros_like(q2)).astype(F32),
                                 jnp.where(pick, 1.0, 0.0)], axis=1)
        qt_ref[hh] = q_ext.T.astype(BF16)
    m_ref[...] = jnp.full_like(m_ref, M_INIT)
    l_ref[...] = jnp.zeros_like(l_ref)
    acc_ref[...] = jnp.zeros_like(acc_ref)
    ratio = tq // tk

    def scores(kj, hh):
        start = pl.multiple_of(kj * tk, tk)
        k_ext = jnp.concatenate([k_ref[pl.ds(start, tk), :], aug_ref[pl.ds(start, tk), :]], axis=1)
        s_ref[hh] = jnp.dot(k_ext, qt_ref[hh], preferred_element_type=F32)

    def consume(kj, hh, key_offset):
        head = 2 * pair + hh
        s = s_ref[hh]
        if key_offset is not None:
            key = lax.broadcasted_iota(jnp.int32, (tk, tq), 0) + key_offset
            qry = lax.broadcasted_iota(jnp.int32, (tk, tq), 1)
            s = jnp.where(key <= qry, s, MASKED)
        off = cref_ref[qi * ratio, head] - cref_ref[kj, head]
        m_old = m_ref[hh]
        m_new = jnp.maximum(m_old, jnp.max(s, axis=0, keepdims=True) + off)
        e = jnp.exp(s - (m_new - off))
        alpha = jnp.exp(m_old - m_new)
        m_ref[hh] = m_new
        l_ref[hh] = alpha * l_ref[hh] + jnp.sum(e, axis=0, keepdims=True)
        v_t = vt_ref[kj, hh * HEAD_DIM:(hh + 1) * HEAD_DIM, :]
        acc_ref[hh] = alpha * acc_ref[hh] + jnp.dot(v_t, e.astype(BF16), preferred_element_type=F32)

    n_full = qi * ratio
    scores(0, 0)

    def body(j, carry):
        for d in range(ratio):
            kj = j * ratio + d
            scores(kj, 1)
            consume(kj, 0, None)
            scores(kj + 1, 0)
            consume(kj, 1, None)
        return carry

    lax.fori_loop(0, qi, body, 0)
    for d in range(ratio):
        kj = n_full + d
        scores(kj, 1)
        consume(kj, 0, d * tk)
        if d + 1 < ratio:
            scores(kj + 1, 0)
        consume(kj, 1, d * tk)
    o_t = jnp.concatenate([acc_ref[hh] / l_ref[hh] for hh in range(2)], axis=0)
    o_ref[...] = o_t.T.astype(o_ref.dtype)


def _fox_attention(qkv, aug, v_t3, cref):
    t = qkv.shape[0]
    tq = FOX_TQ
    nk = t // FOX_T
    kcol = Q_DIM // LANES
    return pl.pallas_call(
        _fox_kernel,
        grid=(N_HEADS // 2, t // tq),
        in_specs=[pl.BlockSpec(memory_space=pltpu.SMEM),
                  pl.BlockSpec((tq, LANES), lambda p, i: (i, p)),
                  pl.BlockSpec((t, LANES), lambda p, i: (0, kcol + p)),
                  pl.BlockSpec((t, LANES), lambda p, i: (0, 0)),
                  pl.BlockSpec((nk, LANES, FOX_T), lambda p, i: (0, p, 0))],
        out_specs=pl.BlockSpec((tq, LANES), lambda p, i: (i, p)),
        out_shape=jax.ShapeDtypeStruct((t, Q_DIM), BF16),
        scratch_shapes=[pltpu.VMEM((2, 1, tq), F32), pltpu.VMEM((2, 1, tq), F32),
                        pltpu.VMEM((2, HEAD_DIM, tq), F32), pltpu.VMEM((2, 2 * LANES, tq), BF16), pltpu.VMEM((2, FOX_T, tq), F32)],
        compiler_params=_cparams(("parallel", "arbitrary")),
        name="fox_attention",
    )(cref, qkv, qkv, aug, v_t3)


def _fox_layer(x, norm_g, w_in, f_bias, w_out):
    t = x.shape[0]
    qkv = _norm_proj(x, norm_g, w_in[:, :3 * Q_DIM].astype(BF16), BF16)
    w_f = w_in[:, 3 * Q_DIM:]
    pad = jnp.zeros((D_MODEL, LANES - FOX_PARTS * N_HEADS), w_f.dtype)
    w_f3 = jnp.concatenate([w_f] * FOX_PARTS + [pad], axis=1).astype(BF16)
    b3 = jnp.concatenate([f_bias.astype(F32)] * FOX_PARTS + [jnp.zeros((LANES - FOX_PARTS * N_HEADS,), F32)])
    f_logit3 = _norm_proj(x, norm_g, w_f3, F32)
    aug, cref = _fox_prep(f_logit3, b3.reshape(1, LANES))
    v_t3 = qkv[:, 2 * Q_DIM:].reshape(t // FOX_T, FOX_T, Q_DIM).transpose(0, 2, 1)
    o = _fox_attention(qkv, aug, v_t3, cref[:, 0, :N_HEADS])
    return _out_proj(x, o, w_out.astype(BF16))


def _nsa_compress_kernel(a_ref, pos_ref, w1_ref, w2_ref, o_ref):
    tc = a_ref.shape[2]
    half = CMP_STRIDE * HEAD_DIM
    pos = pos_ref[0]
    out = jnp.zeros(o_ref.shape[1:], F32)
    for gg in range(2):
        a = a_ref[0, gg]
        top = jnp.dot((a + pos[:, :half]).astype(BF16), w1_ref[0, :half], preferred_element_type=F32)
        bot = jnp.dot((a + pos[:, half:]).astype(BF16), w1_ref[0, half:], preferred_element_type=F32)
        hid = top + pltpu.roll(bot, tc - 1, 0)
        out = out + jnp.dot(jax.nn.gelu(hid).astype(BF16), w2_ref[0, gg], preferred_element_type=F32)
    o_ref[0] = out


def _nsa_compress(a, pos, w1, w2):
    _, g, tc, width = a.shape
    return pl.pallas_call(
        _nsa_compress_kernel,
        grid=(2, g // 2),
        in_specs=[pl.BlockSpec((1, 2, tc, width), lambda kv, gp: (kv, gp, 0, 0)),
                  pl.BlockSpec((1, 1, 2 * width), lambda kv, gp: (kv, 0, 0)),
                  pl.BlockSpec((1, 2 * width, CMP_HIDDEN), lambda kv, gp: (kv, 0, 0)),
                  pl.BlockSpec((1, 2, CMP_HIDDEN, LANES), lambda kv, gp: (kv, 0, 0, 0))],
        out_specs=pl.BlockSpec((1, tc, LANES), lambda kv, gp: (kv, 0, gp)),
        out_shape=jax.ShapeDtypeStruct((2, tc, g * HEAD_DIM), F32),
        compiler_params=_cparams(("parallel", "parallel")),
        name="nsa_compress",
    )(a, pos, w1, w2)


NSA_WIN_TILES = NSA_WINDOW // NSA_TQ + 1


def _nsa_kernel(slopes_ref, q_ref, ks_ref, *refs):
    kw_refs, refs = refs[:NSA_WIN_TILES], refs[NSA_WIN_TILES:]
    vst_ref, refs = refs[0], refs[1:]
    vw_refs, refs = refs[:NSA_WIN_TILES], refs[NSA_WIN_TILES:]
    (kcmp_ref, vcmpt_ref, ovt_ref, gates_ref, ksel_ext_ref, kwin_ext_ref, qext_ref,
     o_ref, selt_ref, m_ref, l_ref, acc_ref, qt_ref, s_ref, active_ref, list_ref) = refs
    g, qi = pl.program_id(0), pl.program_id(1)
    tq = q_ref.shape[0]
    tc = kcmp_ref.shape[0]
    n_sel = ovt_ref.shape[0]
    rr = NSA_GROUP
    t0 = qi * tq
    hg = g % 2
    slope = [slopes_ref[g * rr + r] for r in range(rr)]

    half = _lane_half((tq, LANES))
    parts = []
    for r in range(rr):
        q_pair = q_ref[:, (r // 2) * LANES:(r // 2 + 1) * LANES]
        src = jnp.where(hg == r % 2, q_pair, _swap_halves(q_pair))
        parts.append(jnp.where(half == hg, src, jnp.zeros_like(src)) * jnp.asarray(HEAD_DIM ** -0.5, BF16))
    qp = jnp.concatenate(parts, axis=0)
    qt_ref[0:LANES, :] = qp.astype(F32).T.astype(BF16)

    @pl.when(qi == 0)
    def _():
        qt_ref[LANES:, :] = qext_ref[0]

    def lanes_of(s, r):
        return s[:, r * tq:(r + 1) * tq]

    s_all = jnp.dot(kcmp_ref[...], qt_ref[0:LANES, :], preferred_element_type=F32)
    n_id = lax.broadcasted_iota(jnp.int32, (tc, tq), 0)
    q_id = lax.broadcasted_iota(jnp.int32, (tc, tq), 1)
    neg_d = (CMP_STRIDE * n_id + (CMP_BLOCK - 1) - q_id - t0).astype(F32)
    valid = neg_d <= 0.0
    p_all, p_sum = [], jnp.zeros((tc, tq), F32)
    for r in range(rr):
        s = jnp.where(valid, lanes_of(s_all, r) + slope[r] * neg_d, MASKED)
        m = jnp.maximum(jnp.max(s, axis=0, keepdims=True), M_INIT)
        e = jnp.exp(s - m)
        l = jnp.sum(e, axis=0, keepdims=True)
        p = e * jnp.where(l > 0.0, 1.0 / l, 0.0)
        p_sum = p_sum + p
        p_all.append(p.astype(BF16))
    o_cmp_all = jnp.dot(vcmpt_ref[...], jnp.concatenate(p_all, axis=1), preferred_element_type=F32)
    o_cmp = [lanes_of(o_cmp_all, r) for r in range(rr)]

    imp = jnp.dot(ovt_ref[...], p_sum.astype(BF16), preferred_element_type=F32)
    blk = lax.broadcasted_iota(jnp.int32, (n_sel, tq), 0)
    cur = (t0 + lax.broadcasted_iota(jnp.int32, (n_sel, tq), 1)) // SEL_BLOCK
    blk_f = blk.astype(F32)
    forced = jnp.where(blk == 0, 1.0, jnp.where(blk == cur, 1.0, jnp.where(blk == cur - 1, 1.0, 0.0)))
    score = jnp.where(forced > 0.0, TAKEN, jnp.where(blk <= cur, imp, -1.0))

    def pick(_, sc):
        best = jnp.max(sc, axis=0, keepdims=True)
        first = jnp.min(jnp.where(sc == best, blk_f, float(n_sel)), axis=0, keepdims=True)
        return jnp.where(blk_f == first, TAKEN, sc)

    score = lax.fori_loop(0, N_SELECT - 3, pick, score)
    selt_ref[...] = jnp.where(blk <= cur, jnp.where(score == TAKEN, 1.0, 0.0), 0.0)

    m_ref[...] = jnp.full_like(m_ref, M_INIT)
    l_ref[...] = jnp.zeros_like(l_ref)
    acc_ref[...] = jnp.zeros_like(acc_ref)
    blocks_per_chunk = NSA_TK // SEL_BLOCK

    for c in range(n_sel // blocks_per_chunk):
        active_ref[c] = jnp.max(selt_ref[c * blocks_per_chunk:(c + 1) * blocks_per_chunk, :])

    def compact(c, n):
        list_ref[n] = c
        return n + (active_ref[c] > 0.0).astype(jnp.int32)

    n_active = lax.fori_loop(0, t0 // NSA_TK + 1, compact, 0)

    def sel_scores(c, buf):
        start = pl.multiple_of(c * NSA_TK, NSA_TK)
        rows = selt_ref[pl.ds(pl.multiple_of(c * blocks_per_chunk, blocks_per_chunk), blocks_per_chunk), :]
        mask_rows = jnp.concatenate([jnp.where(rows > 0.5, 0.0, MASKED), jnp.zeros_like(rows)], axis=0)
        first = LANES + buf * NSA_EXT_MASK_ROWS
        qt_ref[first:first + NSA_EXT_MASK_ROWS, :] = jnp.concatenate([mask_rows] * rr, axis=1).astype(BF16)
        k_full = jnp.concatenate([ks_ref[pl.ds(start, NSA_TK), :], ksel_ext_ref[buf]], axis=1)
        s_ref[buf] = jnp.dot(k_full, qt_ref[...], preferred_element_type=F32)

    def sel_consume(c, buf, diagonal):
        shift = (c * NSA_TK - t0).astype(F32)
        if diagonal:
            key = lax.broadcasted_iota(jnp.int32, (NSA_TK, tq), 0) + c * NSA_TK
            visible = key <= lax.broadcasted_iota(jnp.int32, (NSA_TK, tq), 1) + t0
        e_all, alpha_all = [], []
        for r in range(rr):
            off = slope[r] * shift
            s = s_ref[buf][:, r * tq:(r + 1) * tq]
            if diagonal:
                s = jnp.where(visible, s, MASKED)
            m_old = m_ref[r]
            m_new = jnp.maximum(m_old, jnp.max(s, axis=0, keepdims=True) + off)
            e = jnp.exp(s - (m_new - off))
            alpha = jnp.exp(m_old - m_new)
            m_ref[r] = m_new
            l_ref[r] = alpha * l_ref[r] + jnp.sum(e, axis=0, keepdims=True)
            e_all.append(e.astype(BF16))
            alpha_all.append(alpha)
        pv = jnp.dot(vst_ref[c], jnp.concatenate(e_all, axis=1), preferred_element_type=F32)
        acc_ref[...] = jnp.concatenate(alpha_all, axis=1) * acc_ref[...] + pv

    n_before = n_active - 1
    sel_scores(list_ref[0], 0)

    def sel_pair(j, carry):
        c0, c1, c2 = list_ref[2 * j], list_ref[2 * j + 1], list_ref[2 * j + 2]
        sel_scores(c1, 1)
        sel_consume(c0, 0, False)
        sel_scores(c2, 0)
        sel_consume(c1, 1, False)
        return carry

    lax.fori_loop(0, n_before // 2, sel_pair, 0)

    @pl.when(n_before % 2 == 1)
    def _():
        c0, c_last = list_ref[n_before - 1], list_ref[n_before]
        sel_scores(c_last, 1)
        sel_consume(c0, 0, False)
        sel_consume(c_last, 1, True)

    @pl.when(n_before % 2 == 0)
    def _():
        sel_consume(list_ref[n_before], 0, True)

    xw = lax.broadcasted_iota(jnp.int32, (tq, tq), 0)
    qw = lax.broadcasted_iota(jnp.int32, (tq, tq), 1)
    s_win, off_win = [], []
    for idx in range(NSA_WIN_TILES):
        back = NSA_WIN_TILES - 1 - idx
        k_full = jnp.concatenate([kw_refs[idx][...], kwin_ext_ref[...]], axis=1)
        s_t = jnp.dot(k_full, qt_ref[...], preferred_element_type=F32)
        per_head = [lanes_of(s_t, r) for r in range(rr)]
        if back == NSA_WIN_TILES - 1:
            per_head = [jnp.where(xw > qw, s, MASKED) for s in per_head]
        if back == 0:
            per_head = [jnp.where(xw <= qw, s, MASKED) for s in per_head]
        s_win.append(per_head)
        off_win.append([jnp.where(qi >= back, -slope[r] * float(back * tq), MASKED) for r in range(rr)])
    e_win, l_win = [], []
    for r in range(rr):
        m = functools.reduce(jnp.maximum, [jnp.max(s_win[idx][r], axis=0, keepdims=True) + off_win[idx][r]
                                           for idx in range(NSA_WIN_TILES)])
        e_tiles = [jnp.exp(s_win[idx][r] - (m - off_win[idx][r])) for idx in range(NSA_WIN_TILES)]
        l_win.append(functools.reduce(jnp.add, [jnp.sum(e, axis=0, keepdims=True) for e in e_tiles]))
        e_win.append(jnp.concatenate([e.astype(BF16) for e in e_tiles], axis=0))
    v_win = jnp.concatenate([vw_refs[idx][...] for idx in range(NSA_WIN_TILES)], axis=1)
    o_win_all = jnp.dot(v_win, jnp.concatenate(e_win, axis=1), preferred_element_type=F32)
    o_win = [lanes_of(o_win_all, r) / l_win[r] for r in range(rr)]

    gate = jax.nn.sigmoid(gates_ref[0])
    merged = []
    for r in range(rr):
        o_sel = lanes_of(acc_ref[...], r) / l_ref[r]
        merged.append(gate[3 * r:3 * r + 1] * o_cmp[r] + gate[3 * r + 1:3 * r + 2] * o_sel
                      + gate[3 * r + 2:3 * r + 3] * o_win[r])
    o_ref[...] = jnp.concatenate(merged, axis=0).T.astype(o_ref.dtype)


def _nsa_overlap_t(n_sel, tc):
    cmp_start = np.arange(tc - 1) * CMP_STRIDE
    sel_start = np.arange(n_sel) * SEL_BLOCK
    ov = np.clip(np.minimum(cmp_start[None, :] + CMP_BLOCK, sel_start[:, None] + SEL_BLOCK)
                 - np.maximum(cmp_start[None, :], sel_start[:, None]), 0, None) / CMP_BLOCK
    return np.concatenate([ov, np.zeros((n_sel, 1))], axis=1).astype(np.float32)


NSA_EXT_MASK_ROWS = 16
NSA_EXT_SEL = 32
NSA_EXT_WIN = 48
SLOPE_PIECES = 3


def _bf16_pieces(v):
    out, rest = [], np.asarray(v, np.float32)
    for _ in range(SLOPE_PIECES):
        p = rest.astype(BF16).astype(np.float32)
        out.append(p)
        rest = rest - p
    return out


def _nsa_extensions(tq):
    ksel = np.zeros((2, NSA_TK, LANES), np.float32)
    x = np.arange(NSA_TK)
    for buf in range(2):
        ksel[buf, x, buf * NSA_EXT_MASK_ROWS + x // SEL_BLOCK] = 1.0
    for j in range(SLOPE_PIECES):
        ksel[:, :, NSA_EXT_SEL + j] = 256 * (x // 256)
        ksel[:, :, NSA_EXT_SEL + SLOPE_PIECES + j] = x % 256
    kwin = np.zeros((tq, LANES), np.float32)
    for j in range(SLOPE_PIECES):
        kwin[:, NSA_EXT_WIN + j] = np.arange(tq)
    assert tq <= 256, "a window key offset must be exact in bf16"
    pieces = _bf16_pieces(_alibi_slopes())
    qext = np.zeros((NSA_KV_HEADS, LANES, NSA_GROUP * tq), np.float32)
    for g in range(NSA_KV_HEADS):
        for r in range(NSA_GROUP):
            cols = slice(r * tq, (r + 1) * tq)
            for j in range(SLOPE_PIECES):
                p = pieces[j][g * NSA_GROUP + r]
                qext[g, NSA_EXT_SEL + j, cols] = p
                qext[g, NSA_EXT_SEL + SLOPE_PIECES + j, cols] = p
                qext[g, NSA_EXT_WIN + j, cols] = p
    return jnp.asarray(ksel, BF16), jnp.asarray(kwin, BF16), jnp.asarray(qext, BF16)


def _nsa_attention(main, vs_t3, vw_t, k_cmp, v_cmp_t, gates_t):
    t = main.shape[0]
    tq = NSA_TQ
    tc = k_cmp.shape[0]
    n_sel = t // SEL_BLOCK
    nc = t // NSA_TK
    ks_col = Q_DIM // LANES
    kw_col = ks_col + NSA_KV_DIM // LANES
    ov_t = jnp.asarray(_nsa_overlap_t(n_sel, tc), BF16)
    slopes = jnp.asarray(_alibi_slopes())
    ksel_ext, kwin_ext, qext = _nsa_extensions(tq)

    def kw_spec(back):
        return pl.BlockSpec((tq, LANES), lambda g, i: (jnp.maximum(i - back, 0), kw_col + g // 2))

    def vw_spec(back):
        return pl.BlockSpec((HEAD_DIM, tq), lambda g, i: (g, jnp.maximum(i - back, 0)))

    backs = list(range(NSA_WIN_TILES - 1, -1, -1))
    return pl.pallas_call(
        _nsa_kernel,
        grid=(NSA_KV_HEADS, t // tq),
        in_specs=[pl.BlockSpec(memory_space=pltpu.SMEM),
                  pl.BlockSpec((tq, NSA_GROUP * HEAD_DIM), lambda g, i: (i, g)),
                  pl.BlockSpec((t, LANES), lambda g, i: (0, ks_col + g // 2))]
                 + [kw_spec(b) for b in backs]
                 + [pl.BlockSpec((nc, HEAD_DIM, NSA_TK), lambda g, i: (0, g, 0))]
                 + [vw_spec(b) for b in backs]
                 + [pl.BlockSpec((tc, LANES), lambda g, i: (0, g // 2)),
                    pl.BlockSpec((HEAD_DIM, tc), lambda g, i: (g, 0)),
                    pl.BlockSpec((n_sel, tc), lambda g, i: (0, 0)),
                    pl.BlockSpec((1, 16, tq), lambda g, i: (g, 0, i)),
                    pl.BlockSpec(ksel_ext.shape, lambda g, i: (0, 0, 0)),
                    pl.BlockSpec(kwin_ext.shape, lambda g, i: (0, 0)),
                    pl.BlockSpec((1,) + qext.shape[1:], lambda g, i: (g, 0, 0))],
        out_specs=pl.BlockSpec((tq, NSA_GROUP * HEAD_DIM), lambda g, i: (i, g)),
        out_shape=jax.ShapeDtypeStruct((t, Q_DIM), BF16),
        scratch_shapes=[pltpu.VMEM((n_sel, tq), F32), pltpu.VMEM((NSA_GROUP, 1, tq), F32),
                        pltpu.VMEM((NSA_GROUP, 1, tq), F32), pltpu.VMEM((HEAD_DIM, NSA_GROUP * tq), F32),
                        pltpu.VMEM((2 * LANES, NSA_GROUP * tq), BF16),
                        pltpu.VMEM((2, NSA_TK, NSA_GROUP * tq), F32),
                        pltpu.SMEM((nc,), F32), pltpu.SMEM((nc,), jnp.int32)],
        compiler_params=_cparams(("parallel", "arbitrary")),
        name="nsa_attention",
    )(slopes, main, main, *([main] * NSA_WIN_TILES), vs_t3, *([vw_t] * NSA_WIN_TILES), k_cmp, v_cmp_t, ov_t, gates_t,
      ksel_ext, kwin_ext, qext)


def _nsa_layer(x, norm_g, w_in, cmp_pos, cmp_w1, cmp_w2, w_out):
    t = x.shape[0]
    kv = NSA_KV_DIM
    col = lambda i: w_in[:, Q_DIM + i * kv:Q_DIM + (i + 1) * kv]
    w_main = jnp.concatenate([w_in[:, :Q_DIM], col(2), col(4), col(3), col(5)], axis=1).astype(BF16)
    w_cmp = jnp.concatenate([col(0), col(1)], axis=1).astype(BF16)
    n_gate = 3 * N_HEADS
    w_gate = jnp.concatenate([w_in[:, Q_DIM + 6 * kv:], jnp.zeros((D_MODEL, LANES - n_gate), w_in.dtype)],
                             axis=1).astype(BF16)
    main = _norm_proj(x, norm_g, w_main, BF16)
    kcvc = _norm_proj(x, norm_g, w_cmp, F32)
    gate_logit = _norm_proj(x, norm_g, w_gate, F32)

    tc = t // CMP_STRIDE
    a = kcvc.reshape(tc, CMP_STRIDE, 2, NSA_KV_HEADS, HEAD_DIM).transpose(2, 3, 0, 1, 4)
    a = a.reshape(2, NSA_KV_HEADS, tc, CMP_STRIDE * HEAD_DIM)
    zeros = jnp.zeros_like(cmp_w2)
    w2 = jnp.stack([jnp.concatenate([cmp_w2, zeros], axis=2), jnp.concatenate([zeros, cmp_w2], axis=2)], axis=1)
    cmp = _nsa_compress(a, cmp_pos.reshape(2, 1, CMP_BLOCK * HEAD_DIM), cmp_w1.astype(BF16), w2.astype(BF16))
    k_cmp = cmp[0].astype(BF16)
    v_cmp_t = cmp[1].T.astype(BF16)

    off_vs = Q_DIM + 2 * kv
    vs_t3 = main[:, off_vs:off_vs + kv].reshape(t // NSA_TK, NSA_TK, kv).transpose(0, 2, 1)
    vw_t = main[:, off_vs + kv:].T
    gates_t = gate_logit[:, :n_gate].reshape(t, NSA_KV_HEADS, 3 * NSA_GROUP).transpose(1, 2, 0)
    gates_t = jnp.pad(gates_t, ((0, 0), (0, 16 - 3 * NSA_GROUP), (0, 0)))
    o = _nsa_attention(main, vs_t3, vw_t, k_cmp, v_cmp_t, gates_t)
    return _out_proj(x, o, w_out.astype(BF16))


def kernel(x, attn_norm, mlp_norm, final_norm, nsa_w_in, nsa_cmp_pos, nsa_cmp_w1, nsa_cmp_w2, nsa_w_out,
           swa_w_in, swa_sinks, swa_w_out, fox_w_in, fox_f_bias, fox_w_out,
           mlp_w_up, mlp_conv_w, mlp_conv_b, mlp_w_down):
    assert x.shape[0] == 1, "the trunk is written for batch 1"
    h = x[0]
    for i in range(DEPTH):
        kind, j = i % N_MIXERS, i // N_MIXERS
        if kind == 0:
            h = _nsa_layer(h, attn_norm[i], nsa_w_in[j], nsa_cmp_pos[j], nsa_cmp_w1[j], nsa_cmp_w2[j], nsa_w_out[j])
        elif kind == 1:
            h = _swa_layer(h, attn_norm[i], swa_w_in[j], swa_sinks[j], swa_w_out[j])
        else:
            h = _fox_layer(h, attn_norm[i], fox_w_in[j], fox_f_bias[j], fox_w_out[j])
        h = _mlp(h, mlp_norm[i], mlp_w_up[i].astype(BF16), mlp_conv_w[i], mlp_conv_b[i].reshape(1, -1),
                 mlp_w_down[i].astype(BF16))
    return _rmsnorm(h, final_norm)[None]
```

```python
import functools

import numpy as np
import jax
import jax.numpy as jnp
from jax import lax
from jax.experimental import pallas as pl
from jax.experimental.pallas import tpu as pltpu

F32 = jnp.float32
BF16 = jnp.bfloat16

D_MODEL = 1024
DEPTH = 4
N_MIXERS = 3
HEAD_DIM = 64
N_HEADS = 16
Q_DIM = N_HEADS * HEAD_DIM
ALIBI_MAX = 8.0
NORM_EPS = 1e-6
FORCE_SCORE = 1e4
MASKED = -2e30
M_INIT = -1e30
TAKEN = -(2.0 ** 127)

NSA_KV_HEADS = 4
NSA_GROUP = 4
NSA_KV_DIM = NSA_KV_HEADS * HEAD_DIM
CMP_BLOCK = 32
CMP_STRIDE = 16
CMP_HIDDEN = 256
SEL_BLOCK = 64
N_SELECT = 16
NSA_WINDOW = 512
NSA_TQ = 256
NSA_TK = 512

SWA_KV_HEADS = 2
SWA_GROUP = 8
SWA_WINDOW = 128
SWA_TQ = 128

FOX_TQ = 512
FOX_T = 256

D_FF = 2816
LANES = 128
VMEM_LIMIT = 56 * 1024 * 1024


def _cparams(semantics, vmem=VMEM_LIMIT):
    return pltpu.CompilerParams(dimension_semantics=semantics, vmem_limit_bytes=vmem)


def _alibi_slopes():
    return np.asarray(2.0 ** (-ALIBI_MAX * np.arange(1, N_HEADS + 1) / N_HEADS), dtype=np.float32)


def _rms(x, g):
    ms = jnp.mean(x * x, axis=-1, keepdims=True)
    return x * lax.rsqrt(ms + NORM_EPS) * g


def _norm_proj_kernel(x_ref, g_ref, w_ref, o_ref, h_ref):
    @pl.when(pl.program_id(1) == 0)
    def _():
        h_ref[...] = _rms(x_ref[...], g_ref[...]).astype(BF16)

    o_ref[...] = jnp.dot(h_ref[...], w_ref[...], preferred_element_type=F32).astype(o_ref.dtype)


def _pick_tile(n, candidates):
    for c in candidates:
        if n % c == 0:
            return c
    raise ValueError(f"no tile for {n}")


def _norm_proj(x, g, w, out_dtype):
    t, d = x.shape
    n = w.shape[1]
    tm = _pick_tile(t, (1024, 512, 256, 128))
    tn = _pick_tile(n, (1024, 768, 512, 256, 128))
    return pl.pallas_call(
        _norm_proj_kernel,
        grid=(t // tm, n // tn),
        in_specs=[pl.BlockSpec((tm, d), lambda i, j: (i, 0)),
                  pl.BlockSpec((1, d), lambda i, j: (0, 0)),
                  pl.BlockSpec((d, tn), lambda i, j: (0, j))],
        out_specs=pl.BlockSpec((tm, tn), lambda i, j: (i, j)),
        out_shape=jax.ShapeDtypeStruct((t, n), out_dtype),
        scratch_shapes=[pltpu.VMEM((tm, d), BF16)],
        compiler_params=_cparams(("parallel", "arbitrary")),
        name="norm_proj",
    )(x, g.reshape(1, d), w)


def _out_proj_kernel(x_ref, o_ref, w_ref, y_ref):
    y_ref[...] = x_ref[...] + jnp.dot(o_ref[...], w_ref[...], preferred_element_type=F32)


def _out_proj(x, o, w):
    t, d = x.shape
    tm = _pick_tile(t, (1024, 512, 256, 128))
    return pl.pallas_call(
        _out_proj_kernel,
        grid=(t // tm,),
        in_specs=[pl.BlockSpec((tm, d), lambda i: (i, 0)),
                  pl.BlockSpec((tm, o.shape[1]), lambda i: (i, 0)),
                  pl.BlockSpec(w.shape, lambda i: (0, 0))],
        out_specs=pl.BlockSpec((tm, d), lambda i: (i, 0)),
        out_shape=jax.ShapeDtypeStruct((t, d), F32),
        compiler_params=_cparams(("parallel",)),
        name="out_proj",
    )(x, o, w)


MLP_HALO = 16


def _mlp_kernel(x_ref, xh_ref, g_ref, wa_ref, wg_ref, cwa_ref, cwg_ref, cba_ref, cbg_ref, wd_ref,
                y_ref, h_ref, acc_ref, ua_ref, ug_ref):
    i, j = pl.program_id(0), pl.program_id(1)
    tm = x_ref.shape[0]

    @pl.when(j == 0)
    def _():
        halo = _rms(xh_ref[...], g_ref[...])
        h_ref[0:MLP_HALO, :] = jnp.where(i > 0, halo, 0.0).astype(BF16)
        h_ref[MLP_HALO:, :] = _rms(x_ref[...], g_ref[...]).astype(BF16)
        acc_ref[...] = jnp.zeros_like(acc_ref)

    h = h_ref[...]

    def conv(w_ref, cw_ref, cb_ref, u_ref):
        u_ref[...] = jnp.dot(h, w_ref[...], preferred_element_type=F32)
        cw = cw_ref[...]
        return (cw[0:1] * u_ref[MLP_HALO - 2:MLP_HALO - 2 + tm, :] + cw[1:2] * u_ref[MLP_HALO - 1:MLP_HALO - 1 + tm, :]
                + cw[2:3] * u_ref[MLP_HALO:MLP_HALO + tm, :] + cb_ref[...])

    a = conv(wa_ref, cwa_ref, cba_ref, ua_ref)
    gt = conv(wg_ref, cwg_ref, cbg_ref, ug_ref)
    act = (jax.nn.silu(gt) * a).astype(BF16)
    acc_ref[...] += jnp.dot(act, wd_ref[...], preferred_element_type=F32)

    @pl.when(j == pl.num_programs(1) - 1)
    def _():
        y_ref[...] = x_ref[...] + acc_ref[...]


def _mlp(x, g, w_up, conv_w, conv_b, w_down):
    t, d = x.shape
    f = w_down.shape[0]
    tm = _pick_tile(t, (512, 256, 128))
    tf = _pick_tile(f, (1408, 256, 128))
    nf = f // tf
    hb = tm // MLP_HALO
    return pl.pallas_call(
        _mlp_kernel,
        grid=(t // tm, nf),
        in_specs=[pl.BlockSpec((tm, d), lambda i, j: (i, 0)),
                  pl.BlockSpec((MLP_HALO, d), lambda i, j: (jnp.maximum(i * hb - 1, 0), 0)),
                  pl.BlockSpec((1, d), lambda i, j: (0, 0)),
                  pl.BlockSpec((d, tf), lambda i, j: (0, j)),
                  pl.BlockSpec((d, tf), lambda i, j: (0, nf + j)),
                  pl.BlockSpec((3, tf), lambda i, j: (0, j)),
                  pl.BlockSpec((3, tf), lambda i, j: (0, nf + j)),
                  pl.BlockSpec((1, tf), lambda i, j: (0, j)),
                  pl.BlockSpec((1, tf), lambda i, j: (0, nf + j)),
                  pl.BlockSpec((tf, d), lambda i, j: (j, 0))],
        out_specs=pl.BlockSpec((tm, d), lambda i, j: (i, 0)),
        out_shape=jax.ShapeDtypeStruct((t, d), F32),
        scratch_shapes=[pltpu.VMEM((tm + MLP_HALO, d), BF16), pltpu.VMEM((tm, d), F32),
                        pltpu.VMEM((tm + MLP_HALO, tf), F32), pltpu.VMEM((tm + MLP_HALO, tf), F32)],
        compiler_params=_cparams(("parallel", "arbitrary")),
        name="conv_glu_mlp",
    )(x, x, g.reshape(1, d), w_up, w_up, conv_w, conv_w, conv_b, conv_b, w_down)


def _rmsnorm_kernel(x_ref, g_ref, y_ref):
    y_ref[...] = _rms(x_ref[...], g_ref[...])


def _rmsnorm(x, g):
    t, d = x.shape
    tm = _pick_tile(t, (1024, 512, 256, 128))
    return pl.pallas_call(
        _rmsnorm_kernel,
        grid=(t // tm,),
        in_specs=[pl.BlockSpec((tm, d), lambda i: (i, 0)), pl.BlockSpec((1, d), lambda i: (0, 0))],
        out_specs=pl.BlockSpec((tm, d), lambda i: (i, 0)),
        out_shape=jax.ShapeDtypeStruct((t, d), F32),
        compiler_params=_cparams(("parallel",)),
        name="final_rmsnorm",
    )(x, g.reshape(1, d))


def _swap_halves(q_pair):
    return pltpu.roll(q_pair.astype(F32), HEAD_DIM, 1).astype(BF16)


def _lane_half(shape):
    return lax.broadcasted_iota(jnp.int32, shape, 1) // HEAD_DIM


def _rows_to_heads(o_t, n_heads, tq):
    stacked = jnp.concatenate([o_t[:, r * tq:(r + 1) * tq] for r in range(n_heads)], axis=0)
    return stacked.T


def _swa_kernel(q_ref, kp_ref, kc_ref, vp_ref, vc_ref, bias_ref, sink_ref, o_ref):
    tq = q_ref.shape[0]
    rows = SWA_GROUP * tq
    k_ext = jnp.concatenate([kp_ref[...], kc_ref[...]], axis=0)
    v_ext = jnp.concatenate([vp_ref[...], vc_ref[...]], axis=1)
    half = _lane_half((tq, LANES))
    outs = []
    for g in range(SWA_KV_HEADS):
        parts = []
        for r in range(SWA_GROUP):
            hd = g * SWA_GROUP + r
            q_pair = q_ref[:, (hd // 2) * LANES:(hd // 2 + 1) * LANES]
            src = q_pair if hd % 2 == g else _swap_halves(q_pair)
            parts.append(jnp.where(half == g, src, jnp.zeros_like(src)))
        qp = jnp.concatenate(parts, axis=0)
        s = lax.dot_general(k_ext, qp, (((1,), (1,)), ((), ())), preferred_element_type=F32)
        s = s * (HEAD_DIM ** -0.5) + bias_ref[0, g]
        sink = sink_ref[g]
        m = jnp.maximum(jnp.max(s, axis=0, keepdims=True), sink)
        e = jnp.exp(s - m)
        l = jnp.sum(e, axis=0, keepdims=True) + jnp.exp(sink - m)
        o_t = jnp.dot(v_ext[g * HEAD_DIM:(g + 1) * HEAD_DIM], e.astype(BF16), preferred_element_type=F32)
        outs.append(_rows_to_heads(o_t / l, SWA_GROUP, tq))
    o_ref[...] = jnp.concatenate(outs, axis=1).astype(o_ref.dtype)


def _swa_bias():
    tq = SWA_TQ
    slopes = _alibi_slopes().reshape(SWA_KV_HEADS, SWA_GROUP)
    x = np.arange(2 * tq)[:, None] - tq
    q = np.arange(tq)[None, :]
    dist = q - x
    valid = (dist >= 0) & (dist < SWA_WINDOW)
    out = np.empty((2, SWA_KV_HEADS, 2 * tq, SWA_GROUP * tq), np.float32)
    for var in range(2):
        v = valid & ((x >= 0) | (var == 1))
        for g in range(SWA_KV_HEADS):
            for r in range(SWA_GROUP):
                out[var, g, :, r * tq:(r + 1) * tq] = np.where(v, -slopes[g, r] * dist, MASKED)
    return out


def _swa_attention(qkv, v_t, sinks):
    t = qkv.shape[0]
    tq = SWA_TQ
    kcol = Q_DIM // LANES
    bias = jnp.asarray(_swa_bias())
    sink_rows = jnp.repeat(sinks.astype(F32).reshape(SWA_KV_HEADS, 1, SWA_GROUP), tq, axis=2)
    prev = lambda i: jnp.maximum(i - 1, 0)
    return pl.pallas_call(
        _swa_kernel,
        grid=(t // tq,),
        in_specs=[pl.BlockSpec((tq, Q_DIM), lambda i: (i, 0)),
                  pl.BlockSpec((tq, LANES), lambda i: (prev(i), kcol)),
                  pl.BlockSpec((tq, LANES), lambda i: (i, kcol)),
                  pl.BlockSpec((LANES, tq), lambda i: (0, prev(i))),
                  pl.BlockSpec((LANES, tq), lambda i: (0, i)),
                  pl.BlockSpec((1,) + bias.shape[1:], lambda i: (jnp.minimum(i, 1), 0, 0, 0)),
                  pl.BlockSpec(sink_rows.shape, lambda i: (0, 0, 0))],
        out_specs=pl.BlockSpec((tq, Q_DIM), lambda i: (i, 0)),
        out_shape=jax.ShapeDtypeStruct((t, Q_DIM), BF16),
        compiler_params=_cparams(("parallel",)),
        name="swa_attention",
    )(qkv, qkv, qkv, v_t, v_t, bias, sink_rows)


def _swa_layer(x, norm_g, w_in, sinks, w_out):
    qkv = _norm_proj(x, norm_g, w_in.astype(BF16), BF16)
    v_t = qkv[:, Q_DIM + SWA_KV_HEADS * HEAD_DIM:].T
    o = _swa_attention(qkv, v_t, sinks)
    return _out_proj(x, o, w_out.astype(BF16))


FOX_PARTS = 3


def _split_bf16(v):
    parts, rest = [], v
    for _ in range(FOX_PARTS):
        p = rest.astype(BF16)
        parts.append(p)
        rest = rest - p.astype(F32)
    return parts


def _fox_prep_kernel(f_ref, b_ref, aug_ref, cref_ref, carry_ref):
    tm = f_ref.shape[0]

    @pl.when(pl.program_id(0) == 0)
    def _():
        carry_ref[...] = jnp.zeros_like(carry_ref)

    logf = jax.nn.log_sigmoid(f_ref[...] + b_ref[...])
    row = lax.broadcasted_iota(jnp.int32, (tm, tm), 0)
    col = lax.broadcasted_iota(jnp.int32, (tm, tm), 1)
    tri = jnp.where(row >= col, 1.0, 0.0).astype(BF16)
    local = sum(jnp.dot(tri, p, preferred_element_type=F32) for p in _split_bf16(logf))
    cum = local + carry_ref[...]
    carry_ref[...] = cum[tm - 1:tm, :]
    first = cum[0:1, :]
    cref_ref[0] = jnp.broadcast_to(first, cref_ref.shape[1:])
    hi, mid, lo = _split_bf16(first - cum)
    lane = lax.broadcasted_iota(jnp.int32, (tm, LANES), 1)
    zero = jnp.zeros_like(hi)
    aug_ref[...] = jnp.where(lane < N_HEADS, hi, jnp.where(lane < 2 * N_HEADS, mid,
                                                            jnp.where(lane < 3 * N_HEADS, lo, zero)))


def _fox_prep(f_logit3, f_bias3):
    t = f_logit3.shape[0]
    nt = t // FOX_T
    return pl.pallas_call(
        _fox_prep_kernel,
        grid=(nt,),
        in_specs=[pl.BlockSpec((FOX_T, LANES), lambda i: (i, 0)), pl.BlockSpec((1, LANES), lambda i: (0, 0))],
        out_specs=[pl.BlockSpec((FOX_T, LANES), lambda i: (i, 0)), pl.BlockSpec((1, 8, LANES), lambda i: (i, 0, 0))],
        out_shape=[jax.ShapeDtypeStruct((t, LANES), BF16), jax.ShapeDtypeStruct((nt, 8, LANES), F32)],
        scratch_shapes=[pltpu.VMEM((1, LANES), F32)],
        compiler_params=_cparams(("arbitrary",)),
        name="fox_prep",
    )(f_logit3, f_bias3)


def _fox_kernel(cref_ref, q_ref, k_ref, aug_ref, vt_ref, o_ref, m_ref, l_ref, acc_ref, qt_ref, s_ref):
    pair, qi = pl.program_id(0), pl.program_id(1)
    tq = q_ref.shape[0]
    tk = FOX_T
    lane = lax.broadcasted_iota(jnp.int32, (tq, LANES), 1)
    q2 = q_ref[...] * jnp.asarray(HEAD_DIM ** -0.5, BF16)
    for hh in range(2):
        head = 2 * pair + hh
        pick = (lane % N_HEADS == head) & (lane < FOX_PARTS * N_HEADS)
        q_ext = jnp.concatenate([jnp.where(lane // HEAD_DIM == hh, q2, jnp.zeros_like(q2)).astype(F32),
                                 jnp.where(pick, 1.0, 0.0)], axis=1)
        qt_ref[hh] = q_ext.T.astype(BF16)
    m_ref[...] = jnp.full_like(m_ref, M_INIT)
    l_ref[...] = jnp.zeros_like(l_ref)
    acc_ref[...] = jnp.zeros_like(acc_ref)
    ratio = tq // tk

    def scores(kj, hh):
        start = pl.multiple_of(kj * tk, tk)
        k_ext = jnp.concatenate([k_ref[pl.ds(start, tk), :], aug_ref[pl.ds(start, tk), :]], axis=1)
        s_ref[hh] = jnp.dot(k_ext, qt_ref[hh], preferred_element_type=F32)

    def consume(kj, hh, key_offset):
        head = 2 * pair + hh
        s = s_ref[hh]
        if key_offset is not None:
            key = lax.broadcasted_iota(jnp.int32, (tk, tq), 0) + key_offset
            qry = lax.broadcasted_iota(jnp.int32, (tk, tq), 1)
            s = jnp.where(key <= qry, s, MASKED)
        off = cref_ref[qi * ratio, head] - cref_ref[kj, head]
        m_old = m_ref[hh]
        m_new = jnp.maximum(m_old, jnp.max(s, axis=0, keepdims=True) + off)
        e = jnp.exp(s - (m_new - off))
        alpha = jnp.exp(m_old - m_new)
        m_ref[hh] = m_new
        l_ref[hh] = alpha * l_ref[hh] + jnp.sum(e, axis=0, keepdims=True)
        v_t = vt_ref[kj, hh * HEAD_DIM:(hh + 1) * HEAD_DIM, :]
        acc_ref[hh] = alpha * acc_ref[hh] + jnp.dot(v_t, e.astype(BF16), preferred_element_type=F32)

    n_full = qi * ratio
    scores(0, 0)

    def body(j, carry):
        for d in range(ratio):
            kj = j * ratio + d
            scores(kj, 1)
            consume(kj, 0, None)
            scores(kj + 1, 0)
            consume(kj, 1, None)
        return carry

    lax.fori_loop(0, qi, body, 0)
    for d in range(ratio):
        kj = n_full + d
        scores(kj, 1)
        consume(kj, 0, d * tk)
        if d + 1 < ratio:
            scores(kj + 1, 0)
        consume(kj, 1, d * tk)
    o_t = jnp.concatenate([acc_ref[hh] / l_ref[hh] for hh in range(2)], axis=0)
    o_ref[...] = o_t.T.astype(o_ref.dtype)


def _fox_attention(qkv, aug, v_t3, cref):
    t = qkv.shape[0]
    tq = FOX_TQ
    nk = t // FOX_T
    kcol = Q_DIM // LANES
    return pl.pallas_call(
        _fox_kernel,
        grid=(N_HEADS // 2, t // tq),
        in_specs=[pl.BlockSpec(memory_space=pltpu.SMEM),
                  pl.BlockSpec((tq, LANES), lambda p, i: (i, p)),
                  pl.BlockSpec((t, LANES), lambda p, i: (0, kcol + p)),
                  pl.BlockSpec((t, LANES), lambda p, i: (0, 0)),
                  pl.BlockSpec((nk, LANES, FOX_T), lambda p, i: (0, p, 0))],
        out_specs=pl.BlockSpec((tq, LANES), lambda p, i: (i, p)),
        out_shape=jax.ShapeDtypeStruct((t, Q_DIM), BF16),
        scratch_shapes=[pltpu.VMEM((2, 1, tq), F32), pltpu.VMEM((2, 1, tq), F32),
                        pltpu.VMEM((2, HEAD_DIM, tq), F32), pltpu.VMEM((2, 2 * LANES, tq), BF16), pltpu.VMEM((2, FOX_T, tq), F32)],
        compiler_params=_cparams(("parallel", "arbitrary")),
        name="fox_attention",
    )(cref, qkv, qkv, aug, v_t3)


def _fox_layer(x, norm_g, w_in, f_bias, w_out):
    t = x.shape[0]
    qkv = _norm_proj(x, norm_g, w_in[:, :3 * Q_DIM].astype(BF16), BF16)
    w_f = w_in[:, 3 * Q_DIM:]
    pad = jnp.zeros((D_MODEL, LANES - FOX_PARTS * N_HEADS), w_f.dtype)
    w_f3 = jnp.concatenate([w_f] * FOX_PARTS + [pad], axis=1).astype(BF16)
    b3 = jnp.concatenate([f_bias.astype(F32)] * FOX_PARTS + [jnp.zeros((LANES - FOX_PARTS * N_HEADS,), F32)])
    f_logit3 = _norm_proj(x, norm_g, w_f3, F32)
    aug, cref = _fox_prep(f_logit3, b3.reshape(1, LANES))
    v_t3 = qkv[:, 2 * Q_DIM:].reshape(t // FOX_T, FOX_T, Q_DIM).transpose(0, 2, 1)
    o = _fox_attention(qkv, aug, v_t3, cref[:, 0, :N_HEADS])
    return _out_proj(x, o, w_out.astype(BF16))


def _nsa_compress_kernel(a_ref, pos_ref, w1_ref, w2_ref, o_ref):
    tc = a_ref.shape[2]
    half = CMP_STRIDE * HEAD_DIM
    pos = pos_ref[0]
    out = jnp.zeros(o_ref.shape[1:], F32)
    for gg in range(2):
        a = a_ref[0, gg]
        top = jnp.dot((a + pos[:, :half]).astype(BF16), w1_ref[0, :half], preferred_element_type=F32)
        bot = jnp.dot((a + pos[:, half:]).astype(BF16), w1_ref[0, half:], preferred_element_type=F32)
        hid = top + pltpu.roll(bot, tc - 1, 0)
        out = out + jnp.dot(jax.nn.gelu(hid).astype(BF16), w2_ref[0, gg], preferred_element_type=F32)
    o_ref[0] = out


def _nsa_compress(a, pos, w1, w2):
    _, g, tc, width = a.shape
    return pl.pallas_call(
        _nsa_compress_kernel,
        grid=(2, g // 2),
        in_specs=[pl.BlockSpec((1, 2, tc, width), lambda kv, gp: (kv, gp, 0, 0)),
                  pl.BlockSpec((1, 1, 2 * width), lambda kv, gp: (kv, 0, 0)),
                  pl.BlockSpec((1, 2 * width, CMP_HIDDEN), lambda kv, gp: (kv, 0, 0)),
                  pl.BlockSpec((1, 2, CMP_HIDDEN, LANES), lambda kv, gp: (kv, 0, 0, 0))],
        out_specs=pl.BlockSpec((1, tc, LANES), lambda kv, gp: (kv, 0, gp)),
        out_shape=jax.ShapeDtypeStruct((2, tc, g * HEAD_DIM), F32),
        compiler_params=_cparams(("parallel", "parallel")),
        name="nsa_compress",
    )(a, pos, w1, w2)


NSA_WIN_TILES = NSA_WINDOW // NSA_TQ + 1


def _nsa_kernel(slopes_ref, q_ref, ks_ref, *refs):
    kw_refs, refs = refs[:NSA_WIN_TILES], refs[NSA_WIN_TILES:]
    vst_ref, refs = refs[0], refs[1:]
    vw_refs, refs = refs[:NSA_WIN_TILES], refs[NSA_WIN_TILES:]
    (kcmp_ref, vov_ref, gates_ref, ksel_ext_ref, kwin_ext_ref, kcmp_ext_ref, qext_ref,
     o_ref, selt_ref, m_ref, l_ref, acc_ref, qt_ref, s_ref, scmp_ref, cm_ref, cl_ref, oc_ref,
     active_ref, list_ref) = refs
    g, qi = pl.program_id(0), pl.program_id(1)
    tq = q_ref.shape[0]
    tc = kcmp_ref.shape[0]
    n_sel = selt_ref.shape[0]
    rr = NSA_GROUP
    t0 = qi * tq
    hg = g % 2
    slope = [slopes_ref[g * rr + r] for r in range(rr)]

    half = _lane_half((tq, LANES))
    parts = []
    for r in range(rr):
        q_pair = q_ref[:, (r // 2) * LANES:(r // 2 + 1) * LANES]
        src = jnp.where(hg == r % 2, q_pair, _swap_halves(q_pair))
        parts.append(jnp.where(half == hg, src, jnp.zeros_like(src)) * jnp.asarray(HEAD_DIM ** -0.5, BF16))
    qp = jnp.concatenate(parts, axis=0)
    qt_ref[0:LANES, :] = qp.astype(F32).T.astype(BF16)

    @pl.when(qi == 0)
    def _():
        qt_ref[LANES:, :] = qext_ref[0]

    def lanes_of(s, r):
        return s[:, r * tq:(r + 1) * tq]

    chunk_span = NSA_CMP_CHUNK * CMP_STRIDE
    last_chunk = ((t0 + tq - CMP_BLOCK) // CMP_STRIDE) // NSA_CMP_CHUNK
    nl16_q = (CMP_STRIDE * lax.broadcasted_iota(jnp.int32, (NSA_CMP_CHUNK, tq), 0)
              - lax.broadcasted_iota(jnp.int32, (NSA_CMP_CHUNK, tq), 1)).astype(F32)
    cm_ref[...] = jnp.full_like(cm_ref, M_INIT)

    def cmp_rows(ch):
        return pl.ds(pl.multiple_of(ch * NSA_CMP_CHUNK, NSA_CMP_CHUNK), NSA_CMP_CHUNK)

    def cmp_offset(ch, r):
        return slope[r] * (ch * chunk_span).astype(F32)

    def cmp_scores(ch, masked):
        k_full = jnp.concatenate([kcmp_ref[cmp_rows(ch), :], kcmp_ext_ref[...]], axis=1)
        s = jnp.dot(k_full, qt_ref[...], preferred_element_type=F32)
        if masked:
            visible = nl16_q <= (t0 - (CMP_BLOCK - 1) - ch * chunk_span).astype(F32)
            s = jnp.concatenate([jnp.where(visible, lanes_of(s, r), MASKED) for r in range(rr)], axis=1)
        scmp_ref[cmp_rows(ch), :] = s
        for r in range(rr):
            cm_ref[r] = jnp.maximum(cm_ref[r], jnp.max(lanes_of(s, r), axis=0, keepdims=True) + cmp_offset(ch, r))

    def cmp_plain(ch, carry):
        cmp_scores(ch, False)
        return carry

    lax.fori_loop(0, jnp.maximum(last_chunk - 1, 0), cmp_plain, 0)

    @pl.when(last_chunk >= 1)
    def _():
        cmp_scores(last_chunk - 1, True)

    cmp_scores(last_chunk, True)

    cl_ref[...] = jnp.zeros_like(cl_ref)
    oc_ref[...] = jnp.zeros_like(oc_ref)

    def cmp_exp(ch, carry):
        s = scmp_ref[cmp_rows(ch), :]
        e_all = []
        for r in range(rr):
            e = jnp.exp(lanes_of(s, r) - (cm_ref[r] - cmp_offset(ch, r)))
            cl_ref[r] = cl_ref[r] + jnp.sum(e, axis=0, keepdims=True)
            e_all.append(e.astype(BF16))
        oc_ref[...] += jnp.dot(vov_ref[ch, 0], jnp.concatenate(e_all, axis=1), preferred_element_type=F32)
        return carry

    lax.fori_loop(0, last_chunk + 1, cmp_exp, 0)
    inv_l = [jnp.where(cl_ref[r] > 0.0, 1.0 / cl_ref[r], 0.0) for r in range(rr)]
    o_cmp = [oc_ref[0:HEAD_DIM, r * tq:(r + 1) * tq] * inv_l[r] for r in range(rr)]

    imp = functools.reduce(jnp.add, [oc_ref[HEAD_DIM:, r * tq:(r + 1) * tq] * inv_l[r] for r in range(rr)])
    blk = lax.broadcasted_iota(jnp.int32, (n_sel, tq), 0)
    cur = (t0 + lax.broadcasted_iota(jnp.int32, (n_sel, tq), 1)) // SEL_BLOCK
    blk_f = blk.astype(F32)
    forced = jnp.where(blk == 0, 1.0, jnp.where(blk == cur, 1.0, jnp.where(blk == cur - 1, 1.0, 0.0)))
    score = jnp.where(forced > 0.0, TAKEN, jnp.where(blk <= cur, imp, -1.0))

    def pick(_, sc):
        best = jnp.max(sc, axis=0, keepdims=True)
        first = jnp.min(jnp.where(sc == best, blk_f, float(n_sel)), axis=0, keepdims=True)
        return jnp.where(blk_f == first, TAKEN, sc)

    score = lax.fori_loop(0, N_SELECT - 3, pick, score)
    selt_ref[...] = jnp.where(blk <= cur, jnp.where(score == TAKEN, 1.0, 0.0), 0.0)

    m_ref[...] = jnp.full_like(m_ref, M_INIT)
    l_ref[...] = jnp.zeros_like(l_ref)
    acc_ref[...] = jnp.zeros_like(acc_ref)
    blocks_per_chunk = NSA_TK // SEL_BLOCK

    for c in range(n_sel // blocks_per_chunk):
        active_ref[c] = jnp.max(selt_ref[c * blocks_per_chunk:(c + 1) * blocks_per_chunk, :])


    def compact(c, n):
        list_ref[n] = c
        return n + (active_ref[c] > 0.0).astype(jnp.int32)

    n_active = lax.fori_loop(0, t0 // NSA_TK + 1, compact, 0)

    def sel_scores(c, buf):
        start = pl.multiple_of(c * NSA_TK, NSA_TK)
        rows = selt_ref[pl.ds(pl.multiple_of(c * blocks_per_chunk, blocks_per_chunk), blocks_per_chunk), :]
        mask_rows = jnp.concatenate([jnp.where(rows > 0.5, 0.0, MASKED), jnp.zeros_like(rows)], axis=0)
        first = LANES + buf * NSA_EXT_MASK_ROWS
        qt_ref[first:first + NSA_EXT_MASK_ROWS, :] = jnp.concatenate([mask_rows] * rr, axis=1).astype(BF16)
        k_full = jnp.concatenate([ks_ref[pl.ds(start, NSA_TK), :], ksel_ext_ref[buf]], axis=1)
        s_ref[buf] = jnp.dot(k_full, qt_ref[...], preferred_element_type=F32)

    def sel_consume(c, buf, diagonal):
        shift = (c * NSA_TK - t0).astype(F32)
        if diagonal:
            key = lax.broadcasted_iota(jnp.int32, (NSA_TK, tq), 0) + c * NSA_TK
            visible = key <= lax.broadcasted_iota(jnp.int32, (NSA_TK, tq), 1) + t0
        e_all, alpha_all = [], []
        for r in range(rr):
            off = slope[r] * shift
            s = s_ref[buf][:, r * tq:(r + 1) * tq]
            if diagonal:
                s = jnp.where(visible, s, MASKED)
            m_old = m_ref[r]
            m_new = jnp.maximum(m_old, jnp.max(s, axis=0, keepdims=True) + off)
            e = jnp.exp(s - (m_new - off))
            alpha = jnp.exp(m_old - m_new)
            m_ref[r] = m_new
            l_ref[r] = alpha * l_ref[r] + jnp.sum(e, axis=0, keepdims=True)
            e_all.append(e.astype(BF16))
            alpha_all.append(alpha)
        pv = jnp.dot(vst_ref[c], jnp.concatenate(e_all, axis=1), preferred_element_type=F32)
        acc_ref[...] = jnp.concatenate(alpha_all, axis=1) * acc_ref[...] + pv

    n_before = n_active - 1
    sel_scores(list_ref[0], 0)

    def sel_pair(j, carry):
        c0, c1, c2 = list_ref[2 * j], list_ref[2 * j + 1], list_ref[2 * j + 2]
        sel_scores(c1, 1)
        sel_consume(c0, 0, False)
        sel_scores(c2, 0)
        sel_consume(c1, 1, False)
        return carry

    lax.fori_loop(0, n_before // 2, sel_pair, 0)

    @pl.when(n_before % 2 == 1)
    def _():
        c0, c_last = list_ref[n_before - 1], list_ref[n_before]
        sel_scores(c_last, 1)
        sel_consume(c0, 0, False)
        sel_consume(c_last, 1, True)

    @pl.when(n_before % 2 == 0)
    def _():
        sel_consume(list_ref[n_before], 0, True)

    xw = lax.broadcasted_iota(jnp.int32, (tq, tq), 0)
    qw = lax.broadcasted_iota(jnp.int32, (tq, tq), 1)
    s_win, off_win = [], []
    for idx in range(NSA_WIN_TILES):
        back = NSA_WIN_TILES - 1 - idx
        k_full = jnp.concatenate([kw_refs[idx][...], kwin_ext_ref[...]], axis=1)
        s_t = jnp.dot(k_full, qt_ref[...], preferred_element_type=F32)
        per_head = [lanes_of(s_t, r) for r in range(rr)]
        if back == NSA_WIN_TILES - 1:
            per_head = [jnp.where(xw > qw, s, MASKED) for s in per_head]
        if back == 0:
            per_head = [jnp.where(xw <= qw, s, MASKED) for s in per_head]
        s_win.append(per_head)
        off_win.append([jnp.where(qi >= back, -slope[r] * float(back * tq), MASKED) for r in range(rr)])
    e_win, l_win = [], []
    for r in range(rr):
        m = functools.reduce(jnp.maximum, [jnp.max(s_win[idx][r], axis=0, keepdims=True) + off_win[idx][r]
                                           for idx in range(NSA_WIN_TILES)])
        e_tiles = [jnp.exp(s_win[idx][r] - (m - off_win[idx][r])) for idx in range(NSA_WIN_TILES)]
        l_win.append(functools.reduce(jnp.add, [jnp.sum(e, axis=0, keepdims=True) for e in e_tiles]))
        e_win.append(jnp.concatenate([e.astype(BF16) for e in e_tiles], axis=0))
    v_win = jnp.concatenate([vw_refs[idx][...] for idx in range(NSA_WIN_TILES)], axis=1)
    o_win_all = jnp.dot(v_win, jnp.concatenate(e_win, axis=1), preferred_element_type=F32)
    o_win = [lanes_of(o_win_all, r) / l_win[r] for r in range(rr)]

    gate = jax.nn.sigmoid(gates_ref[0])
    merged = []
    for r in range(rr):
        o_sel = lanes_of(acc_ref[...], r) / l_ref[r]
        merged.append(gate[3 * r:3 * r + 1] * o_cmp[r] + gate[3 * r + 1:3 * r + 2] * o_sel
                      + gate[3 * r + 2:3 * r + 3] * o_win[r])
    o_ref[...] = jnp.concatenate(merged, axis=0).T.astype(o_ref.dtype)


def _nsa_overlap_t(n_sel, tc):
    cmp_start = np.arange(tc - 1) * CMP_STRIDE
    sel_start = np.arange(n_sel) * SEL_BLOCK
    ov = np.clip(np.minimum(cmp_start[None, :] + CMP_BLOCK, sel_start[:, None] + SEL_BLOCK)
                 - np.maximum(cmp_start[None, :], sel_start[:, None]), 0, None) / CMP_BLOCK
    return np.concatenate([ov, np.zeros((n_sel, 1))], axis=1).astype(np.float32)


NSA_EXT_MASK_ROWS = 16
NSA_EXT_SEL = 32
NSA_EXT_WIN = 48
NSA_EXT_CMP = 64
NSA_CMP_CHUNK = 256
SLOPE_PIECES = 3


def _bf16_pieces(v):
    out, rest = [], np.asarray(v, np.float32)
    for _ in range(SLOPE_PIECES):
        p = rest.astype(BF16).astype(np.float32)
        out.append(p)
        rest = rest - p
    return out


def _nsa_extensions(tq):
    ksel = np.zeros((2, NSA_TK, LANES), np.float32)
    x = np.arange(NSA_TK)
    for buf in range(2):
        ksel[buf, x, buf * NSA_EXT_MASK_ROWS + x // SEL_BLOCK] = 1.0
    for j in range(SLOPE_PIECES):
        ksel[:, :, NSA_EXT_SEL + j] = 256 * (x // 256)
        ksel[:, :, NSA_EXT_SEL + SLOPE_PIECES + j] = x % 256
    kwin = np.zeros((tq, LANES), np.float32)
    for j in range(SLOPE_PIECES):
        kwin[:, NSA_EXT_WIN + j] = np.arange(tq)
    assert tq <= 256, "a window key offset must be exact in bf16"
    kcmp = np.zeros((NSA_CMP_CHUNK, LANES), np.float32)
    for j in range(SLOPE_PIECES):
        kcmp[:, NSA_EXT_CMP + j] = np.arange(NSA_CMP_CHUNK)
    pieces = _bf16_pieces(_alibi_slopes())
    qext = np.zeros((NSA_KV_HEADS, LANES, NSA_GROUP * tq), np.float32)
    for g in range(NSA_KV_HEADS):
        for r in range(NSA_GROUP):
            cols = slice(r * tq, (r + 1) * tq)
            for j in range(SLOPE_PIECES):
                p = pieces[j][g * NSA_GROUP + r]
                qext[g, NSA_EXT_SEL + j, cols] = p
                qext[g, NSA_EXT_SEL + SLOPE_PIECES + j, cols] = p
                qext[g, NSA_EXT_WIN + j, cols] = p
                qext[g, NSA_EXT_CMP + j, cols] = CMP_STRIDE * p
    return jnp.asarray(ksel, BF16), jnp.asarray(kwin, BF16), jnp.asarray(kcmp, BF16), jnp.asarray(qext, BF16)


def _nsa_attention(main, vs_t3, vw_t, k_cmp, v_cmp_t, gates_t):
    t = main.shape[0]
    tq = NSA_TQ
    tc = k_cmp.shape[0]
    n_sel = t // SEL_BLOCK
    nc = t // NSA_TK
    ks_col = Q_DIM // LANES
    kw_col = ks_col + NSA_KV_DIM // LANES
    n_ch = tc // NSA_CMP_CHUNK
    ov_t = jnp.asarray(_nsa_overlap_t(n_sel, tc).reshape(n_sel, n_ch, NSA_CMP_CHUNK).transpose(1, 0, 2), BF16)
    v_cmp_t = v_cmp_t.reshape(NSA_KV_HEADS, HEAD_DIM, n_ch, NSA_CMP_CHUNK).transpose(2, 0, 1, 3)
    vov = jnp.concatenate([v_cmp_t, jnp.broadcast_to(ov_t[:, None], (n_ch, NSA_KV_HEADS, n_sel, NSA_CMP_CHUNK))],
                          axis=2)
    slopes = jnp.asarray(_alibi_slopes())
    ksel_ext, kwin_ext, kcmp_ext, qext = _nsa_extensions(tq)

    def kw_spec(back):
        return pl.BlockSpec((tq, LANES), lambda g, i: (jnp.maximum(i - back, 0), kw_col + g // 2))

    def vw_spec(back):
        return pl.BlockSpec((HEAD_DIM, tq), lambda g, i: (g, jnp.maximum(i - back, 0)))

    backs = list(range(NSA_WIN_TILES - 1, -1, -1))
    return pl.pallas_call(
        _nsa_kernel,
        grid=(NSA_KV_HEADS, t // tq),
        in_specs=[pl.BlockSpec(memory_space=pltpu.SMEM),
                  pl.BlockSpec((tq, NSA_GROUP * HEAD_DIM), lambda g, i: (i, g)),
                  pl.BlockSpec((t, LANES), lambda g, i: (0, ks_col + g // 2))]
                 + [kw_spec(b) for b in backs]
                 + [pl.BlockSpec((nc, HEAD_DIM, NSA_TK), lambda g, i: (0, g, 0))]
                 + [vw_spec(b) for b in backs]
                 + [pl.BlockSpec((tc, LANES), lambda g, i: (0, g // 2)),
                    pl.BlockSpec((n_ch, 1, HEAD_DIM + n_sel, NSA_CMP_CHUNK), lambda g, i: (0, g, 0, 0)),
                    pl.BlockSpec((1, 16, tq), lambda g, i: (g, 0, i)),
                    pl.BlockSpec(ksel_ext.shape, lambda g, i: (0, 0, 0)),
                    pl.BlockSpec(kwin_ext.shape, lambda g, i: (0, 0)),
                    pl.BlockSpec(kcmp_ext.shape, lambda g, i: (0, 0)),
                    pl.BlockSpec((1,) + qext.shape[1:], lambda g, i: (g, 0, 0))],
        out_specs=pl.BlockSpec((tq, NSA_GROUP * HEAD_DIM), lambda g, i: (i, g)),
        out_shape=jax.ShapeDtypeStruct((t, Q_DIM), BF16),
        scratch_shapes=[pltpu.VMEM((n_sel, tq), F32), pltpu.VMEM((NSA_GROUP, 1, tq), F32),
                        pltpu.VMEM((NSA_GROUP, 1, tq), F32), pltpu.VMEM((HEAD_DIM, NSA_GROUP * tq), F32),
                        pltpu.VMEM((2 * LANES, NSA_GROUP * tq), BF16),
                        pltpu.VMEM((2, NSA_TK, NSA_GROUP * tq), F32),
                        pltpu.VMEM((tc, NSA_GROUP * tq), F32), pltpu.VMEM((NSA_GROUP, 1, tq), F32),
                        pltpu.VMEM((NSA_GROUP, 1, tq), F32), pltpu.VMEM((HEAD_DIM + n_sel, NSA_GROUP * tq), F32),
                        pltpu.SMEM((nc,), F32), pltpu.SMEM((nc,), jnp.int32)],
        compiler_params=_cparams(("parallel", "arbitrary")),
        name="nsa_attention",
    )(slopes, main, main, *([main] * NSA_WIN_TILES), vs_t3, *([vw_t] * NSA_WIN_TILES), k_cmp, vov, gates_t,
      ksel_ext, kwin_ext, kcmp_ext, qext)


def _nsa_layer(x, norm_g, w_in, cmp_pos, cmp_w1, cmp_w2, w_out):
    t = x.shape[0]
    kv = NSA_KV_DIM
    col = lambda i: w_in[:, Q_DIM + i * kv:Q_DIM + (i + 1) * kv]
    w_main = jnp.concatenate([w_in[:, :Q_DIM], col(2), col(4), col(3), col(5)], axis=1).astype(BF16)
    w_cmp = jnp.concatenate([col(0), col(1)], axis=1).astype(BF16)
    n_gate = 3 * N_HEADS
    w_gate = jnp.concatenate([w_in[:, Q_DIM + 6 * kv:], jnp.zeros((D_MODEL, LANES - n_gate), w_in.dtype)],
                             axis=1).astype(BF16)
    main = _norm_proj(x, norm_g, w_main, BF16)
    kcvc = _norm_proj(x, norm_g, w_cmp, F32)
    gate_logit = _norm_proj(x, norm_g, w_gate, F32)

    tc = t // CMP_STRIDE
    a = kcvc.reshape(tc, CMP_STRIDE, 2, NSA_KV_HEADS, HEAD_DIM).transpose(2, 3, 0, 1, 4)
    a = a.reshape(2, NSA_KV_HEADS, tc, CMP_STRIDE * HEAD_DIM)
    zeros = jnp.zeros_like(cmp_w2)
    w2 = jnp.stack([jnp.concatenate([cmp_w2, zeros], axis=2), jnp.concatenate([zeros, cmp_w2], axis=2)], axis=1)
    cmp = _nsa_compress(a, cmp_pos.reshape(2, 1, CMP_BLOCK * HEAD_DIM), cmp_w1.astype(BF16), w2.astype(BF16))
    k_cmp = cmp[0].astype(BF16)
    v_cmp_t = cmp[1].T.astype(BF16)

    off_vs = Q_DIM + 2 * kv
    vs_t3 = main[:, off_vs:off_vs + kv].reshape(t // NSA_TK, NSA_TK, kv).transpose(0, 2, 1)
    vw_t = main[:, off_vs + kv:].T
    gates_t = gate_logit[:, :n_gate].reshape(t, NSA_KV_HEADS, 3 * NSA_GROUP).transpose(1, 2, 0)
    gates_t = jnp.pad(gates_t, ((0, 0), (0, 16 - 3 * NSA_GROUP), (0, 0)))
    o = _nsa_attention(main, vs_t3, vw_t, k_cmp, v_cmp_t, gates_t)
    return _out_proj(x, o, w_out.astype(BF16))


def kernel(x, attn_norm, mlp_norm, final_norm, nsa_w_in, nsa_cmp_pos, nsa_cmp_w1, nsa_cmp_w2, nsa_w_out,
           swa_w_in, swa_sinks, swa_w_out, fox_w_in, fox_f_bias, fox_w_out,
           mlp_w_up, mlp_conv_w, mlp_conv_b, mlp_w_down):
    assert x.shape[0] == 1, "the trunk is written for batch 1"
    h = x[0]
    for i in range(DEPTH):
        kind, j = i % N_MIXERS, i // N_MIXERS
        if kind == 0:
            h = _nsa_layer(h, attn_norm[i], nsa_w_in[j], nsa_cmp_pos[j], nsa_cmp_w1[j], nsa_cmp_w2[j], nsa_w_out[j])
        elif kind == 1:
            h = _swa_layer(h, attn_norm[i], swa_w_in[j], swa_sinks[j], swa_w_out[j])
        else:
            h = _fox_layer(h, attn_norm[i], fox_w_in[j], fox_f_bias[j], fox_w_out[j])
        h = _mlp(h, mlp_norm[i], mlp_w_up[i].astype(BF16), mlp_conv_w[i], mlp_conv_b[i].reshape(1, -1),
                 mlp_w_down[i].astype(BF16))
    return _rmsnorm(h, final_norm)[None]
```

```python
import functools

import numpy as np
import jax
import jax.numpy as jnp
from jax import lax
from jax.experimental import pallas as pl
from jax.experimental.pallas import tpu as pltpu

F32 = jnp.float32
BF16 = jnp.bfloat16

D_MODEL = 1024
DEPTH = 4
N_MIXERS = 3
HEAD_DIM = 64
N_HEADS = 16
Q_DIM = N_HEADS * HEAD_DIM
ALIBI_MAX = 8.0
NORM_EPS = 1e-6
FORCE_SCORE = 1e4
MASKED = -2e30
M_INIT = -1e30
TAKEN = -(2.0 ** 127)

NSA_KV_HEADS = 4
NSA_GROUP = 4
NSA_KV_DIM = NSA_KV_HEADS * HEAD_DIM
CMP_BLOCK = 32
CMP_STRIDE = 16
CMP_HIDDEN = 256
SEL_BLOCK = 64
N_SELECT = 16
NSA_WINDOW = 512
NSA_TQ = 256
NSA_TK = 512

SWA_KV_HEADS = 2
SWA_GROUP = 8
SWA_WINDOW = 128
SWA_TQ = 128

FOX_TQ = 1024
FOX_T = 512

D_FF = 2816
LANES = 128
VMEM_LIMIT = 56 * 1024 * 1024


def _cparams(semantics, vmem=VMEM_LIMIT):
    return pltpu.CompilerParams(dimension_semantics=semantics, vmem_limit_bytes=vmem)


def _alibi_slopes():
    return np.asarray(2.0 ** (-ALIBI_MAX * np.arange(1, N_HEADS + 1) / N_HEADS), dtype=np.float32)


def _rms(x, g):
    ms = jnp.mean(x * x, axis=-1, keepdims=True)
    return x * lax.rsqrt(ms + NORM_EPS) * g


def _norm_proj_kernel(x_ref, g_ref, w_ref, o_ref, h_ref):
    @pl.when(pl.program_id(1) == 0)
    def _():
        h_ref[...] = _rms(x_ref[...], g_ref[...]).astype(BF16)

    o_ref[...] = jnp.dot(h_ref[...], w_ref[...], preferred_element_type=F32).astype(o_ref.dtype)


def _pick_tile(n, candidates):
    for c in candidates:
        if n % c == 0:
            return c
    raise ValueError(f"no tile for {n}")


def _norm_proj(x, g, w, out_dtype):
    t, d = x.shape
    n = w.shape[1]
    tm = _pick_tile(t, (1024, 512, 256, 128))
    tn = _pick_tile(n, (3072, 2048, 1536, 1280, 1024, 768, 512, 256, 128))
    return pl.pallas_call(
        _norm_proj_kernel,
        grid=(t // tm, n // tn),
        in_specs=[pl.BlockSpec((tm, d), lambda i, j: (i, 0)),
                  pl.BlockSpec((1, d), lambda i, j: (0, 0)),
                  pl.BlockSpec((d, tn), lambda i, j: (0, j))],
        out_specs=pl.BlockSpec((tm, tn), lambda i, j: (i, j)),
        out_shape=jax.ShapeDtypeStruct((t, n), out_dtype),
        scratch_shapes=[pltpu.VMEM((tm, d), BF16)],
        compiler_params=_cparams(("parallel", "arbitrary")),
        name="norm_proj",
    )(x, g.reshape(1, d), w)


def _out_proj_kernel(x_ref, o_ref, w_ref, y_ref):
    y_ref[...] = x_ref[...] + jnp.dot(o_ref[...], w_ref[...], preferred_element_type=F32)


def _out_proj(x, o, w):
    t, d = x.shape
    tm = _pick_tile(t, (1024, 512, 256, 128))
    return pl.pallas_call(
        _out_proj_kernel,
        grid=(t // tm,),
        in_specs=[pl.BlockSpec((tm, d), lambda i: (i, 0)),
                  pl.BlockSpec((tm, o.shape[1]), lambda i: (i, 0)),
                  pl.BlockSpec(w.shape, lambda i: (0, 0))],
        out_specs=pl.BlockSpec((tm, d), lambda i: (i, 0)),
        out_shape=jax.ShapeDtypeStruct((t, d), F32),
        compiler_params=_cparams(("parallel",)),
        name="out_proj",
    )(x, o, w)


MLP_HALO = 16


def _mlp_kernel(x_ref, xh_ref, g_ref, wa_ref, wg_ref, cwa_ref, cwg_ref, cba_ref, cbg_ref, wd_ref,
                y_ref, h_ref, acc_ref, ua_ref, ug_ref):
    i, j = pl.program_id(0), pl.program_id(1)
    tm = x_ref.shape[0]

    @pl.when(j == 0)
    def _():
        halo = _rms(xh_ref[...], g_ref[...])
        h_ref[0:MLP_HALO, :] = jnp.where(i > 0, halo, 0.0).astype(BF16)
        h_ref[MLP_HALO:, :] = _rms(x_ref[...], g_ref[...]).astype(BF16)
        acc_ref[...] = jnp.zeros_like(acc_ref)

    h = h_ref[...]

    def conv(w_ref, cw_ref, cb_ref, u_ref):
        u_ref[...] = jnp.dot(h, w_ref[...], preferred_element_type=F32)
        cw = cw_ref[...]
        return (cw[0:1] * u_ref[MLP_HALO - 2:MLP_HALO - 2 + tm, :] + cw[1:2] * u_ref[MLP_HALO - 1:MLP_HALO - 1 + tm, :]
                + cw[2:3] * u_ref[MLP_HALO:MLP_HALO + tm, :] + cb_ref[...])

    a = conv(wa_ref, cwa_ref, cba_ref, ua_ref)
    gt = conv(wg_ref, cwg_ref, cbg_ref, ug_ref)
    act = (jax.nn.silu(gt) * a).astype(BF16)
    acc_ref[...] += jnp.dot(act, wd_ref[...], preferred_element_type=F32)

    @pl.when(j == pl.num_programs(1) - 1)
    def _():
        y_ref[...] = x_ref[...] + acc_ref[...]


def _mlp(x, g, w_up, conv_w, conv_b, w_down):
    t, d = x.shape
    f = w_down.shape[0]
    tm = _pick_tile(t, (512, 256, 128))
    tf = _pick_tile(f, (1408, 256, 128))
    nf = f // tf
    hb = tm // MLP_HALO
    return pl.pallas_call(
        _mlp_kernel,
        grid=(t // tm, nf),
        in_specs=[pl.BlockSpec((tm, d), lambda i, j: (i, 0)),
                  pl.BlockSpec((MLP_HALO, d), lambda i, j: (jnp.maximum(i * hb - 1, 0), 0)),
                  pl.BlockSpec((1, d), lambda i, j: (0, 0)),
                  pl.BlockSpec((d, tf), lambda i, j: (0, j)),
                  pl.BlockSpec((d, tf), lambda i, j: (0, nf + j)),
                  pl.BlockSpec((3, tf), lambda i, j: (0, j)),
                  pl.BlockSpec((3, tf), lambda i, j: (0, nf + j)),
                  pl.BlockSpec((1, tf), lambda i, j: (0, j)),
                  pl.BlockSpec((1, tf), lambda i, j: (0, nf + j)),
                  pl.BlockSpec((tf, d), lambda i, j: (j, 0))],
        out_specs=pl.BlockSpec((tm, d), lambda i, j: (i, 0)),
        out_shape=jax.ShapeDtypeStruct((t, d), F32),
        scratch_shapes=[pltpu.VMEM((tm + MLP_HALO, d), BF16), pltpu.VMEM((tm, d), F32),
                        pltpu.VMEM((tm + MLP_HALO, tf), F32), pltpu.VMEM((tm + MLP_HALO, tf), F32)],
        compiler_params=_cparams(("parallel", "arbitrary")),
        name="conv_glu_mlp",
    )(x, x, g.reshape(1, d), w_up, w_up, conv_w, conv_w, conv_b, conv_b, w_down)


def _rmsnorm_kernel(x_ref, g_ref, y_ref):
    y_ref[...] = _rms(x_ref[...], g_ref[...])


def _rmsnorm(x, g):
    t, d = x.shape
    tm = _pick_tile(t, (1024, 512, 256, 128))
    return pl.pallas_call(
        _rmsnorm_kernel,
        grid=(t // tm,),
        in_specs=[pl.BlockSpec((tm, d), lambda i: (i, 0)), pl.BlockSpec((1, d), lambda i: (0, 0))],
        out_specs=pl.BlockSpec((tm, d), lambda i: (i, 0)),
        out_shape=jax.ShapeDtypeStruct((t, d), F32),
        compiler_params=_cparams(("parallel",)),
        name="final_rmsnorm",
    )(x, g.reshape(1, d))


def _swap_halves(q_pair):
    return pltpu.roll(q_pair.astype(F32), HEAD_DIM, 1).astype(BF16)


def _lane_half(shape):
    return lax.broadcasted_iota(jnp.int32, shape, 1) // HEAD_DIM


def _rows_to_heads(o_t, n_heads, tq):
    stacked = jnp.concatenate([o_t[:, r * tq:(r + 1) * tq] for r in range(n_heads)], axis=0)
    return stacked.T


def _swa_kernel(q_ref, kp_ref, kc_ref, vp_ref, vc_ref, bias_ref, sink_ref, o_ref):
    tq = q_ref.shape[0]
    rows = SWA_GROUP * tq
    k_ext = jnp.concatenate([kp_ref[...], kc_ref[...]], axis=0)
    v_ext = jnp.concatenate([vp_ref[...], vc_ref[...]], axis=1)
    half = _lane_half((tq, LANES))
    outs = []
    for g in range(SWA_KV_HEADS):
        parts = []
        for r in range(SWA_GROUP):
            hd = g * SWA_GROUP + r
            q_pair = q_ref[:, (hd // 2) * LANES:(hd // 2 + 1) * LANES]
            src = q_pair if hd % 2 == g else _swap_halves(q_pair)
            parts.append(jnp.where(half == g, src, jnp.zeros_like(src)))
        qp = jnp.concatenate(parts, axis=0)
        s = lax.dot_general(k_ext, qp, (((1,), (1,)), ((), ())), preferred_element_type=F32)
        s = s * (HEAD_DIM ** -0.5) + bias_ref[0, g]
        sink = sink_ref[g]
        m = jnp.maximum(jnp.max(s, axis=0, keepdims=True), sink)
        e = jnp.exp(s - m)
        l = jnp.sum(e, axis=0, keepdims=True) + jnp.exp(sink - m)
        o_t = jnp.dot(v_ext[g * HEAD_DIM:(g + 1) * HEAD_DIM], e.astype(BF16), preferred_element_type=F32)
        outs.append(_rows_to_heads(o_t / l, SWA_GROUP, tq))
    o_ref[...] = jnp.concatenate(outs, axis=1).astype(o_ref.dtype)


def _swa_bias():
    tq = SWA_TQ
    slopes = _alibi_slopes().reshape(SWA_KV_HEADS, SWA_GROUP)
    x = np.arange(2 * tq)[:, None] - tq
    q = np.arange(tq)[None, :]
    dist = q - x
    valid = (dist >= 0) & (dist < SWA_WINDOW)
    out = np.empty((2, SWA_KV_HEADS, 2 * tq, SWA_GROUP * tq), np.float32)
    for var in range(2):
        v = valid & ((x >= 0) | (var == 1))
        for g in range(SWA_KV_HEADS):
            for r in range(SWA_GROUP):
                out[var, g, :, r * tq:(r + 1) * tq] = np.where(v, -slopes[g, r] * dist, MASKED)
    return out


def _swa_attention(qkv, v_t, sinks):
    t = qkv.shape[0]
    tq = SWA_TQ
    kcol = Q_DIM // LANES
    bias = jnp.asarray(_swa_bias())
    sink_rows = jnp.repeat(sinks.astype(F32).reshape(SWA_KV_HEADS, 1, SWA_GROUP), tq, axis=2)
    prev = lambda i: jnp.maximum(i - 1, 0)
    return pl.pallas_call(
        _swa_kernel,
        grid=(t // tq,),
        in_specs=[pl.BlockSpec((tq, Q_DIM), lambda i: (i, 0)),
                  pl.BlockSpec((tq, LANES), lambda i: (prev(i), kcol)),
                  pl.BlockSpec((tq, LANES), lambda i: (i, kcol)),
                  pl.BlockSpec((LANES, tq), lambda i: (0, prev(i))),
                  pl.BlockSpec((LANES, tq), lambda i: (0, i)),
                  pl.BlockSpec((1,) + bias.shape[1:], lambda i: (jnp.minimum(i, 1), 0, 0, 0)),
                  pl.BlockSpec(sink_rows.shape, lambda i: (0, 0, 0))],
        out_specs=pl.BlockSpec((tq, Q_DIM), lambda i: (i, 0)),
        out_shape=jax.ShapeDtypeStruct((t, Q_DIM), BF16),
        compiler_params=_cparams(("parallel",)),
        name="swa_attention",
    )(qkv, qkv, qkv, v_t, v_t, bias, sink_rows)


def _swa_layer(x, norm_g, w_in, sinks, w_out):
    qkv = _norm_proj(x, norm_g, w_in.astype(BF16), BF16)
    v_t = qkv[:, Q_DIM + SWA_KV_HEADS * HEAD_DIM:].T
    o = _swa_attention(qkv, v_t, sinks)
    return _out_proj(x, o, w_out.astype(BF16))


FOX_PARTS = 3


def _split_bf16(v):
    parts, rest = [], v
    for _ in range(FOX_PARTS):
        p = rest.astype(BF16)
        parts.append(p)
        rest = rest - p.astype(F32)
    return parts


def _fox_prep_kernel(f_ref, b_ref, aug_ref, cref_ref, carry_ref):
    tm = f_ref.shape[0]

    @pl.when(pl.program_id(0) == 0)
    def _():
        carry_ref[...] = jnp.zeros_like(carry_ref)

    logf = jax.nn.log_sigmoid(f_ref[...] + b_ref[...])
    row = lax.broadcasted_iota(jnp.int32, (tm, tm), 0)
    col = lax.broadcasted_iota(jnp.int32, (tm, tm), 1)
    tri = jnp.where(row >= col, 1.0, 0.0).astype(BF16)
    local = sum(jnp.dot(tri, p, preferred_element_type=F32) for p in _split_bf16(logf))
    cum = local + carry_ref[...]
    carry_ref[...] = cum[tm - 1:tm, :]
    first = cum[0:1, :]
    cref_ref[0] = jnp.broadcast_to(first, cref_ref.shape[1:])
    hi, mid, lo = _split_bf16(first - cum)
    lane = lax.broadcasted_iota(jnp.int32, (tm, LANES), 1)
    zero = jnp.zeros_like(hi)
    aug_ref[...] = jnp.where(lane < N_HEADS, hi, jnp.where(lane < 2 * N_HEADS, mid,
                                                            jnp.where(lane < 3 * N_HEADS, lo, zero)))


def _fox_prep(f_logit3, f_bias3):
    t = f_logit3.shape[0]
    nt = t // FOX_T
    return pl.pallas_call(
        _fox_prep_kernel,
        grid=(nt,),
        in_specs=[pl.BlockSpec((FOX_T, LANES), lambda i: (i, 0)), pl.BlockSpec((1, LANES), lambda i: (0, 0))],
        out_specs=[pl.BlockSpec((FOX_T, LANES), lambda i: (i, 0)), pl.BlockSpec((1, 8, LANES), lambda i: (i, 0, 0))],
        out_shape=[jax.ShapeDtypeStruct((t, LANES), BF16), jax.ShapeDtypeStruct((nt, 8, LANES), F32)],
        scratch_shapes=[pltpu.VMEM((1, LANES), F32)],
        compiler_params=_cparams(("arbitrary",)),
        name="fox_prep",
    )(f_logit3, f_bias3)


def _fox_kernel(cref_ref, q_ref, k_ref, aug_ref, vt_ref, o_ref, m_ref, l_ref, acc_ref, qt_ref, s_ref):
    pair, qi = pl.program_id(0), pl.program_id(1)
    tq = q_ref.shape[0]
    tk = FOX_T
    lane = lax.broadcasted_iota(jnp.int32, (tq, LANES), 1)
    q2 = q_ref[...] * jnp.asarray(HEAD_DIM ** -0.5, BF16)
    for hh in range(2):
        head = 2 * pair + hh
        pick = (lane % N_HEADS == head) & (lane < FOX_PARTS * N_HEADS)
        q_ext = jnp.concatenate([jnp.where(lane // HEAD_DIM == hh, q2, jnp.zeros_like(q2)).astype(F32),
                                 jnp.where(pick, 1.0, 0.0)], axis=1)
        qt_ref[hh] = q_ext.T.astype(BF16)
    m_ref[...] = jnp.full_like(m_ref, M_INIT)
    l_ref[...] = jnp.zeros_like(l_ref)
    acc_ref[...] = jnp.zeros_like(acc_ref)
    ratio = tq // tk

    def scores(kj, hh):
        start = pl.multiple_of(kj * tk, tk)
        k_ext = jnp.concatenate([k_ref[pl.ds(start, tk), :], aug_ref[pl.ds(start, tk), :]], axis=1)
        s_ref[hh] = jnp.dot(k_ext, qt_ref[hh], preferred_element_type=F32)

    def consume(kj, hh, key_offset):
        head = 2 * pair + hh
        s = s_ref[hh]
        if key_offset is not None:
            key = lax.broadcasted_iota(jnp.int32, (tk, tq), 0) + key_offset
            qry = lax.broadcasted_iota(jnp.int32, (tk, tq), 1)
            s = jnp.where(key <= qry, s, MASKED)
        off = cref_ref[qi * ratio, head] - cref_ref[kj, head]
        m_old = m_ref[hh]
        m_new = jnp.maximum(m_old, jnp.max(s, axis=0, keepdims=True) + off)
        e = jnp.exp(s - (m_new - off))
        alpha = jnp.exp(m_old - m_new)
        m_ref[hh] = m_new
        l_ref[hh] = alpha * l_ref[hh] + jnp.sum(e, axis=0, keepdims=True)
        v_t = vt_ref[kj, hh * HEAD_DIM:(hh + 1) * HEAD_DIM, :]
        acc_ref[hh] = alpha * acc_ref[hh] + jnp.dot(v_t, e.astype(BF16), preferred_element_type=F32)

    n_full = qi * ratio
    scores(0, 0)

    def body(j, carry):
        for d in range(ratio):
            kj = j * ratio + d
            scores(kj, 1)
            consume(kj, 0, None)
            scores(kj + 1, 0)
            consume(kj, 1, None)
        return carry

    lax.fori_loop(0, qi, body, 0)
    for d in range(ratio):
        kj = n_full + d
        scores(kj, 1)
        consume(kj, 0, d * tk)
        if d + 1 < ratio:
            scores(kj + 1, 0)
        consume(kj, 1, d * tk)
    o_t = jnp.concatenate([acc_ref[hh] / l_ref[hh] for hh in range(2)], axis=0)
    o_ref[...] = o_t.T.astype(o_ref.dtype)


def _fox_attention(qkv, aug, v_t3, cref):
    t = qkv.shape[0]
    tq = FOX_TQ
    nk = t // FOX_T
    kcol = Q_DIM // LANES
    return pl.pallas_call(
        _fox_kernel,
        grid=(N_HEADS // 2, t // tq),
        in_specs=[pl.BlockSpec(memory_space=pltpu.SMEM),
                  pl.BlockSpec((tq, LANES), lambda p, i: (i, p)),
                  pl.BlockSpec((t, LANES), lambda p, i: (0, kcol + p)),
                  pl.BlockSpec((t, LANES), lambda p, i: (0, 0)),
                  pl.BlockSpec((nk, LANES, FOX_T), lambda p, i: (0, p, 0))],
        out_specs=pl.BlockSpec((tq, LANES), lambda p, i: (i, p)),
        out_shape=jax.ShapeDtypeStruct((t, Q_DIM), BF16),
        scratch_shapes=[pltpu.VMEM((2, 1, tq), F32), pltpu.VMEM((2, 1, tq), F32),
                        pltpu.VMEM((2, HEAD_DIM, tq), F32), pltpu.VMEM((2, 2 * LANES, tq), BF16), pltpu.VMEM((2, FOX_T, tq), F32)],
        compiler_params=_cparams(("parallel", "arbitrary")),
        name="fox_attention",
    )(cref, qkv, qkv, aug, v_t3)


def _fox_layer(x, norm_g, w_in, f_bias, w_out):
    t = x.shape[0]
    qkv = _norm_proj(x, norm_g, w_in[:, :3 * Q_DIM].astype(BF16), BF16)
    w_f = w_in[:, 3 * Q_DIM:]
    pad = jnp.zeros((D_MODEL, LANES - FOX_PARTS * N_HEADS), w_f.dtype)
    w_f3 = jnp.concatenate([w_f] * FOX_PARTS + [pad], axis=1).astype(BF16)
    b3 = jnp.concatenate([f_bias.astype(F32)] * FOX_PARTS + [jnp.zeros((LANES - FOX_PARTS * N_HEADS,), F32)])
    f_logit3 = _norm_proj(x, norm_g, w_f3, F32)
    aug, cref = _fox_prep(f_logit3, b3.reshape(1, LANES))
    v_t3 = qkv[:, 2 * Q_DIM:].reshape(t // FOX_T, FOX_T, Q_DIM).transpose(0, 2, 1)
    o = _fox_attention(qkv, aug, v_t3, cref[:, 0, :N_HEADS])
    return _out_proj(x, o, w_out.astype(BF16))


def _nsa_compress_kernel(a_ref, pos_ref, w1_ref, w2_ref, o_ref):
    tc = a_ref.shape[2]
    half = CMP_STRIDE * HEAD_DIM
    pos = pos_ref[0]
    out = jnp.zeros(o_ref.shape[1:], F32)
    for gg in range(2):
        a = a_ref[0, gg]
        top = jnp.dot((a + pos[:, :half]).astype(BF16), w1_ref[0, :half], preferred_element_type=F32)
        bot = jnp.dot((a + pos[:, half:]).astype(BF16), w1_ref[0, half:], preferred_element_type=F32)
        hid = top + pltpu.roll(bot, tc - 1, 0)
        out = out + jnp.dot(jax.nn.gelu(hid).astype(BF16), w2_ref[0, gg], preferred_element_type=F32)
    o_ref[0] = out


def _nsa_compress(a, pos, w1, w2):
    _, g, tc, width = a.shape
    return pl.pallas_call(
        _nsa_compress_kernel,
        grid=(2, g // 2),
        in_specs=[pl.BlockSpec((1, 2, tc, width), lambda kv, gp: (kv, gp, 0, 0)),
                  pl.BlockSpec((1, 1, 2 * width), lambda kv, gp: (kv, 0, 0)),
                  pl.BlockSpec((1, 2 * width, CMP_HIDDEN), lambda kv, gp: (kv, 0, 0)),
                  pl.BlockSpec((1, 2, CMP_HIDDEN, LANES), lambda kv, gp: (kv, 0, 0, 0))],
        out_specs=pl.BlockSpec((1, tc, LANES), lambda kv, gp: (kv, 0, gp)),
        out_shape=jax.ShapeDtypeStruct((2, tc, g * HEAD_DIM), F32),
        compiler_params=_cparams(("parallel", "parallel")),
        name="nsa_compress",
    )(a, pos, w1, w2)


NSA_WIN_TILES = NSA_WINDOW // NSA_TQ + 1


def _nsa_kernel(slopes_ref, q_ref, ks_ref, *refs):
    kw_refs, refs = refs[:NSA_WIN_TILES], refs[NSA_WIN_TILES:]
    vst_ref, refs = refs[0], refs[1:]
    vw_refs, refs = refs[:NSA_WIN_TILES], refs[NSA_WIN_TILES:]
    (kcmp_ref, vov_ref, gates_ref, ksel_ext_ref, kwin_ext_ref, kcmp_ext_ref, qext_ref,
     o_ref, selt_ref, m_ref, l_ref, acc_ref, qt_ref, s_ref, scmp_ref, cm_ref, cl_ref, oc_ref,
     active_ref, list_ref) = refs
    g, qi = pl.program_id(0), pl.program_id(1)
    tq = q_ref.shape[0]
    tc = kcmp_ref.shape[0]
    n_sel = selt_ref.shape[0]
    rr = NSA_GROUP
    t0 = qi * tq
    hg = g % 2
    slope = [slopes_ref[g * rr + r] for r in range(rr)]

    half = _lane_half((tq, LANES))
    parts = []
    for r in range(rr):
        q_pair = q_ref[:, (r // 2) * LANES:(r // 2 + 1) * LANES]
        src = jnp.where(hg == r % 2, q_pair, _swap_halves(q_pair))
        parts.append(jnp.where(half == hg, src, jnp.zeros_like(src)) * jnp.asarray(HEAD_DIM ** -0.5, BF16))
    qp = jnp.concatenate(parts, axis=0)
    qt_ref[0:LANES, :] = qp.astype(F32).T.astype(BF16)

    @pl.when(qi == 0)
    def _():
        qt_ref[LANES:, :] = qext_ref[0]

    def lanes_of(s, r):
        return s[:, r * tq:(r + 1) * tq]

    chunk_span = NSA_CMP_CHUNK * CMP_STRIDE
    last_chunk = ((t0 + tq - CMP_BLOCK) // CMP_STRIDE) // NSA_CMP_CHUNK
    nl16_q = (CMP_STRIDE * lax.broadcasted_iota(jnp.int32, (NSA_CMP_CHUNK, tq), 0)
              - lax.broadcasted_iota(jnp.int32, (NSA_CMP_CHUNK, tq), 1)).astype(F32)
    cm_ref[...] = jnp.full_like(cm_ref, M_INIT)

    def cmp_rows(ch):
        return pl.ds(pl.multiple_of(ch * NSA_CMP_CHUNK, NSA_CMP_CHUNK), NSA_CMP_CHUNK)

    def cmp_offset(ch, r):
        return slope[r] * (ch * chunk_span).astype(F32)

    def cmp_scores(ch, masked):
        k_full = jnp.concatenate([kcmp_ref[cmp_rows(ch), :], kcmp_ext_ref[...]], axis=1)
        s = jnp.dot(k_full, qt_ref[...], preferred_element_type=F32)
        if masked:
            visible = nl16_q <= (t0 - (CMP_BLOCK - 1) - ch * chunk_span).astype(F32)
            s = jnp.concatenate([jnp.where(visible, lanes_of(s, r), MASKED) for r in range(rr)], axis=1)
        scmp_ref[cmp_rows(ch), :] = s
        for r in range(rr):
            cm_ref[r] = jnp.maximum(cm_ref[r], jnp.max(lanes_of(s, r), axis=0, keepdims=True) + cmp_offset(ch, r))

    def cmp_plain(ch, carry):
        cmp_scores(ch, False)
        return carry

    lax.fori_loop(0, jnp.maximum(last_chunk - 1, 0), cmp_plain, 0)

    @pl.when(last_chunk >= 1)
    def _():
        cmp_scores(last_chunk - 1, True)

    cmp_scores(last_chunk, True)

    cl_ref[...] = jnp.zeros_like(cl_ref)
    oc_ref[...] = jnp.zeros_like(oc_ref)

    def cmp_exp(ch, carry):
        s = scmp_ref[cmp_rows(ch), :]
        e_all = []
        for r in range(rr):
            e = jnp.exp(lanes_of(s, r) - (cm_ref[r] - cmp_offset(ch, r)))
            cl_ref[r] = cl_ref[r] + jnp.sum(e, axis=0, keepdims=True)
            e_all.append(e.astype(BF16))
        oc_ref[...] += jnp.dot(vov_ref[ch, 0], jnp.concatenate(e_all, axis=1), preferred_element_type=F32)
        return carry

    lax.fori_loop(0, last_chunk + 1, cmp_exp, 0)
    inv_l = [jnp.where(cl_ref[r] > 0.0, 1.0 / cl_ref[r], 0.0) for r in range(rr)]
    o_cmp = [oc_ref[0:HEAD_DIM, r * tq:(r + 1) * tq] * inv_l[r] for r in range(rr)]

    imp = functools.reduce(jnp.add, [oc_ref[HEAD_DIM:, r * tq:(r + 1) * tq] * inv_l[r] for r in range(rr)])
    blk = lax.broadcasted_iota(jnp.int32, (n_sel, tq), 0)
    cur = (t0 + lax.broadcasted_iota(jnp.int32, (n_sel, tq), 1)) // SEL_BLOCK
    blk_f = blk.astype(F32)
    forced = jnp.where(blk == 0, 1.0, jnp.where(blk == cur, 1.0, jnp.where(blk == cur - 1, 1.0, 0.0)))
    score = jnp.where(forced > 0.0, TAKEN, jnp.where(blk <= cur, imp, -1.0))

    def pick(_, sc):
        best = jnp.max(sc, axis=0, keepdims=True)
        first = jnp.min(jnp.where(sc == best, blk_f, float(n_sel)), axis=0, keepdims=True)
        return jnp.where(blk_f == first, TAKEN, sc)

    score = lax.fori_loop(0, N_SELECT - 3, pick, score)
    selt_ref[...] = jnp.where(blk <= cur, jnp.where(score == TAKEN, 1.0, 0.0), 0.0)

    m_ref[...] = jnp.full_like(m_ref, M_INIT)
    l_ref[...] = jnp.zeros_like(l_ref)
    acc_ref[...] = jnp.zeros_like(acc_ref)
    blocks_per_chunk = NSA_TK // SEL_BLOCK

    for c in range(n_sel // blocks_per_chunk):
        active_ref[c] = jnp.max(selt_ref[c * blocks_per_chunk:(c + 1) * blocks_per_chunk, :])


    def compact(c, n):
        list_ref[n] = c
        return n + (active_ref[c] > 0.0).astype(jnp.int32)

    n_active = lax.fori_loop(0, t0 // NSA_TK + 1, compact, 0)

    def sel_scores(c, buf):
        start = pl.multiple_of(c * NSA_TK, NSA_TK)
        rows = selt_ref[pl.ds(pl.multiple_of(c * blocks_per_chunk, blocks_per_chunk), blocks_per_chunk), :]
        mask_rows = jnp.concatenate([jnp.where(rows > 0.5, 0.0, MASKED), jnp.zeros_like(rows)], axis=0)
        first = LANES + buf * NSA_EXT_MASK_ROWS
        qt_ref[first:first + NSA_EXT_MASK_ROWS, :] = jnp.concatenate([mask_rows] * rr, axis=1).astype(BF16)
        k_full = jnp.concatenate([ks_ref[pl.ds(start, NSA_TK), :], ksel_ext_ref[buf]], axis=1)
        s_ref[buf] = jnp.dot(k_full, qt_ref[...], preferred_element_type=F32)

    def sel_consume(c, buf, diagonal):
        shift = (c * NSA_TK - t0).astype(F32)
        if diagonal:
            key = lax.broadcasted_iota(jnp.int32, (NSA_TK, tq), 0) + c * NSA_TK
            visible = key <= lax.broadcasted_iota(jnp.int32, (NSA_TK, tq), 1) + t0
        e_all, alpha_all = [], []
        for r in range(rr):
            off = slope[r] * shift
            s = s_ref[buf][:, r * tq:(r + 1) * tq]
            if diagonal:
                s = jnp.where(visible, s, MASKED)
            m_old = m_ref[r]
            m_new = jnp.maximum(m_old, jnp.max(s, axis=0, keepdims=True) + off)
            e = jnp.exp(s - (m_new - off))
            alpha = jnp.exp(m_old - m_new)
            m_ref[r] = m_new
            l_ref[r] = alpha * l_ref[r] + jnp.sum(e, axis=0, keepdims=True)
            e_all.append(e.astype(BF16))
            alpha_all.append(alpha)
        pv = jnp.dot(vst_ref[c], jnp.concatenate(e_all, axis=1), preferred_element_type=F32)
        acc_ref[...] = jnp.concatenate(alpha_all, axis=1) * acc_ref[...] + pv

    n_before = n_active - 1
    sel_scores(list_ref[0], 0)

    def sel_pair(j, carry):
        c0, c1, c2 = list_ref[2 * j], list_ref[2 * j + 1], list_ref[2 * j + 2]
        sel_scores(c1, 1)
        sel_consume(c0, 0, False)
        sel_scores(c2, 0)
        sel_consume(c1, 1, False)
        return carry

    lax.fori_loop(0, n_before // 2, sel_pair, 0)

    @pl.when(n_before % 2 == 1)
    def _():
        c0, c_last = list_ref[n_before - 1], list_ref[n_before]
        sel_scores(c_last, 1)
        sel_consume(c0, 0, False)
        sel_consume(c_last, 1, True)

    @pl.when(n_before % 2 == 0)
    def _():
        sel_consume(list_ref[n_before], 0, True)

    xw = lax.broadcasted_iota(jnp.int32, (tq, tq), 0)
    qw = lax.broadcasted_iota(jnp.int32, (tq, tq), 1)
    s_win, off_win = [], []
    for idx in range(NSA_WIN_TILES):
        back = NSA_WIN_TILES - 1 - idx
        k_full = jnp.concatenate([kw_refs[idx][...], kwin_ext_ref[...]], axis=1)
        s_t = jnp.dot(k_full, qt_ref[...], preferred_element_type=F32)
        per_head = [lanes_of(s_t, r) for r in range(rr)]
        if back == NSA_WIN_TILES - 1:
            per_head = [jnp.where(xw > qw, s, MASKED) for s in per_head]
        if back == 0:
            per_head = [jnp.where(xw <= qw, s, MASKED) for s in per_head]
        s_win.append(per_head)
        off_win.append([jnp.where(qi >= back, -slope[r] * float(back * tq), MASKED) for r in range(rr)])
    e_win, l_win = [], []
    for r in range(rr):
        m = functools.reduce(jnp.maximum, [jnp.max(s_win[idx][r], axis=0, keepdims=True) + off_win[idx][r]
                                           for idx in range(NSA_WIN_TILES)])
        e_tiles = [jnp.exp(s_win[idx][r] - (m - off_win[idx][r])) for idx in range(NSA_WIN_TILES)]
        l_win.append(functools.reduce(jnp.add, [jnp.sum(e, axis=0, keepdims=True) for e in e_tiles]))
        e_win.append(jnp.concatenate([e.astype(BF16) for e in e_tiles], axis=0))
    v_win = jnp.concatenate([vw_refs[idx][...] for idx in range(NSA_WIN_TILES)], axis=1)
    o_win_all = jnp.dot(v_win, jnp.concatenate(e_win, axis=1), preferred_element_type=F32)
    o_win = [lanes_of(o_win_all, r) / l_win[r] for r in range(rr)]

    gate = jax.nn.sigmoid(gates_ref[0])
    merged = []
    for r in range(rr):
        o_sel = lanes_of(acc_ref[...], r) / l_ref[r]
        merged.append(gate[3 * r:3 * r + 1] * o_cmp[r] + gate[3 * r + 1:3 * r + 2] * o_sel
                      + gate[3 * r + 2:3 * r + 3] * o_win[r])
    o_ref[...] = jnp.concatenate(merged, axis=0).T.astype(o_ref.dtype)


def _nsa_overlap_t(n_sel, tc):
    cmp_start = np.arange(tc - 1) * CMP_STRIDE
    sel_start = np.arange(n_sel) * SEL_BLOCK
    ov = np.clip(np.minimum(cmp_start[None, :] + CMP_BLOCK, sel_start[:, None] + SEL_BLOCK)
                 - np.maximum(cmp_start[None, :], sel_start[:, None]), 0, None) / CMP_BLOCK
    return np.concatenate([ov, np.zeros((n_sel, 1))], axis=1).astype(np.float32)


NSA_EXT_MASK_ROWS = 16
NSA_EXT_SEL = 32
NSA_EXT_WIN = 48
NSA_EXT_CMP = 64
NSA_CMP_CHUNK = 256
SLOPE_PIECES = 3


def _bf16_pieces(v):
    out, rest = [], np.asarray(v, np.float32)
    for _ in range(SLOPE_PIECES):
        p = rest.astype(BF16).astype(np.float32)
        out.append(p)
        rest = rest - p
    return out


def _nsa_extensions(tq):
    ksel = np.zeros((2, NSA_TK, LANES), np.float32)
    x = np.arange(NSA_TK)
    for buf in range(2):
        ksel[buf, x, buf * NSA_EXT_MASK_ROWS + x // SEL_BLOCK] = 1.0
    for j in range(SLOPE_PIECES):
        ksel[:, :, NSA_EXT_SEL + j] = 256 * (x // 256)
        ksel[:, :, NSA_EXT_SEL + SLOPE_PIECES + j] = x % 256
    kwin = np.zeros((tq, LANES), np.float32)
    for j in range(SLOPE_PIECES):
        kwin[:, NSA_EXT_WIN + j] = np.arange(tq)
    assert tq <= 256, "a window key offset must be exact in bf16"
    kcmp = np.zeros((NSA_CMP_CHUNK, LANES), np.float32)
    for j in range(SLOPE_PIECES):
        kcmp[:, NSA_EXT_CMP + j] = np.arange(NSA_CMP_CHUNK)
    pieces = _bf16_pieces(_alibi_slopes())
    qext = np.zeros((NSA_KV_HEADS, LANES, NSA_GROUP * tq), np.float32)
    for g in range(NSA_KV_HEADS):
        for r in range(NSA_GROUP):
            cols = slice(r * tq, (r + 1) * tq)
            for j in range(SLOPE_PIECES):
                p = pieces[j][g * NSA_GROUP + r]
                qext[g, NSA_EXT_SEL + j, cols] = p
                qext[g, NSA_EXT_SEL + SLOPE_PIECES + j, cols] = p
                qext[g, NSA_EXT_WIN + j, cols] = p
                qext[g, NSA_EXT_CMP + j, cols] = CMP_STRIDE * p
    return jnp.asarray(ksel, BF16), jnp.asarray(kwin, BF16), jnp.asarray(kcmp, BF16), jnp.asarray(qext, BF16)


def _nsa_attention(main, vs_t3, vw_t, k_cmp, v_cmp_t, gates_t):
    t = main.shape[0]
    tq = NSA_TQ
    tc = k_cmp.shape[0]
    n_sel = t // SEL_BLOCK
    nc = t // NSA_TK
    ks_col = Q_DIM // LANES
    kw_col = ks_col + NSA_KV_DIM // LANES
    n_ch = tc // NSA_CMP_CHUNK
    ov_t = jnp.asarray(_nsa_overlap_t(n_sel, tc).reshape(n_sel, n_ch, NSA_CMP_CHUNK).transpose(1, 0, 2), BF16)
    v_cmp_t = v_cmp_t.reshape(NSA_KV_HEADS, HEAD_DIM, n_ch, NSA_CMP_CHUNK).transpose(2, 0, 1, 3)
    vov = jnp.concatenate([v_cmp_t, jnp.broadcast_to(ov_t[:, None], (n_ch, NSA_KV_HEADS, n_sel, NSA_CMP_CHUNK))],
                          axis=2)
    slopes = jnp.asarray(_alibi_slopes())
    ksel_ext, kwin_ext, kcmp_ext, qext = _nsa_extensions(tq)

    def kw_spec(back):
        return pl.BlockSpec((tq, LANES), lambda g, i: (jnp.maximum(i - back, 0), kw_col + g // 2))

    def vw_spec(back):
        return pl.BlockSpec((HEAD_DIM, tq), lambda g, i: (g, jnp.maximum(i - back, 0)))

    backs = list(range(NSA_WIN_TILES - 1, -1, -1))
    return pl.pallas_call(
        _nsa_kernel,
        grid=(NSA_KV_HEADS, t // tq),
        in_specs=[pl.BlockSpec(memory_space=pltpu.SMEM),
                  pl.BlockSpec((tq, NSA_GROUP * HEAD_DIM), lambda g, i: (i, g)),
                  pl.BlockSpec((t, LANES), lambda g, i: (0, ks_col + g // 2))]
                 + [kw_spec(b) for b in backs]
                 + [pl.BlockSpec((nc, HEAD_DIM, NSA_TK), lambda g, i: (0, g, 0))]
                 + [vw_spec(b) for b in backs]
                 + [pl.BlockSpec((tc, LANES), lambda g, i: (0, g // 2)),
                    pl.BlockSpec((n_ch, 1, HEAD_DIM + n_sel, NSA_CMP_CHUNK), lambda g, i: (0, g, 0, 0)),
                    pl.BlockSpec((1, 16, tq), lambda g, i: (g, 0, i)),
                    pl.BlockSpec(ksel_ext.shape, lambda g, i: (0, 0, 0)),
                    pl.BlockSpec(kwin_ext.shape, lambda g, i: (0, 0)),
                    pl.BlockSpec(kcmp_ext.shape, lambda g, i: (0, 0)),
                    pl.BlockSpec((1,) + qext.shape[1:], lambda g, i: (g, 0, 0))],
        out_specs=pl.BlockSpec((tq, NSA_GROUP * HEAD_DIM), lambda g, i: (i, g)),
        out_shape=jax.ShapeDtypeStruct((t, Q_DIM), BF16),
        scratch_shapes=[pltpu.VMEM((n_sel, tq), F32), pltpu.VMEM((NSA_GROUP, 1, tq), F32),
                        pltpu.VMEM((NSA_GROUP, 1, tq), F32), pltpu.VMEM((HEAD_DIM, NSA_GROUP * tq), F32),
                        pltpu.VMEM((2 * LANES, NSA_GROUP * tq), BF16),
                        pltpu.VMEM((2, NSA_TK, NSA_GROUP * tq), F32),
                        pltpu.VMEM((tc, NSA_GROUP * tq), F32), pltpu.VMEM((NSA_GROUP, 1, tq), F32),
                        pltpu.VMEM((NSA_GROUP, 1, tq), F32), pltpu.VMEM((HEAD_DIM + n_sel, NSA_GROUP * tq), F32),
                        pltpu.SMEM((nc,), F32), pltpu.SMEM((nc,), jnp.int32)],
        compiler_params=_cparams(("parallel", "arbitrary")),
        name="nsa_attention",
    )(slopes, main, main, *([main] * NSA_WIN_TILES), vs_t3, *([vw_t] * NSA_WIN_TILES), k_cmp, vov, gates_t,
      ksel_ext, kwin_ext, kcmp_ext, qext)


def _nsa_layer(x, norm_g, w_in, cmp_pos, cmp_w1, cmp_w2, w_out):
    t = x.shape[0]
    kv = NSA_KV_DIM
    col = lambda i: w_in[:, Q_DIM + i * kv:Q_DIM + (i + 1) * kv]
    w_main = jnp.concatenate([w_in[:, :Q_DIM], col(2), col(4), col(3), col(5)], axis=1).astype(BF16)
    w_cmp = jnp.concatenate([col(0), col(1)], axis=1).astype(BF16)
    n_gate = 3 * N_HEADS
    w_gate = jnp.concatenate([w_in[:, Q_DIM + 6 * kv:], jnp.zeros((D_MODEL, LANES - n_gate), w_in.dtype)],
                             axis=1).astype(BF16)
    main = _norm_proj(x, norm_g, w_main, BF16)
    kcvc = _norm_proj(x, norm_g, w_cmp, F32)
    gate_logit = _norm_proj(x, norm_g, w_gate, F32)

    tc = t // CMP_STRIDE
    a = kcvc.reshape(tc, CMP_STRIDE, 2, NSA_KV_HEADS, HEAD_DIM).transpose(2, 3, 0, 1, 4)
    a = a.reshape(2, NSA_KV_HEADS, tc, CMP_STRIDE * HEAD_DIM)
    zeros = jnp.zeros_like(cmp_w2)
    w2 = jnp.stack([jnp.concatenate([cmp_w2, zeros], axis=2), jnp.concatenate([zeros, cmp_w2], axis=2)], axis=1)
    cmp = _nsa_compress(a, cmp_pos.reshape(2, 1, CMP_BLOCK * HEAD_DIM), cmp_w1.astype(BF16), w2.astype(BF16))
    k_cmp = cmp[0].astype(BF16)
    v_cmp_t = cmp[1].T.astype(BF16)

    off_vs = Q_DIM + 2 * kv
    vs_t3 = main[:, off_vs:off_vs + kv].reshape(t // NSA_TK, NSA_TK, kv).transpose(0, 2, 1)
    vw_t = main[:, off_vs + kv:].T
    gates_t = gate_logit[:, :n_gate].reshape(t, NSA_KV_HEADS, 3 * NSA_GROUP).transpose(1, 2, 0)
    gates_t = jnp.pad(gates_t, ((0, 0), (0, 16 - 3 * NSA_GROUP), (0, 0)))
    o = _nsa_attention(main, vs_t3, vw_t, k_cmp, v_cmp_t, gates_t)
    return _out_proj(x, o, w_out.astype(BF16))


def kernel(x, attn_norm, mlp_norm, final_norm, nsa_w_in, nsa_cmp_pos, nsa_cmp_w1, nsa_cmp_w2, nsa_w_out,
           swa_w_in, swa_sinks, swa_w_out, fox_w_in, fox_f_bias, fox_w_out,
           mlp_w_up, mlp_conv_w, mlp_conv_b, mlp_w_down):
    assert x.shape[0] == 1, "the trunk is written for batch 1"
    h = x[0]
    for i in range(DEPTH):
        kind, j = i % N_MIXERS, i // N_MIXERS
        if kind == 0:
            h = _nsa_layer(h, attn_norm[i], nsa_w_in[j], nsa_cmp_pos[j], nsa_cmp_w1[j], nsa_cmp_w2[j], nsa_w_out[j])
        elif kind == 1:
            h = _swa_layer(h, attn_norm[i], swa_w_in[j], swa_sinks[j], swa_w_out[j])
        else:
            h = _fox_layer(h, attn_norm[i], fox_w_in[j], fox_f_bias[j], fox_w_out[j])
        h = _mlp(h, mlp_norm[i], mlp_w_up[i].astype(BF16), mlp_conv_w[i], mlp_conv_b[i].reshape(1, -1),
                 mlp_w_down[i].astype(BF16))
    return _rmsnorm(h, final_norm)[None]
```

```python
import functools

import numpy as np
import jax
import jax.numpy as jnp
from jax import lax
from jax.experimental import pallas as pl
from jax.experimental.pallas import tpu as pltpu

F32 = jnp.float32
BF16 = jnp.bfloat16

D_MODEL = 1024
DEPTH = 4
N_MIXERS = 3
HEAD_DIM = 64
N_HEADS = 16
Q_DIM = N_HEADS * HEAD_DIM
ALIBI_MAX = 8.0
NORM_EPS = 1e-6
FORCE_SCORE = 1e4
MASKED = -2e30
M_INIT = -1e30
TAKEN = -(2.0 ** 127)

NSA_KV_HEADS = 4
NSA_GROUP = 4
NSA_KV_DIM = NSA_KV_HEADS * HEAD_DIM
CMP_BLOCK = 32
CMP_STRIDE = 16
CMP_HIDDEN = 256
SEL_BLOCK = 64
N_SELECT = 16
NSA_WINDOW = 512
NSA_TQ = 256
NSA_TK = 512

SWA_KV_HEADS = 2
SWA_GROUP = 8
SWA_WINDOW = 128
SWA_TQ = 128

FOX_TQ = 1024
FOX_T = 512

D_FF = 2816
LANES = 128
VMEM_LIMIT = 56 * 1024 * 1024


def _cparams(semantics, vmem=VMEM_LIMIT):
    return pltpu.CompilerParams(dimension_semantics=semantics, vmem_limit_bytes=vmem)


def _alibi_slopes():
    return np.asarray(2.0 ** (-ALIBI_MAX * np.arange(1, N_HEADS + 1) / N_HEADS), dtype=np.float32)


def _rms(x, g):
    ms = jnp.mean(x * x, axis=-1, keepdims=True)
    return x * lax.rsqrt(ms + NORM_EPS) * g


def _norm_proj_kernel(x_ref, g_ref, w_ref, o_ref, h_ref):
    @pl.when(pl.program_id(1) == 0)
    def _():
        h_ref[...] = _rms(x_ref[...], g_ref[...]).astype(BF16)

    o_ref[...] = jnp.dot(h_ref[...], w_ref[...], preferred_element_type=F32).astype(o_ref.dtype)


def _pick_tile(n, candidates):
    for c in candidates:
        if n % c == 0:
            return c
    raise ValueError(f"no tile for {n}")


def _norm_proj(x, g, w, out_dtype):
    t, d = x.shape
    n = w.shape[1]
    tm = _pick_tile(t, (1024, 512, 256, 128))
    tn = _pick_tile(n, (3072, 2048, 1536, 1280, 1024, 768, 512, 256, 128))
    return pl.pallas_call(
        _norm_proj_kernel,
        grid=(t // tm, n // tn),
        in_specs=[pl.BlockSpec((tm, d), lambda i, j: (i, 0)),
                  pl.BlockSpec((1, d), lambda i, j: (0, 0)),
                  pl.BlockSpec((d, tn), lambda i, j: (0, j))],
        out_specs=pl.BlockSpec((tm, tn), lambda i, j: (i, j)),
        out_shape=jax.ShapeDtypeStruct((t, n), out_dtype),
        scratch_shapes=[pltpu.VMEM((tm, d), BF16)],
        compiler_params=_cparams(("parallel", "arbitrary")),
        name="norm_proj",
    )(x, g.reshape(1, d), w)


MLP_HALO = 16


def _mixer_out_mlp_kernel(final_norm, x_ref, xh_ref, o_ref, oh_ref, wo_ref, g_ref, fg_ref, wa_ref, wg_ref,
                          cwa_ref, cwg_ref, cba_ref, cbg_ref, wd_ref, y_ref, x1_ref, h_ref, acc_ref, ua_ref, ug_ref):
    i, j = pl.program_id(0), pl.program_id(1)
    tm = x_ref.shape[0]

    @pl.when(j == 0)
    def _():
        wo = wo_ref[...]
        x1_ref[0:MLP_HALO, :] = xh_ref[...] + jnp.dot(oh_ref[...], wo, preferred_element_type=F32)
        x1_ref[MLP_HALO:, :] = x_ref[...] + jnp.dot(o_ref[...], wo, preferred_element_type=F32)
        halo = _rms(x1_ref[0:MLP_HALO, :], g_ref[...])
        h_ref[0:MLP_HALO, :] = jnp.where(i > 0, halo, 0.0).astype(BF16)
        h_ref[MLP_HALO:, :] = _rms(x1_ref[MLP_HALO:, :], g_ref[...]).astype(BF16)
        acc_ref[...] = jnp.zeros_like(acc_ref)

    h = h_ref[...]

    def conv(w_ref, cw_ref, cb_ref, u_ref):
        u_ref[...] = jnp.dot(h, w_ref[...], preferred_element_type=F32)
        cw = cw_ref[...]
        return (cw[0:1] * u_ref[MLP_HALO - 2:MLP_HALO - 2 + tm, :] + cw[1:2] * u_ref[MLP_HALO - 1:MLP_HALO - 1 + tm, :]
                + cw[2:3] * u_ref[MLP_HALO:MLP_HALO + tm, :] + cb_ref[...])

    a = conv(wa_ref, cwa_ref, cba_ref, ua_ref)
    gt = conv(wg_ref, cwg_ref, cbg_ref, ug_ref)
    act = (jax.nn.silu(gt) * a).astype(BF16)
    acc_ref[...] += jnp.dot(act, wd_ref[...], preferred_element_type=F32)

    @pl.when(j == pl.num_programs(1) - 1)
    def _():
        y = x1_ref[MLP_HALO:, :] + acc_ref[...]
        y_ref[...] = _rms(y, fg_ref[...]) if final_norm else y


def _mixer_out_mlp(x, o, w_out, g, w_up, conv_w, conv_b, w_down, final_g=None):
    t, d = x.shape
    f = w_down.shape[0]
    q = o.shape[1]
    tm = _pick_tile(t, (512, 256, 128))
    tf = _pick_tile(f, (1408, 256, 128))
    nf = f // tf
    hb = tm // MLP_HALO
    halo_map = lambda i, j: (jnp.maximum(i * hb - 1, 0), 0)
    fg = g if final_g is None else final_g
    return pl.pallas_call(
        functools.partial(_mixer_out_mlp_kernel, final_g is not None),
        grid=(t // tm, nf),
        in_specs=[pl.BlockSpec((tm, d), lambda i, j: (i, 0)),
                  pl.BlockSpec((MLP_HALO, d), halo_map),
                  pl.BlockSpec((tm, q), lambda i, j: (i, 0)),
                  pl.BlockSpec((MLP_HALO, q), halo_map),
                  pl.BlockSpec((q, d), lambda i, j: (0, 0)),
                  pl.BlockSpec((1, d), lambda i, j: (0, 0)),
                  pl.BlockSpec((1, d), lambda i, j: (0, 0)),
                  pl.BlockSpec((d, tf), lambda i, j: (0, j)),
                  pl.BlockSpec((d, tf), lambda i, j: (0, nf + j)),
                  pl.BlockSpec((3, tf), lambda i, j: (0, j)),
                  pl.BlockSpec((3, tf), lambda i, j: (0, nf + j)),
                  pl.BlockSpec((1, tf), lambda i, j: (0, j)),
                  pl.BlockSpec((1, tf), lambda i, j: (0, nf + j)),
                  pl.BlockSpec((tf, d), lambda i, j: (j, 0))],
        out_specs=pl.BlockSpec((tm, d), lambda i, j: (i, 0)),
        out_shape=jax.ShapeDtypeStruct((t, d), F32),
        scratch_shapes=[pltpu.VMEM((tm + MLP_HALO, d), F32), pltpu.VMEM((tm + MLP_HALO, d), BF16),
                        pltpu.VMEM((tm, d), F32),
                        pltpu.VMEM((tm + MLP_HALO, tf), F32), pltpu.VMEM((tm + MLP_HALO, tf), F32)],
        compiler_params=_cparams(("parallel", "arbitrary")),
        name="mixer_out_mlp",
    )(x, x, o, o, w_out, g.reshape(1, d), fg.reshape(1, d), w_up, w_up, conv_w, conv_w, conv_b, conv_b, w_down)


def _swap_halves(q_pair):
    return pltpu.roll(q_pair.astype(F32), HEAD_DIM, 1).astype(BF16)


def _lane_half(shape):
    return lax.broadcasted_iota(jnp.int32, shape, 1) // HEAD_DIM


def _rows_to_heads(o_t, n_heads, tq):
    stacked = jnp.concatenate([o_t[:, r * tq:(r + 1) * tq] for r in range(n_heads)], axis=0)
    return stacked.T


def _swa_kernel(q_ref, kp_ref, kc_ref, vp_ref, vc_ref, bias_ref, sink_ref, o_ref):
    tq = q_ref.shape[0]
    rows = SWA_GROUP * tq
    k_ext = jnp.concatenate([kp_ref[...], kc_ref[...]], axis=0)
    v_ext = jnp.concatenate([vp_ref[...], vc_ref[...]], axis=1)
    half = _lane_half((tq, LANES))
    outs = []
    for g in range(SWA_KV_HEADS):
        parts = []
        for r in range(SWA_GROUP):
            hd = g * SWA_GROUP + r
            q_pair = q_ref[:, (hd // 2) * LANES:(hd // 2 + 1) * LANES]
            src = q_pair if hd % 2 == g else _swap_halves(q_pair)
            parts.append(jnp.where(half == g, src, jnp.zeros_like(src)))
        qp = jnp.concatenate(parts, axis=0)
        s = lax.dot_general(k_ext, qp, (((1,), (1,)), ((), ())), preferred_element_type=F32)
        s = s * (HEAD_DIM ** -0.5) + bias_ref[0, g]
        sink = sink_ref[g]
        m = jnp.maximum(jnp.max(s, axis=0, keepdims=True), sink)
        e = jnp.exp(s - m)
        l = jnp.sum(e, axis=0, keepdims=True) + jnp.exp(sink - m)
        o_t = jnp.dot(v_ext[g * HEAD_DIM:(g + 1) * HEAD_DIM], e.astype(BF16), preferred_element_type=F32)
        outs.append(_rows_to_heads(o_t / l, SWA_GROUP, tq))
    o_ref[...] = jnp.concatenate(outs, axis=1).astype(o_ref.dtype)


def _swa_bias():
    tq = SWA_TQ
    slopes = _alibi_slopes().reshape(SWA_KV_HEADS, SWA_GROUP)
    x = np.arange(2 * tq)[:, None] - tq
    q = np.arange(tq)[None, :]
    dist = q - x
    valid = (dist >= 0) & (dist < SWA_WINDOW)
    out = np.empty((2, SWA_KV_HEADS, 2 * tq, SWA_GROUP * tq), np.float32)
    for var in range(2):
        v = valid & ((x >= 0) | (var == 1))
        for g in range(SWA_KV_HEADS):
            for r in range(SWA_GROUP):
                out[var, g, :, r * tq:(r + 1) * tq] = np.where(v, -slopes[g, r] * dist, MASKED)
    return out


def _swa_attention(qkv, v_t, sinks):
    t = qkv.shape[0]
    tq = SWA_TQ
    kcol = Q_DIM // LANES
    bias = jnp.asarray(_swa_bias())
    sink_rows = jnp.repeat(sinks.astype(F32).reshape(SWA_KV_HEADS, 1, SWA_GROUP), tq, axis=2)
    prev = lambda i: jnp.maximum(i - 1, 0)
    return pl.pallas_call(
        _swa_kernel,
        grid=(t // tq,),
        in_specs=[pl.BlockSpec((tq, Q_DIM), lambda i: (i, 0)),
                  pl.BlockSpec((tq, LANES), lambda i: (prev(i), kcol)),
                  pl.BlockSpec((tq, LANES), lambda i: (i, kcol)),
                  pl.BlockSpec((LANES, tq), lambda i: (0, prev(i))),
                  pl.BlockSpec((LANES, tq), lambda i: (0, i)),
                  pl.BlockSpec((1,) + bias.shape[1:], lambda i: (jnp.minimum(i, 1), 0, 0, 0)),
                  pl.BlockSpec(sink_rows.shape, lambda i: (0, 0, 0))],
        out_specs=pl.BlockSpec((tq, Q_DIM), lambda i: (i, 0)),
        out_shape=jax.ShapeDtypeStruct((t, Q_DIM), BF16),
        compiler_params=_cparams(("parallel",)),
        name="swa_attention",
    )(qkv, qkv, qkv, v_t, v_t, bias, sink_rows)


def _swa_heads(x, norm_g, w_in, sinks):
    qkv = _norm_proj(x, norm_g, w_in.astype(BF16), BF16)
    v_t = qkv[:, Q_DIM + SWA_KV_HEADS * HEAD_DIM:].T
    return _swa_attention(qkv, v_t, sinks)


FOX_PARTS = 3


def _split_bf16(v):
    parts, rest = [], v
    for _ in range(FOX_PARTS):
        p = rest.astype(BF16)
        parts.append(p)
        rest = rest - p.astype(F32)
    return parts


def _fox_prep_kernel(f_ref, b_ref, aug_ref, cref_ref, carry_ref):
    tm = f_ref.shape[0]

    @pl.when(pl.program_id(0) == 0)
    def _():
        carry_ref[...] = jnp.zeros_like(carry_ref)

    logf = jax.nn.log_sigmoid(f_ref[...] + b_ref[...])
    row = lax.broadcasted_iota(jnp.int32, (tm, tm), 0)
    col = lax.broadcasted_iota(jnp.int32, (tm, tm), 1)
    tri = jnp.where(row >= col, 1.0, 0.0).astype(BF16)
    local = sum(jnp.dot(tri, p, preferred_element_type=F32) for p in _split_bf16(logf))
    cum = local + carry_ref[...]
    carry_ref[...] = cum[tm - 1:tm, :]
    first = cum[0:1, :]
    cref_ref[0] = jnp.broadcast_to(first, cref_ref.shape[1:])
    hi, mid, lo = _split_bf16(first - cum)
    lane = lax.broadcasted_iota(jnp.int32, (tm, LANES), 1)
    zero = jnp.zeros_like(hi)
    aug_ref[...] = jnp.where(lane < N_HEADS, hi, jnp.where(lane < 2 * N_HEADS, mid,
                                                            jnp.where(lane < 3 * N_HEADS, lo, zero)))


def _fox_prep(f_logit3, f_bias3):
    t = f_logit3.shape[0]
    nt = t // FOX_T
    return pl.pallas_call(
        _fox_prep_kernel,
        grid=(nt,),
        in_specs=[pl.BlockSpec((FOX_T, LANES), lambda i: (i, 0)), pl.BlockSpec((1, LANES), lambda i: (0, 0))],
        out_specs=[pl.BlockSpec((FOX_T, LANES), lambda i: (i, 0)), pl.BlockSpec((1, 8, LANES), lambda i: (i, 0, 0))],
        out_shape=[jax.ShapeDtypeStruct((t, LANES), BF16), jax.ShapeDtypeStruct((nt, 8, LANES), F32)],
        scratch_shapes=[pltpu.VMEM((1, LANES), F32)],
        compiler_params=_cparams(("arbitrary",)),
        name="fox_prep",
    )(f_logit3, f_bias3)


def _fox_kernel(cref_ref, q_ref, k_ref, aug_ref, vt_ref, o_ref, m_ref, l_ref, acc_ref, qt_ref, s_ref):
    pair, qi = pl.program_id(0), pl.program_id(1)
    tq = q_ref.shape[0]
    tk = FOX_T
    lane = lax.broadcasted_iota(jnp.int32, (tq, LANES), 1)
    q2 = q_ref[...] * jnp.asarray(HEAD_DIM ** -0.5, BF16)
    for hh in range(2):
        head = 2 * pair + hh
        pick = (lane % N_HEADS == head) & (lane < FOX_PARTS * N_HEADS)
        q_ext = jnp.concatenate([jnp.where(lane // HEAD_DIM == hh, q2, jnp.zeros_like(q2)).astype(F32),
                                 jnp.where(pick, 1.0, 0.0)], axis=1)
        qt_ref[hh] = q_ext.T.astype(BF16)
    m_ref[...] = jnp.full_like(m_ref, M_INIT)
    l_ref[...] = jnp.zeros_like(l_ref)
    acc_ref[...] = jnp.zeros_like(acc_ref)
    ratio = tq // tk

    def scores(kj, hh):
        start = pl.multiple_of(kj * tk, tk)
        k_ext = jnp.concatenate([k_ref[pl.ds(start, tk), :], aug_ref[pl.ds(start, tk), :]], axis=1)
        s_ref[hh] = jnp.dot(k_ext, qt_ref[hh], preferred_element_type=F32)

    def consume(kj, hh, key_offset):
        head = 2 * pair + hh
        s = s_ref[hh]
        if key_offset is not None:
            key = lax.broadcasted_iota(jnp.int32, (tk, tq), 0) + key_offset
            qry = lax.broadcasted_iota(jnp.int32, (tk, tq), 1)
            s = jnp.where(key <= qry, s, MASKED)
        off = cref_ref[qi * ratio, head] - cref_ref[kj, head]
        m_old = m_ref[hh]
        m_new = jnp.maximum(m_old, jnp.max(s, axis=0, keepdims=True) + off)
        e = jnp.exp(s - (m_new - off))
        alpha = jnp.exp(m_old - m_new)
        m_ref[hh] = m_new
        l_ref[hh] = alpha * l_ref[hh] + jnp.sum(e, axis=0, keepdims=True)
        v_t = vt_ref[kj, hh * HEAD_DIM:(hh + 1) * HEAD_DIM, :]
        acc_ref[hh] = alpha * acc_ref[hh] + jnp.dot(v_t, e.astype(BF16), preferred_element_type=F32)

    n_full = qi * ratio
    scores(0, 0)

    def body(j, carry):
        for d in range(ratio):
            kj = j * ratio + d
            scores(kj, 1)
            consume(kj, 0, None)
            scores(kj + 1, 0)
            consume(kj, 1, None)
        return carry

    lax.fori_loop(0, qi, body, 0)
    for d in range(ratio):
        kj = n_full + d
        scores(kj, 1)
        consume(kj, 0, d * tk)
        if d + 1 < ratio:
            scores(kj + 1, 0)
        consume(kj, 1, d * tk)
    o_t = jnp.concatenate([acc_ref[hh] / l_ref[hh] for hh in range(2)], axis=0)
    o_ref[...] = o_t.T.astype(o_ref.dtype)


def _fox_attention(qkv, aug, v_t3, cref):
    t = qkv.shape[0]
    tq = FOX_TQ
    nk = t // FOX_T
    kcol = Q_DIM // LANES
    return pl.pallas_call(
        _fox_kernel,
        grid=(N_HEADS // 2, t // tq),
        in_specs=[pl.BlockSpec(memory_space=pltpu.SMEM),
                  pl.BlockSpec((tq, LANES), lambda p, i: (i, p)),
                  pl.BlockSpec((t, LANES), lambda p, i: (0, kcol + p)),
                  pl.BlockSpec((t, LANES), lambda p, i: (0, 0)),
                  pl.BlockSpec((nk, LANES, FOX_T), lambda p, i: (0, p, 0))],
        out_specs=pl.BlockSpec((tq, LANES), lambda p, i: (i, p)),
        out_shape=jax.ShapeDtypeStruct((t, Q_DIM), BF16),
        scratch_shapes=[pltpu.VMEM((2, 1, tq), F32), pltpu.VMEM((2, 1, tq), F32),
                        pltpu.VMEM((2, HEAD_DIM, tq), F32), pltpu.VMEM((2, 2 * LANES, tq), BF16), pltpu.VMEM((2, FOX_T, tq), F32)],
        compiler_params=_cparams(("parallel", "arbitrary")),
        name="fox_attention",
    )(cref, qkv, qkv, aug, v_t3)


def _fox_heads(x, norm_g, w_in, f_bias):
    t = x.shape[0]
    qkv = _norm_proj(x, norm_g, w_in[:, :3 * Q_DIM].astype(BF16), BF16)
    w_f = w_in[:, 3 * Q_DIM:]
    pad = jnp.zeros((D_MODEL, LANES - FOX_PARTS * N_HEADS), w_f.dtype)
    w_f3 = jnp.concatenate([w_f] * FOX_PARTS + [pad], axis=1).astype(BF16)
    b3 = jnp.concatenate([f_bias.astype(F32)] * FOX_PARTS + [jnp.zeros((LANES - FOX_PARTS * N_HEADS,), F32)])
    f_logit3 = _norm_proj(x, norm_g, w_f3, F32)
    aug, cref = _fox_prep(f_logit3, b3.reshape(1, LANES))
    v_t3 = qkv[:, 2 * Q_DIM:].reshape(t // FOX_T, FOX_T, Q_DIM).transpose(0, 2, 1)
    return _fox_attention(qkv, aug, v_t3, cref[:, 0, :N_HEADS])


def _nsa_compress_kernel(a_ref, pos_ref, w1_ref, w2_ref, o_ref):
    tc = a_ref.shape[2]
    half = CMP_STRIDE * HEAD_DIM
    pos = pos_ref[0]
    out = jnp.zeros(o_ref.shape[1:], F32)
    for gg in range(2):
        a = a_ref[0, gg]
        top = jnp.dot((a + pos[:, :half]).astype(BF16), w1_ref[0, :half], preferred_element_type=F32)
        bot = jnp.dot((a + pos[:, half:]).astype(BF16), w1_ref[0, half:], preferred_element_type=F32)
        hid = top + pltpu.roll(bot, tc - 1, 0)
        out = out + jnp.dot(jax.nn.gelu(hid).astype(BF16), w2_ref[0, gg], preferred_element_type=F32)
    o_ref[0] = out


def _nsa_compress(a, pos, w1, w2):
    _, g, tc, width = a.shape
    return pl.pallas_call(
        _nsa_compress_kernel,
        grid=(2, g // 2),
        in_specs=[pl.BlockSpec((1, 2, tc, width), lambda kv, gp: (kv, gp, 0, 0)),
                  pl.BlockSpec((1, 1, 2 * width), lambda kv, gp: (kv, 0, 0)),
                  pl.BlockSpec((1, 2 * width, CMP_HIDDEN), lambda kv, gp: (kv, 0, 0)),
                  pl.BlockSpec((1, 2, CMP_HIDDEN, LANES), lambda kv, gp: (kv, 0, 0, 0))],
        out_specs=pl.BlockSpec((1, tc, LANES), lambda kv, gp: (kv, 0, gp)),
        out_shape=jax.ShapeDtypeStruct((2, tc, g * HEAD_DIM), F32),
        compiler_params=_cparams(("parallel", "parallel")),
        name="nsa_compress",
    )(a, pos, w1, w2)


NSA_WIN_TILES = NSA_WINDOW // NSA_TQ + 1


def _nsa_kernel(slopes_ref, q_ref, ks_ref, *refs):
    kw_refs, refs = refs[:NSA_WIN_TILES], refs[NSA_WIN_TILES:]
    vst_ref, refs = refs[0], refs[1:]
    vw_refs, refs = refs[:NSA_WIN_TILES], refs[NSA_WIN_TILES:]
    (kcmp_ref, vov_ref, gates_ref, ksel_ext_ref, kwin_ext_ref, kcmp_ext_ref, qext_ref,
     o_ref, selt_ref, m_ref, l_ref, acc_ref, qt_ref, s_ref, scmp_ref, cm_ref, cl_ref, oc_ref,
     active_ref, list_ref) = refs
    g, qi = pl.program_id(0), pl.program_id(1)
    tq = q_ref.shape[0]
    tc = kcmp_ref.shape[0]
    n_sel = selt_ref.shape[0]
    rr = NSA_GROUP
    t0 = qi * tq
    hg = g % 2
    slope = [slopes_ref[g * rr + r] for r in range(rr)]

    half = _lane_half((tq, LANES))
    parts = []
    for r in range(rr):
        q_pair = q_ref[:, (r // 2) * LANES:(r // 2 + 1) * LANES]
        src = jnp.where(hg == r % 2, q_pair, _swap_halves(q_pair))
        parts.append(jnp.where(half == hg, src, jnp.zeros_like(src)) * jnp.asarray(HEAD_DIM ** -0.5, BF16))
    qp = jnp.concatenate(parts, axis=0)
    qt_ref[0:LANES, :] = qp.astype(F32).T.astype(BF16)

    @pl.when(qi == 0)
    def _():
        qt_ref[LANES:, :] = qext_ref[0]

    def lanes_of(s, r):
        return s[:, r * tq:(r + 1) * tq]

    chunk_span = NSA_CMP_CHUNK * CMP_STRIDE
    last_chunk = ((t0 + tq - CMP_BLOCK) // CMP_STRIDE) // NSA_CMP_CHUNK
    nl16_q = (CMP_STRIDE * lax.broadcasted_iota(jnp.int32, (NSA_CMP_CHUNK, tq), 0)
              - lax.broadcasted_iota(jnp.int32, (NSA_CMP_CHUNK, tq), 1)).astype(F32)
    cm_ref[...] = jnp.full_like(cm_ref, M_INIT)

    def cmp_rows(ch):
        return pl.ds(pl.multiple_of(ch * NSA_CMP_CHUNK, NSA_CMP_CHUNK), NSA_CMP_CHUNK)

    def cmp_offset(ch, r):
        return slope[r] * (ch * chunk_span).astype(F32)

    def cmp_scores(ch, masked):
        k_full = jnp.concatenate([kcmp_ref[cmp_rows(ch), :], kcmp_ext_ref[...]], axis=1)
        s = jnp.dot(k_full, qt_ref[...], preferred_element_type=F32)
        if masked:
            visible = nl16_q <= (t0 - (CMP_BLOCK - 1) - ch * chunk_span).astype(F32)
            s = jnp.concatenate([jnp.where(visible, lanes_of(s, r), MASKED) for r in range(rr)], axis=1)
        scmp_ref[cmp_rows(ch), :] = s
        for r in range(rr):
            cm_ref[r] = jnp.maximum(cm_ref[r], jnp.max(lanes_of(s, r), axis=0, keepdims=True) + cmp_offset(ch, r))

    def cmp_plain(ch, carry):
        cmp_scores(ch, False)
        return carry

    lax.fori_loop(0, jnp.maximum(last_chunk - 1, 0), cmp_plain, 0)

    @pl.when(last_chunk >= 1)
    def _():
        cmp_scores(last_chunk - 1, True)

    cmp_scores(last_chunk, True)

    cl_ref[...] = jnp.zeros_like(cl_ref)
    oc_ref[...] = jnp.zeros_like(oc_ref)

    def cmp_exp(ch, carry):
        s = scmp_ref[cmp_rows(ch), :]
        e_all = []
        for r in range(rr):
            e = jnp.exp(lanes_of(s, r) - (cm_ref[r] - cmp_offset(ch, r)))
            cl_ref[r] = cl_ref[r] + jnp.sum(e, axis=0, keepdims=True)
            e_all.append(e.astype(BF16))
        oc_ref[...] += jnp.dot(vov_ref[ch, 0], jnp.concatenate(e_all, axis=1), preferred_element_type=F32)
        return carry

    lax.fori_loop(0, last_chunk + 1, cmp_exp, 0)
    inv_l = [jnp.where(cl_ref[r] > 0.0, 1.0 / cl_ref[r], 0.0) for r in range(rr)]
    o_cmp = [oc_ref[0:HEAD_DIM, r * tq:(r + 1) * tq] * inv_l[r] for r in range(rr)]

    imp = functools.reduce(jnp.add, [oc_ref[HEAD_DIM:, r * tq:(r + 1) * tq] * inv_l[r] for r in range(rr)])
    blk = lax.broadcasted_iota(jnp.int32, (n_sel, tq), 0)
    cur = (t0 + lax.broadcasted_iota(jnp.int32, (n_sel, tq), 1)) // SEL_BLOCK
    blk_f = blk.astype(F32)
    forced = jnp.where(blk == 0, 1.0, jnp.where(blk == cur, 1.0, jnp.where(blk == cur - 1, 1.0, 0.0)))
    score = jnp.where(forced > 0.0, TAKEN, jnp.where(blk <= cur, imp, -1.0))

    def pick(_, sc):
        best = jnp.max(sc, axis=0, keepdims=True)
        first = jnp.min(jnp.where(sc == best, blk_f, float(n_sel)), axis=0, keepdims=True)
        return jnp.where(blk_f == first, TAKEN, sc)

    score = lax.fori_loop(0, N_SELECT - 3, pick, score)
    selt_ref[...] = jnp.where(blk <= cur, jnp.where(score == TAKEN, 1.0, 0.0), 0.0)

    m_ref[...] = jnp.full_like(m_ref, M_INIT)
    l_ref[...] = jnp.zeros_like(l_ref)
    acc_ref[...] = jnp.zeros_like(acc_ref)
    blocks_per_chunk = NSA_TK // SEL_BLOCK

    for c in range(n_sel // blocks_per_chunk):
        active_ref[c] = jnp.max(selt_ref[c * blocks_per_chunk:(c + 1) * blocks_per_chunk, :])


    def compact(c, n):
        list_ref[n] = c
        return n + (active_ref[c] > 0.0).astype(jnp.int32)

    n_active = lax.fori_loop(0, t0 // NSA_TK + 1, compact, 0)

    def sel_scores(c, buf):
        start = pl.multiple_of(c * NSA_TK, NSA_TK)
        rows = selt_ref[pl.ds(pl.multiple_of(c * blocks_per_chunk, blocks_per_chunk), blocks_per_chunk), :]
        mask_rows = jnp.concatenate([jnp.where(rows > 0.5, 0.0, MASKED), jnp.zeros_like(rows)], axis=0)
        first = LANES + buf * NSA_EXT_MASK_ROWS
        qt_ref[first:first + NSA_EXT_MASK_ROWS, :] = jnp.concatenate([mask_rows] * rr, axis=1).astype(BF16)
        k_full = jnp.concatenate([ks_ref[pl.ds(start, NSA_TK), :], ksel_ext_ref[buf]], axis=1)
        s_ref[buf] = jnp.dot(k_full, qt_ref[...], preferred_element_type=F32)

    def sel_consume(c, buf, diagonal):
        shift = (c * NSA_TK - t0).astype(F32)
        if diagonal:
            key = lax.broadcasted_iota(jnp.int32, (NSA_TK, tq), 0) + c * NSA_TK
            visible = key <= lax.broadcasted_iota(jnp.int32, (NSA_TK, tq), 1) + t0
        e_all, alpha_all = [], []
        for r in range(rr):
            off = slope[r] * shift
            s = s_ref[buf][:, r * tq:(r + 1) * tq]
            if diagonal:
                s = jnp.where(visible, s, MASKED)
            m_old = m_ref[r]
            m_new = jnp.maximum(m_old, jnp.max(s, axis=0, keepdims=True) + off)
            e = jnp.exp(s - (m_new - off))
            alpha = jnp.exp(m_old - m_new)
            m_ref[r] = m_new
            l_ref[r] = alpha * l_ref[r] + jnp.sum(e, axis=0, keepdims=True)
            e_all.append(e.astype(BF16))
            alpha_all.append(alpha)
        pv = jnp.dot(vst_ref[c], jnp.concatenate(e_all, axis=1), preferred_element_type=F32)
        acc_ref[...] = jnp.concatenate(alpha_all, axis=1) * acc_ref[...] + pv

    n_before = n_active - 1
    sel_scores(list_ref[0], 0)

    def sel_pair(j, carry):
        c0, c1, c2 = list_ref[2 * j], list_ref[2 * j + 1], list_ref[2 * j + 2]
        sel_scores(c1, 1)
        sel_consume(c0, 0, False)
        sel_scores(c2, 0)
        sel_consume(c1, 1, False)
        return carry

    lax.fori_loop(0, n_before // 2, sel_pair, 0)

    @pl.when(n_before % 2 == 1)
    def _():
        c0, c_last = list_ref[n_before - 1], list_ref[n_before]
        sel_scores(c_last, 1)
        sel_consume(c0, 0, False)
        sel_consume(c_last, 1, True)

    @pl.when(n_before % 2 == 0)
    def _():
        sel_consume(list_ref[n_before], 0, True)

    xw = lax.broadcasted_iota(jnp.int32, (tq, tq), 0)
    qw = lax.broadcasted_iota(jnp.int32, (tq, tq), 1)
    s_win, off_win = [], []
    for idx in range(NSA_WIN_TILES):
        back = NSA_WIN_TILES - 1 - idx
        k_full = jnp.concatenate([kw_refs[idx][...], kwin_ext_ref[...]], axis=1)
        s_t = jnp.dot(k_full, qt_ref[...], preferred_element_type=F32)
        per_head = [lanes_of(s_t, r) for r in range(rr)]
        if back == NSA_WIN_TILES - 1:
            per_head = [jnp.where(xw > qw, s, MASKED) for s in per_head]
        if back == 0:
            per_head = [jnp.where(xw <= qw, s, MASKED) for s in per_head]
        s_win.append(per_head)
        off_win.append([jnp.where(qi >= back, -slope[r] * float(back * tq), MASKED) for r in range(rr)])
    e_win, l_win = [], []
    for r in range(rr):
        m = functools.reduce(jnp.maximum, [jnp.max(s_win[idx][r], axis=0, keepdims=True) + off_win[idx][r]
                                           for idx in range(NSA_WIN_TILES)])
        e_tiles = [jnp.exp(s_win[idx][r] - (m - off_win[idx][r])) for idx in range(NSA_WIN_TILES)]
        l_win.append(functools.reduce(jnp.add, [jnp.sum(e, axis=0, keepdims=True) for e in e_tiles]))
        e_win.append(jnp.concatenate([e.astype(BF16) for e in e_tiles], axis=0))
    v_win = jnp.concatenate([vw_refs[idx][...] for idx in range(NSA_WIN_TILES)], axis=1)
    o_win_all = jnp.dot(v_win, jnp.concatenate(e_win, axis=1), preferred_element_type=F32)
    o_win = [lanes_of(o_win_all, r) / l_win[r] for r in range(rr)]

    gate = jax.nn.sigmoid(gates_ref[0])
    merged = []
    for r in range(rr):
        o_sel = lanes_of(acc_ref[...], r) / l_ref[r]
        merged.append(gate[3 * r:3 * r + 1] * o_cmp[r] + gate[3 * r + 1:3 * r + 2] * o_sel
                      + gate[3 * r + 2:3 * r + 3] * o_win[r])
    o_ref[...] = jnp.concatenate(merged, axis=0).T.astype(o_ref.dtype)


def _nsa_overlap_t(n_sel, tc):
    cmp_start = np.arange(tc - 1) * CMP_STRIDE
    sel_start = np.arange(n_sel) * SEL_BLOCK
    ov = np.clip(np.minimum(cmp_start[None, :] + CMP_BLOCK, sel_start[:, None] + SEL_BLOCK)
                 - np.maximum(cmp_start[None, :], sel_start[:, None]), 0, None) / CMP_BLOCK
    return np.concatenate([ov, np.zeros((n_sel, 1))], axis=1).astype(np.float32)


NSA_EXT_MASK_ROWS = 16
NSA_EXT_SEL = 32
NSA_EXT_WIN = 48
NSA_EXT_CMP = 64
NSA_CMP_CHUNK = 256
SLOPE_PIECES = 3


def _bf16_pieces(v):
    out, rest = [], np.asarray(v, np.float32)
    for _ in range(SLOPE_PIECES):
        p = rest.astype(BF16).astype(np.float32)
        out.append(p)
        rest = rest - p
    return out


def _nsa_extensions(tq):
    ksel = np.zeros((2, NSA_TK, LANES), np.float32)
    x = np.arange(NSA_TK)
    for buf in range(2):
        ksel[buf, x, buf * NSA_EXT_MASK_ROWS + x // SEL_BLOCK] = 1.0
    for j in range(SLOPE_PIECES):
        ksel[:, :, NSA_EXT_SEL + j] = 256 * (x // 256)
        ksel[:, :, NSA_EXT_SEL + SLOPE_PIECES + j] = x % 256
    kwin = np.zeros((tq, LANES), np.float32)
    for j in range(SLOPE_PIECES):
        kwin[:, NSA_EXT_WIN + j] = np.arange(tq)
    assert tq <= 256, "a window key offset must be exact in bf16"
    kcmp = np.zeros((NSA_CMP_CHUNK, LANES), np.float32)
    for j in range(SLOPE_PIECES):
        kcmp[:, NSA_EXT_CMP + j] = np.arange(NSA_CMP_CHUNK)
    pieces = _bf16_pieces(_alibi_slopes())
    qext = np.zeros((NSA_KV_HEADS, LANES, NSA_GROUP * tq), np.float32)
    for g in range(NSA_KV_HEADS):
        for r in range(NSA_GROUP):
            cols = slice(r * tq, (r + 1) * tq)
            for j in range(SLOPE_PIECES):
                p = pieces[j][g * NSA_GROUP + r]
                qext[g, NSA_EXT_SEL + j, cols] = p
                qext[g, NSA_EXT_SEL + SLOPE_PIECES + j, cols] = p
                qext[g, NSA_EXT_WIN + j, cols] = p
                qext[g, NSA_EXT_CMP + j, cols] = CMP_STRIDE * p
    return jnp.asarray(ksel, BF16), jnp.asarray(kwin, BF16), jnp.asarray(kcmp, BF16), jnp.asarray(qext, BF16)


def _nsa_attention(main, vs_t3, vw_t, k_cmp, v_cmp_t, gates_t):
    t = main.shape[0]
    tq = NSA_TQ
    tc = k_cmp.shape[0]
    n_sel = t // SEL_BLOCK
    nc = t // NSA_TK
    ks_col = Q_DIM // LANES
    kw_col = ks_col + NSA_KV_DIM // LANES
    n_ch = tc // NSA_CMP_CHUNK
    ov_t = jnp.asarray(_nsa_overlap_t(n_sel, tc).reshape(n_sel, n_ch, NSA_CMP_CHUNK).transpose(1, 0, 2), BF16)
    v_cmp_t = v_cmp_t.reshape(NSA_KV_HEADS, HEAD_DIM, n_ch, NSA_CMP_CHUNK).transpose(2, 0, 1, 3)
    vov = jnp.concatenate([v_cmp_t, jnp.broadcast_to(ov_t[:, None], (n_ch, NSA_KV_HEADS, n_sel, NSA_CMP_CHUNK))],
                          axis=2)
    slopes = jnp.asarray(_alibi_slopes())
    ksel_ext, kwin_ext, kcmp_ext, qext = _nsa_extensions(tq)

    def kw_spec(back):
        return pl.BlockSpec((tq, LANES), lambda g, i: (jnp.maximum(i - back, 0), kw_col + g // 2))

    def vw_spec(back):
        return pl.BlockSpec((HEAD_DIM, tq), lambda g, i: (g, jnp.maximum(i - back, 0)))

    backs = list(range(NSA_WIN_TILES - 1, -1, -1))
    return pl.pallas_call(
        _nsa_kernel,
        grid=(NSA_KV_HEADS, t // tq),
        in_specs=[pl.BlockSpec(memory_space=pltpu.SMEM),
                  pl.BlockSpec((tq, NSA_GROUP * HEAD_DIM), lambda g, i: (i, g)),
                  pl.BlockSpec((t, LANES), lambda g, i: (0, ks_col + g // 2))]
                 + [kw_spec(b) for b in backs]
                 + [pl.BlockSpec((nc, HEAD_DIM, NSA_TK), lambda g, i: (0, g, 0))]
                 + [vw_spec(b) for b in backs]
                 + [pl.BlockSpec((tc, LANES), lambda g, i: (0, g // 2)),
                    pl.BlockSpec((n_ch, 1, HEAD_DIM + n_sel, NSA_CMP_CHUNK), lambda g, i: (0, g, 0, 0)),
                    pl.BlockSpec((1, 16, tq), lambda g, i: (g, 0, i)),
                    pl.BlockSpec(ksel_ext.shape, lambda g, i: (0, 0, 0)),
                    pl.BlockSpec(kwin_ext.shape, lambda g, i: (0, 0)),
                    pl.BlockSpec(kcmp_ext.shape, lambda g, i: (0, 0)),
                    pl.BlockSpec((1,) + qext.shape[1:], lambda g, i: (g, 0, 0))],
        out_specs=pl.BlockSpec((tq, NSA_GROUP * HEAD_DIM), lambda g, i: (i, g)),
        out_shape=jax.ShapeDtypeStruct((t, Q_DIM), BF16),
        scratch_shapes=[pltpu.VMEM((n_sel, tq), F32), pltpu.VMEM((NSA_GROUP, 1, tq), F32),
                        pltpu.VMEM((NSA_GROUP, 1, tq), F32), pltpu.VMEM((HEAD_DIM, NSA_GROUP * tq), F32),
                        pltpu.VMEM((2 * LANES, NSA_GROUP * tq), BF16),
                        pltpu.VMEM((2, NSA_TK, NSA_GROUP * tq), F32),
                        pltpu.VMEM((tc, NSA_GROUP * tq), F32), pltpu.VMEM((NSA_GROUP, 1, tq), F32),
                        pltpu.VMEM((NSA_GROUP, 1, tq), F32), pltpu.VMEM((HEAD_DIM + n_sel, NSA_GROUP * tq), F32),
                        pltpu.SMEM((nc,), F32), pltpu.SMEM((nc,), jnp.int32)],
        compiler_params=_cparams(("parallel", "arbitrary")),
        name="nsa_attention",
    )(slopes, main, main, *([main] * NSA_WIN_TILES), vs_t3, *([vw_t] * NSA_WIN_TILES), k_cmp, vov, gates_t,
      ksel_ext, kwin_ext, kcmp_ext, qext)


def _nsa_heads(x, norm_g, w_in, cmp_pos, cmp_w1, cmp_w2):
    t = x.shape[0]
    kv = NSA_KV_DIM
    col = lambda i: w_in[:, Q_DIM + i * kv:Q_DIM + (i + 1) * kv]
    w_main = jnp.concatenate([w_in[:, :Q_DIM], col(2), col(4), col(3), col(5)], axis=1).astype(BF16)
    w_cmp = jnp.concatenate([col(0), col(1)], axis=1).astype(BF16)
    n_gate = 3 * N_HEADS
    w_gate = jnp.concatenate([w_in[:, Q_DIM + 6 * kv:], jnp.zeros((D_MODEL, LANES - n_gate), w_in.dtype)],
                             axis=1).astype(BF16)
    main = _norm_proj(x, norm_g, w_main, BF16)
    kcvc = _norm_proj(x, norm_g, w_cmp, F32)
    gate_logit = _norm_proj(x, norm_g, w_gate, F32)

    tc = t // CMP_STRIDE
    a = kcvc.reshape(tc, CMP_STRIDE, 2, NSA_KV_HEADS, HEAD_DIM).transpose(2, 3, 0, 1, 4)
    a = a.reshape(2, NSA_KV_HEADS, tc, CMP_STRIDE * HEAD_DIM)
    zeros = jnp.zeros_like(cmp_w2)
    w2 = jnp.stack([jnp.concatenate([cmp_w2, zeros], axis=2), jnp.concatenate([zeros, cmp_w2], axis=2)], axis=1)
    cmp = _nsa_compress(a, cmp_pos.reshape(2, 1, CMP_BLOCK * HEAD_DIM), cmp_w1.astype(BF16), w2.astype(BF16))
    k_cmp = cmp[0].astype(BF16)
    v_cmp_t = cmp[1].T.astype(BF16)

    off_vs = Q_DIM + 2 * kv
    vs_t3 = main[:, off_vs:off_vs + kv].reshape(t // NSA_TK, NSA_TK, kv).transpose(0, 2, 1)
    vw_t = main[:, off_vs + kv:].T
    gates_t = gate_logit[:, :n_gate].reshape(t, NSA_KV_HEADS, 3 * NSA_GROUP).transpose(1, 2, 0)
    gates_t = jnp.pad(gates_t, ((0, 0), (0, 16 - 3 * NSA_GROUP), (0, 0)))
    return _nsa_attention(main, vs_t3, vw_t, k_cmp, v_cmp_t, gates_t)


def kernel(x, attn_norm, mlp_norm, final_norm, nsa_w_in, nsa_cmp_pos, nsa_cmp_w1, nsa_cmp_w2, nsa_w_out,
           swa_w_in, swa_sinks, swa_w_out, fox_w_in, fox_f_bias, fox_w_out,
           mlp_w_up, mlp_conv_w, mlp_conv_b, mlp_w_down):
    assert x.shape[0] == 1, "the trunk is written for batch 1"
    h = x[0]
    for i in range(DEPTH):
        kind, j = i % N_MIXERS, i // N_MIXERS
        if kind == 0:
            o = _nsa_heads(h, attn_norm[i], nsa_w_in[j], nsa_cmp_pos[j], nsa_cmp_w1[j], nsa_cmp_w2[j])
            w_out = nsa_w_out[j]
        elif kind == 1:
            o = _swa_heads(h, attn_norm[i], swa_w_in[j], swa_sinks[j])
            w_out = swa_w_out[j]
        else:
            o = _fox_heads(h, attn_norm[i], fox_w_in[j], fox_f_bias[j])
            w_out = fox_w_out[j]
        h = _mixer_out_mlp(h, o, w_out.astype(BF16), mlp_norm[i], mlp_w_up[i].astype(BF16), mlp_conv_w[i],
                           mlp_conv_b[i].reshape(1, -1), mlp_w_down[i].astype(BF16),
                           final_g=final_norm if i == DEPTH - 1 else None)
    return h[None]
```

```python
import functools

import numpy as np
import jax
import jax.numpy as jnp
from jax import lax
from jax.experimental import pallas as pl
from jax.experimental.pallas import tpu as pltpu

F32 = jnp.float32
BF16 = jnp.bfloat16

D_MODEL = 1024
DEPTH = 4
N_MIXERS = 3
HEAD_DIM = 64
N_HEADS = 16
Q_DIM = N_HEADS * HEAD_DIM
ALIBI_MAX = 8.0
NORM_EPS = 1e-6
FORCE_SCORE = 1e4
MASKED = -2e30
M_INIT = -1e30
TAKEN = -(2.0 ** 127)

NSA_KV_HEADS = 4
NSA_GROUP = 4
NSA_KV_DIM = NSA_KV_HEADS * HEAD_DIM
CMP_BLOCK = 32
CMP_STRIDE = 16
CMP_HIDDEN = 256
SEL_BLOCK = 64
N_SELECT = 16
NSA_WINDOW = 512
NSA_TQ = 256
NSA_TK = 512

SWA_KV_HEADS = 2
SWA_GROUP = 8
SWA_WINDOW = 128
SWA_TQ = 128

FOX_TQ = 1024
FOX_T = 512

D_FF = 2816
LANES = 128
VMEM_LIMIT = 56 * 1024 * 1024


def _cparams(semantics, vmem=VMEM_LIMIT):
    return pltpu.CompilerParams(dimension_semantics=semantics, vmem_limit_bytes=vmem)


def _alibi_slopes():
    return np.asarray(2.0 ** (-ALIBI_MAX * np.arange(1, N_HEADS + 1) / N_HEADS), dtype=np.float32)


def _rms(x, g):
    ms = jnp.mean(x * x, axis=-1, keepdims=True)
    return x * lax.rsqrt(ms + NORM_EPS) * g


def _norm_proj_kernel(n_out, x_ref, g_ref, *refs):
    h = _rms(x_ref[...], g_ref[...]).astype(BF16)
    for w_ref, o_ref in zip(refs[:n_out], refs[n_out:]):
        o_ref[...] = jnp.dot(h, w_ref[...], preferred_element_type=F32).astype(o_ref.dtype)


def _pick_tile(n, candidates):
    for c in candidates:
        if n % c == 0:
            return c
    raise ValueError(f"no tile for {n}")


def _norm_proj(x, g, weights, out_dtypes):
    t, d = x.shape
    tm = _pick_tile(t, (1024, 512, 256, 128))
    return pl.pallas_call(
        functools.partial(_norm_proj_kernel, len(weights)),
        grid=(t // tm,),
        in_specs=[pl.BlockSpec((tm, d), lambda i: (i, 0)), pl.BlockSpec((1, d), lambda i: (0, 0))]
                 + [pl.BlockSpec(w.shape, lambda i: (0, 0)) for w in weights],
        out_specs=[pl.BlockSpec((tm, w.shape[1]), lambda i: (i, 0)) for w in weights],
        out_shape=[jax.ShapeDtypeStruct((t, w.shape[1]), dt) for w, dt in zip(weights, out_dtypes)],
        compiler_params=_cparams(("parallel",)),
        name="norm_proj",
    )(x, g.reshape(1, d), *weights)


MLP_HALO = 16


def _mixer_out_mlp_kernel(final_norm, x_ref, xh_ref, o_ref, oh_ref, wo_ref, g_ref, fg_ref, wa_ref, wg_ref,
                          cwa_ref, cwg_ref, cba_ref, cbg_ref, wd_ref, y_ref, x1_ref, h_ref, acc_ref, ua_ref, ug_ref):
    i, j = pl.program_id(0), pl.program_id(1)
    tm = x_ref.shape[0]

    @pl.when(j == 0)
    def _():
        wo = wo_ref[...]
        x1_ref[0:MLP_HALO, :] = xh_ref[...] + jnp.dot(oh_ref[...], wo, preferred_element_type=F32)
        x1_ref[MLP_HALO:, :] = x_ref[...] + jnp.dot(o_ref[...], wo, preferred_element_type=F32)
        halo = _rms(x1_ref[0:MLP_HALO, :], g_ref[...])
        h_ref[0:MLP_HALO, :] = jnp.where(i > 0, halo, 0.0).astype(BF16)
        h_ref[MLP_HALO:, :] = _rms(x1_ref[MLP_HALO:, :], g_ref[...]).astype(BF16)
        acc_ref[...] = jnp.zeros_like(acc_ref)

    h = h_ref[...]

    def conv(w_ref, cw_ref, cb_ref, u_ref):
        u_ref[...] = jnp.dot(h, w_ref[...], preferred_element_type=F32)
        cw = cw_ref[...]
        return (cw[0:1] * u_ref[MLP_HALO - 2:MLP_HALO - 2 + tm, :] + cw[1:2] * u_ref[MLP_HALO - 1:MLP_HALO - 1 + tm, :]
                + cw[2:3] * u_ref[MLP_HALO:MLP_HALO + tm, :] + cb_ref[...])

    a = conv(wa_ref, cwa_ref, cba_ref, ua_ref)
    gt = conv(wg_ref, cwg_ref, cbg_ref, ug_ref)
    act = (jax.nn.silu(gt) * a).astype(BF16)
    acc_ref[...] += jnp.dot(act, wd_ref[...], preferred_element_type=F32)

    @pl.when(j == pl.num_programs(1) - 1)
    def _():
        y = x1_ref[MLP_HALO:, :] + acc_ref[...]
        y_ref[...] = _rms(y, fg_ref[...]) if final_norm else y


def _mixer_out_mlp(x, o, w_out, g, w_up, conv_w, conv_b, w_down, final_g=None):
    t, d = x.shape
    f = w_down.shape[0]
    q = o.shape[1]
    tm = _pick_tile(t, (512, 256, 128))
    tf = _pick_tile(f, (1408, 256, 128))
    nf = f // tf
    hb = tm // MLP_HALO
    halo_map = lambda i, j: (jnp.maximum(i * hb - 1, 0), 0)
    fg = g if final_g is None else final_g
    return pl.pallas_call(
        functools.partial(_mixer_out_mlp_kernel, final_g is not None),
        grid=(t // tm, nf),
        in_specs=[pl.BlockSpec((tm, d), lambda i, j: (i, 0)),
                  pl.BlockSpec((MLP_HALO, d), halo_map),
                  pl.BlockSpec((tm, q), lambda i, j: (i, 0)),
                  pl.BlockSpec((MLP_HALO, q), halo_map),
                  pl.BlockSpec((q, d), lambda i, j: (0, 0)),
                  pl.BlockSpec((1, d), lambda i, j: (0, 0)),
                  pl.BlockSpec((1, d), lambda i, j: (0, 0)),
                  pl.BlockSpec((d, tf), lambda i, j: (0, j)),
                  pl.BlockSpec((d, tf), lambda i, j: (0, nf + j)),
                  pl.BlockSpec((3, tf), lambda i, j: (0, j)),
                  pl.BlockSpec((3, tf), lambda i, j: (0, nf + j)),
                  pl.BlockSpec((1, tf), lambda i, j: (0, j)),
                  pl.BlockSpec((1, tf), lambda i, j: (0, nf + j)),
                  pl.BlockSpec((tf, d), lambda i, j: (j, 0))],
        out_specs=pl.BlockSpec((tm, d), lambda i, j: (i, 0)),
        out_shape=jax.ShapeDtypeStruct((t, d), F32),
        scratch_shapes=[pltpu.VMEM((tm + MLP_HALO, d), F32), pltpu.VMEM((tm + MLP_HALO, d), BF16),
                        pltpu.VMEM((tm, d), F32),
                        pltpu.VMEM((tm + MLP_HALO, tf), F32), pltpu.VMEM((tm + MLP_HALO, tf), F32)],
        compiler_params=_cparams(("parallel", "arbitrary")),
        name="mixer_out_mlp",
    )(x, x, o, o, w_out, g.reshape(1, d), fg.reshape(1, d), w_up, w_up, conv_w, conv_w, conv_b, conv_b, w_down)


def _swap_halves(q_pair):
    return pltpu.roll(q_pair.astype(F32), HEAD_DIM, 1).astype(BF16)


def _lane_half(shape):
    return lax.broadcasted_iota(jnp.int32, shape, 1) // HEAD_DIM


def _rows_to_heads(o_t, n_heads, tq):
    stacked = jnp.concatenate([o_t[:, r * tq:(r + 1) * tq] for r in range(n_heads)], axis=0)
    return stacked.T


def _swa_kernel(q_ref, kp_ref, kc_ref, vp_ref, vc_ref, bias_ref, sink_ref, o_ref):
    tq = q_ref.shape[0]
    rows = SWA_GROUP * tq
    k_ext = jnp.concatenate([kp_ref[...], kc_ref[...]], axis=0)
    v_ext = jnp.concatenate([vp_ref[...], vc_ref[...]], axis=1)
    half = _lane_half((tq, LANES))
    outs = []
    for g in range(SWA_KV_HEADS):
        parts = []
        for r in range(SWA_GROUP):
            hd = g * SWA_GROUP + r
            q_pair = q_ref[:, (hd // 2) * LANES:(hd // 2 + 1) * LANES]
            src = q_pair if hd % 2 == g else _swap_halves(q_pair)
            parts.append(jnp.where(half == g, src, jnp.zeros_like(src)))
        qp = jnp.concatenate(parts, axis=0)
        s = lax.dot_general(k_ext, qp, (((1,), (1,)), ((), ())), preferred_element_type=F32)
        s = s * (HEAD_DIM ** -0.5) + bias_ref[0, g]
        sink = sink_ref[g]
        m = jnp.maximum(jnp.max(s, axis=0, keepdims=True), sink)
        e = jnp.exp(s - m)
        l = jnp.sum(e, axis=0, keepdims=True) + jnp.exp(sink - m)
        o_t = jnp.dot(v_ext[g * HEAD_DIM:(g + 1) * HEAD_DIM], e.astype(BF16), preferred_element_type=F32)
        outs.append(_rows_to_heads(o_t / l, SWA_GROUP, tq))
    o_ref[...] = jnp.concatenate(outs, axis=1).astype(o_ref.dtype)


def _swa_bias():
    tq = SWA_TQ
    slopes = _alibi_slopes().reshape(SWA_KV_HEADS, SWA_GROUP)
    x = np.arange(2 * tq)[:, None] - tq
    q = np.arange(tq)[None, :]
    dist = q - x
    valid = (dist >= 0) & (dist < SWA_WINDOW)
    out = np.empty((2, SWA_KV_HEADS, 2 * tq, SWA_GROUP * tq), np.float32)
    for var in range(2):
        v = valid & ((x >= 0) | (var == 1))
        for g in range(SWA_KV_HEADS):
            for r in range(SWA_GROUP):
                out[var, g, :, r * tq:(r + 1) * tq] = np.where(v, -slopes[g, r] * dist, MASKED)
    return out


def _swa_attention(qkv, v_t, sinks):
    t = qkv.shape[0]
    tq = SWA_TQ
    kcol = Q_DIM // LANES
    bias = jnp.asarray(_swa_bias())
    sink_rows = jnp.repeat(sinks.astype(F32).reshape(SWA_KV_HEADS, 1, SWA_GROUP), tq, axis=2)
    prev = lambda i: jnp.maximum(i - 1, 0)
    return pl.pallas_call(
        _swa_kernel,
        grid=(t // tq,),
        in_specs=[pl.BlockSpec((tq, Q_DIM), lambda i: (i, 0)),
                  pl.BlockSpec((tq, LANES), lambda i: (prev(i), kcol)),
                  pl.BlockSpec((tq, LANES), lambda i: (i, kcol)),
                  pl.BlockSpec((LANES, tq), lambda i: (0, prev(i))),
                  pl.BlockSpec((LANES, tq), lambda i: (0, i)),
                  pl.BlockSpec((1,) + bias.shape[1:], lambda i: (jnp.minimum(i, 1), 0, 0, 0)),
                  pl.BlockSpec(sink_rows.shape, lambda i: (0, 0, 0))],
        out_specs=pl.BlockSpec((tq, Q_DIM), lambda i: (i, 0)),
        out_shape=jax.ShapeDtypeStruct((t, Q_DIM), BF16),
        compiler_params=_cparams(("parallel",)),
        name="swa_attention",
    )(qkv, qkv, qkv, v_t, v_t, bias, sink_rows)


def _swa_heads(x, norm_g, w_in, sinks):
    qkv, = _norm_proj(x, norm_g, [w_in.astype(BF16)], [BF16])
    v_t =qkv[:, Q_DIM + SWA_KV_HEADS * HEAD_DIM:].T
    return _swa_attention(qkv, v_t, sinks)


FOX_PARTS = 3


def _split_bf16(v):
    parts, rest = [], v
    for _ in range(FOX_PARTS):
        p = rest.astype(BF16)
        parts.append(p)
        rest = rest - p.astype(F32)
    return parts


def _fox_prep_kernel(f_ref, b_ref, aug_ref, cref_ref, carry_ref):
    tm = f_ref.shape[0]

    @pl.when(pl.program_id(0) == 0)
    def _():
        carry_ref[...] = jnp.zeros_like(carry_ref)

    logf = jax.nn.log_sigmoid(f_ref[...] + b_ref[...])
    row = lax.broadcasted_iota(jnp.int32, (tm, tm), 0)
    col = lax.broadcasted_iota(jnp.int32, (tm, tm), 1)
    tri = jnp.where(row >= col, 1.0, 0.0).astype(BF16)
    local = sum(jnp.dot(tri, p, preferred_element_type=F32) for p in _split_bf16(logf))
    cum = local + carry_ref[...]
    carry_ref[...] = cum[tm - 1:tm, :]
    first = cum[0:1, :]
    cref_ref[0] = jnp.broadcast_to(first, cref_ref.shape[1:])
    hi, mid, lo = _split_bf16(first - cum)
    lane = lax.broadcasted_iota(jnp.int32, (tm, LANES), 1)
    zero = jnp.zeros_like(hi)
    aug_ref[...] = jnp.where(lane < N_HEADS, hi, jnp.where(lane < 2 * N_HEADS, mid,
                                                            jnp.where(lane < 3 * N_HEADS, lo, zero)))


def _fox_prep(f_logit3, f_bias3):
    t = f_logit3.shape[0]
    nt = t // FOX_T
    return pl.pallas_call(
        _fox_prep_kernel,
        grid=(nt,),
        in_specs=[pl.BlockSpec((FOX_T, LANES), lambda i: (i, 0)), pl.BlockSpec((1, LANES), lambda i: (0, 0))],
        out_specs=[pl.BlockSpec((FOX_T, LANES), lambda i: (i, 0)), pl.BlockSpec((1, 8, LANES), lambda i: (i, 0, 0))],
        out_shape=[jax.ShapeDtypeStruct((t, LANES), BF16), jax.ShapeDtypeStruct((nt, 8, LANES), F32)],
        scratch_shapes=[pltpu.VMEM((1, LANES), F32)],
        compiler_params=_cparams(("arbitrary",)),
        name="fox_prep",
    )(f_logit3, f_bias3)


def _fox_kernel(cref_ref, q_ref, k_ref, aug_ref, vt_ref, o_ref, m_ref, l_ref, acc_ref, qt_ref, s_ref):
    pair, qi = pl.program_id(0), pl.program_id(1)
    tq = q_ref.shape[0]
    tk = FOX_T
    lane = lax.broadcasted_iota(jnp.int32, (tq, LANES), 1)
    q2 = q_ref[...] * jnp.asarray(HEAD_DIM ** -0.5, BF16)
    for hh in range(2):
        head = 2 * pair + hh
        pick = (lane % N_HEADS == head) & (lane < FOX_PARTS * N_HEADS)
        q_ext = jnp.concatenate([jnp.where(lane // HEAD_DIM == hh, q2, jnp.zeros_like(q2)).astype(F32),
                                 jnp.where(pick, 1.0, 0.0)], axis=1)
        qt_ref[hh] = q_ext.T.astype(BF16)
    m_ref[...] = jnp.full_like(m_ref, M_INIT)
    l_ref[...] = jnp.zeros_like(l_ref)
    acc_ref[...] = jnp.zeros_like(acc_ref)
    ratio = tq // tk

    def scores(kj, hh):
        start = pl.multiple_of(kj * tk, tk)
        k_ext = jnp.concatenate([k_ref[pl.ds(start, tk), :], aug_ref[pl.ds(start, tk), :]], axis=1)
        s_ref[hh] = jnp.dot(k_ext, qt_ref[hh], preferred_element_type=F32)

    def consume(kj, hh, key_offset):
        head = 2 * pair + hh
        s = s_ref[hh]
        if key_offset is not None:
            key = lax.broadcasted_iota(jnp.int32, (tk, tq), 0) + key_offset
            qry = lax.broadcasted_iota(jnp.int32, (tk, tq), 1)
            s = jnp.where(key <= qry, s, MASKED)
        off = cref_ref[qi * ratio, head] - cref_ref[kj, head]
        m_old = m_ref[hh]
        m_new = jnp.maximum(m_old, jnp.max(s, axis=0, keepdims=True) + off)
        e = jnp.exp(s - (m_new - off))
        alpha = jnp.exp(m_old - m_new)
        m_ref[hh] = m_new
        l_ref[hh] = alpha * l_ref[hh] + jnp.sum(e, axis=0, keepdims=True)
        v_t = vt_ref[kj, hh * HEAD_DIM:(hh + 1) * HEAD_DIM, :]
        acc_ref[hh] = alpha * acc_ref[hh] + jnp.dot(v_t, e.astype(BF16), preferred_element_type=F32)

    n_full = qi * ratio
    scores(0, 0)

    def body(j, carry):
        for d in range(ratio):
            kj = j * ratio + d
            scores(kj, 1)
            consume(kj, 0, None)
            scores(kj + 1, 0)
            consume(kj, 1, None)
        return carry

    lax.fori_loop(0, qi, body, 0)
    for d in range(ratio):
        kj = n_full + d
        scores(kj, 1)
        consume(kj, 0, d * tk)
        if d + 1 < ratio:
            scores(kj + 1, 0)
        consume(kj, 1, d * tk)
    o_t = jnp.concatenate([acc_ref[hh] / l_ref[hh] for hh in range(2)], axis=0)
    o_ref[...] = o_t.T.astype(o_ref.dtype)


def _fox_attention(qkv, aug, v_t3, cref):
    t = qkv.shape[0]
    tq = FOX_TQ
    nk = t // FOX_T
    kcol = Q_DIM // LANES
    return pl.pallas_call(
        _fox_kernel,
        grid=(N_HEADS // 2, t // tq),
        in_specs=[pl.BlockSpec(memory_space=pltpu.SMEM),
                  pl.BlockSpec((tq, LANES), lambda p, i: (i, p)),
                  pl.BlockSpec((t, LANES), lambda p, i: (0, kcol + p)),
                  pl.BlockSpec((t, LANES), lambda p, i: (0, 0)),
                  pl.BlockSpec((nk, LANES, FOX_T), lambda p, i: (0, p, 0))],
        out_specs=pl.BlockSpec((tq, LANES), lambda p, i: (i, p)),
        out_shape=jax.ShapeDtypeStruct((t, Q_DIM), BF16),
        scratch_shapes=[pltpu.VMEM((2, 1, tq), F32), pltpu.VMEM((2, 1, tq), F32),
                        pltpu.VMEM((2, HEAD_DIM, tq), F32), pltpu.VMEM((2, 2 * LANES, tq), BF16), pltpu.VMEM((2, FOX_T, tq), F32)],
        compiler_params=_cparams(("parallel", "arbitrary")),
        name="fox_attention",
    )(cref, qkv, qkv, aug, v_t3)


def _fox_heads(x, norm_g, w_in, f_bias):
    t = x.shape[0]
    w_f = w_in[:, 3 * Q_DIM:]
    pad = jnp.zeros((D_MODEL, LANES - FOX_PARTS * N_HEADS), w_f.dtype)
    w_f3 = jnp.concatenate([w_f] * FOX_PARTS + [pad], axis=1).astype(BF16)
    b3 = jnp.concatenate([f_bias.astype(F32)] * FOX_PARTS + [jnp.zeros((LANES - FOX_PARTS * N_HEADS,), F32)])
    qkv, f_logit3 = _norm_proj(x, norm_g, [w_in[:, :3 * Q_DIM].astype(BF16), w_f3], [BF16, F32])
    aug, cref = _fox_prep(f_logit3, b3.reshape(1, LANES))
    v_t3 = qkv[:, 2 * Q_DIM:].reshape(t // FOX_T, FOX_T, Q_DIM).transpose(0, 2, 1)
    return _fox_attention(qkv, aug, v_t3, cref[:, 0, :N_HEADS])


def _nsa_compress_kernel(a_ref, pos_ref, w1_ref, w2_ref, o_ref):
    tc = a_ref.shape[2]
    half = CMP_STRIDE * HEAD_DIM
    pos = pos_ref[0]
    out = jnp.zeros(o_ref.shape[1:], F32)
    for gg in range(2):
        a = a_ref[0, gg]
        top = jnp.dot((a + pos[:, :half]).astype(BF16), w1_ref[0, :half], preferred_element_type=F32)
        bot = jnp.dot((a + pos[:, half:]).astype(BF16), w1_ref[0, half:], preferred_element_type=F32)
        hid = top + pltpu.roll(bot, tc - 1, 0)
        out = out + jnp.dot(jax.nn.gelu(hid).astype(BF16), w2_ref[0, gg], preferred_element_type=F32)
    o_ref[0] = out


def _nsa_compress(a, pos, w1, w2):
    _, g, tc, width = a.shape
    return pl.pallas_call(
        _nsa_compress_kernel,
        grid=(2, g // 2),
        in_specs=[pl.BlockSpec((1, 2, tc, width), lambda kv, gp: (kv, gp, 0, 0)),
                  pl.BlockSpec((1, 1, 2 * width), lambda kv, gp: (kv, 0, 0)),
                  pl.BlockSpec((1, 2 * width, CMP_HIDDEN), lambda kv, gp: (kv, 0, 0)),
                  pl.BlockSpec((1, 2, CMP_HIDDEN, LANES), lambda kv, gp: (kv, 0, 0, 0))],
        out_specs=pl.BlockSpec((1, tc, LANES), lambda kv, gp: (kv, 0, gp)),
        out_shape=jax.ShapeDtypeStruct((2, tc, g * HEAD_DIM), F32),
        compiler_params=_cparams(("parallel", "parallel")),
        name="nsa_compress",
    )(a, pos, w1, w2)


NSA_WIN_TILES = NSA_WINDOW // NSA_TQ + 1


def _nsa_kernel(slopes_ref, q_ref, ks_ref, *refs):
    kw_refs, refs = refs[:NSA_WIN_TILES], refs[NSA_WIN_TILES:]
    vst_ref, refs = refs[0], refs[1:]
    vw_refs, refs = refs[:NSA_WIN_TILES], refs[NSA_WIN_TILES:]
    (kcmp_ref, vov_ref, gates_ref, ksel_ext_ref, kwin_ext_ref, kcmp_ext_ref, qext_ref,
     o_ref, selt_ref, m_ref, l_ref, acc_ref, qt_ref, s_ref, scmp_ref, cm_ref, cl_ref, oc_ref,
     active_ref, list_ref) = refs
    g, qi = pl.program_id(0), pl.program_id(1)
    tq = q_ref.shape[0]
    tc = kcmp_ref.shape[0]
    n_sel = selt_ref.shape[0]
    rr = NSA_GROUP
    t0 = qi * tq
    hg = g % 2
    slope = [slopes_ref[g * rr + r] for r in range(rr)]

    half = _lane_half((tq, LANES))
    parts = []
    for r in range(rr):
        q_pair = q_ref[:, (r // 2) * LANES:(r // 2 + 1) * LANES]
        src = jnp.where(hg == r % 2, q_pair, _swap_halves(q_pair))
        parts.append(jnp.where(half == hg, src, jnp.zeros_like(src)) * jnp.asarray(HEAD_DIM ** -0.5, BF16))
    qp = jnp.concatenate(parts, axis=0)
    qt_ref[0:LANES, :] = qp.astype(F32).T.astype(BF16)

    @pl.when(qi == 0)
    def _():
        qt_ref[LANES:, :] = qext_ref[0]

    def lanes_of(s, r):
        return s[:, r * tq:(r + 1) * tq]

    chunk_span = NSA_CMP_CHUNK * CMP_STRIDE
    last_chunk = ((t0 + tq - CMP_BLOCK) // CMP_STRIDE) // NSA_CMP_CHUNK
    nl16_q = (CMP_STRIDE * lax.broadcasted_iota(jnp.int32, (NSA_CMP_CHUNK, tq), 0)
              - lax.broadcasted_iota(jnp.int32, (NSA_CMP_CHUNK, tq), 1)).astype(F32)
    cm_ref[...] = jnp.full_like(cm_ref, M_INIT)

    def cmp_rows(ch):
        return pl.ds(pl.multiple_of(ch * NSA_CMP_CHUNK, NSA_CMP_CHUNK), NSA_CMP_CHUNK)

    def cmp_offset(ch, r):
        return slope[r] * (ch * chunk_span).astype(F32)

    def cmp_scores(ch, masked):
        k_full = jnp.concatenate([kcmp_ref[cmp_rows(ch), :], kcmp_ext_ref[...]], axis=1)
        s = jnp.dot(k_full, qt_ref[...], preferred_element_type=F32)
        if masked:
            visible = nl16_q <= (t0 - (CMP_BLOCK - 1) - ch * chunk_span).astype(F32)
            s = jnp.concatenate([jnp.where(visible, lanes_of(s, r), MASKED) for r in range(rr)], axis=1)
        scmp_ref[cmp_rows(ch), :] = s
        for r in range(rr):
            cm_ref[r] = jnp.maximum(cm_ref[r], jnp.max(lanes_of(s, r), axis=0, keepdims=True) + cmp_offset(ch, r))

    def cmp_plain(ch, carry):
        cmp_scores(ch, False)
        return carry

    lax.fori_loop(0, jnp.maximum(last_chunk - 1, 0), cmp_plain, 0)

    @pl.when(last_chunk >= 1)
    def _():
        cmp_scores(last_chunk - 1, True)

    cmp_scores(last_chunk, True)

    cl_ref[...] = jnp.zeros_like(cl_ref)
    oc_ref[...] = jnp.zeros_like(oc_ref)

    def cmp_exp(ch, carry):
        s = scmp_ref[cmp_rows(ch), :]
        e_all = []
        for r in range(rr):
            e = jnp.exp(lanes_of(s, r) - (cm_ref[r] - cmp_offset(ch, r)))
            cl_ref[r] = cl_ref[r] + jnp.sum(e, axis=0, keepdims=True)
            e_all.append(e.astype(BF16))
        oc_ref[...] += jnp.dot(vov_ref[ch, 0], jnp.concatenate(e_all, axis=1), preferred_element_type=F32)
        return carry

    lax.fori_loop(0, last_chunk + 1, cmp_exp, 0)
    inv_l = [jnp.where(cl_ref[r] > 0.0, 1.0 / cl_ref[r], 0.0) for r in range(rr)]
    o_cmp = [oc_ref[0:HEAD_DIM, r * tq:(r + 1) * tq] * inv_l[r] for r in range(rr)]

    imp = functools.reduce(jnp.add, [oc_ref[HEAD_DIM:, r * tq:(r + 1) * tq] * inv_l[r] for r in range(rr)])
    blk = lax.broadcasted_iota(jnp.int32, (n_sel, tq), 0)
    cur = (t0 + lax.broadcasted_iota(jnp.int32, (n_sel, tq), 1)) // SEL_BLOCK
    blk_f = blk.astype(F32)
    forced = jnp.where(blk == 0, 1.0, jnp.where(blk == cur, 1.0, jnp.where(blk == cur - 1, 1.0, 0.0)))
    score = jnp.where(forced > 0.0, TAKEN, jnp.where(blk <= cur, imp, -1.0))

    selt_ref[...] = score
    row_step = min(n_sel, NSA_TOPK_ROW_STEP)
    variant = ((t0 + tq - 1) // SEL_BLOCK) // row_step

    for v in range(n_sel // row_step):
        rows = row_step * (v + 1)

        @pl.when(variant == v)
        def _(rows=rows):
            ids = lax.broadcasted_iota(jnp.int32, (rows, tq), 0).astype(F32)

            def pick(_, sc):
                best = jnp.max(sc, axis=0, keepdims=True)
                first = jnp.min(jnp.where(sc == best, ids, float(n_sel)), axis=0, keepdims=True)
                return jnp.where(ids == first, TAKEN, sc)

            selt_ref[0:rows, :] = lax.fori_loop(0, N_SELECT - 3, pick, selt_ref[0:rows, :])

    selt_ref[...] = jnp.where(blk <= cur, jnp.where(selt_ref[...] == TAKEN, 1.0, 0.0), 0.0)

    m_ref[...] = jnp.full_like(m_ref, M_INIT)
    l_ref[...] = jnp.zeros_like(l_ref)
    acc_ref[...] = jnp.zeros_like(acc_ref)
    blocks_per_chunk = NSA_TK // SEL_BLOCK

    for c in range(n_sel // blocks_per_chunk):
        active_ref[c] = jnp.max(selt_ref[c * blocks_per_chunk:(c + 1) * blocks_per_chunk, :])


    def compact(c, n):
        list_ref[n] = c
        return n + (active_ref[c] > 0.0).astype(jnp.int32)

    n_active = lax.fori_loop(0, t0 // NSA_TK + 1, compact, 0)

    def sel_scores(c, buf):
        start = pl.multiple_of(c * NSA_TK, NSA_TK)
        rows = selt_ref[pl.ds(pl.multiple_of(c * blocks_per_chunk, blocks_per_chunk), blocks_per_chunk), :]
        mask_rows = jnp.concatenate([jnp.where(rows > 0.5, 0.0, MASKED), jnp.zeros_like(rows)], axis=0)
        first = LANES + buf * NSA_EXT_MASK_ROWS
        qt_ref[first:first + NSA_EXT_MASK_ROWS, :] = jnp.concatenate([mask_rows] * rr, axis=1).astype(BF16)
        k_full = jnp.concatenate([ks_ref[pl.ds(start, NSA_TK), :], ksel_ext_ref[buf]], axis=1)
        s_ref[buf] = jnp.dot(k_full, qt_ref[...], preferred_element_type=F32)

    def sel_consume(c, buf, diagonal):
        shift = (c * NSA_TK - t0).astype(F32)
        if diagonal:
            key = lax.broadcasted_iota(jnp.int32, (NSA_TK, tq), 0) + c * NSA_TK
            visible = key <= lax.broadcasted_iota(jnp.int32, (NSA_TK, tq), 1) + t0
        e_all, alpha_all = [], []
        for r in range(rr):
            off = slope[r] * shift
            s = s_ref[buf][:, r * tq:(r + 1) * tq]
            if diagonal:
                s = jnp.where(visible, s, MASKED)
            m_old = m_ref[r]
            m_new = jnp.maximum(m_old, jnp.max(s, axis=0, keepdims=True) + off)
            e = jnp.exp(s - (m_new - off))
            alpha = jnp.exp(m_old - m_new)
            m_ref[r] = m_new
            l_ref[r] = alpha * l_ref[r] + jnp.sum(e, axis=0, keepdims=True)
            e_all.append(e.astype(BF16))
            alpha_all.append(alpha)
        pv = jnp.dot(vst_ref[c], jnp.concatenate(e_all, axis=1), preferred_element_type=F32)
        acc_ref[...] = jnp.concatenate(alpha_all, axis=1) * acc_ref[...] + pv

    n_before = n_active - 1
    sel_scores(list_ref[0], 0)

    def sel_pair(j, carry):
        c0, c1, c2 = list_ref[2 * j], list_ref[2 * j + 1], list_ref[2 * j + 2]
        sel_scores(c1, 1)
        sel_consume(c0, 0, False)
        sel_scores(c2, 0)
        sel_consume(c1, 1, False)
        return carry

    lax.fori_loop(0, n_before // 2, sel_pair, 0)

    @pl.when(n_before % 2 == 1)
    def _():
        c0, c_last = list_ref[n_before - 1], list_ref[n_before]
        sel_scores(c_last, 1)
        sel_consume(c0, 0, False)
        sel_consume(c_last, 1, True)

    @pl.when(n_before % 2 == 0)
    def _():
        sel_consume(list_ref[n_before], 0, True)

    xw = lax.broadcasted_iota(jnp.int32, (tq, tq), 0)
    qw = lax.broadcasted_iota(jnp.int32, (tq, tq), 1)
    s_win, off_win = [], []
    for idx in range(NSA_WIN_TILES):
        back = NSA_WIN_TILES - 1 - idx
        k_full = jnp.concatenate([kw_refs[idx][...], kwin_ext_ref[...]], axis=1)
        s_t = jnp.dot(k_full, qt_ref[...], preferred_element_type=F32)
        per_head = [lanes_of(s_t, r) for r in range(rr)]
        if back == NSA_WIN_TILES - 1:
            per_head = [jnp.where(xw > qw, s, MASKED) for s in per_head]
        if back == 0:
            per_head = [jnp.where(xw <= qw, s, MASKED) for s in per_head]
        s_win.append(per_head)
        off_win.append([jnp.where(qi >= back, -slope[r] * float(back * tq), MASKED) for r in range(rr)])
    e_win, l_win = [], []
    for r in range(rr):
        m = functools.reduce(jnp.maximum, [jnp.max(s_win[idx][r], axis=0, keepdims=True) + off_win[idx][r]
                                           for idx in range(NSA_WIN_TILES)])
        e_tiles = [jnp.exp(s_win[idx][r] - (m - off_win[idx][r])) for idx in range(NSA_WIN_TILES)]
        l_win.append(functools.reduce(jnp.add, [jnp.sum(e, axis=0, keepdims=True) for e in e_tiles]))
        e_win.append(jnp.concatenate([e.astype(BF16) for e in e_tiles], axis=0))
    v_win = jnp.concatenate([vw_refs[idx][...] for idx in range(NSA_WIN_TILES)], axis=1)
    o_win_all = jnp.dot(v_win, jnp.concatenate(e_win, axis=1), preferred_element_type=F32)
    o_win = [lanes_of(o_win_all, r) / l_win[r] for r in range(rr)]

    gate = jax.nn.sigmoid(gates_ref[0])
    merged = []
    for r in range(rr):
        o_sel = lanes_of(acc_ref[...], r) / l_ref[r]
        merged.append(gate[3 * r:3 * r + 1] * o_cmp[r] + gate[3 * r + 1:3 * r + 2] * o_sel
                      + gate[3 * r + 2:3 * r + 3] * o_win[r])
    o_ref[...] = jnp.concatenate(merged, axis=0).T.astype(o_ref.dtype)


def _nsa_overlap_t(n_sel, tc):
    cmp_start = np.arange(tc - 1) * CMP_STRIDE
    sel_start = np.arange(n_sel) * SEL_BLOCK
    ov = np.clip(np.minimum(cmp_start[None, :] + CMP_BLOCK, sel_start[:, None] + SEL_BLOCK)
                 - np.maximum(cmp_start[None, :], sel_start[:, None]), 0, None) / CMP_BLOCK
    return np.concatenate([ov, np.zeros((n_sel, 1))], axis=1).astype(np.float32)


NSA_EXT_MASK_ROWS = 16
NSA_EXT_SEL = 32
NSA_EXT_WIN = 48
NSA_EXT_CMP = 64
NSA_CMP_CHUNK = 256
NSA_TOPK_ROW_STEP = 64
SLOPE_PIECES = 3


def _bf16_pieces(v):
    out, rest = [], np.asarray(v, np.float32)
    for _ in range(SLOPE_PIECES):
        p = rest.astype(BF16).astype(np.float32)
        out.append(p)
        rest = rest - p
    return out


def _nsa_extensions(tq):
    ksel = np.zeros((2, NSA_TK, LANES), np.float32)
    x = np.arange(NSA_TK)
    for buf in range(2):
        ksel[buf, x, buf * NSA_EXT_MASK_ROWS + x // SEL_BLOCK] = 1.0
    for j in range(SLOPE_PIECES):
        ksel[:, :, NSA_EXT_SEL + j] = 256 * (x // 256)
        ksel[:, :, NSA_EXT_SEL + SLOPE_PIECES + j] = x % 256
    kwin = np.zeros((tq, LANES), np.float32)
    for j in range(SLOPE_PIECES):
        kwin[:, NSA_EXT_WIN + j] = np.arange(tq)
    assert tq <= 256, "a window key offset must be exact in bf16"
    kcmp = np.zeros((NSA_CMP_CHUNK, LANES), np.float32)
    for j in range(SLOPE_PIECES):
        kcmp[:, NSA_EXT_CMP + j] = np.arange(NSA_CMP_CHUNK)
    pieces = _bf16_pieces(_alibi_slopes())
    qext = np.zeros((NSA_KV_HEADS, LANES, NSA_GROUP * tq), np.float32)
    for g in range(NSA_KV_HEADS):
        for r in range(NSA_GROUP):
            cols = slice(r * tq, (r + 1) * tq)
            for j in range(SLOPE_PIECES):
                p = pieces[j][g * NSA_GROUP + r]
                qext[g, NSA_EXT_SEL + j, cols] = p
                qext[g, NSA_EXT_SEL + SLOPE_PIECES + j, cols] = p
                qext[g, NSA_EXT_WIN + j, cols] = p
                qext[g, NSA_EXT_CMP + j, cols] = CMP_STRIDE * p
    return jnp.asarray(ksel, BF16), jnp.asarray(kwin, BF16), jnp.asarray(kcmp, BF16), jnp.asarray(qext, BF16)


def _nsa_attention(main, vs_t3, vw_t, k_cmp, v_cmp_t, gates_t):
    t = main.shape[0]
    tq = NSA_TQ
    tc = k_cmp.shape[0]
    n_sel = t // SEL_BLOCK
    nc = t // NSA_TK
    ks_col = Q_DIM // LANES
    kw_col = ks_col + NSA_KV_DIM // LANES
    n_ch = tc // NSA_CMP_CHUNK
    ov_t = jnp.asarray(_nsa_overlap_t(n_sel, tc).reshape(n_sel, n_ch, NSA_CMP_CHUNK).transpose(1, 0, 2), BF16)
    v_cmp_t = v_cmp_t.reshape(NSA_KV_HEADS, HEAD_DIM, n_ch, NSA_CMP_CHUNK).transpose(2, 0, 1, 3)
    vov = jnp.concatenate([v_cmp_t, jnp.broadcast_to(ov_t[:, None], (n_ch, NSA_KV_HEADS, n_sel, NSA_CMP_CHUNK))],
                          axis=2)
    slopes = jnp.asarray(_alibi_slopes())
    ksel_ext, kwin_ext, kcmp_ext, qext = _nsa_extensions(tq)

    def kw_spec(back):
        return pl.BlockSpec((tq, LANES), lambda g, i: (jnp.maximum(i - back, 0), kw_col + g // 2))

    def vw_spec(back):
        return pl.BlockSpec((HEAD_DIM, tq), lambda g, i: (g, jnp.maximum(i - back, 0)))

    backs = list(range(NSA_WIN_TILES - 1, -1, -1))
    return pl.pallas_call(
        _nsa_kernel,
        grid=(NSA_KV_HEADS, t // tq),
        in_specs=[pl.BlockSpec(memory_space=pltpu.SMEM),
                  pl.BlockSpec((tq, NSA_GROUP * HEAD_DIM), lambda g, i: (i, g)),
                  pl.BlockSpec((t, LANES), lambda g, i: (0, ks_col + g // 2))]
                 + [kw_spec(b) for b in backs]
                 + [pl.BlockSpec((nc, HEAD_DIM, NSA_TK), lambda g, i: (0, g, 0))]
                 + [vw_spec(b) for b in backs]
                 + [pl.BlockSpec((tc, LANES), lambda g, i: (0, g // 2)),
                    pl.BlockSpec((n_ch, 1, HEAD_DIM + n_sel, NSA_CMP_CHUNK), lambda g, i: (0, g, 0, 0)),
                    pl.BlockSpec((1, 16, tq), lambda g, i: (g, 0, i)),
                    pl.BlockSpec(ksel_ext.shape, lambda g, i: (0, 0, 0)),
                    pl.BlockSpec(kwin_ext.shape, lambda g, i: (0, 0)),
                    pl.BlockSpec(kcmp_ext.shape, lambda g, i: (0, 0)),
                    pl.BlockSpec((1,) + qext.shape[1:], lambda g, i: (g, 0, 0))],
        out_specs=pl.BlockSpec((tq, NSA_GROUP * HEAD_DIM), lambda g, i: (i, g)),
        out_shape=jax.ShapeDtypeStruct((t, Q_DIM), BF16),
        scratch_shapes=[pltpu.VMEM((n_sel, tq), F32), pltpu.VMEM((NSA_GROUP, 1, tq), F32),
                        pltpu.VMEM((NSA_GROUP, 1, tq), F32), pltpu.VMEM((HEAD_DIM, NSA_GROUP * tq), F32),
                        pltpu.VMEM((2 * LANES, NSA_GROUP * tq), BF16),
                        pltpu.VMEM((2, NSA_TK, NSA_GROUP * tq), F32),
                        pltpu.VMEM((tc, NSA_GROUP * tq), F32), pltpu.VMEM((NSA_GROUP, 1, tq), F32),
                        pltpu.VMEM((NSA_GROUP, 1, tq), F32), pltpu.VMEM((HEAD_DIM + n_sel, NSA_GROUP * tq), F32),
                        pltpu.SMEM((nc,), F32), pltpu.SMEM((nc,), jnp.int32)],
        compiler_params=_cparams(("parallel", "arbitrary")),
        name="nsa_attention",
    )(slopes, main, main, *([main] * NSA_WIN_TILES), vs_t3, *([vw_t] * NSA_WIN_TILES), k_cmp, vov, gates_t,
      ksel_ext, kwin_ext, kcmp_ext, qext)


def _nsa_heads(x, norm_g, w_in, cmp_pos, cmp_w1, cmp_w2):
    t = x.shape[0]
    kv = NSA_KV_DIM
    col = lambda i: w_in[:, Q_DIM + i * kv:Q_DIM + (i + 1) * kv]
    w_main = jnp.concatenate([w_in[:, :Q_DIM], col(2), col(4), col(3), col(5)], axis=1).astype(BF16)
    w_cmp = jnp.concatenate([col(0), col(1)], axis=1).astype(BF16)
    n_gate = 3 * N_HEADS
    w_gate = jnp.concatenate([w_in[:, Q_DIM + 6 * kv:], jnp.zeros((D_MODEL, LANES - n_gate), w_in.dtype)],
                             axis=1).astype(BF16)
    main, kcvc, gate_logit = _norm_proj(x, norm_g, [w_main, w_cmp, w_gate], [BF16, F32, F32])

    tc = t // CMP_STRIDE
    a = kcvc.reshape(tc, CMP_STRIDE, 2, NSA_KV_HEADS, HEAD_DIM).transpose(2, 3, 0, 1, 4)
    a = a.reshape(2, NSA_KV_HEADS, tc, CMP_STRIDE * HEAD_DIM)
    zeros = jnp.zeros_like(cmp_w2)
    w2 = jnp.stack([jnp.concatenate([cmp_w2, zeros], axis=2), jnp.concatenate([zeros, cmp_w2], axis=2)], axis=1)
    cmp = _nsa_compress(a, cmp_pos.reshape(2, 1, CMP_BLOCK * HEAD_DIM), cmp_w1.astype(BF16), w2.astype(BF16))
    k_cmp = cmp[0].astype(BF16)
    v_cmp_t = cmp[1].T.astype(BF16)

    off_vs = Q_DIM + 2 * kv
    vs_t3 = main[:, off_vs:off_vs + kv].reshape(t // NSA_TK, NSA_TK, kv).transpose(0, 2, 1)
    vw_t = main[:, off_vs + kv:].T
    gates_t = gate_logit[:, :n_gate].reshape(t, NSA_KV_HEADS, 3 * NSA_GROUP).transpose(1, 2, 0)
    gates_t = jnp.pad(gates_t, ((0, 0), (0, 16 - 3 * NSA_GROUP), (0, 0)))
    return _nsa_attention(main, vs_t3, vw_t, k_cmp, v_cmp_t, gates_t)


def kernel(x, attn_norm, mlp_norm, final_norm, nsa_w_in, nsa_cmp_pos, nsa_cmp_w1, nsa_cmp_w2, nsa_w_out,
           swa_w_in, swa_sinks, swa_w_out, fox_w_in, fox_f_bias, fox_w_out,
           mlp_w_up, mlp_conv_w, mlp_conv_b, mlp_w_down):
    assert x.shape[0] == 1, "the trunk is written for batch 1"
    h = x[0]
    for i in range(DEPTH):
        kind, j = i % N_MIXERS, i // N_MIXERS
        if kind == 0:
            o = _nsa_heads(h, attn_norm[i], nsa_w_in[j], nsa_cmp_pos[j], nsa_cmp_w1[j], nsa_cmp_w2[j])
            w_out = nsa_w_out[j]
        elif kind == 1:
            o = _swa_heads(h, attn_norm[i], swa_w_in[j], swa_sinks[j])
            w_out = swa_w_out[j]
        else:
            o = _fox_heads(h, attn_norm[i], fox_w_in[j], fox_f_bias[j])
            w_out = fox_w_out[j]
        h = _mixer_out_mlp(h, o, w_out.astype(BF16), mlp_norm[i], mlp_w_up[i].astype(BF16), mlp_conv_w[i],
                           mlp_conv_b[i].reshape(1, -1), mlp_w_down[i].astype(BF16),
                           final_g=final_norm if i == DEPTH - 1 else None)
    return h[None]
```

```python
import functools

import numpy as np
import jax
import jax.numpy as jnp
from jax import lax
from jax.experimental import pallas as pl
from jax.experimental.pallas import tpu as pltpu

F32 = jnp.float32
BF16 = jnp.bfloat16

D_MODEL = 1024
DEPTH = 4
N_MIXERS = 3
HEAD_DIM = 64
N_HEADS = 16
Q_DIM = N_HEADS * HEAD_DIM
ALIBI_MAX = 8.0
NORM_EPS = 1e-6
FORCE_SCORE = 1e4
MASKED = -2e30
M_INIT = -1e30
TAKEN = -(2.0 ** 127)

NSA_KV_HEADS = 4
NSA_GROUP = 4
NSA_KV_DIM = NSA_KV_HEADS * HEAD_DIM
CMP_BLOCK = 32
CMP_STRIDE = 16
CMP_HIDDEN = 256
SEL_BLOCK = 64
N_SELECT = 16
NSA_WINDOW = 512
NSA_TQ = 256
NSA_TK = 512

SWA_KV_HEADS = 2
SWA_GROUP = 8
SWA_WINDOW = 128
SWA_TQ = 128

FOX_TQ = 1024
FOX_T = 512

D_FF = 2816
LANES = 128
VMEM_LIMIT = 56 * 1024 * 1024


def _cparams(semantics, vmem=VMEM_LIMIT):
    return pltpu.CompilerParams(dimension_semantics=semantics, vmem_limit_bytes=vmem)


def _alibi_slopes():
    return np.asarray(2.0 ** (-ALIBI_MAX * np.arange(1, N_HEADS + 1) / N_HEADS), dtype=np.float32)


def _rms(x, g):
    ms = jnp.mean(x * x, axis=-1, keepdims=True)
    return x * lax.rsqrt(ms + NORM_EPS) * g


def _norm_proj_kernel(chunks, x_ref, g_ref, *refs):
    n_out = len(chunks)
    h = _rms(x_ref[...], g_ref[...]).astype(BF16)
    for w_ref, o_ref, chunk in zip(refs[:n_out], refs[n_out:], chunks):
        res = jnp.dot(h, w_ref[...], preferred_element_type=F32)
        if chunk is None:
            o_ref[...] = res.astype(o_ref.dtype)
        else:
            for c in range(res.shape[0] // chunk):
                o_ref[c] = res[c * chunk:(c + 1) * chunk].T.astype(o_ref.dtype)


def _pick_tile(n, candidates):
    for c in candidates:
        if n % c == 0:
            return c
    raise ValueError(f"no tile for {n}")


def _norm_proj(x, g, weights, out_dtypes, chunks=None):
    t, d = x.shape
    tm = _pick_tile(t, (1024, 512, 256, 128))
    chunks = tuple(chunks) if chunks else (None,) * len(weights)
    out_specs, out_shape = [], []
    for w, dt, c in zip(weights, out_dtypes, chunks):
        n = w.shape[1]
        if c is None:
            out_specs.append(pl.BlockSpec((tm, n), lambda i: (i, 0)))
            out_shape.append(jax.ShapeDtypeStruct((t, n), dt))
        else:
            out_specs.append(pl.BlockSpec((tm // c, n, c), lambda i: (i, 0, 0)))
            out_shape.append(jax.ShapeDtypeStruct((t // c, n, c), dt))
    return pl.pallas_call(
        functools.partial(_norm_proj_kernel, chunks),
        grid=(t // tm,),
        in_specs=[pl.BlockSpec((tm, d), lambda i: (i, 0)), pl.BlockSpec((1, d), lambda i: (0, 0))]
                 + [pl.BlockSpec(w.shape, lambda i: (0, 0)) for w in weights],
        out_specs=out_specs,
        out_shape=out_shape,
        compiler_params=_cparams(("parallel",)),
        name="norm_proj",
    )(x, g.reshape(1, d), *weights)


MLP_HALO = 16


def _mixer_out_mlp_kernel(final_norm, x_ref, xh_ref, o_ref, oh_ref, wo_ref, g_ref, fg_ref, wa_ref, wg_ref,
                          cwa_ref, cwg_ref, cba_ref, cbg_ref, wd_ref, y_ref, x1_ref, h_ref, acc_ref, ua_ref, ug_ref):
    i, j = pl.program_id(0), pl.program_id(1)
    tm = x_ref.shape[0]

    @pl.when(j == 0)
    def _():
        wo = wo_ref[...]
        x1_ref[0:MLP_HALO, :] = xh_ref[...] + jnp.dot(oh_ref[...], wo, preferred_element_type=F32)
        x1_ref[MLP_HALO:, :] = x_ref[...] + jnp.dot(o_ref[...], wo, preferred_element_type=F32)
        halo = _rms(x1_ref[0:MLP_HALO, :], g_ref[...])
        h_ref[0:MLP_HALO, :] = jnp.where(i > 0, halo, 0.0).astype(BF16)
        h_ref[MLP_HALO:, :] = _rms(x1_ref[MLP_HALO:, :], g_ref[...]).astype(BF16)
        acc_ref[...] = jnp.zeros_like(acc_ref)

    h = h_ref[...]

    def conv(w_ref, cw_ref, cb_ref, u_ref):
        u_ref[...] = jnp.dot(h, w_ref[...], preferred_element_type=F32)
        cw = cw_ref[...]
        return (cw[0:1] * u_ref[MLP_HALO - 2:MLP_HALO - 2 + tm, :] + cw[1:2] * u_ref[MLP_HALO - 1:MLP_HALO - 1 + tm, :]
                + cw[2:3] * u_ref[MLP_HALO:MLP_HALO + tm, :] + cb_ref[...])

    a = conv(wa_ref, cwa_ref, cba_ref, ua_ref)
    gt = conv(wg_ref, cwg_ref, cbg_ref, ug_ref)
    act = (jax.nn.silu(gt) * a).astype(BF16)
    acc_ref[...] += jnp.dot(act, wd_ref[...], preferred_element_type=F32)

    @pl.when(j == pl.num_programs(1) - 1)
    def _():
        y = x1_ref[MLP_HALO:, :] + acc_ref[...]
        y_ref[...] = _rms(y, fg_ref[...]) if final_norm else y


def _mixer_out_mlp(x, o, w_out, g, w_up, conv_w, conv_b, w_down, final_g=None):
    t, d = x.shape
    f = w_down.shape[0]
    q = o.shape[1]
    tm = _pick_tile(t, (512, 256, 128))
    tf = _pick_tile(f, (1408, 256, 128))
    nf = f // tf
    hb = tm // MLP_HALO
    halo_map = lambda i, j: (jnp.maximum(i * hb - 1, 0), 0)
    fg = g if final_g is None else final_g
    return pl.pallas_call(
        functools.partial(_mixer_out_mlp_kernel, final_g is not None),
        grid=(t // tm, nf),
        in_specs=[pl.BlockSpec((tm, d), lambda i, j: (i, 0)),
                  pl.BlockSpec((MLP_HALO, d), halo_map),
                  pl.BlockSpec((tm, q), lambda i, j: (i, 0)),
                  pl.BlockSpec((MLP_HALO, q), halo_map),
                  pl.BlockSpec((q, d), lambda i, j: (0, 0)),
                  pl.BlockSpec((1, d), lambda i, j: (0, 0)),
                  pl.BlockSpec((1, d), lambda i, j: (0, 0)),
                  pl.BlockSpec((d, tf), lambda i, j: (0, j)),
                  pl.BlockSpec((d, tf), lambda i, j: (0, nf + j)),
                  pl.BlockSpec((3, tf), lambda i, j: (0, j)),
                  pl.BlockSpec((3, tf), lambda i, j: (0, nf + j)),
                  pl.BlockSpec((1, tf), lambda i, j: (0, j)),
                  pl.BlockSpec((1, tf), lambda i, j: (0, nf + j)),
                  pl.BlockSpec((tf, d), lambda i, j: (j, 0))],
        out_specs=pl.BlockSpec((tm, d), lambda i, j: (i, 0)),
        out_shape=jax.ShapeDtypeStruct((t, d), F32),
        scratch_shapes=[pltpu.VMEM((tm + MLP_HALO, d), F32), pltpu.VMEM((tm + MLP_HALO, d), BF16),
                        pltpu.VMEM((tm, d), F32),
                        pltpu.VMEM((tm + MLP_HALO, tf), F32), pltpu.VMEM((tm + MLP_HALO, tf), F32)],
        compiler_params=_cparams(("parallel", "arbitrary")),
        name="mixer_out_mlp",
    )(x, x, o, o, w_out, g.reshape(1, d), fg.reshape(1, d), w_up, w_up, conv_w, conv_w, conv_b, conv_b, w_down)


def _swap_halves(q_pair):
    return pltpu.roll(q_pair.astype(F32), HEAD_DIM, 1).astype(BF16)


def _lane_half(shape):
    return lax.broadcasted_iota(jnp.int32, shape, 1) // HEAD_DIM


def _rows_to_heads(o_t, n_heads, tq):
    stacked = jnp.concatenate([o_t[:, r * tq:(r + 1) * tq] for r in range(n_heads)], axis=0)
    return stacked.T


def _swa_kernel(q_ref, kp_ref, kc_ref, vp_ref, vc_ref, bias_ref, sink_ref, o_ref):
    tq = q_ref.shape[0]
    rows = SWA_GROUP * tq
    k_ext = jnp.concatenate([kp_ref[...], kc_ref[...]], axis=0)
    v_ext = jnp.concatenate([vp_ref[0], vc_ref[0]], axis=1)
    half = _lane_half((tq, LANES))
    outs = []
    for g in range(SWA_KV_HEADS):
        parts = []
        for r in range(SWA_GROUP):
            hd = g * SWA_GROUP + r
            q_pair = q_ref[:, (hd // 2) * LANES:(hd // 2 + 1) * LANES]
            src = q_pair if hd % 2 == g else _swap_halves(q_pair)
            parts.append(jnp.where(half == g, src, jnp.zeros_like(src)))
        qp = jnp.concatenate(parts, axis=0)
        s = lax.dot_general(k_ext, qp, (((1,), (1,)), ((), ())), preferred_element_type=F32)
        s = s * (HEAD_DIM ** -0.5) + bias_ref[0, g]
        sink = sink_ref[g]
        m = jnp.maximum(jnp.max(s, axis=0, keepdims=True), sink)
        e = jnp.exp(s - m)
        l = jnp.sum(e, axis=0, keepdims=True) + jnp.exp(sink - m)
        o_t = jnp.dot(v_ext[g * HEAD_DIM:(g + 1) * HEAD_DIM], e.astype(BF16), preferred_element_type=F32)
        outs.append(_rows_to_heads(o_t / l, SWA_GROUP, tq))
    o_ref[...] = jnp.concatenate(outs, axis=1).astype(o_ref.dtype)


def _swa_bias():
    tq = SWA_TQ
    slopes = _alibi_slopes().reshape(SWA_KV_HEADS, SWA_GROUP)
    x = np.arange(2 * tq)[:, None] - tq
    q = np.arange(tq)[None, :]
    dist = q - x
    valid = (dist >= 0) & (dist < SWA_WINDOW)
    out = np.empty((2, SWA_KV_HEADS, 2 * tq, SWA_GROUP * tq), np.float32)
    for var in range(2):
        v = valid & ((x >= 0) | (var == 1))
        for g in range(SWA_KV_HEADS):
            for r in range(SWA_GROUP):
                out[var, g, :, r * tq:(r + 1) * tq] = np.where(v, -slopes[g, r] * dist, MASKED)
    return out


def _swa_attention(qkv, v_t, sinks):
    t = qkv.shape[0]
    tq = SWA_TQ
    kcol = Q_DIM // LANES
    bias = jnp.asarray(_swa_bias())
    sink_rows = jnp.repeat(sinks.astype(F32).reshape(SWA_KV_HEADS, 1, SWA_GROUP), tq, axis=2)
    prev = lambda i: jnp.maximum(i - 1, 0)
    return pl.pallas_call(
        _swa_kernel,
        grid=(t // tq,),
        in_specs=[pl.BlockSpec((tq, Q_DIM), lambda i: (i, 0)),
                  pl.BlockSpec((tq, LANES), lambda i: (prev(i), kcol)),
                  pl.BlockSpec((tq, LANES), lambda i: (i, kcol)),
                  pl.BlockSpec((1, LANES, tq), lambda i: (prev(i), 0, 0)),
                  pl.BlockSpec((1, LANES, tq), lambda i: (i, 0, 0)),
                  pl.BlockSpec((1,) + bias.shape[1:], lambda i: (jnp.minimum(i, 1), 0, 0, 0)),
                  pl.BlockSpec(sink_rows.shape, lambda i: (0, 0, 0))],
        out_specs=pl.BlockSpec((tq, Q_DIM), lambda i: (i, 0)),
        out_shape=jax.ShapeDtypeStruct((t, Q_DIM), BF16),
        compiler_params=_cparams(("parallel",)),
        name="swa_attention",
    )(qkv, qkv, qkv, v_t, v_t, bias, sink_rows)


def _swa_heads(x, norm_g, w_in, sinks):
    w = w_in.astype(BF16)
    n_qk = Q_DIM + SWA_KV_HEADS * HEAD_DIM
    qk, v_t = _norm_proj(x, norm_g, [w[:, :n_qk], w[:, n_qk:]], [BF16, BF16], chunks=[None, SWA_TQ])
    return _swa_attention(qk, v_t, sinks)


FOX_PARTS = 3


def _split_bf16(v):
    parts, rest = [], v
    for _ in range(FOX_PARTS):
        p = rest.astype(BF16)
        parts.append(p)
        rest = rest - p.astype(F32)
    return parts


def _fox_prep_kernel(f_ref, b_ref, aug_ref, cref_ref, carry_ref):
    tm = f_ref.shape[0]

    @pl.when(pl.program_id(0) == 0)
    def _():
        carry_ref[...] = jnp.zeros_like(carry_ref)

    logf = jax.nn.log_sigmoid(f_ref[...] + b_ref[...])
    row = lax.broadcasted_iota(jnp.int32, (tm, tm), 0)
    col = lax.broadcasted_iota(jnp.int32, (tm, tm), 1)
    tri = jnp.where(row >= col, 1.0, 0.0).astype(BF16)
    local = sum(jnp.dot(tri, p, preferred_element_type=F32) for p in _split_bf16(logf))
    cum = local + carry_ref[...]
    carry_ref[...] = cum[tm - 1:tm, :]
    first = cum[0:1, :]
    cref_ref[0] = jnp.broadcast_to(first, cref_ref.shape[1:])
    hi, mid, lo = _split_bf16(first - cum)
    lane = lax.broadcasted_iota(jnp.int32, (tm, LANES), 1)
    zero = jnp.zeros_like(hi)
    aug_ref[...] = jnp.where(lane < N_HEADS, hi, jnp.where(lane < 2 * N_HEADS, mid,
                                                            jnp.where(lane < 3 * N_HEADS, lo, zero)))


def _fox_prep(f_logit3, f_bias3):
    t = f_logit3.shape[0]
    nt = t // FOX_T
    return pl.pallas_call(
        _fox_prep_kernel,
        grid=(nt,),
        in_specs=[pl.BlockSpec((FOX_T, LANES), lambda i: (i, 0)), pl.BlockSpec((1, LANES), lambda i: (0, 0))],
        out_specs=[pl.BlockSpec((FOX_T, LANES), lambda i: (i, 0)), pl.BlockSpec((1, 8, LANES), lambda i: (i, 0, 0))],
        out_shape=[jax.ShapeDtypeStruct((t, LANES), BF16), jax.ShapeDtypeStruct((nt, 8, LANES), F32)],
        scratch_shapes=[pltpu.VMEM((1, LANES), F32)],
        compiler_params=_cparams(("arbitrary",)),
        name="fox_prep",
    )(f_logit3, f_bias3)


def _fox_kernel(cref_ref, q_ref, k_ref, aug_ref, vt_ref, o_ref, m_ref, l_ref, acc_ref, qt_ref, s_ref):
    pair, qi = pl.program_id(0), pl.program_id(1)
    tq = q_ref.shape[0]
    tk = FOX_T
    lane = lax.broadcasted_iota(jnp.int32, (tq, LANES), 1)
    q2 = q_ref[...] * jnp.asarray(HEAD_DIM ** -0.5, BF16)
    for hh in range(2):
        head = 2 * pair + hh
        pick = (lane % N_HEADS == head) & (lane < FOX_PARTS * N_HEADS)
        q_ext = jnp.concatenate([jnp.where(lane // HEAD_DIM == hh, q2, jnp.zeros_like(q2)).astype(F32),
                                 jnp.where(pick, 1.0, 0.0)], axis=1)
        qt_ref[hh] = q_ext.T.astype(BF16)
    m_ref[...] = jnp.full_like(m_ref, M_INIT)
    l_ref[...] = jnp.zeros_like(l_ref)
    acc_ref[...] = jnp.zeros_like(acc_ref)
    ratio = tq // tk

    def scores(kj, hh):
        start = pl.multiple_of(kj * tk, tk)
        k_ext = jnp.concatenate([k_ref[pl.ds(start, tk), :], aug_ref[pl.ds(start, tk), :]], axis=1)
        s_ref[hh] = jnp.dot(k_ext, qt_ref[hh], preferred_element_type=F32)

    def consume(kj, hh, key_offset):
        head = 2 * pair + hh
        s = s_ref[hh]
        if key_offset is not None:
            key = lax.broadcasted_iota(jnp.int32, (tk, tq), 0) + key_offset
            qry = lax.broadcasted_iota(jnp.int32, (tk, tq), 1)
            s = jnp.where(key <= qry, s, MASKED)
        off = cref_ref[qi * ratio, head] - cref_ref[kj, head]
        m_old = m_ref[hh]
        m_new = jnp.maximum(m_old, jnp.max(s, axis=0, keepdims=True) + off)
        e = jnp.exp(s - (m_new - off))
        alpha = jnp.exp(m_old - m_new)
        m_ref[hh] = m_new
        l_ref[hh] = alpha * l_ref[hh] + jnp.sum(e, axis=0, keepdims=True)
        v_t = vt_ref[kj, hh * HEAD_DIM:(hh + 1) * HEAD_DIM, :]
        acc_ref[hh] = alpha * acc_ref[hh] + jnp.dot(v_t, e.astype(BF16), preferred_element_type=F32)

    n_full = qi * ratio
    scores(0, 0)

    def body(j, carry):
        for d in range(ratio):
            kj = j * ratio + d
            scores(kj, 1)
            consume(kj, 0, None)
            scores(kj + 1, 0)
            consume(kj, 1, None)
        return carry

    lax.fori_loop(0, qi, body, 0)
    for d in range(ratio):
        kj = n_full + d
        scores(kj, 1)
        consume(kj, 0, d * tk)
        if d + 1 < ratio:
            scores(kj + 1, 0)
        consume(kj, 1, d * tk)
    o_t = jnp.concatenate([acc_ref[hh] / l_ref[hh] for hh in range(2)], axis=0)
    o_ref[...] = o_t.T.astype(o_ref.dtype)


def _fox_attention(qkv, aug, v_t3, cref):
    t = qkv.shape[0]
    tq = FOX_TQ
    nk = t // FOX_T
    kcol = Q_DIM // LANES
    return pl.pallas_call(
        _fox_kernel,
        grid=(N_HEADS // 2, t // tq),
        in_specs=[pl.BlockSpec(memory_space=pltpu.SMEM),
                  pl.BlockSpec((tq, LANES), lambda p, i: (i, p)),
                  pl.BlockSpec((t, LANES), lambda p, i: (0, kcol + p)),
                  pl.BlockSpec((t, LANES), lambda p, i: (0, 0)),
                  pl.BlockSpec((nk, LANES, FOX_T), lambda p, i: (0, p, 0))],
        out_specs=pl.BlockSpec((tq, LANES), lambda p, i: (i, p)),
        out_shape=jax.ShapeDtypeStruct((t, Q_DIM), BF16),
        scratch_shapes=[pltpu.VMEM((2, 1, tq), F32), pltpu.VMEM((2, 1, tq), F32),
                        pltpu.VMEM((2, HEAD_DIM, tq), F32), pltpu.VMEM((2, 2 * LANES, tq), BF16), pltpu.VMEM((2, FOX_T, tq), F32)],
        compiler_params=_cparams(("parallel", "arbitrary")),
        name="fox_attention",
    )(cref, qkv, qkv, aug, v_t3)


def _fox_heads(x, norm_g, w_in, f_bias):
    t = x.shape[0]
    w_f = w_in[:, 3 * Q_DIM:]
    pad = jnp.zeros((D_MODEL, LANES - FOX_PARTS * N_HEADS), w_f.dtype)
    w_f3 = jnp.concatenate([w_f] * FOX_PARTS + [pad], axis=1).astype(BF16)
    b3 = jnp.concatenate([f_bias.astype(F32)] * FOX_PARTS + [jnp.zeros((LANES - FOX_PARTS * N_HEADS,), F32)])
    w = w_in.astype(BF16)
    qk, v_t3, f_logit3 = _norm_proj(x, norm_g, [w[:, :2 * Q_DIM], w[:, 2 * Q_DIM:3 * Q_DIM], w_f3], [BF16, BF16, F32],
                                    chunks=[None, FOX_T, None])
    aug, cref = _fox_prep(f_logit3, b3.reshape(1, LANES))
    return _fox_attention(qk, aug, v_t3, cref[:, 0, :N_HEADS])


def _nsa_compress_kernel(a_ref, pos_ref, w1_ref, w2_ref, o_ref):
    tc = a_ref.shape[2]
    half = CMP_STRIDE * HEAD_DIM
    pos = pos_ref[0]
    out = jnp.zeros(o_ref.shape[1:], F32)
    for gg in range(2):
        a = a_ref[0, gg]
        top = jnp.dot((a + pos[:, :half]).astype(BF16), w1_ref[0, :half], preferred_element_type=F32)
        bot = jnp.dot((a + pos[:, half:]).astype(BF16), w1_ref[0, half:], preferred_element_type=F32)
        hid = top + pltpu.roll(bot, tc - 1, 0)
        out = out + jnp.dot(jax.nn.gelu(hid).astype(BF16), w2_ref[0, gg], preferred_element_type=F32)
    o_ref[0] = out


def _nsa_compress(a, pos, w1, w2):
    _, g, tc, width = a.shape
    return pl.pallas_call(
        _nsa_compress_kernel,
        grid=(2, g // 2),
        in_specs=[pl.BlockSpec((1, 2, tc, width), lambda kv, gp: (kv, gp, 0, 0)),
                  pl.BlockSpec((1, 1, 2 * width), lambda kv, gp: (kv, 0, 0)),
                  pl.BlockSpec((1, 2 * width, CMP_HIDDEN), lambda kv, gp: (kv, 0, 0)),
                  pl.BlockSpec((1, 2, CMP_HIDDEN, LANES), lambda kv, gp: (kv, 0, 0, 0))],
        out_specs=pl.BlockSpec((1, tc, LANES), lambda kv, gp: (kv, 0, gp)),
        out_shape=jax.ShapeDtypeStruct((2, tc, g * HEAD_DIM), F32),
        compiler_params=_cparams(("parallel", "parallel")),
        name="nsa_compress",
    )(a, pos, w1, w2)


NSA_WIN_TILES = NSA_WINDOW // NSA_TQ + 1


def _nsa_kernel(slopes_ref, q_ref, ks_ref, *refs):
    kw_refs, refs = refs[:NSA_WIN_TILES], refs[NSA_WIN_TILES:]
    vst_ref, refs = refs[0], refs[1:]
    vw_refs, refs = refs[:NSA_WIN_TILES], refs[NSA_WIN_TILES:]
    (kcmp_ref, vov_ref, gates_ref, ksel_ext_ref, kwin_ext_ref, kcmp_ext_ref, qext_ref,
     o_ref, selt_ref, m_ref, l_ref, acc_ref, qt_ref, s_ref, scmp_ref, cm_ref, cl_ref, oc_ref,
     active_ref, list_ref) = refs
    g, qi = pl.program_id(0), pl.program_id(1)
    tq = q_ref.shape[0]
    tc = kcmp_ref.shape[0]
    n_sel = selt_ref.shape[0]
    rr = NSA_GROUP
    t0 = qi * tq
    hg = g % 2
    slope = [slopes_ref[g * rr + r] for r in range(rr)]

    half = _lane_half((tq, LANES))
    parts = []
    for r in range(rr):
        q_pair = q_ref[:, (r // 2) * LANES:(r // 2 + 1) * LANES]
        src = jnp.where(hg == r % 2, q_pair, _swap_halves(q_pair))
        parts.append(jnp.where(half == hg, src, jnp.zeros_like(src)) * jnp.asarray(HEAD_DIM ** -0.5, BF16))
    qp = jnp.concatenate(parts, axis=0)
    qt_ref[0:LANES, :] = qp.astype(F32).T.astype(BF16)

    @pl.when(qi == 0)
    def _():
        qt_ref[LANES:, :] = qext_ref[0]

    def lanes_of(s, r):
        return s[:, r * tq:(r + 1) * tq]

    chunk_span = NSA_CMP_CHUNK * CMP_STRIDE
    last_chunk = ((t0 + tq - CMP_BLOCK) // CMP_STRIDE) // NSA_CMP_CHUNK
    nl16_q = (CMP_STRIDE * lax.broadcasted_iota(jnp.int32, (NSA_CMP_CHUNK, tq), 0)
              - lax.broadcasted_iota(jnp.int32, (NSA_CMP_CHUNK, tq), 1)).astype(F32)
    cm_ref[...] = jnp.full_like(cm_ref, M_INIT)

    def cmp_rows(ch):
        return pl.ds(pl.multiple_of(ch * NSA_CMP_CHUNK, NSA_CMP_CHUNK), NSA_CMP_CHUNK)

    def cmp_offset(ch, r):
        return slope[r] * (ch * chunk_span).astype(F32)

    def cmp_scores(ch, masked):
        k_full = jnp.concatenate([kcmp_ref[cmp_rows(ch), :], kcmp_ext_ref[...]], axis=1)
        s = jnp.dot(k_full, qt_ref[...], preferred_element_type=F32)
        if masked:
            visible = nl16_q <= (t0 - (CMP_BLOCK - 1) - ch * chunk_span).astype(F32)
            s = jnp.concatenate([jnp.where(visible, lanes_of(s, r), MASKED) for r in range(rr)], axis=1)
        scmp_ref[cmp_rows(ch), :] = s
        for r in range(rr):
            cm_ref[r] = jnp.maximum(cm_ref[r], jnp.max(lanes_of(s, r), axis=0, keepdims=True) + cmp_offset(ch, r))

    def cmp_plain(ch, carry):
        cmp_scores(ch, False)
        return carry

    lax.fori_loop(0, jnp.maximum(last_chunk - 1, 0), cmp_plain, 0)

    @pl.when(last_chunk >= 1)
    def _():
        cmp_scores(last_chunk - 1, True)

    cmp_scores(last_chunk, True)

    cl_ref[...] = jnp.zeros_like(cl_ref)
    oc_ref[...] = jnp.zeros_like(oc_ref)

    def cmp_exp(ch, carry):
        s = scmp_ref[cmp_rows(ch), :]
        e_all = []
        for r in range(rr):
            e = jnp.exp(lanes_of(s, r) - (cm_ref[r] - cmp_offset(ch, r)))
            cl_ref[r] = cl_ref[r] + jnp.sum(e, axis=0, keepdims=True)
            e_all.append(e.astype(BF16))
        oc_ref[...] += jnp.dot(vov_ref[ch, 0], jnp.concatenate(e_all, axis=1), preferred_element_type=F32)
        return carry

    lax.fori_loop(0, last_chunk + 1, cmp_exp, 0)
    inv_l = [jnp.where(cl_ref[r] > 0.0, 1.0 / cl_ref[r], 0.0) for r in range(rr)]
    o_cmp = [oc_ref[0:HEAD_DIM, r * tq:(r + 1) * tq] * inv_l[r] for r in range(rr)]

    imp = functools.reduce(jnp.add, [oc_ref[HEAD_DIM:, r * tq:(r + 1) * tq] * inv_l[r] for r in range(rr)])
    blk = lax.broadcasted_iota(jnp.int32, (n_sel, tq), 0)
    cur = (t0 + lax.broadcasted_iota(jnp.int32, (n_sel, tq), 1)) // SEL_BLOCK
    blk_f = blk.astype(F32)
    forced = jnp.where(blk == 0, 1.0, jnp.where(blk == cur, 1.0, jnp.where(blk == cur - 1, 1.0, 0.0)))
    score = jnp.where(forced > 0.0, TAKEN, jnp.where(blk <= cur, imp, -1.0))

    selt_ref[...] = score
    row_step = min(n_sel, NSA_TOPK_ROW_STEP)
    variant = ((t0 + tq - 1) // SEL_BLOCK) // row_step

    for v in range(n_sel // row_step):
        rows = row_step * (v + 1)

        @pl.when(variant == v)
        def _(rows=rows):
            ids = lax.broadcasted_iota(jnp.int32, (rows, tq), 0).astype(F32)

            def pick(_, sc):
                best = jnp.max(sc, axis=0, keepdims=True)
                first = jnp.min(jnp.where(sc == best, ids, float(n_sel)), axis=0, keepdims=True)
                return jnp.where(ids == first, TAKEN, sc)

            selt_ref[0:rows, :] = lax.fori_loop(0, N_SELECT - 3, pick, selt_ref[0:rows, :])

    selt_ref[...] = jnp.where(blk <= cur, jnp.where(selt_ref[...] == TAKEN, 1.0, 0.0), 0.0)

    m_ref[...] = jnp.full_like(m_ref, M_INIT)
    l_ref[...] = jnp.zeros_like(l_ref)
    acc_ref[...] = jnp.zeros_like(acc_ref)
    blocks_per_chunk = NSA_TK // SEL_BLOCK

    for c in range(n_sel // blocks_per_chunk):
        active_ref[c] = jnp.max(selt_ref[c * blocks_per_chunk:(c + 1) * blocks_per_chunk, :])


    def compact(c, n):
        list_ref[n] = c
        return n + (active_ref[c] > 0.0).astype(jnp.int32)

    n_active = lax.fori_loop(0, t0 // NSA_TK + 1, compact, 0)

    def sel_scores(c, buf):
        start = pl.multiple_of(c * NSA_TK, NSA_TK)
        rows = selt_ref[pl.ds(pl.multiple_of(c * blocks_per_chunk, blocks_per_chunk), blocks_per_chunk), :]
        mask_rows = jnp.concatenate([jnp.where(rows > 0.5, 0.0, MASKED), jnp.zeros_like(rows)], axis=0)
        first = LANES + buf * NSA_EXT_MASK_ROWS
        qt_ref[first:first + NSA_EXT_MASK_ROWS, :] = jnp.concatenate([mask_rows] * rr, axis=1).astype(BF16)
        k_full = jnp.concatenate([ks_ref[pl.ds(start, NSA_TK), :], ksel_ext_ref[buf]], axis=1)
        s_ref[buf] = jnp.dot(k_full, qt_ref[...], preferred_element_type=F32)

    def sel_consume(c, buf, diagonal):
        shift = (c * NSA_TK - t0).astype(F32)
        if diagonal:
            key = lax.broadcasted_iota(jnp.int32, (NSA_TK, tq), 0) + c * NSA_TK
            visible = key <= lax.broadcasted_iota(jnp.int32, (NSA_TK, tq), 1) + t0
        e_all, alpha_all = [], []
        for r in range(rr):
            off = slope[r] * shift
            s = s_ref[buf][:, r * tq:(r + 1) * tq]
            if diagonal:
                s = jnp.where(visible, s, MASKED)
            m_old = m_ref[r]
            m_new = jnp.maximum(m_old, jnp.max(s, axis=0, keepdims=True) + off)
            e = jnp.exp(s - (m_new - off))
            alpha = jnp.exp(m_old - m_new)
            m_ref[r] = m_new
            l_ref[r] = alpha * l_ref[r] + jnp.sum(e, axis=0, keepdims=True)
            e_all.append(e.astype(BF16))
            alpha_all.append(alpha)
        pv = jnp.dot(vst_ref[c], jnp.concatenate(e_all, axis=1), preferred_element_type=F32)
        acc_ref[...] = jnp.concatenate(alpha_all, axis=1) * acc_ref[...] + pv

    n_before = n_active - 1
    sel_scores(list_ref[0], 0)

    def sel_pair(j, carry):
        c0, c1, c2 = list_ref[2 * j], list_ref[2 * j + 1], list_ref[2 * j + 2]
        sel_scores(c1, 1)
        sel_consume(c0, 0, False)
        sel_scores(c2, 0)
        sel_consume(c1, 1, False)
        return carry

    lax.fori_loop(0, n_before // 2, sel_pair, 0)

    @pl.when(n_before % 2 == 1)
    def _():
        c0, c_last = list_ref[n_before - 1], list_ref[n_before]
        sel_scores(c_last, 1)
        sel_consume(c0, 0, False)
        sel_consume(c_last, 1, True)

    @pl.when(n_before % 2 == 0)
    def _():
        sel_consume(list_ref[n_before], 0, True)

    xw = lax.broadcasted_iota(jnp.int32, (tq, tq), 0)
    qw = lax.broadcasted_iota(jnp.int32, (tq, tq), 1)
    s_win, off_win = [], []
    for idx in range(NSA_WIN_TILES):
        back = NSA_WIN_TILES - 1 - idx
        k_full = jnp.concatenate([kw_refs[idx][...], kwin_ext_ref[...]], axis=1)
        s_t = jnp.dot(k_full, qt_ref[...], preferred_element_type=F32)
        per_head = [lanes_of(s_t, r) for r in range(rr)]
        if back == NSA_WIN_TILES - 1:
            per_head = [jnp.where(xw > qw, s, MASKED) for s in per_head]
        if back == 0:
            per_head = [jnp.where(xw <= qw, s, MASKED) for s in per_head]
        s_win.append(per_head)
        off_win.append([jnp.where(qi >= back, -slope[r] * float(back * tq), MASKED) for r in range(rr)])
    e_win, l_win = [], []
    for r in range(rr):
        m = functools.reduce(jnp.maximum, [jnp.max(s_win[idx][r], axis=0, keepdims=True) + off_win[idx][r]
                                           for idx in range(NSA_WIN_TILES)])
        e_tiles = [jnp.exp(s_win[idx][r] - (m - off_win[idx][r])) for idx in range(NSA_WIN_TILES)]
        l_win.append(functools.reduce(jnp.add, [jnp.sum(e, axis=0, keepdims=True) for e in e_tiles]))
        e_win.append(jnp.concatenate([e.astype(BF16) for e in e_tiles], axis=0))
    v_win = jnp.concatenate([vw_refs[idx][0] for idx in range(NSA_WIN_TILES)], axis=1)
    o_win_all = jnp.dot(v_win, jnp.concatenate(e_win, axis=1), preferred_element_type=F32)
    o_win = [lanes_of(o_win_all, r) / l_win[r] for r in range(rr)]

    gate = jax.nn.sigmoid(gates_ref[0])
    merged = []
    for r in range(rr):
        o_sel = lanes_of(acc_ref[...], r) / l_ref[r]
        merged.append(gate[3 * r:3 * r + 1] * o_cmp[r] + gate[3 * r + 1:3 * r + 2] * o_sel
                      + gate[3 * r + 2:3 * r + 3] * o_win[r])
    o_ref[...] = jnp.concatenate(merged, axis=0).T.astype(o_ref.dtype)


def _nsa_overlap_t(n_sel, tc):
    cmp_start = np.arange(tc - 1) * CMP_STRIDE
    sel_start = np.arange(n_sel) * SEL_BLOCK
    ov = np.clip(np.minimum(cmp_start[None, :] + CMP_BLOCK, sel_start[:, None] + SEL_BLOCK)
                 - np.maximum(cmp_start[None, :], sel_start[:, None]), 0, None) / CMP_BLOCK
    return np.concatenate([ov, np.zeros((n_sel, 1))], axis=1).astype(np.float32)


NSA_EXT_MASK_ROWS = 16
NSA_EXT_SEL = 32
NSA_EXT_WIN = 48
NSA_EXT_CMP = 64
NSA_CMP_CHUNK = 256
NSA_TOPK_ROW_STEP = 64
SLOPE_PIECES = 3


def _bf16_pieces(v):
    out, rest = [], np.asarray(v, np.float32)
    for _ in range(SLOPE_PIECES):
        p = rest.astype(BF16).astype(np.float32)
        out.append(p)
        rest = rest - p
    return out


def _nsa_extensions(tq):
    ksel = np.zeros((2, NSA_TK, LANES), np.float32)
    x = np.arange(NSA_TK)
    for buf in range(2):
        ksel[buf, x, buf * NSA_EXT_MASK_ROWS + x // SEL_BLOCK] = 1.0
    for j in range(SLOPE_PIECES):
        ksel[:, :, NSA_EXT_SEL + j] = 256 * (x // 256)
        ksel[:, :, NSA_EXT_SEL + SLOPE_PIECES + j] = x % 256
    kwin = np.zeros((tq, LANES), np.float32)
    for j in range(SLOPE_PIECES):
        kwin[:, NSA_EXT_WIN + j] = np.arange(tq)
    assert tq <= 256, "a window key offset must be exact in bf16"
    kcmp = np.zeros((NSA_CMP_CHUNK, LANES), np.float32)
    for j in range(SLOPE_PIECES):
        kcmp[:, NSA_EXT_CMP + j] = np.arange(NSA_CMP_CHUNK)
    pieces = _bf16_pieces(_alibi_slopes())
    qext = np.zeros((NSA_KV_HEADS, LANES, NSA_GROUP * tq), np.float32)
    for g in range(NSA_KV_HEADS):
        for r in range(NSA_GROUP):
            cols = slice(r * tq, (r + 1) * tq)
            for j in range(SLOPE_PIECES):
                p = pieces[j][g * NSA_GROUP + r]
                qext[g, NSA_EXT_SEL + j, cols] = p
                qext[g, NSA_EXT_SEL + SLOPE_PIECES + j, cols] = p
                qext[g, NSA_EXT_WIN + j, cols] = p
                qext[g, NSA_EXT_CMP + j, cols] = CMP_STRIDE * p
    return jnp.asarray(ksel, BF16), jnp.asarray(kwin, BF16), jnp.asarray(kcmp, BF16), jnp.asarray(qext, BF16)


def _nsa_attention(main, vs_t3, vw_t, k_cmp, v_cmp_t, gates_t):
    t = main.shape[0]
    tq = NSA_TQ
    tc = k_cmp.shape[0]
    n_sel = t // SEL_BLOCK
    nc = t // NSA_TK
    ks_col = Q_DIM // LANES
    kw_col = ks_col + NSA_KV_DIM // LANES
    n_ch = tc // NSA_CMP_CHUNK
    ov_t = jnp.asarray(_nsa_overlap_t(n_sel, tc).reshape(n_sel, n_ch, NSA_CMP_CHUNK).transpose(1, 0, 2), BF16)
    v_cmp_t = v_cmp_t.reshape(NSA_KV_HEADS, HEAD_DIM, n_ch, NSA_CMP_CHUNK).transpose(2, 0, 1, 3)
    vov = jnp.concatenate([v_cmp_t, jnp.broadcast_to(ov_t[:, None], (n_ch, NSA_KV_HEADS, n_sel, NSA_CMP_CHUNK))],
                          axis=2)
    slopes = jnp.asarray(_alibi_slopes())
    ksel_ext, kwin_ext, kcmp_ext, qext = _nsa_extensions(tq)

    def kw_spec(back):
        return pl.BlockSpec((tq, LANES), lambda g, i: (jnp.maximum(i - back, 0), kw_col + g // 2))

    def vw_spec(back):
        return pl.BlockSpec((1, HEAD_DIM, tq), lambda g, i: (jnp.maximum(i - back, 0), g, 0))

    backs = list(range(NSA_WIN_TILES - 1, -1, -1))
    return pl.pallas_call(
        _nsa_kernel,
        grid=(NSA_KV_HEADS, t // tq),
        in_specs=[pl.BlockSpec(memory_space=pltpu.SMEM),
                  pl.BlockSpec((tq, NSA_GROUP * HEAD_DIM), lambda g, i: (i, g)),
                  pl.BlockSpec((t, LANES), lambda g, i: (0, ks_col + g // 2))]
                 + [kw_spec(b) for b in backs]
                 + [pl.BlockSpec((nc, HEAD_DIM, NSA_TK), lambda g, i: (0, g, 0))]
                 + [vw_spec(b) for b in backs]
                 + [pl.BlockSpec((tc, LANES), lambda g, i: (0, g // 2)),
                    pl.BlockSpec((n_ch, 1, HEAD_DIM + n_sel, NSA_CMP_CHUNK), lambda g, i: (0, g, 0, 0)),
                    pl.BlockSpec((1, 16, tq), lambda g, i: (g, 0, i)),
                    pl.BlockSpec(ksel_ext.shape, lambda g, i: (0, 0, 0)),
                    pl.BlockSpec(kwin_ext.shape, lambda g, i: (0, 0)),
                    pl.BlockSpec(kcmp_ext.shape, lambda g, i: (0, 0)),
                    pl.BlockSpec((1,) + qext.shape[1:], lambda g, i: (g, 0, 0))],
        out_specs=pl.BlockSpec((tq, NSA_GROUP * HEAD_DIM), lambda g, i: (i, g)),
        out_shape=jax.ShapeDtypeStruct((t, Q_DIM), BF16),
        scratch_shapes=[pltpu.VMEM((n_sel, tq), F32), pltpu.VMEM((NSA_GROUP, 1, tq), F32),
                        pltpu.VMEM((NSA_GROUP, 1, tq), F32), pltpu.VMEM((HEAD_DIM, NSA_GROUP * tq), F32),
                        pltpu.VMEM((2 * LANES, NSA_GROUP * tq), BF16),
                        pltpu.VMEM((2, NSA_TK, NSA_GROUP * tq), F32),
                        pltpu.VMEM((tc, NSA_GROUP * tq), F32), pltpu.VMEM((NSA_GROUP, 1, tq), F32),
                        pltpu.VMEM((NSA_GROUP, 1, tq), F32), pltpu.VMEM((HEAD_DIM + n_sel, NSA_GROUP * tq), F32),
                        pltpu.SMEM((nc,), F32), pltpu.SMEM((nc,), jnp.int32)],
        compiler_params=_cparams(("parallel", "arbitrary")),
        name="nsa_attention",
    )(slopes, main, main, *([main] * NSA_WIN_TILES), vs_t3, *([vw_t] * NSA_WIN_TILES), k_cmp, vov, gates_t,
      ksel_ext, kwin_ext, kcmp_ext, qext)


def _nsa_heads(x, norm_g, w_in, cmp_pos, cmp_w1, cmp_w2):
    t = x.shape[0]
    kv = NSA_KV_DIM
    col = lambda i: w_in[:, Q_DIM + i * kv:Q_DIM + (i + 1) * kv]
    w_main = jnp.concatenate([w_in[:, :Q_DIM], col(2), col(4)], axis=1).astype(BF16)
    w_cmp = jnp.concatenate([col(0), col(1)], axis=1).astype(BF16)
    n_gate = 3 * N_HEADS
    w_gate = jnp.concatenate([w_in[:, Q_DIM + 6 * kv:], jnp.zeros((D_MODEL, LANES - n_gate), w_in.dtype)],
                             axis=1).astype(BF16)
    main, vs_t3, vw_t3, kcvc, gate_logit = _norm_proj(
        x, norm_g, [w_main, col(3).astype(BF16), col(5).astype(BF16), w_cmp, w_gate], [BF16, BF16, BF16, F32, F32],
        chunks=[None, NSA_TK, NSA_TQ, None, None])

    tc = t // CMP_STRIDE
    a = kcvc.reshape(tc, CMP_STRIDE, 2, NSA_KV_HEADS, HEAD_DIM).transpose(2, 3, 0, 1, 4)
    a = a.reshape(2, NSA_KV_HEADS, tc, CMP_STRIDE * HEAD_DIM)
    zeros = jnp.zeros_like(cmp_w2)
    w2 = jnp.stack([jnp.concatenate([cmp_w2, zeros], axis=2), jnp.concatenate([zeros, cmp_w2], axis=2)], axis=1)
    cmp = _nsa_compress(a, cmp_pos.reshape(2, 1, CMP_BLOCK * HEAD_DIM), cmp_w1.astype(BF16), w2.astype(BF16))
    k_cmp = cmp[0].astype(BF16)
    v_cmp_t = cmp[1].T.astype(BF16)

    gates_t =gate_logit[:, :n_gate].reshape(t, NSA_KV_HEADS, 3 * NSA_GROUP).transpose(1, 2, 0)
    gates_t = jnp.pad(gates_t, ((0, 0), (0, 16 - 3 * NSA_GROUP), (0, 0)))
    return _nsa_attention(main, vs_t3, vw_t3, k_cmp, v_cmp_t, gates_t)


def kernel(x, attn_norm, mlp_norm, final_norm, nsa_w_in, nsa_cmp_pos, nsa_cmp_w1, nsa_cmp_w2, nsa_w_out,
           swa_w_in, swa_sinks, swa_w_out, fox_w_in, fox_f_bias, fox_w_out,
           mlp_w_up, mlp_conv_w, mlp_conv_b, mlp_w_down):
    assert x.shape[0] == 1, "the trunk is written for batch 1"
    h = x[0]
    for i in range(DEPTH):
        kind, j = i % N_MIXERS, i // N_MIXERS
        if kind == 0:
            o = _nsa_heads(h, attn_norm[i], nsa_w_in[j], nsa_cmp_pos[j], nsa_cmp_w1[j], nsa_cmp_w2[j])
            w_out = nsa_w_out[j]
        elif kind == 1:
            o = _swa_heads(h, attn_norm[i], swa_w_in[j], swa_sinks[j])
            w_out = swa_w_out[j]
        else:
            o = _fox_heads(h, attn_norm[i], fox_w_in[j], fox_f_bias[j])
            w_out = fox_w_out[j]
        h = _mixer_out_mlp(h, o, w_out.astype(BF16), mlp_norm[i], mlp_w_up[i].astype(BF16), mlp_conv_w[i],
                           mlp_conv_b[i].reshape(1, -1), mlp_w_down[i].astype(BF16),
                           final_g=final_norm if i == DEPTH - 1 else None)
    return h[None]
```

```python
import functools

import numpy as np
import jax
import jax.numpy as jnp
from jax import lax
from jax.experimental import pallas as pl
from jax.experimental.pallas import tpu as pltpu

F32 = jnp.float32
BF16 = jnp.bfloat16

D_MODEL = 1024
DEPTH = 4
N_MIXERS = 3
HEAD_DIM = 64
N_HEADS = 16
Q_DIM = N_HEADS * HEAD_DIM
ALIBI_MAX = 8.0
NORM_EPS = 1e-6
MASKED = -2e30
M_INIT = -1e30
TAKEN = -(2.0 ** 127)

NSA_KV_HEADS = 4
NSA_GROUP = 4
NSA_KV_DIM = NSA_KV_HEADS * HEAD_DIM
CMP_BLOCK = 32
CMP_STRIDE = 16
CMP_HIDDEN = 256
SEL_BLOCK = 64
N_SELECT = 16
NSA_WINDOW = 512
NSA_TQ = 256
NSA_TK = 512

SWA_KV_HEADS = 2
SWA_GROUP = 8
SWA_WINDOW = 128
SWA_TQ = 128

FOX_TQ = 1024
FOX_T = 512

LANES = 128
VMEM_LIMIT = 56 * 1024 * 1024


def _cparams(semantics, vmem=VMEM_LIMIT):
    return pltpu.CompilerParams(dimension_semantics=semantics, vmem_limit_bytes=vmem)


def _alibi_slopes():
    return np.asarray(2.0 ** (-ALIBI_MAX * np.arange(1, N_HEADS + 1) / N_HEADS), dtype=np.float32)


def _rms(x, g):
    ms = jnp.mean(x * x, axis=-1, keepdims=True)
    return x * lax.rsqrt(ms + NORM_EPS) * g


def _norm_proj_kernel(chunks, x_ref, g_ref, *refs):
    n_out = len(chunks)
    h = _rms(x_ref[...], g_ref[...]).astype(BF16)
    for w_ref, o_ref, chunk in zip(refs[:n_out], refs[n_out:], chunks):
        res = jnp.dot(h, w_ref[...], preferred_element_type=F32)
        if chunk is None:
            o_ref[...] = res.astype(o_ref.dtype)
        else:
            for c in range(res.shape[0] // chunk):
                o_ref[c] = res[c * chunk:(c + 1) * chunk].T.astype(o_ref.dtype)


def _pick_tile(n, candidates):
    for c in candidates:
        if n % c == 0:
            return c
    raise ValueError(f"no tile for {n}")


def _norm_proj(x, g, weights, out_dtypes, chunks=None):
    t, d = x.shape
    tm = _pick_tile(t, (1024, 512, 256, 128))
    chunks = tuple(chunks) if chunks else (None,) * len(weights)
    out_specs, out_shape = [], []
    for w, dt, c in zip(weights, out_dtypes, chunks):
        n = w.shape[1]
        if c is None:
            out_specs.append(pl.BlockSpec((tm, n), lambda i: (i, 0)))
            out_shape.append(jax.ShapeDtypeStruct((t, n), dt))
        else:
            out_specs.append(pl.BlockSpec((tm // c, n, c), lambda i: (i, 0, 0)))
            out_shape.append(jax.ShapeDtypeStruct((t // c, n, c), dt))
    return pl.pallas_call(
        functools.partial(_norm_proj_kernel, chunks),
        grid=(t // tm,),
        in_specs=[pl.BlockSpec((tm, d), lambda i: (i, 0)), pl.BlockSpec((1, d), lambda i: (0, 0))]
                 + [pl.BlockSpec(w.shape, lambda i: (0, 0)) for w in weights],
        out_specs=out_specs,
        out_shape=out_shape,
        compiler_params=_cparams(("parallel",)),
        name="norm_proj",
    )(x, g.reshape(1, d), *weights)


MLP_HALO = 16


def _mixer_out_mlp_kernel(final_norm, x_ref, xh_ref, o_ref, oh_ref, wo_ref, g_ref, fg_ref, wa_ref, wg_ref,
                          cwa_ref, cwg_ref, cba_ref, cbg_ref, wd_ref, y_ref, x1_ref, h_ref, acc_ref, u_ref):
    i, j = pl.program_id(0), pl.program_id(1)
    tm = x_ref.shape[0]

    @pl.when(j == 0)
    def _():
        wo = wo_ref[...]
        x1_ref[0:MLP_HALO, :] = xh_ref[...] + jnp.dot(oh_ref[...], wo, preferred_element_type=F32)
        x1_ref[MLP_HALO:, :] = x_ref[...] + jnp.dot(o_ref[...], wo, preferred_element_type=F32)
        halo = _rms(x1_ref[0:MLP_HALO, :], g_ref[...])
        h_ref[0:MLP_HALO, :] = jnp.where(i > 0, halo, 0.0).astype(BF16)
        h_ref[MLP_HALO:, :] = _rms(x1_ref[MLP_HALO:, :], g_ref[...]).astype(BF16)
        acc_ref[...] = jnp.zeros_like(acc_ref)

    h = h_ref[...]

    tf = wa_ref.shape[1]
    u_ref[...] = jnp.dot(h, jnp.concatenate([wa_ref[...], wg_ref[...]], axis=1), preferred_element_type=F32)

    def conv(cols, cw_ref, cb_ref):
        cw = cw_ref[...]
        return (cw[0:1] * u_ref[MLP_HALO - 2:MLP_HALO - 2 + tm, cols] + cw[1:2] * u_ref[MLP_HALO - 1:MLP_HALO - 1 + tm, cols]
                + cw[2:3] * u_ref[MLP_HALO:MLP_HALO + tm, cols] + cb_ref[...])

    a = conv(slice(0, tf), cwa_ref, cba_ref)
    gt = conv(slice(tf, 2 * tf), cwg_ref, cbg_ref)
    act = (jax.nn.silu(gt) * a).astype(BF16)
    acc_ref[...] += jnp.dot(act, wd_ref[...], preferred_element_type=F32)

    @pl.when(j == pl.num_programs(1) - 1)
    def _():
        y = x1_ref[MLP_HALO:, :] + acc_ref[...]
        y_ref[...] = _rms(y, fg_ref[...]) if final_norm else y


def _mixer_out_mlp(x, o, w_out, g, w_up, conv_w, conv_b, w_down, final_g=None):
    t, d = x.shape
    f = w_down.shape[0]
    q = o.shape[1]
    tm = _pick_tile(t, (512, 256, 128))
    tf = _pick_tile(f, (1408, 256, 128))
    nf = f // tf
    hb = tm // MLP_HALO
    halo_map = lambda i, j: (jnp.maximum(i * hb - 1, 0), 0)
    fg = g if final_g is None else final_g
    return pl.pallas_call(
        functools.partial(_mixer_out_mlp_kernel, final_g is not None),
        grid=(t // tm, nf),
        in_specs=[pl.BlockSpec((tm, d), lambda i, j: (i, 0)),
                  pl.BlockSpec((MLP_HALO, d), halo_map),
                  pl.BlockSpec((tm, q), lambda i, j: (i, 0)),
                  pl.BlockSpec((MLP_HALO, q), halo_map),
                  pl.BlockSpec((q, d), lambda i, j: (0, 0)),
                  pl.BlockSpec((1, d), lambda i, j: (0, 0)),
                  pl.BlockSpec((1, d), lambda i, j: (0, 0)),
                  pl.BlockSpec((d, tf), lambda i, j: (0, j)),
                  pl.BlockSpec((d, tf), lambda i, j: (0, nf + j)),
                  pl.BlockSpec((3, tf), lambda i, j: (0, j)),
                  pl.BlockSpec((3, tf), lambda i, j: (0, nf + j)),
                  pl.BlockSpec((1, tf), lambda i, j: (0, j)),
                  pl.BlockSpec((1, tf), lambda i, j: (0, nf + j)),
                  pl.BlockSpec((tf, d), lambda i, j: (j, 0))],
        out_specs=pl.BlockSpec((tm, d), lambda i, j: (i, 0)),
        out_shape=jax.ShapeDtypeStruct((t, d), F32),
        scratch_shapes=[pltpu.VMEM((tm + MLP_HALO, d), F32), pltpu.VMEM((tm + MLP_HALO, d), BF16),
                        pltpu.VMEM((tm, d), F32),
                        pltpu.VMEM((tm + MLP_HALO, 2 * tf), F32)],
        compiler_params=_cparams(("parallel", "arbitrary")),
        name="mixer_out_mlp",
    )(x, x, o, o, w_out, g.reshape(1, d), fg.reshape(1, d), w_up, w_up, conv_w, conv_w, conv_b, conv_b, w_down)


def _swap_halves(q_pair):
    return pltpu.roll(q_pair.astype(F32), HEAD_DIM, 1).astype(BF16)


def _lane_half(shape):
    return lax.broadcasted_iota(jnp.int32, shape, 1) // HEAD_DIM


def _rows_to_heads(o_t, n_heads, tq):
    stacked = jnp.concatenate([o_t[:, r * tq:(r + 1) * tq] for r in range(n_heads)], axis=0)
    return stacked.T


def _swa_kernel(q_ref, kp_ref, kc_ref, vp_ref, vc_ref, bias_ref, sink_ref, o_ref):
    tq = q_ref.shape[0]
    k_ext =jnp.concatenate([kp_ref[...], kc_ref[...]], axis=0)
    v_ext = jnp.concatenate([vp_ref[0], vc_ref[0]], axis=1)
    half = _lane_half((tq, LANES))
    outs = []
    for g in range(SWA_KV_HEADS):
        parts = []
        for r in range(SWA_GROUP):
            hd = g * SWA_GROUP + r
            q_pair = q_ref[:, (hd // 2) * LANES:(hd // 2 + 1) * LANES]
            src = q_pair if hd % 2 == g else _swap_halves(q_pair)
            parts.append(jnp.where(half == g, src, jnp.zeros_like(src)))
        qp = jnp.concatenate(parts, axis=0)
        s = lax.dot_general(k_ext, qp, (((1,), (1,)), ((), ())), preferred_element_type=F32)
        s = s * (HEAD_DIM ** -0.5) + bias_ref[0, g]
        sink = sink_ref[g]
        m = jnp.maximum(jnp.max(s, axis=0, keepdims=True), sink)
        e = jnp.exp(s - m)
        l = jnp.sum(e, axis=0, keepdims=True) + jnp.exp(sink - m)
        o_t = jnp.dot(v_ext[g * HEAD_DIM:(g + 1) * HEAD_DIM], e.astype(BF16), preferred_element_type=F32)
        outs.append(_rows_to_heads(o_t / l, SWA_GROUP, tq))
    o_ref[...] = jnp.concatenate(outs, axis=1).astype(o_ref.dtype)


def _swa_bias():
    tq = SWA_TQ
    slopes = _alibi_slopes().reshape(SWA_KV_HEADS, SWA_GROUP)
    x = np.arange(2 * tq)[:, None] - tq
    q = np.arange(tq)[None, :]
    dist = q - x
    valid = (dist >= 0) & (dist < SWA_WINDOW)
    out = np.empty((2, SWA_KV_HEADS, 2 * tq, SWA_GROUP * tq), np.float32)
    for var in range(2):
        v = valid & ((x >= 0) | (var == 1))
        for g in range(SWA_KV_HEADS):
            for r in range(SWA_GROUP):
                out[var, g, :, r * tq:(r + 1) * tq] = np.where(v, -slopes[g, r] * dist, MASKED)
    return out


def _swa_attention(qkv, v_t, sinks):
    t = qkv.shape[0]
    tq = SWA_TQ
    kcol = Q_DIM // LANES
    bias = jnp.asarray(_swa_bias())
    sink_rows = jnp.repeat(sinks.astype(F32).reshape(SWA_KV_HEADS, 1, SWA_GROUP), tq, axis=2)
    prev = lambda i: jnp.maximum(i - 1, 0)
    return pl.pallas_call(
        _swa_kernel,
        grid=(t // tq,),
        in_specs=[pl.BlockSpec((tq, Q_DIM), lambda i: (i, 0)),
                  pl.BlockSpec((tq, LANES), lambda i: (prev(i), kcol)),
                  pl.BlockSpec((tq, LANES), lambda i: (i, kcol)),
                  pl.BlockSpec((1, LANES, tq), lambda i: (prev(i), 0, 0)),
                  pl.BlockSpec((1, LANES, tq), lambda i: (i, 0, 0)),
                  pl.BlockSpec((1,) + bias.shape[1:], lambda i: (jnp.minimum(i, 1), 0, 0, 0)),
                  pl.BlockSpec(sink_rows.shape, lambda i: (0, 0, 0))],
        out_specs=pl.BlockSpec((tq, Q_DIM), lambda i: (i, 0)),
        out_shape=jax.ShapeDtypeStruct((t, Q_DIM), BF16),
        compiler_params=_cparams(("parallel",)),
        name="swa_attention",
    )(qkv, qkv, qkv, v_t, v_t, bias, sink_rows)


def _swa_heads(x, norm_g, w_in, sinks):
    w = w_in.astype(BF16)
    n_qk = Q_DIM + SWA_KV_HEADS * HEAD_DIM
    qk, v_t = _norm_proj(x, norm_g, [w[:, :n_qk], w[:, n_qk:]], [BF16, BF16], chunks=[None, SWA_TQ])
    return _swa_attention(qk, v_t, sinks)


FOX_PARTS = 3


def _split_bf16(v):
    parts, rest = [], v
    for _ in range(FOX_PARTS):
        p = rest.astype(BF16)
        parts.append(p)
        rest = rest - p.astype(F32)
    return parts


def _fox_prep_kernel(f_ref, b_ref, aug_ref, cref_ref, carry_ref):
    tm = f_ref.shape[0]

    @pl.when(pl.program_id(0) == 0)
    def _():
        carry_ref[...] = jnp.zeros_like(carry_ref)

    logf = jax.nn.log_sigmoid(f_ref[...] + b_ref[...])
    row = lax.broadcasted_iota(jnp.int32, (tm, tm), 0)
    col = lax.broadcasted_iota(jnp.int32, (tm, tm), 1)
    tri = jnp.where(row >= col, 1.0, 0.0).astype(BF16)
    local = sum(jnp.dot(tri, p, preferred_element_type=F32) for p in _split_bf16(logf))
    cum = local + carry_ref[...]
    carry_ref[...] = cum[tm - 1:tm, :]
    first = cum[0:1, :]
    cref_ref[0] = jnp.broadcast_to(first, cref_ref.shape[1:])
    hi, mid, lo = _split_bf16(first - cum)
    lane = lax.broadcasted_iota(jnp.int32, (tm, LANES), 1)
    zero = jnp.zeros_like(hi)
    aug_ref[...] = jnp.where(lane < N_HEADS, hi, jnp.where(lane < 2 * N_HEADS, mid,
                                                            jnp.where(lane < 3 * N_HEADS, lo, zero)))


def _fox_prep(f_logit3, f_bias3):
    t = f_logit3.shape[0]
    nt = t // FOX_T
    return pl.pallas_call(
        _fox_prep_kernel,
        grid=(nt,),
        in_specs=[pl.BlockSpec((FOX_T, LANES), lambda i: (i, 0)), pl.BlockSpec((1, LANES), lambda i: (0, 0))],
        out_specs=[pl.BlockSpec((FOX_T, LANES), lambda i: (i, 0)), pl.BlockSpec((1, 8, LANES), lambda i: (i, 0, 0))],
        out_shape=[jax.ShapeDtypeStruct((t, LANES), BF16), jax.ShapeDtypeStruct((nt, 8, LANES), F32)],
        scratch_shapes=[pltpu.VMEM((1, LANES), F32)],
        compiler_params=_cparams(("arbitrary",)),
        name="fox_prep",
    )(f_logit3, f_bias3)


def _fox_kernel(cref_ref, q_ref, k_ref, aug_ref, vt_ref, o_ref, m_ref, l_ref, acc_ref, qt_ref, s_ref):
    pair, qi = pl.program_id(0), pl.program_id(1)
    tq = q_ref.shape[0]
    tk = FOX_T
    lane = lax.broadcasted_iota(jnp.int32, (tq, LANES), 1)
    q2 = q_ref[...] * jnp.asarray(HEAD_DIM ** -0.5, BF16)
    for hh in range(2):
        head = 2 * pair + hh
        pick = (lane % N_HEADS == head) & (lane < FOX_PARTS * N_HEADS)
        q_ext = jnp.concatenate([jnp.where(lane // HEAD_DIM == hh, q2, jnp.zeros_like(q2)).astype(F32),
                                 jnp.where(pick, 1.0, 0.0)], axis=1)
        qt_ref[hh] = q_ext.T.astype(BF16)
    m_ref[...] = jnp.full_like(m_ref, M_INIT)
    l_ref[...] = jnp.zeros_like(l_ref)
    acc_ref[...] = jnp.zeros_like(acc_ref)
    ratio = tq // tk

    def scores(kj, hh):
        start = pl.multiple_of(kj * tk, tk)
        k_ext = jnp.concatenate([k_ref[pl.ds(start, tk), :], aug_ref[pl.ds(start, tk), :]], axis=1)
        s_ref[hh] = jnp.dot(k_ext, qt_ref[hh], preferred_element_type=F32)

    def consume(kj, hh, key_offset):
        head = 2 * pair + hh
        s = s_ref[hh]
        if key_offset is not None:
            key = lax.broadcasted_iota(jnp.int32, (tk, tq), 0) + key_offset
            qry = lax.broadcasted_iota(jnp.int32, (tk, tq), 1)
            s = jnp.where(key <= qry, s, MASKED)
        off = cref_ref[qi * ratio, head] - cref_ref[kj, head]
        m_old = m_ref[hh]
        m_new = jnp.maximum(m_old, jnp.max(s, axis=0, keepdims=True) + off)
        e = jnp.exp(s - (m_new - off))
        alpha = jnp.exp(m_old - m_new)
        m_ref[hh] = m_new
        l_ref[hh] = alpha * l_ref[hh] + jnp.sum(e, axis=0, keepdims=True)
        v_t = vt_ref[kj, hh * HEAD_DIM:(hh + 1) * HEAD_DIM, :]
        acc_ref[hh] = alpha * acc_ref[hh] + jnp.dot(v_t, e.astype(BF16), preferred_element_type=F32)

    n_full = qi * ratio
    scores(0, 0)

    def body(j, carry):
        for d in range(ratio):
            kj = j * ratio + d
            scores(kj, 1)
            consume(kj, 0, None)
            scores(kj + 1, 0)
            consume(kj, 1, None)
        return carry

    lax.fori_loop(0, qi, body, 0)
    for d in range(ratio):
        kj = n_full + d
        scores(kj, 1)
        consume(kj, 0, d * tk)
        if d + 1 < ratio:
            scores(kj + 1, 0)
        consume(kj, 1, d * tk)
    o_t = jnp.concatenate([acc_ref[hh] / l_ref[hh] for hh in range(2)], axis=0)
    o_ref[...] = o_t.T.astype(o_ref.dtype)


def _fox_attention(qkv, aug, v_t3, cref):
    t = qkv.shape[0]
    tq = FOX_TQ
    nk = t // FOX_T
    kcol = Q_DIM // LANES
    return pl.pallas_call(
        _fox_kernel,
        grid=(N_HEADS // 2, t // tq),
        in_specs=[pl.BlockSpec(memory_space=pltpu.SMEM),
                  pl.BlockSpec((tq, LANES), lambda p, i: (i, p)),
                  pl.BlockSpec((t, LANES), lambda p, i: (0, kcol + p)),
                  pl.BlockSpec((t, LANES), lambda p, i: (0, 0)),
                  pl.BlockSpec((nk, LANES, FOX_T), lambda p, i: (0, p, 0))],
        out_specs=pl.BlockSpec((tq, LANES), lambda p, i: (i, p)),
        out_shape=jax.ShapeDtypeStruct((t, Q_DIM), BF16),
        scratch_shapes=[pltpu.VMEM((2, 1, tq), F32), pltpu.VMEM((2, 1, tq), F32),
                        pltpu.VMEM((2, HEAD_DIM, tq), F32), pltpu.VMEM((2, 2 * LANES, tq), BF16), pltpu.VMEM((2, FOX_T, tq), F32)],
        compiler_params=_cparams(("parallel", "arbitrary")),
        name="fox_attention",
    )(cref, qkv, qkv, aug, v_t3)


def _fox_heads(x, norm_g, w_in, f_bias):
    t = x.shape[0]
    w_f = w_in[:, 3 * Q_DIM:]
    pad = jnp.zeros((D_MODEL, LANES - FOX_PARTS * N_HEADS), w_f.dtype)
    w_f3 = jnp.concatenate([w_f] * FOX_PARTS + [pad], axis=1).astype(BF16)
    b3 = jnp.concatenate([f_bias.astype(F32)] * FOX_PARTS + [jnp.zeros((LANES - FOX_PARTS * N_HEADS,), F32)])
    w = w_in.astype(BF16)
    qk, v_t3, f_logit3 = _norm_proj(x, norm_g, [w[:, :2 * Q_DIM], w[:, 2 * Q_DIM:3 * Q_DIM], w_f3], [BF16, BF16, F32],
                                    chunks=[None, FOX_T, None])
    aug, cref = _fox_prep(f_logit3, b3.reshape(1, LANES))
    return _fox_attention(qk, aug, v_t3, cref[:, 0, :N_HEADS])


def _nsa_compress_kernel(a_ref, pos_ref, w1_ref, w2_ref, o_ref):
    tc = a_ref.shape[2]
    half = CMP_STRIDE * HEAD_DIM
    pos = pos_ref[0]
    out = jnp.zeros(o_ref.shape[1:], F32)
    for gg in range(2):
        a = a_ref[0, gg]
        top = jnp.dot((a + pos[:, :half]).astype(BF16), w1_ref[0, :half], preferred_element_type=F32)
        bot = jnp.dot((a + pos[:, half:]).astype(BF16), w1_ref[0, half:], preferred_element_type=F32)
        hid = top + pltpu.roll(bot, tc - 1, 0)
        out = out + jnp.dot(jax.nn.gelu(hid).astype(BF16), w2_ref[0, gg], preferred_element_type=F32)
    o_ref[0] = out


def _nsa_compress(a, pos, w1, w2):
    _, g, tc, width = a.shape
    return pl.pallas_call(
        _nsa_compress_kernel,
        grid=(2, g // 2),
        in_specs=[pl.BlockSpec((1, 2, tc, width), lambda kv, gp: (kv, gp, 0, 0)),
                  pl.BlockSpec((1, 1, 2 * width), lambda kv, gp: (kv, 0, 0)),
                  pl.BlockSpec((1, 2 * width, CMP_HIDDEN), lambda kv, gp: (kv, 0, 0)),
                  pl.BlockSpec((1, 2, CMP_HIDDEN, LANES), lambda kv, gp: (kv, 0, 0, 0))],
        out_specs=pl.BlockSpec((1, tc, LANES), lambda kv, gp: (kv, 0, gp)),
        out_shape=jax.ShapeDtypeStruct((2, tc, g * HEAD_DIM), F32),
        compiler_params=_cparams(("parallel", "parallel")),
        name="nsa_compress",
    )(a, pos, w1, w2)


NSA_WIN_TILES = NSA_WINDOW // NSA_TQ + 1


def _nsa_kernel(slopes_ref, kb_ref, q_ref, ks_ref, *refs):
    kw_refs, refs = refs[:NSA_WIN_TILES], refs[NSA_WIN_TILES:]
    vst_ref, refs = refs[0], refs[1:]
    vw_refs, refs = refs[:NSA_WIN_TILES], refs[NSA_WIN_TILES:]
    (kcmp_ref, vov_ref, gates_ref, ksel_ext_ref, kwin_ext_ref, kcmp_ext_ref, qext_ref,
     o_ref, selt_ref, m_ref, l_ref, acc_ref, qt_ref, s_ref, scmp_ref, cm_ref, cl_ref, oc_ref,
     active_ref, list_ref) = refs
    g, qi = pl.program_id(0), pl.program_id(1)
    tq = q_ref.shape[0]
    tc = kcmp_ref.shape[0]
    n_sel = selt_ref.shape[0]
    rr = NSA_GROUP
    t0 = qi * tq
    hg = g % 2
    slope = [slopes_ref[g * rr + r] for r in range(rr)]

    half = _lane_half((tq, LANES))
    parts = []
    for r in range(rr):
        q_pair = q_ref[:, (r // 2) * LANES:(r // 2 + 1) * LANES]
        src = jnp.where(hg == r % 2, q_pair, _swap_halves(q_pair))
        parts.append(jnp.where(half == hg, src, jnp.zeros_like(src)) * jnp.asarray(HEAD_DIM ** -0.5, BF16))
    qp = jnp.concatenate(parts, axis=0)
    qt_ref[0:LANES, :] = qp.astype(F32).T.astype(BF16)

    @pl.when(qi == 0)
    def _():
        qt_ref[LANES:, :] = qext_ref[0]

    def lanes_of(s, r):
        return s[:, r * tq:(r + 1) * tq]

    chunk_span = NSA_CMP_CHUNK * CMP_STRIDE
    last_chunk = ((t0 + tq - CMP_BLOCK) // CMP_STRIDE) // NSA_CMP_CHUNK
    nl16_q = (CMP_STRIDE * lax.broadcasted_iota(jnp.int32, (NSA_CMP_CHUNK, tq), 0)
              - lax.broadcasted_iota(jnp.int32, (NSA_CMP_CHUNK, tq), 1)).astype(F32)
    cm_ref[...] = jnp.full_like(cm_ref, M_INIT)

    def cmp_rows(ch):
        return pl.ds(pl.multiple_of(ch * NSA_CMP_CHUNK, NSA_CMP_CHUNK), NSA_CMP_CHUNK)

    def cmp_offset(ch, r):
        return slope[r] * (ch * chunk_span).astype(F32)

    def cmp_scores(ch, masked):
        k_full = jnp.concatenate([kcmp_ref[cmp_rows(ch), :], kcmp_ext_ref[...]], axis=1)
        s = jnp.dot(k_full, qt_ref[...], preferred_element_type=F32)
        if masked:
            visible = nl16_q <= (t0 - (CMP_BLOCK - 1) - ch * chunk_span).astype(F32)
            s = jnp.concatenate([jnp.where(visible, lanes_of(s, r), MASKED) for r in range(rr)], axis=1)
        scmp_ref[cmp_rows(ch), :] = s
        for r in range(rr):
            cm_ref[r] = jnp.maximum(cm_ref[r], jnp.max(lanes_of(s, r), axis=0, keepdims=True) + cmp_offset(ch, r))

    def cmp_plain(ch, carry):
        cmp_scores(ch, False)
        return carry

    lax.fori_loop(0, jnp.maximum(last_chunk - 1, 0), cmp_plain, 0)

    @pl.when(last_chunk >= 1)
    def _():
        cmp_scores(last_chunk - 1, True)

    cmp_scores(last_chunk, True)

    cl_ref[...] = jnp.zeros_like(cl_ref)
    oc_ref[...] = jnp.zeros_like(oc_ref)

    def cmp_exp(ch, carry):
        s = scmp_ref[cmp_rows(ch), :]
        e_all = []
        for r in range(rr):
            e = jnp.exp(lanes_of(s, r) - (cm_ref[r] - cmp_offset(ch, r)))
            cl_ref[r] = cl_ref[r] + jnp.sum(e, axis=0, keepdims=True)
            e_all.append(e.astype(BF16))
        oc_ref[...] += jnp.dot(vov_ref[ch, 0], jnp.concatenate(e_all, axis=1), preferred_element_type=F32)
        return carry

    lax.fori_loop(0, last_chunk + 1, cmp_exp, 0)
    inv_l = [jnp.where(cl_ref[r] > 0.0, 1.0 / cl_ref[r], 0.0) for r in range(rr)]
    o_cmp = [oc_ref[0:HEAD_DIM, r * tq:(r + 1) * tq] * inv_l[r] for r in range(rr)]

    imp = functools.reduce(jnp.add, [oc_ref[HEAD_DIM:, r * tq:(r + 1) * tq] * inv_l[r] for r in range(rr)])
    blk = lax.broadcasted_iota(jnp.int32, (n_sel, tq), 0)
    cur = (t0 + lax.broadcasted_iota(jnp.int32, (n_sel, tq), 1)) // SEL_BLOCK
    blk_f = blk.astype(F32)
    forced = jnp.where(blk == 0, 1.0, jnp.where(blk == cur, 1.0, jnp.where(blk == cur - 1, 1.0, 0.0)))
    score = jnp.where(forced > 0.0, TAKEN, jnp.where(blk <= cur, imp, -1.0))

    selt_ref[...] = score
    row_step = min(n_sel, NSA_TOPK_ROW_STEP)
    variant = ((t0 + tq - 1) // SEL_BLOCK) // row_step

    for v in range(n_sel // row_step):
        rows = row_step * (v + 1)

        @pl.when(variant == v)
        def _(rows=rows):
            ids = lax.broadcasted_iota(jnp.int32, (rows, tq), 0).astype(F32)

            def pick(_, sc):
                best = jnp.max(sc, axis=0, keepdims=True)
                first = jnp.min(jnp.where(sc == best, ids, float(n_sel)), axis=0, keepdims=True)
                return jnp.where(ids == first, TAKEN, sc)

            selt_ref[0:rows, :] = lax.fori_loop(0, N_SELECT - 3, pick, selt_ref[0:rows, :])

    selt_ref[...] = jnp.where(blk <= cur, jnp.where(selt_ref[...] == TAKEN, 1.0, 0.0), 0.0)

    m_ref[...] = jnp.full_like(m_ref, M_INIT)
    l_ref[...] = jnp.zeros_like(l_ref)
    acc_ref[...] = jnp.zeros_like(acc_ref)
    blocks_per_chunk = NSA_TK // SEL_BLOCK

    for c in range(n_sel // blocks_per_chunk):
        active_ref[c] = jnp.max(selt_ref[c * blocks_per_chunk:(c + 1) * blocks_per_chunk, :])

    q_sq = jnp.square(qp.astype(F32)).astype(BF16)
    q_norm2 = jnp.dot(q_sq, jnp.ones((LANES, LANES), BF16), preferred_element_type=F32)
    q_bound = jnp.max(jnp.sqrt(jnp.max(q_norm2, axis=0, keepdims=True))) * NORM_SLACK
    own_chunk = t0 // NSA_TK

    def compact(c, n):
        list_ref[n] = c
        gap = slope[rr - 1] * (t0 - (c * NSA_TK + NSA_TK - 1)).astype(F32)
        reach = q_bound * (kb_ref[c, g] + kb_ref[own_chunk, g]) + SKIP_GAP
        keep = (active_ref[c] > 0.0) & ((c == own_chunk) | (gap <= reach))
        return n + keep.astype(jnp.int32)

    n_active = lax.fori_loop(0, own_chunk + 1, compact, 0)

    def sel_scores(c, buf):
        start = pl.multiple_of(c * NSA_TK, NSA_TK)
        rows = selt_ref[pl.ds(pl.multiple_of(c * blocks_per_chunk, blocks_per_chunk), blocks_per_chunk), :]
        mask_rows = jnp.concatenate([jnp.where(rows > 0.5, 0.0, MASKED), jnp.zeros_like(rows)], axis=0)
        first = LANES + buf * NSA_EXT_MASK_ROWS
        qt_ref[first:first + NSA_EXT_MASK_ROWS, :] = jnp.concatenate([mask_rows] * rr, axis=1).astype(BF16)
        k_full = jnp.concatenate([ks_ref[pl.ds(start, NSA_TK), :], ksel_ext_ref[buf]], axis=1)
        s_ref[buf] = jnp.dot(k_full, qt_ref[...], preferred_element_type=F32)

    def sel_consume(c, buf, diagonal):
        shift = (c * NSA_TK - t0).astype(F32)
        if diagonal:
            key = lax.broadcasted_iota(jnp.int32, (NSA_TK, tq), 0) + c * NSA_TK
            visible = key <= lax.broadcasted_iota(jnp.int32, (NSA_TK, tq), 1) + t0
        e_all, alpha_all = [], []
        for r in range(rr):
            off = slope[r] * shift
            s = s_ref[buf][:, r * tq:(r + 1) * tq]
            if diagonal:
                s = jnp.where(visible, s, MASKED)
            m_old = m_ref[r]
            m_new = jnp.maximum(m_old, jnp.max(s, axis=0, keepdims=True) + off)
            e = jnp.exp(s - (m_new - off))
            alpha = jnp.exp(m_old - m_new)
            m_ref[r] = m_new
            l_ref[r] = alpha * l_ref[r] + jnp.sum(e, axis=0, keepdims=True)
            e_all.append(e.astype(BF16))
            alpha_all.append(alpha)
        pv = jnp.dot(vst_ref[c], jnp.concatenate(e_all, axis=1), preferred_element_type=F32)
        acc_ref[...] = jnp.concatenate(alpha_all, axis=1) * acc_ref[...] + pv

    n_before = n_active - 1
    sel_scores(list_ref[0], 0)

    def sel_pair(j, carry):
        c0, c1, c2 = list_ref[2 * j], list_ref[2 * j + 1], list_ref[2 * j + 2]
        sel_scores(c1, 1)
        sel_consume(c0, 0, False)
        sel_scores(c2, 0)
        sel_consume(c1, 1, False)
        return carry

    lax.fori_loop(0, n_before // 2, sel_pair, 0)

    @pl.when(n_before % 2 == 1)
    def _():
        c0, c_last = list_ref[n_before - 1], list_ref[n_before]
        sel_scores(c_last, 1)
        sel_consume(c0, 0, False)
        sel_consume(c_last, 1, True)

    @pl.when(n_before % 2 == 0)
    def _():
        sel_consume(list_ref[n_before], 0, True)

    xw = lax.broadcasted_iota(jnp.int32, (tq, tq), 0)
    qw = lax.broadcasted_iota(jnp.int32, (tq, tq), 1)
    s_win, off_win = [], []
    for idx in range(NSA_WIN_TILES):
        back = NSA_WIN_TILES - 1 - idx
        k_full = jnp.concatenate([kw_refs[idx][...], kwin_ext_ref[...]], axis=1)
        s_t = jnp.dot(k_full, qt_ref[...], preferred_element_type=F32)
        per_head = [lanes_of(s_t, r) for r in range(rr)]
        if back == NSA_WIN_TILES - 1:
            per_head = [jnp.where(xw > qw, s, MASKED) for s in per_head]
        if back == 0:
            per_head = [jnp.where(xw <= qw, s, MASKED) for s in per_head]
        s_win.append(per_head)
        off_win.append([jnp.where(qi >= back, -slope[r] * float(back * tq), MASKED) for r in range(rr)])
    e_win, l_win = [], []
    for r in range(rr):
        m = functools.reduce(jnp.maximum, [jnp.max(s_win[idx][r], axis=0, keepdims=True) + off_win[idx][r]
                                           for idx in range(NSA_WIN_TILES)])
        e_tiles = [jnp.exp(s_win[idx][r] - (m - off_win[idx][r])) for idx in range(NSA_WIN_TILES)]
        l_win.append(functools.reduce(jnp.add, [jnp.sum(e, axis=0, keepdims=True) for e in e_tiles]))
        e_win.append(jnp.concatenate([e.astype(BF16) for e in e_tiles], axis=0))
    v_win = jnp.concatenate([vw_refs[idx][0] for idx in range(NSA_WIN_TILES)], axis=1)
    o_win_all = jnp.dot(v_win, jnp.concatenate(e_win, axis=1), preferred_element_type=F32)
    o_win = [lanes_of(o_win_all, r) / l_win[r] for r in range(rr)]

    gate = jax.nn.sigmoid(gates_ref[0])
    merged = []
    for r in range(rr):
        o_sel = lanes_of(acc_ref[...], r) / l_ref[r]
        merged.append(gate[3 * r:3 * r + 1] * o_cmp[r] + gate[3 * r + 1:3 * r + 2] * o_sel
                      + gate[3 * r + 2:3 * r + 3] * o_win[r])
    o_ref[...] = jnp.concatenate(merged, axis=0).T.astype(o_ref.dtype)


def _nsa_key_bounds_kernel(k_ref, o_ref):
    sq = jnp.square(k_ref[...].astype(F32))
    lane = lax.broadcasted_iota(jnp.int32, o_ref.shape[1:], 1)
    out = jnp.zeros(o_ref.shape[1:], F32)
    for g in range(NSA_KV_HEADS):
        norm2 = jnp.sum(sq[:, g * HEAD_DIM:(g + 1) * HEAD_DIM], axis=1, keepdims=True)
        bound = jnp.sqrt(jnp.max(norm2, axis=0, keepdims=True)) * NORM_SLACK
        out = jnp.where(lane == g, bound, out)
    o_ref[0] = out


def _nsa_key_bounds(main):
    t = main.shape[0]
    nc = t // NSA_TK
    col = Q_DIM // NSA_KV_DIM
    return pl.pallas_call(
        _nsa_key_bounds_kernel,
        grid=(nc,),
        in_specs=[pl.BlockSpec((NSA_TK, NSA_KV_DIM), lambda c: (c, col))],
        out_specs=pl.BlockSpec((1, 8, LANES), lambda c: (c, 0, 0)),
        out_shape=jax.ShapeDtypeStruct((nc, 8, LANES), F32),
        compiler_params=_cparams(("parallel",)),
        name="nsa_key_bounds",
    )(main)


def _nsa_overlap_t(n_sel, tc):
    cmp_start = np.arange(tc - 1) * CMP_STRIDE
    sel_start = np.arange(n_sel) * SEL_BLOCK
    ov = np.clip(np.minimum(cmp_start[None, :] + CMP_BLOCK, sel_start[:, None] + SEL_BLOCK)
                 - np.maximum(cmp_start[None, :], sel_start[:, None]), 0, None) / CMP_BLOCK
    return np.concatenate([ov, np.zeros((n_sel, 1))], axis=1).astype(np.float32)


NSA_EXT_MASK_ROWS = 16
NSA_EXT_SEL = 32
NSA_EXT_WIN = 48
NSA_EXT_CMP = 64
NSA_CMP_CHUNK = 256
NSA_TOPK_ROW_STEP = 64
SKIP_GAP = 110.0
NORM_SLACK = 1.01
SLOPE_PIECES = 3


def _bf16_pieces(v):
    out, rest = [], np.asarray(v, np.float32)
    for _ in range(SLOPE_PIECES):
        p = rest.astype(BF16).astype(np.float32)
        out.append(p)
        rest = rest - p
    return out


def _nsa_extensions(tq):
    ksel = np.zeros((2, NSA_TK, LANES), np.float32)
    x = np.arange(NSA_TK)
    for buf in range(2):
        ksel[buf, x, buf * NSA_EXT_MASK_ROWS + x // SEL_BLOCK] = 1.0
    for j in range(SLOPE_PIECES):
        ksel[:, :, NSA_EXT_SEL + j] = 256 * (x // 256)
        ksel[:, :, NSA_EXT_SEL + SLOPE_PIECES + j] = x % 256
    kwin = np.zeros((tq, LANES), np.float32)
    for j in range(SLOPE_PIECES):
        kwin[:, NSA_EXT_WIN + j] = np.arange(tq)
    assert tq <= 256, "a window key offset must be exact in bf16"
    kcmp = np.zeros((NSA_CMP_CHUNK, LANES), np.float32)
    for j in range(SLOPE_PIECES):
        kcmp[:, NSA_EXT_CMP + j] = np.arange(NSA_CMP_CHUNK)
    pieces = _bf16_pieces(_alibi_slopes())
    qext = np.zeros((NSA_KV_HEADS, LANES, NSA_GROUP * tq), np.float32)
    for g in range(NSA_KV_HEADS):
        for r in range(NSA_GROUP):
            cols = slice(r * tq, (r + 1) * tq)
            for j in range(SLOPE_PIECES):
                p = pieces[j][g * NSA_GROUP + r]
                qext[g, NSA_EXT_SEL + j, cols] = p
                qext[g, NSA_EXT_SEL + SLOPE_PIECES + j, cols] = p
                qext[g, NSA_EXT_WIN + j, cols] = p
                qext[g, NSA_EXT_CMP + j, cols] = CMP_STRIDE * p
    return jnp.asarray(ksel, BF16), jnp.asarray(kwin, BF16), jnp.asarray(kcmp, BF16), jnp.asarray(qext, BF16)


def _nsa_attention(main, vs_t3, vw_t, k_cmp, v_cmp_t, gates_t):
    t = main.shape[0]
    tq = NSA_TQ
    tc = k_cmp.shape[0]
    n_sel = t // SEL_BLOCK
    nc = t // NSA_TK
    ks_col = Q_DIM // LANES
    kw_col = ks_col + NSA_KV_DIM // LANES
    n_ch = tc // NSA_CMP_CHUNK
    ov_t = jnp.asarray(_nsa_overlap_t(n_sel, tc).reshape(n_sel, n_ch, NSA_CMP_CHUNK).transpose(1, 0, 2), BF16)
    v_cmp_t = v_cmp_t.reshape(NSA_KV_HEADS, HEAD_DIM, n_ch, NSA_CMP_CHUNK).transpose(2, 0, 1, 3)
    vov = jnp.concatenate([v_cmp_t, jnp.broadcast_to(ov_t[:, None], (n_ch, NSA_KV_HEADS, n_sel, NSA_CMP_CHUNK))],
                          axis=2)
    slopes = jnp.asarray(_alibi_slopes())
    key_bounds = _nsa_key_bounds(main)[:, 0, :NSA_KV_HEADS]
    ksel_ext, kwin_ext, kcmp_ext, qext = _nsa_extensions(tq)

    def kw_spec(back):
        return pl.BlockSpec((tq, LANES), lambda g, i: (jnp.maximum(i - back, 0), kw_col + g // 2))

    def vw_spec(back):
        return pl.BlockSpec((1, HEAD_DIM, tq), lambda g, i: (jnp.maximum(i - back, 0), g, 0))

    backs = list(range(NSA_WIN_TILES - 1, -1, -1))
    return pl.pallas_call(
        _nsa_kernel,
        grid=(NSA_KV_HEADS, t // tq),
        in_specs=[pl.BlockSpec(memory_space=pltpu.SMEM), pl.BlockSpec(memory_space=pltpu.SMEM),
                  pl.BlockSpec((tq, NSA_GROUP * HEAD_DIM), lambda g, i: (i, g)),
                  pl.BlockSpec((t, LANES), lambda g, i: (0, ks_col + g // 2))]
                 + [kw_spec(b) for b in backs]
                 + [pl.BlockSpec((nc, HEAD_DIM, NSA_TK), lambda g, i: (0, g, 0))]
                 + [vw_spec(b) for b in backs]
                 + [pl.BlockSpec((tc, LANES), lambda g, i: (0, g // 2)),
                    pl.BlockSpec((n_ch, 1, HEAD_DIM + n_sel, NSA_CMP_CHUNK), lambda g, i: (0, g, 0, 0)),
                    pl.BlockSpec((1, 16, tq), lambda g, i: (g, 0, i)),
                    pl.BlockSpec(ksel_ext.shape, lambda g, i: (0, 0, 0)),
                    pl.BlockSpec(kwin_ext.shape, lambda g, i: (0, 0)),
                    pl.BlockSpec(kcmp_ext.shape, lambda g, i: (0, 0)),
                    pl.BlockSpec((1,) + qext.shape[1:], lambda g, i: (g, 0, 0))],
        out_specs=pl.BlockSpec((tq, NSA_GROUP * HEAD_DIM), lambda g, i: (i, g)),
        out_shape=jax.ShapeDtypeStruct((t, Q_DIM), BF16),
        scratch_shapes=[pltpu.VMEM((n_sel, tq), F32), pltpu.VMEM((NSA_GROUP, 1, tq), F32),
                        pltpu.VMEM((NSA_GROUP, 1, tq), F32), pltpu.VMEM((HEAD_DIM, NSA_GROUP * tq), F32),
                        pltpu.VMEM((2 * LANES, NSA_GROUP * tq), BF16),
                        pltpu.VMEM((2, NSA_TK, NSA_GROUP * tq), F32),
                        pltpu.VMEM((tc, NSA_GROUP * tq), F32), pltpu.VMEM((NSA_GROUP, 1, tq), F32),
                        pltpu.VMEM((NSA_GROUP, 1, tq), F32), pltpu.VMEM((HEAD_DIM + n_sel, NSA_GROUP * tq), F32),
                        pltpu.SMEM((nc,), F32), pltpu.SMEM((nc,), jnp.int32)],
        compiler_params=_cparams(("parallel", "arbitrary")),
        name="nsa_attention",
    )(slopes, key_bounds, main, main, *([main] * NSA_WIN_TILES), vs_t3, *([vw_t] * NSA_WIN_TILES), k_cmp, vov, gates_t,
      ksel_ext, kwin_ext, kcmp_ext, qext)


def _nsa_heads(x, norm_g, w_in, cmp_pos, cmp_w1, cmp_w2):
    t = x.shape[0]
    kv = NSA_KV_DIM
    col = lambda i: w_in[:, Q_DIM + i * kv:Q_DIM + (i + 1) * kv]
    w_main = jnp.concatenate([w_in[:, :Q_DIM], col(2), col(4)], axis=1).astype(BF16)
    w_cmp = jnp.concatenate([col(0), col(1)], axis=1).astype(BF16)
    n_gate = 3 * N_HEADS
    w_gate = jnp.concatenate([w_in[:, Q_DIM + 6 * kv:], jnp.zeros((D_MODEL, LANES - n_gate), w_in.dtype)],
                             axis=1).astype(BF16)
    main, vs_t3, vw_t3, kcvc, gate_logit = _norm_proj(
        x, norm_g, [w_main, col(3).astype(BF16), col(5).astype(BF16), w_cmp, w_gate], [BF16, BF16, BF16, F32, F32],
        chunks=[None, NSA_TK, NSA_TQ, None, None])

    tc = t // CMP_STRIDE
    a = kcvc.reshape(tc, CMP_STRIDE, 2, NSA_KV_HEADS, HEAD_DIM).transpose(2, 3, 0, 1, 4)
    a = a.reshape(2, NSA_KV_HEADS, tc, CMP_STRIDE * HEAD_DIM)
    zeros = jnp.zeros_like(cmp_w2)
    w2 = jnp.stack([jnp.concatenate([cmp_w2, zeros], axis=2), jnp.concatenate([zeros, cmp_w2], axis=2)], axis=1)
    cmp = _nsa_compress(a, cmp_pos.reshape(2, 1, CMP_BLOCK * HEAD_DIM), cmp_w1.astype(BF16), w2.astype(BF16))
    k_cmp = cmp[0].astype(BF16)
    v_cmp_t = cmp[1].T.astype(BF16)

    gates_t =gate_logit[:, :n_gate].reshape(t, NSA_KV_HEADS, 3 * NSA_GROUP).transpose(1, 2, 0)
    gates_t = jnp.pad(gates_t, ((0, 0), (0, 16 - 3 * NSA_GROUP), (0, 0)))
    return _nsa_attention(main, vs_t3, vw_t3, k_cmp, v_cmp_t, gates_t)


def kernel(x, attn_norm, mlp_norm, final_norm, nsa_w_in, nsa_cmp_pos, nsa_cmp_w1, nsa_cmp_w2, nsa_w_out,
           swa_w_in, swa_sinks, swa_w_out, fox_w_in, fox_f_bias, fox_w_out,
           mlp_w_up, mlp_conv_w, mlp_conv_b, mlp_w_down):
    assert x.shape[0] == 1, "the trunk is written for batch 1"
    h = x[0]
    for i in range(DEPTH):
        kind, j = i % N_MIXERS, i // N_MIXERS
        if kind == 0:
            o = _nsa_heads(h, attn_norm[i], nsa_w_in[j], nsa_cmp_pos[j], nsa_cmp_w1[j], nsa_cmp_w2[j])
            w_out = nsa_w_out[j]
        elif kind == 1:
            o = _swa_heads(h, attn_norm[i], swa_w_in[j], swa_sinks[j])
            w_out = swa_w_out[j]
        else:
            o = _fox_heads(h, attn_norm[i], fox_w_in[j], fox_f_bias[j])
            w_out = fox_w_out[j]
        h = _mixer_out_mlp(h, o, w_out.astype(BF16), mlp_norm[i], mlp_w_up[i].astype(BF16), mlp_conv_w[i],
                           mlp_conv_b[i].reshape(1, -1), mlp_w_down[i].astype(BF16),
                           final_g=final_norm if i == DEPTH - 1 else None)
    return h[None]
```

```python
import functools

import numpy as np
import jax
import jax.numpy as jnp
from jax import lax
from jax.experimental import pallas as pl
from jax.experimental.pallas import tpu as pltpu

F32 = jnp.float32
BF16 = jnp.bfloat16

D_MODEL = 1024
DEPTH = 4
N_MIXERS = 3
HEAD_DIM = 64
N_HEADS = 16
Q_DIM = N_HEADS * HEAD_DIM
ALIBI_MAX = 8.0
NORM_EPS = 1e-6
MASKED = -2e30
M_INIT = -1e30
TAKEN = -(2.0 ** 127)

NSA_KV_HEADS = 4
NSA_GROUP = 4
NSA_KV_DIM = NSA_KV_HEADS * HEAD_DIM
CMP_BLOCK = 32
CMP_STRIDE = 16
CMP_HIDDEN = 256
SEL_BLOCK = 64
N_SELECT = 16
NSA_WINDOW = 512
NSA_TQ = 256
NSA_TK = 512

SWA_KV_HEADS = 2
SWA_GROUP = 8
SWA_WINDOW = 128
SWA_TQ = 128

FOX_TQ = 1024
FOX_T = 512

LANES = 128
VMEM_LIMIT = 56 * 1024 * 1024


def _cparams(semantics, vmem=VMEM_LIMIT):
    return pltpu.CompilerParams(dimension_semantics=semantics, vmem_limit_bytes=vmem)


def _alibi_slopes():
    return np.asarray(2.0 ** (-ALIBI_MAX * np.arange(1, N_HEADS + 1) / N_HEADS), dtype=np.float32)


def _rms(x, g):
    ms = jnp.mean(x * x, axis=-1, keepdims=True)
    return x * lax.rsqrt(ms + NORM_EPS) * g


def _norm_proj_kernel(chunks, x_ref, g_ref, *refs):
    n_out = len(chunks)
    h = _rms(x_ref[...], g_ref[...]).astype(BF16)
    for w_ref, o_ref, chunk in zip(refs[:n_out], refs[n_out:], chunks):
        res = jnp.dot(h, w_ref[...], preferred_element_type=F32)
        if chunk is None:
            o_ref[...] = res.astype(o_ref.dtype)
        else:
            for c in range(res.shape[0] // chunk):
                o_ref[c] = res[c * chunk:(c + 1) * chunk].T.astype(o_ref.dtype)


def _pick_tile(n, candidates):
    for c in candidates:
        if n % c == 0:
            return c
    raise ValueError(f"no tile for {n}")


def _norm_proj(x, g, weights, out_dtypes, chunks=None):
    t, d = x.shape
    tm = _pick_tile(t, (1024, 512, 256, 128))
    chunks = tuple(chunks) if chunks else (None,) * len(weights)
    out_specs, out_shape = [], []
    for w, dt, c in zip(weights, out_dtypes, chunks):
        n = w.shape[1]
        if c is None:
            out_specs.append(pl.BlockSpec((tm, n), lambda i: (i, 0)))
            out_shape.append(jax.ShapeDtypeStruct((t, n), dt))
        else:
            out_specs.append(pl.BlockSpec((tm // c, n, c), lambda i: (i, 0, 0)))
            out_shape.append(jax.ShapeDtypeStruct((t // c, n, c), dt))
    return pl.pallas_call(
        functools.partial(_norm_proj_kernel, chunks),
        grid=(t // tm,),
        in_specs=[pl.BlockSpec((tm, d), lambda i: (i, 0)), pl.BlockSpec((1, d), lambda i: (0, 0))]
                 + [pl.BlockSpec(w.shape, lambda i: (0, 0)) for w in weights],
        out_specs=out_specs,
        out_shape=out_shape,
        compiler_params=_cparams(("parallel",)),
        name="norm_proj",
    )(x, g.reshape(1, d), *weights)


MLP_HALO = 16


def _mixer_out_mlp_kernel(final_norm, x_ref, xh_ref, o_ref, oh_ref, wo_ref, g_ref, fg_ref, wa_ref, wg_ref,
                          cwa_ref, cwg_ref, cba_ref, cbg_ref, wd_ref, y_ref, x1_ref, h_ref, acc_ref, ua_ref, ug_ref):
    i, j = pl.program_id(0), pl.program_id(1)
    tm = x_ref.shape[0]

    @pl.when(j == 0)
    def _():
        wo = wo_ref[...]
        x1_ref[0:MLP_HALO, :] = xh_ref[...] + jnp.dot(oh_ref[...], wo, preferred_element_type=F32)
        x1_ref[MLP_HALO:, :] = x_ref[...] + jnp.dot(o_ref[...], wo, preferred_element_type=F32)
        halo = _rms(x1_ref[0:MLP_HALO, :], g_ref[...])
        h_ref[0:MLP_HALO, :] = jnp.where(i > 0, halo, 0.0).astype(BF16)
        h_ref[MLP_HALO:, :] = _rms(x1_ref[MLP_HALO:, :], g_ref[...]).astype(BF16)
        acc_ref[...] = jnp.zeros_like(acc_ref)

    h = h_ref[...]

    def conv(w_ref, cw_ref, cb_ref, u_ref):
        u_ref[...] = jnp.dot(h, w_ref[...], preferred_element_type=F32)
        cw = cw_ref[...]
        return (cw[0:1] * u_ref[MLP_HALO - 2:MLP_HALO - 2 + tm, :] + cw[1:2] * u_ref[MLP_HALO - 1:MLP_HALO - 1 + tm, :]
                + cw[2:3] * u_ref[MLP_HALO:MLP_HALO + tm, :] + cb_ref[...])

    a = conv(wa_ref, cwa_ref, cba_ref, ua_ref)
    gt = conv(wg_ref, cwg_ref, cbg_ref, ug_ref)
    act = (jax.nn.silu(gt) * a).astype(BF16)
    acc_ref[...] += jnp.dot(act, wd_ref[...], preferred_element_type=F32)

    @pl.when(j == pl.num_programs(1) - 1)
    def _():
        y = x1_ref[MLP_HALO:, :] + acc_ref[...]
        y_ref[...] = _rms(y, fg_ref[...]) if final_norm else y


def _mixer_out_mlp(x, o, w_out, g, w_up, conv_w, conv_b, w_down, final_g=None):
    t, d = x.shape
    f = w_down.shape[0]
    q = o.shape[1]
    tm = _pick_tile(t, (512, 256, 128))
    tf = _pick_tile(f, (1408, 256, 128))
    nf = f // tf
    hb = tm // MLP_HALO
    halo_map = lambda i, j: (jnp.maximum(i * hb - 1, 0), 0)
    fg = g if final_g is None else final_g
    return pl.pallas_call(
        functools.partial(_mixer_out_mlp_kernel, final_g is not None),
        grid=(t // tm, nf),
        in_specs=[pl.BlockSpec((tm, d), lambda i, j: (i, 0)),
                  pl.BlockSpec((MLP_HALO, d), halo_map),
                  pl.BlockSpec((tm, q), lambda i, j: (i, 0)),
                  pl.BlockSpec((MLP_HALO, q), halo_map),
                  pl.BlockSpec((q, d), lambda i, j: (0, 0)),
                  pl.BlockSpec((1, d), lambda i, j: (0, 0)),
                  pl.BlockSpec((1, d), lambda i, j: (0, 0)),
                  pl.BlockSpec((d, tf), lambda i, j: (0, j)),
                  pl.BlockSpec((d, tf), lambda i, j: (0, nf + j)),
                  pl.BlockSpec((3, tf), lambda i, j: (0, j)),
                  pl.BlockSpec((3, tf), lambda i, j: (0, nf + j)),
                  pl.BlockSpec((1, tf), lambda i, j: (0, j)),
                  pl.BlockSpec((1, tf), lambda i, j: (0, nf + j)),
                  pl.BlockSpec((tf, d), lambda i, j: (j, 0))],
        out_specs=pl.BlockSpec((tm, d), lambda i, j: (i, 0)),
        out_shape=jax.ShapeDtypeStruct((t, d), F32),
        scratch_shapes=[pltpu.VMEM((tm + MLP_HALO, d), F32), pltpu.VMEM((tm + MLP_HALO, d), BF16),
                        pltpu.VMEM((tm, d), F32),
                        pltpu.VMEM((tm + MLP_HALO, tf), F32), pltpu.VMEM((tm + MLP_HALO, tf), F32)],
        compiler_params=_cparams(("parallel", "arbitrary")),
        name="mixer_out_mlp",
    )(x, x, o, o, w_out, g.reshape(1, d), fg.reshape(1, d), w_up, w_up, conv_w, conv_w, conv_b, conv_b, w_down)


def _swap_halves(q_pair):
    return pltpu.roll(q_pair.astype(F32), HEAD_DIM, 1).astype(BF16)


def _lane_half(shape):
    return lax.broadcasted_iota(jnp.int32, shape, 1) // HEAD_DIM


def _rows_to_heads(o_t, n_heads, tq):
    stacked = jnp.concatenate([o_t[:, r * tq:(r + 1) * tq] for r in range(n_heads)], axis=0)
    return stacked.T


def _swa_kernel(q_ref, kp_ref, kc_ref, vp_ref, vc_ref, bias_ref, sink_ref, o_ref):
    tq = q_ref.shape[0]
    k_ext = jnp.concatenate([kp_ref[...], kc_ref[...]], axis=0)
    v_ext = jnp.concatenate([vp_ref[0], vc_ref[0]], axis=1)
    half = _lane_half((tq, LANES))
    outs = []
    for g in range(SWA_KV_HEADS):
        parts = []
        for r in range(SWA_GROUP):
            hd = g * SWA_GROUP + r
            q_pair = q_ref[:, (hd // 2) * LANES:(hd // 2 + 1) * LANES]
            src = q_pair if hd % 2 == g else _swap_halves(q_pair)
            parts.append(jnp.where(half == g, src, jnp.zeros_like(src)))
        qp = jnp.concatenate(parts, axis=0)
        s = lax.dot_general(k_ext, qp, (((1,), (1,)), ((), ())), preferred_element_type=F32)
        s = s * (HEAD_DIM ** -0.5) + bias_ref[0, g]
        sink = sink_ref[g]
        m = jnp.maximum(jnp.max(s, axis=0, keepdims=True), sink)
        e = jnp.exp(s - m)
        l = jnp.sum(e, axis=0, keepdims=True) + jnp.exp(sink - m)
        o_t = jnp.dot(v_ext[g * HEAD_DIM:(g + 1) * HEAD_DIM], e.astype(BF16), preferred_element_type=F32)
        outs.append(_rows_to_heads(o_t / l, SWA_GROUP, tq))
    o_ref[...] = jnp.concatenate(outs, axis=1).astype(o_ref.dtype)


def _swa_bias():
    tq = SWA_TQ
    slopes = _alibi_slopes().reshape(SWA_KV_HEADS, SWA_GROUP)
    x = np.arange(2 * tq)[:, None] - tq
    q = np.arange(tq)[None, :]
    dist = q - x
    valid = (dist >= 0) & (dist < SWA_WINDOW)
    out = np.empty((2, SWA_KV_HEADS, 2 * tq, SWA_GROUP * tq), np.float32)
    for var in range(2):
        v = valid & ((x >= 0) | (var == 1))
        for g in range(SWA_KV_HEADS):
            for r in range(SWA_GROUP):
                out[var, g, :, r * tq:(r + 1) * tq] = np.where(v, -slopes[g, r] * dist, MASKED)
    return out


def _swa_attention(qkv, v_t, sinks):
    t = qkv.shape[0]
    tq = SWA_TQ
    kcol = Q_DIM // LANES
    bias = jnp.asarray(_swa_bias())
    sink_rows = jnp.repeat(sinks.astype(F32).reshape(SWA_KV_HEADS, 1, SWA_GROUP), tq, axis=2)
    prev = lambda i: jnp.maximum(i - 1, 0)
    return pl.pallas_call(
        _swa_kernel,
        grid=(t // tq,),
        in_specs=[pl.BlockSpec((tq, Q_DIM), lambda i: (i, 0)),
                  pl.BlockSpec((tq, LANES), lambda i: (prev(i), kcol)),
                  pl.BlockSpec((tq, LANES), lambda i: (i, kcol)),
                  pl.BlockSpec((1, LANES, tq), lambda i: (prev(i), 0, 0)),
                  pl.BlockSpec((1, LANES, tq), lambda i: (i, 0, 0)),
                  pl.BlockSpec((1,) + bias.shape[1:], lambda i: (jnp.minimum(i, 1), 0, 0, 0)),
                  pl.BlockSpec(sink_rows.shape, lambda i: (0, 0, 0))],
        out_specs=pl.BlockSpec((tq, Q_DIM), lambda i: (i, 0)),
        out_shape=jax.ShapeDtypeStruct((t, Q_DIM), BF16),
        compiler_params=_cparams(("parallel",)),
        name="swa_attention",
    )(qkv, qkv, qkv, v_t, v_t, bias, sink_rows)


def _swa_heads(x, norm_g, w_in, sinks):
    w = w_in.astype(BF16)
    n_qk = Q_DIM + SWA_KV_HEADS * HEAD_DIM
    qk, v_t = _norm_proj(x, norm_g, [w[:, :n_qk], w[:, n_qk:]], [BF16, BF16], chunks=[None, SWA_TQ])
    return _swa_attention(qk, v_t, sinks)


FOX_PARTS = 3


def _split_bf16(v):
    parts, rest = [], v
    for _ in range(FOX_PARTS):
        p = rest.astype(BF16)
        parts.append(p)
        rest = rest - p.astype(F32)
    return parts


def _fox_prep_kernel(f_ref, b_ref, aug_ref, cref_ref, carry_ref):
    tm = f_ref.shape[0]

    @pl.when(pl.program_id(0) == 0)
    def _():
        carry_ref[...] = jnp.zeros_like(carry_ref)

    logf = jax.nn.log_sigmoid(f_ref[...] + b_ref[...])
    row = lax.broadcasted_iota(jnp.int32, (tm, tm), 0)
    col = lax.broadcasted_iota(jnp.int32, (tm, tm), 1)
    tri = jnp.where(row >= col, 1.0, 0.0).astype(BF16)
    local = sum(jnp.dot(tri, p, preferred_element_type=F32) for p in _split_bf16(logf))
    cum = local + carry_ref[...]
    carry_ref[...] = cum[tm - 1:tm, :]
    first = cum[0:1, :]
    cref_ref[0] = jnp.broadcast_to(first, cref_ref.shape[1:])
    hi, mid, lo = _split_bf16(first - cum)
    lane = lax.broadcasted_iota(jnp.int32, (tm, LANES), 1)
    zero = jnp.zeros_like(hi)
    aug_ref[...] = jnp.where(lane < N_HEADS, hi, jnp.where(lane < 2 * N_HEADS, mid,
                                                            jnp.where(lane < 3 * N_HEADS, lo, zero)))


def _fox_prep(f_logit3, f_bias3):
    t = f_logit3.shape[0]
    nt = t // FOX_T
    return pl.pallas_call(
        _fox_prep_kernel,
        grid=(nt,),
        in_specs=[pl.BlockSpec((FOX_T, LANES), lambda i: (i, 0)), pl.BlockSpec((1, LANES), lambda i: (0, 0))],
        out_specs=[pl.BlockSpec((FOX_T, LANES), lambda i: (i, 0)), pl.BlockSpec((1, 8, LANES), lambda i: (i, 0, 0))],
        out_shape=[jax.ShapeDtypeStruct((t, LANES), BF16), jax.ShapeDtypeStruct((nt, 8, LANES), F32)],
        scratch_shapes=[pltpu.VMEM((1, LANES), F32)],
        compiler_params=_cparams(("arbitrary",)),
        name="fox_prep",
    )(f_logit3, f_bias3)


def _fox_kernel(cref_ref, q_ref, k_ref, aug_ref, vt_ref, o_ref, m_ref, l_ref, acc_ref, qt_ref, s_ref):
    pair, qi = pl.program_id(0), pl.program_id(1)
    tq = q_ref.shape[0]
    tk = FOX_T
    lane = lax.broadcasted_iota(jnp.int32, (tq, LANES), 1)
    q2 = q_ref[...] * jnp.asarray(HEAD_DIM ** -0.5, BF16)
    for hh in range(2):
        head = 2 * pair + hh
        pick = (lane % N_HEADS == head) & (lane < FOX_PARTS * N_HEADS)
        q_ext = jnp.concatenate([jnp.where(lane // HEAD_DIM == hh, q2, jnp.zeros_like(q2)).astype(F32),
                                 jnp.where(pick, 1.0, 0.0)], axis=1)
        qt_ref[hh] = q_ext.T.astype(BF16)
    m_ref[...] = jnp.full_like(m_ref, M_INIT)
    l_ref[...] = jnp.zeros_like(l_ref)
    acc_ref[...] = jnp.zeros_like(acc_ref)
    ratio = tq // tk

    def scores(kj, hh):
        start = pl.multiple_of(kj * tk, tk)
        k_ext = jnp.concatenate([k_ref[pl.ds(start, tk), :], aug_ref[pl.ds(start, tk), :]], axis=1)
        s_ref[hh] = jnp.dot(k_ext, qt_ref[hh], preferred_element_type=F32)

    def consume(kj, hh, key_offset):
        head = 2 * pair + hh
        s = s_ref[hh]
        if key_offset is not None:
            key = lax.broadcasted_iota(jnp.int32, (tk, tq), 0) + key_offset
            qry = lax.broadcasted_iota(jnp.int32, (tk, tq), 1)
            s = jnp.where(key <= qry, s, MASKED)
        off = cref_ref[qi * ratio, head] - cref_ref[kj, head]
        m_old = m_ref[hh]
        m_new = jnp.maximum(m_old, jnp.max(s, axis=0, keepdims=True) + off)
        e = jnp.exp(s - (m_new - off))
        alpha = jnp.exp(m_old - m_new)
        m_ref[hh] = m_new
        l_ref[hh] = alpha * l_ref[hh] + jnp.sum(e, axis=0, keepdims=True)
        v_t = vt_ref[kj, hh * HEAD_DIM:(hh + 1) * HEAD_DIM, :]
        acc_ref[hh] = alpha * acc_ref[hh] + jnp.dot(v_t, e.astype(BF16), preferred_element_type=F32)

    n_full = qi * ratio
    scores(0, 0)

    def body(j, carry):
        for d in range(ratio):
            kj = j * ratio + d
            scores(kj, 1)
            consume(kj, 0, None)
            scores(kj + 1, 0)
            consume(kj, 1, None)
        return carry

    lax.fori_loop(0, qi, body, 0)
    for d in range(ratio):
        kj = n_full + d
        scores(kj, 1)
        consume(kj, 0, d * tk)
        if d + 1 < ratio:
            scores(kj + 1, 0)
        consume(kj, 1, d * tk)
    o_t = jnp.concatenate([acc_ref[hh] / l_ref[hh] for hh in range(2)], axis=0)
    o_ref[...] = o_t.T.astype(o_ref.dtype)


def _fox_attention(qkv, aug, v_t3, cref):
    t = qkv.shape[0]
    tq = FOX_TQ
    nk = t // FOX_T
    kcol = Q_DIM // LANES
    return pl.pallas_call(
        _fox_kernel,
        grid=(N_HEADS // 2, t // tq),
        in_specs=[pl.BlockSpec(memory_space=pltpu.SMEM),
                  pl.BlockSpec((tq, LANES), lambda p, i: (i, p)),
                  pl.BlockSpec((t, LANES), lambda p, i: (0, kcol + p)),
                  pl.BlockSpec((t, LANES), lambda p, i: (0, 0)),
                  pl.BlockSpec((nk, LANES, FOX_T), lambda p, i: (0, p, 0))],
        out_specs=pl.BlockSpec((tq, LANES), lambda p, i: (i, p)),
        out_shape=jax.ShapeDtypeStruct((t, Q_DIM), BF16),
        scratch_shapes=[pltpu.VMEM((2, 1, tq), F32), pltpu.VMEM((2, 1, tq), F32),
                        pltpu.VMEM((2, HEAD_DIM, tq), F32), pltpu.VMEM((2, 2 * LANES, tq), BF16), pltpu.VMEM((2, FOX_T, tq), F32)],
        compiler_params=_cparams(("parallel", "arbitrary")),
        name="fox_attention",
    )(cref, qkv, qkv, aug, v_t3)


def _fox_heads(x, norm_g, w_in, f_bias):
    t = x.shape[0]
    w_f = w_in[:, 3 * Q_DIM:]
    pad = jnp.zeros((D_MODEL, LANES - FOX_PARTS * N_HEADS), w_f.dtype)
    w_f3 = jnp.concatenate([w_f] * FOX_PARTS + [pad], axis=1).astype(BF16)
    b3 = jnp.concatenate([f_bias.astype(F32)] * FOX_PARTS + [jnp.zeros((LANES - FOX_PARTS * N_HEADS,), F32)])
    w = w_in.astype(BF16)
    qk, v_t3, f_logit3 = _norm_proj(x, norm_g, [w[:, :2 * Q_DIM], w[:, 2 * Q_DIM:3 * Q_DIM], w_f3], [BF16, BF16, F32],
                                    chunks=[None, FOX_T, None])
    aug, cref = _fox_prep(f_logit3, b3.reshape(1, LANES))
    return _fox_attention(qk, aug, v_t3, cref[:, 0, :N_HEADS])


def _nsa_compress_kernel(a_ref, pos_ref, w1_ref, w2_ref, o_ref):
    tc = a_ref.shape[2]
    half = CMP_STRIDE * HEAD_DIM
    pos = pos_ref[0]
    out = jnp.zeros(o_ref.shape[1:], F32)
    for gg in range(2):
        a = a_ref[0, gg]
        top = jnp.dot((a + pos[:, :half]).astype(BF16), w1_ref[0, :half], preferred_element_type=F32)
        bot = jnp.dot((a + pos[:, half:]).astype(BF16), w1_ref[0, half:], preferred_element_type=F32)
        hid = top + pltpu.roll(bot, tc - 1, 0)
        out = out + jnp.dot(jax.nn.gelu(hid).astype(BF16), w2_ref[0, gg], preferred_element_type=F32)
    o_ref[0] = out


def _nsa_compress(a, pos, w1, w2):
    _, g, tc, width = a.shape
    return pl.pallas_call(
        _nsa_compress_kernel,
        grid=(2, g // 2),
        in_specs=[pl.BlockSpec((1, 2, tc, width), lambda kv, gp: (kv, gp, 0, 0)),
                  pl.BlockSpec((1, 1, 2 * width), lambda kv, gp: (kv, 0, 0)),
                  pl.BlockSpec((1, 2 * width, CMP_HIDDEN), lambda kv, gp: (kv, 0, 0)),
                  pl.BlockSpec((1, 2, CMP_HIDDEN, LANES), lambda kv, gp: (kv, 0, 0, 0))],
        out_specs=pl.BlockSpec((1, tc, LANES), lambda kv, gp: (kv, 0, gp)),
        out_shape=jax.ShapeDtypeStruct((2, tc, g * HEAD_DIM), F32),
        compiler_params=_cparams(("parallel", "parallel")),
        name="nsa_compress",
    )(a, pos, w1, w2)


NSA_WIN_TILES = NSA_WINDOW // NSA_TQ + 1


def _nsa_kernel(slopes_ref, q_ref, ks_ref, *refs):
    kw_refs, refs = refs[:NSA_WIN_TILES], refs[NSA_WIN_TILES:]
    vst_ref, refs = refs[0], refs[1:]
    vw_refs, refs = refs[:NSA_WIN_TILES], refs[NSA_WIN_TILES:]
    (kcmp_ref, vov_ref, gates_ref, ksel_ext_ref, kwin_ext_ref, kcmp_ext_ref, qext_ref,
     o_ref, selt_ref, m_ref, l_ref, acc_ref, qt_ref, s_ref, scmp_ref, cm_ref, cl_ref, oc_ref,
     active_ref, list_ref) = refs
    g, qi = pl.program_id(0), pl.program_id(1)
    tq = q_ref.shape[0]
    tc = kcmp_ref.shape[0]
    n_sel = selt_ref.shape[0]
    rr = NSA_GROUP
    t0 = qi * tq
    hg = g % 2
    slope = [slopes_ref[g * rr + r] for r in range(rr)]

    half = _lane_half((tq, LANES))
    parts = []
    for r in range(rr):
        q_pair = q_ref[:, (r // 2) * LANES:(r // 2 + 1) * LANES]
        src = jnp.where(hg == r % 2, q_pair, _swap_halves(q_pair))
        parts.append(jnp.where(half == hg, src, jnp.zeros_like(src)) * jnp.asarray(HEAD_DIM ** -0.5, BF16))
    qp = jnp.concatenate(parts, axis=0)
    qt_ref[0:LANES, :] = qp.astype(F32).T.astype(BF16)

    @pl.when(qi == 0)
    def _():
        qt_ref[LANES:, :] = qext_ref[0]

    def lanes_of(s, r):
        return s[:, r * tq:(r + 1) * tq]

    chunk_span = NSA_CMP_CHUNK * CMP_STRIDE
    last_chunk = ((t0 + tq - CMP_BLOCK) // CMP_STRIDE) // NSA_CMP_CHUNK
    nl16_q = (CMP_STRIDE * lax.broadcasted_iota(jnp.int32, (NSA_CMP_CHUNK, tq), 0)
              - lax.broadcasted_iota(jnp.int32, (NSA_CMP_CHUNK, tq), 1)).astype(F32)
    cm_ref[...] = jnp.full_like(cm_ref, M_INIT)

    def cmp_rows(ch):
        return pl.ds(pl.multiple_of(ch * NSA_CMP_CHUNK, NSA_CMP_CHUNK), NSA_CMP_CHUNK)

    def cmp_offset(ch, r):
        return slope[r] * (ch * chunk_span).astype(F32)

    def cmp_scores(ch, masked):
        k_full = jnp.concatenate([kcmp_ref[cmp_rows(ch), :], kcmp_ext_ref[...]], axis=1)
        s = jnp.dot(k_full, qt_ref[...], preferred_element_type=F32)
        if masked:
            visible = nl16_q <= (t0 - (CMP_BLOCK - 1) - ch * chunk_span).astype(F32)
            s = jnp.concatenate([jnp.where(visible, lanes_of(s, r), MASKED) for r in range(rr)], axis=1)
        scmp_ref[cmp_rows(ch), :] = s
        for r in range(rr):
            cm_ref[r] = jnp.maximum(cm_ref[r], jnp.max(lanes_of(s, r), axis=0, keepdims=True) + cmp_offset(ch, r))

    def cmp_plain(ch, carry):
        cmp_scores(ch, False)
        return carry

    lax.fori_loop(0, jnp.maximum(last_chunk - 1, 0), cmp_plain, 0)

    @pl.when(last_chunk >= 1)
    def _():
        cmp_scores(last_chunk - 1, True)

    cmp_scores(last_chunk, True)

    cl_ref[...] = jnp.zeros_like(cl_ref)
    oc_ref[...] = jnp.zeros_like(oc_ref)

    def cmp_exp(ch, carry):
        s = scmp_ref[cmp_rows(ch), :]
        e_all = []
        for r in range(rr):
            e = jnp.exp(lanes_of(s, r) - (cm_ref[r] - cmp_offset(ch, r)))
            cl_ref[r] = cl_ref[r] + jnp.sum(e, axis=0, keepdims=True)
            e_all.append(e.astype(BF16))
        oc_ref[...] += jnp.dot(vov_ref[ch, 0], jnp.concatenate(e_all, axis=1), preferred_element_type=F32)
        return carry

    lax.fori_loop(0, last_chunk + 1, cmp_exp, 0)
    inv_l = [jnp.where(cl_ref[r] > 0.0, 1.0 / cl_ref[r], 0.0) for r in range(rr)]
    o_cmp = [oc_ref[0:HEAD_DIM, r * tq:(r + 1) * tq] * inv_l[r] for r in range(rr)]

    imp = functools.reduce(jnp.add, [oc_ref[HEAD_DIM:, r * tq:(r + 1) * tq] * inv_l[r] for r in range(rr)])
    blk = lax.broadcasted_iota(jnp.int32, (n_sel, tq), 0)
    cur = (t0 + lax.broadcasted_iota(jnp.int32, (n_sel, tq), 1)) // SEL_BLOCK
    blk_f = blk.astype(F32)
    forced = jnp.where(blk == 0, 1.0, jnp.where(blk == cur, 1.0, jnp.where(blk == cur - 1, 1.0, 0.0)))
    score = jnp.where(forced > 0.0, TAKEN, jnp.where(blk <= cur, imp, -1.0))

    selt_ref[...] = score
    row_step = min(n_sel, NSA_TOPK_ROW_STEP)
    variant = ((t0 + tq - 1) // SEL_BLOCK) // row_step

    for v in range(n_sel // row_step):
        rows = row_step * (v + 1)

        @pl.when(variant == v)
        def _(rows=rows):
            ids = lax.broadcasted_iota(jnp.int32, (rows, tq), 0).astype(F32)

            def pick(_, sc):
                best = jnp.max(sc, axis=0, keepdims=True)
                first = jnp.min(jnp.where(sc == best, ids, float(n_sel)), axis=0, keepdims=True)
                return jnp.where(ids == first, TAKEN, sc)

            selt_ref[0:rows, :] = lax.fori_loop(0, N_SELECT - 3, pick, selt_ref[0:rows, :])

    selt_ref[...] = jnp.where(blk <= cur, jnp.where(selt_ref[...] == TAKEN, 1.0, 0.0), 0.0)

    m_ref[...] = jnp.full_like(m_ref, M_INIT)
    l_ref[...] = jnp.zeros_like(l_ref)
    acc_ref[...] = jnp.zeros_like(acc_ref)
    blocks_per_chunk = NSA_TK // SEL_BLOCK

    for c in range(n_sel // blocks_per_chunk):
        active_ref[c] = jnp.max(selt_ref[c * blocks_per_chunk:(c + 1) * blocks_per_chunk, :])

    def compact(c, n):
        list_ref[n] = c
        return n + (active_ref[c] > 0.0).astype(jnp.int32)

    n_active = lax.fori_loop(0, t0 // NSA_TK + 1, compact, 0)

    def sel_scores(c, buf):
        start = pl.multiple_of(c * NSA_TK, NSA_TK)
        rows = selt_ref[pl.ds(pl.multiple_of(c * blocks_per_chunk, blocks_per_chunk), blocks_per_chunk), :]
        mask_rows = jnp.concatenate([jnp.where(rows > 0.5, 0.0, MASKED), jnp.zeros_like(rows)], axis=0)
        first = LANES + buf * NSA_EXT_MASK_ROWS
        qt_ref[first:first + NSA_EXT_MASK_ROWS, :] = jnp.concatenate([mask_rows] * rr, axis=1).astype(BF16)
        k_full = jnp.concatenate([ks_ref[pl.ds(start, NSA_TK), :], ksel_ext_ref[buf]], axis=1)
        s_ref[buf] = jnp.dot(k_full, qt_ref[...], preferred_element_type=F32)

    def sel_consume(c, buf, diagonal):
        shift = (c * NSA_TK - t0).astype(F32)
        if diagonal:
            key = lax.broadcasted_iota(jnp.int32, (NSA_TK, tq), 0) + c * NSA_TK
            visible = key <= lax.broadcasted_iota(jnp.int32, (NSA_TK, tq), 1) + t0
        e_all, alpha_all = [], []
        for r in range(rr):
            off = slope[r] * shift
            s = s_ref[buf][:, r * tq:(r + 1) * tq]
            if diagonal:
                s = jnp.where(visible, s, MASKED)
            m_old = m_ref[r]
            m_new = jnp.maximum(m_old, jnp.max(s, axis=0, keepdims=True) + off)
            e = jnp.exp(s - (m_new - off))
            alpha = jnp.exp(m_old - m_new)
            m_ref[r] = m_new
            l_ref[r] = alpha * l_ref[r] + jnp.sum(e, axis=0, keepdims=True)
            e_all.append(e.astype(BF16))
            alpha_all.append(alpha)
        pv = jnp.dot(vst_ref[c], jnp.concatenate(e_all, axis=1), preferred_element_type=F32)
        acc_ref[...] = jnp.concatenate(alpha_all, axis=1) * acc_ref[...] + pv

    n_before = n_active - 1
    sel_scores(list_ref[0], 0)

    def sel_pair(j, carry):
        c0, c1, c2 = list_ref[2 * j], list_ref[2 * j + 1], list_ref[2 * j + 2]
        sel_scores(c1, 1)
        sel_consume(c0, 0, False)
        sel_scores(c2, 0)
        sel_consume(c1, 1, False)
        return carry

    lax.fori_loop(0, n_before // 2, sel_pair, 0)

    @pl.when(n_before % 2 == 1)
    def _():
        c0, c_last = list_ref[n_before - 1], list_ref[n_before]
        sel_scores(c_last, 1)
        sel_consume(c0, 0, False)
        sel_consume(c_last, 1, True)

    @pl.when(n_before % 2 == 0)
    def _():
        sel_consume(list_ref[n_before], 0, True)

    xw = lax.broadcasted_iota(jnp.int32, (tq, tq), 0)
    qw = lax.broadcasted_iota(jnp.int32, (tq, tq), 1)
    s_win, off_win = [], []
    for idx in range(NSA_WIN_TILES):
        back = NSA_WIN_TILES - 1 - idx
        k_full = jnp.concatenate([kw_refs[idx][...], kwin_ext_ref[...]], axis=1)
        s_t = jnp.dot(k_full, qt_ref[...], preferred_element_type=F32)
        per_head = [lanes_of(s_t, r) for r in range(rr)]
        if back == NSA_WIN_TILES - 1:
            per_head = [jnp.where(xw > qw, s, MASKED) for s in per_head]
        if back == 0:
            per_head = [jnp.where(xw <= qw, s, MASKED) for s in per_head]
        s_win.append(per_head)
        off_win.append([jnp.where(qi >= back, -slope[r] * float(back * tq), MASKED) for r in range(rr)])
    e_win, l_win = [], []
    for r in range(rr):
        m = functools.reduce(jnp.maximum, [jnp.max(s_win[idx][r], axis=0, keepdims=True) + off_win[idx][r]
                                           for idx in range(NSA_WIN_TILES)])
        e_tiles = [jnp.exp(s_win[idx][r] - (m - off_win[idx][r])) for idx in range(NSA_WIN_TILES)]
        l_win.append(functools.reduce(jnp.add, [jnp.sum(e, axis=0, keepdims=True) for e in e_tiles]))
        e_win.append(jnp.concatenate([e.astype(BF16) for e in e_tiles], axis=0))
    v_win = jnp.concatenate([vw_refs[idx][0] for idx in range(NSA_WIN_TILES)], axis=1)
    o_win_all = jnp.dot(v_win, jnp.concatenate(e_win, axis=1), preferred_element_type=F32)
    o_win = [lanes_of(o_win_all, r) / l_win[r] for r in range(rr)]

    gate = jax.nn.sigmoid(gates_ref[0])
    merged = []
    for r in range(rr):
        o_sel = lanes_of(acc_ref[...], r) / l_ref[r]
        merged.append(gate[3 * r:3 * r + 1] * o_cmp[r] + gate[3 * r + 1:3 * r + 2] * o_sel
                      + gate[3 * r + 2:3 * r + 3] * o_win[r])
    o_ref[...] = jnp.concatenate(merged, axis=0).T.astype(o_ref.dtype)


def _nsa_overlap_t(n_sel, tc):
    cmp_start = np.arange(tc - 1) * CMP_STRIDE
    sel_start = np.arange(n_sel) * SEL_BLOCK
    ov = np.clip(np.minimum(cmp_start[None, :] + CMP_BLOCK, sel_start[:, None] + SEL_BLOCK)
                 - np.maximum(cmp_start[None, :], sel_start[:, None]), 0, None) / CMP_BLOCK
    return np.concatenate([ov, np.zeros((n_sel, 1))], axis=1).astype(np.float32)


NSA_EXT_MASK_ROWS = 16
NSA_EXT_SEL = 32
NSA_EXT_WIN = 48
NSA_EXT_CMP = 64
NSA_CMP_CHUNK = 256
NSA_TOPK_ROW_STEP = 64
SLOPE_PIECES = 3


def _bf16_pieces(v):
    out, rest = [], np.asarray(v, np.float32)
    for _ in range(SLOPE_PIECES):
        p = rest.astype(BF16).astype(np.float32)
        out.append(p)
        rest = rest - p
    return out


def _nsa_extensions(tq):
    ksel = np.zeros((2, NSA_TK, LANES), np.float32)
    x = np.arange(NSA_TK)
    for buf in range(2):
        ksel[buf, x, buf * NSA_EXT_MASK_ROWS + x // SEL_BLOCK] = 1.0
    for j in range(SLOPE_PIECES):
        ksel[:, :, NSA_EXT_SEL + j] = 256 * (x // 256)
        ksel[:, :, NSA_EXT_SEL + SLOPE_PIECES + j] = x % 256
    kwin = np.zeros((tq, LANES), np.float32)
    for j in range(SLOPE_PIECES):
        kwin[:, NSA_EXT_WIN + j] = np.arange(tq)
    assert tq <= 256, "a window key offset must be exact in bf16"
    kcmp = np.zeros((NSA_CMP_CHUNK, LANES), np.float32)
    for j in range(SLOPE_PIECES):
        kcmp[:, NSA_EXT_CMP + j] = np.arange(NSA_CMP_CHUNK)
    pieces = _bf16_pieces(_alibi_slopes())
    qext = np.zeros((NSA_KV_HEADS, LANES, NSA_GROUP * tq), np.float32)
    for g in range(NSA_KV_HEADS):
        for r in range(NSA_GROUP):
            cols = slice(r * tq, (r + 1) * tq)
            for j in range(SLOPE_PIECES):
                p = pieces[j][g * NSA_GROUP + r]
                qext[g, NSA_EXT_SEL + j, cols] = p
                qext[g, NSA_EXT_SEL + SLOPE_PIECES + j, cols] = p
                qext[g, NSA_EXT_WIN + j, cols] = p
                qext[g, NSA_EXT_CMP + j, cols] = CMP_STRIDE * p
    return jnp.asarray(ksel, BF16), jnp.asarray(kwin, BF16), jnp.asarray(kcmp, BF16), jnp.asarray(qext, BF16)


def _nsa_attention(main, vs_t3, vw_t, k_cmp, v_cmp_t, gates_t):
    t = main.shape[0]
    tq = NSA_TQ
    tc = k_cmp.shape[0]
    n_sel = t // SEL_BLOCK
    nc = t // NSA_TK
    ks_col = Q_DIM // LANES
    kw_col = ks_col + NSA_KV_DIM // LANES
    n_ch = tc // NSA_CMP_CHUNK
    ov_t = jnp.asarray(_nsa_overlap_t(n_sel, tc).reshape(n_sel, n_ch, NSA_CMP_CHUNK).transpose(1, 0, 2), BF16)
    v_cmp_t = v_cmp_t.reshape(NSA_KV_HEADS, HEAD_DIM, n_ch, NSA_CMP_CHUNK).transpose(2, 0, 1, 3)
    vov = jnp.concatenate([v_cmp_t, jnp.broadcast_to(ov_t[:, None], (n_ch, NSA_KV_HEADS, n_sel, NSA_CMP_CHUNK))],
                          axis=2)
    slopes = jnp.asarray(_alibi_slopes())
    ksel_ext, kwin_ext, kcmp_ext, qext = _nsa_extensions(tq)

    def kw_spec(back):
        return pl.BlockSpec((tq, LANES), lambda g, i: (jnp.maximum(i - back, 0), kw_col + g // 2))

    def vw_spec(back):
        return pl.BlockSpec((1, HEAD_DIM, tq), lambda g, i: (jnp.maximum(i - back, 0), g, 0))

    backs = list(range(NSA_WIN_TILES - 1, -1, -1))
    return pl.pallas_call(
        _nsa_kernel,
        grid=(NSA_KV_HEADS, t // tq),
        in_specs=[pl.BlockSpec(memory_space=pltpu.SMEM),
                  pl.BlockSpec((tq, NSA_GROUP * HEAD_DIM), lambda g, i: (i, g)),
                  pl.BlockSpec((t, LANES), lambda g, i: (0, ks_col + g // 2))]
                 + [kw_spec(b) for b in backs]
                 + [pl.BlockSpec((nc, HEAD_DIM, NSA_TK), lambda g, i: (0, g, 0))]
                 + [vw_spec(b) for b in backs]
                 + [pl.BlockSpec((tc, LANES), lambda g, i: (0, g // 2)),
                    pl.BlockSpec((n_ch, 1, HEAD_DIM + n_sel, NSA_CMP_CHUNK), lambda g, i: (0, g, 0, 0)),
                    pl.BlockSpec((1, 16, tq), lambda g, i: (g, 0, i)),
                    pl.BlockSpec(ksel_ext.shape, lambda g, i: (0, 0, 0)),
                    pl.BlockSpec(kwin_ext.shape, lambda g, i: (0, 0)),
                    pl.BlockSpec(kcmp_ext.shape, lambda g, i: (0, 0)),
                    pl.BlockSpec((1,) + qext.shape[1:], lambda g, i: (g, 0, 0))],
        out_specs=pl.BlockSpec((tq, NSA_GROUP * HEAD_DIM), lambda g, i: (i, g)),
        out_shape=jax.ShapeDtypeStruct((t, Q_DIM), BF16),
        scratch_shapes=[pltpu.VMEM((n_sel, tq), F32), pltpu.VMEM((NSA_GROUP, 1, tq), F32),
                        pltpu.VMEM((NSA_GROUP, 1, tq), F32), pltpu.VMEM((HEAD_DIM, NSA_GROUP * tq), F32),
                        pltpu.VMEM((2 * LANES, NSA_GROUP * tq), BF16),
                        pltpu.VMEM((2, NSA_TK, NSA_GROUP * tq), F32),
                        pltpu.VMEM((tc, NSA_GROUP * tq), F32), pltpu.VMEM((NSA_GROUP, 1, tq), F32),
                        pltpu.VMEM((NSA_GROUP, 1, tq), F32), pltpu.VMEM((HEAD_DIM + n_sel, NSA_GROUP * tq), F32),
                        pltpu.SMEM((nc,), F32), pltpu.SMEM((nc,), jnp.int32)],
        compiler_params=_cparams(("parallel", "arbitrary")),
        name="nsa_attention",
    )(slopes, main, main, *([main] * NSA_WIN_TILES), vs_t3, *([vw_t] * NSA_WIN_TILES), k_cmp, vov, gates_t,
      ksel_ext, kwin_ext, kcmp_ext, qext)


def _nsa_heads(x, norm_g, w_in, cmp_pos, cmp_w1, cmp_w2):
    t = x.shape[0]
    kv = NSA_KV_DIM
    col = lambda i: w_in[:, Q_DIM + i * kv:Q_DIM + (i + 1) * kv]
    w_main = jnp.concatenate([w_in[:, :Q_DIM], col(2), col(4)], axis=1).astype(BF16)
    w_cmp = jnp.concatenate([col(0), col(1)], axis=1).astype(BF16)
    n_gate = 3 * N_HEADS
    w_gate = jnp.concatenate([w_in[:, Q_DIM + 6 * kv:], jnp.zeros((D_MODEL, LANES - n_gate), w_in.dtype)],
                             axis=1).astype(BF16)
    main, vs_t3, vw_t3, kcvc, gate_logit = _norm_proj(
        x, norm_g, [w_main, col(3).astype(BF16), col(5).astype(BF16), w_cmp, w_gate], [BF16, BF16, BF16, F32, F32],
        chunks=[None, NSA_TK, NSA_TQ, None, None])

    tc = t // CMP_STRIDE
    a = kcvc.reshape(tc, CMP_STRIDE, 2, NSA_KV_HEADS, HEAD_DIM).transpose(2, 3, 0, 1, 4)
    a = a.reshape(2, NSA_KV_HEADS, tc, CMP_STRIDE * HEAD_DIM)
    zeros = jnp.zeros_like(cmp_w2)
    w2 = jnp.stack([jnp.concatenate([cmp_w2, zeros], axis=2), jnp.concatenate([zeros, cmp_w2], axis=2)], axis=1)
    cmp = _nsa_compress(a, cmp_pos.reshape(2, 1, CMP_BLOCK * HEAD_DIM), cmp_w1.astype(BF16), w2.astype(BF16))
    k_cmp = cmp[0].astype(BF16)
    v_cmp_t = cmp[1].T.astype(BF16)

    gates_t =gate_logit[:, :n_gate].reshape(t, NSA_KV_HEADS, 3 * NSA_GROUP).transpose(1, 2, 0)
    gates_t = jnp.pad(gates_t, ((0, 0), (0, 16 - 3 * NSA_GROUP), (0, 0)))
    return _nsa_attention(main, vs_t3, vw_t3, k_cmp, v_cmp_t, gates_t)


def kernel(x, attn_norm, mlp_norm, final_norm, nsa_w_in, nsa_cmp_pos, nsa_cmp_w1, nsa_cmp_w2, nsa_w_out,
           swa_w_in, swa_sinks, swa_w_out, fox_w_in, fox_f_bias, fox_w_out,
           mlp_w_up, mlp_conv_w, mlp_conv_b, mlp_w_down):
    assert x.shape[0] == 1, "the trunk is written for batch 1"
    h = x[0]
    for i in range(DEPTH):
        kind, j = i % N_MIXERS, i // N_MIXERS
        if kind == 0:
            o = _nsa_heads(h, attn_norm[i], nsa_w_in[j], nsa_cmp_pos[j], nsa_cmp_w1[j], nsa_cmp_w2[j])
            w_out = nsa_w_out[j]
        elif kind == 1:
            o = _swa_heads(h, attn_norm[i], swa_w_in[j], swa_sinks[j])
            w_out = swa_w_out[j]
        else:
            o = _fox_heads(h, attn_norm[i], fox_w_in[j], fox_f_bias[j])
            w_out = fox_w_out[j]
        h = _mixer_out_mlp(h, o, w_out.astype(BF16), mlp_norm[i], mlp_w_up[i].astype(BF16), mlp_conv_w[i],
                           mlp_conv_b[i].reshape(1, -1), mlp_w_down[i].astype(BF16),
                           final_g=final_norm if i == DEPTH - 1 else None)
    return h[None]
```

```python
import functools

import numpy as np
import jax
import jax.numpy as jnp
from jax import lax
from jax.experimental import pallas as pl
from jax.experimental.pallas import tpu as pltpu

F32 = jnp.float32
BF16 = jnp.bfloat16

D_MODEL = 1024
DEPTH = 4
N_MIXERS = 3
HEAD_DIM = 64
N_HEADS = 16
Q_DIM = N_HEADS * HEAD_DIM
ALIBI_MAX = 8.0
NORM_EPS = 1e-6
MASKED = -2e30
M_INIT = -1e30
TAKEN = -(2.0 ** 127)

NSA_KV_HEADS = 4
NSA_GROUP = 4
NSA_KV_DIM = NSA_KV_HEADS * HEAD_DIM
CMP_BLOCK = 32
CMP_STRIDE = 16
CMP_HIDDEN = 256
SEL_BLOCK = 64
N_SELECT = 16
NSA_WINDOW = 512
NSA_TQ = 512
NSA_TK = 512

SWA_KV_HEADS = 2
SWA_GROUP = 8
SWA_WINDOW = 128
SWA_TQ = 128

FOX_TQ = 1024
FOX_T = 512

LANES = 128
VMEM_LIMIT = 56 * 1024 * 1024


def _cparams(semantics, vmem=VMEM_LIMIT):
    return pltpu.CompilerParams(dimension_semantics=semantics, vmem_limit_bytes=vmem)


def _alibi_slopes():
    return np.asarray(2.0 ** (-ALIBI_MAX * np.arange(1, N_HEADS + 1) / N_HEADS), dtype=np.float32)


def _rms(x, g):
    ms = jnp.mean(x * x, axis=-1, keepdims=True)
    return x * lax.rsqrt(ms + NORM_EPS) * g


def _norm_proj_kernel(chunks, x_ref, g_ref, *refs):
    n_out = len(chunks)
    h = _rms(x_ref[...], g_ref[...]).astype(BF16)
    for w_ref, o_ref, chunk in zip(refs[:n_out], refs[n_out:], chunks):
        res = jnp.dot(h, w_ref[...], preferred_element_type=F32)
        if chunk is None:
            o_ref[...] = res.astype(o_ref.dtype)
        else:
            for c in range(res.shape[0] // chunk):
                o_ref[c] = res[c * chunk:(c + 1) * chunk].T.astype(o_ref.dtype)


def _pick_tile(n, candidates):
    for c in candidates:
        if n % c == 0:
            return c
    raise ValueError(f"no tile for {n}")


def _norm_proj(x, g, weights, out_dtypes, chunks=None):
    t, d = x.shape
    tm = _pick_tile(t, (1024, 512, 256, 128))
    chunks = tuple(chunks) if chunks else (None,) * len(weights)
    out_specs, out_shape = [], []
    for w, dt, c in zip(weights, out_dtypes, chunks):
        n = w.shape[1]
        if c is None:
            out_specs.append(pl.BlockSpec((tm, n), lambda i: (i, 0)))
            out_shape.append(jax.ShapeDtypeStruct((t, n), dt))
        else:
            out_specs.append(pl.BlockSpec((tm // c, n, c), lambda i: (i, 0, 0)))
            out_shape.append(jax.ShapeDtypeStruct((t // c, n, c), dt))
    return pl.pallas_call(
        functools.partial(_norm_proj_kernel, chunks),
        grid=(t // tm,),
        in_specs=[pl.BlockSpec((tm, d), lambda i: (i, 0)), pl.BlockSpec((1, d), lambda i: (0, 0))]
                 + [pl.BlockSpec(w.shape, lambda i: (0, 0)) for w in weights],
        out_specs=out_specs,
        out_shape=out_shape,
        compiler_params=_cparams(("parallel",)),
        name="norm_proj",
    )(x, g.reshape(1, d), *weights)


MLP_HALO = 16


def _mixer_out_mlp_kernel(final_norm, x_ref, xh_ref, o_ref, oh_ref, wo_ref, g_ref, fg_ref, wa_ref, wg_ref,
                          cwa_ref, cwg_ref, cba_ref, cbg_ref, wd_ref, y_ref, x1_ref, h_ref, acc_ref, ua_ref, ug_ref):
    i, j = pl.program_id(0), pl.program_id(1)
    tm = x_ref.shape[0]

    @pl.when(j == 0)
    def _():
        wo = wo_ref[...]
        x1_ref[0:MLP_HALO, :] = xh_ref[...] + jnp.dot(oh_ref[...], wo, preferred_element_type=F32)
        x1_ref[MLP_HALO:, :] = x_ref[...] + jnp.dot(o_ref[...], wo, preferred_element_type=F32)
        halo = _rms(x1_ref[0:MLP_HALO, :], g_ref[...])
        h_ref[0:MLP_HALO, :] = jnp.where(i > 0, halo, 0.0).astype(BF16)
        h_ref[MLP_HALO:, :] = _rms(x1_ref[MLP_HALO:, :], g_ref[...]).astype(BF16)
        acc_ref[...] = jnp.zeros_like(acc_ref)

    h = h_ref[...]

    def conv(w_ref, cw_ref, cb_ref, u_ref):
        u_ref[...] = jnp.dot(h, w_ref[...], preferred_element_type=F32)
        cw = cw_ref[...]
        return (cw[0:1] * u_ref[MLP_HALO - 2:MLP_HALO - 2 + tm, :] + cw[1:2] * u_ref[MLP_HALO - 1:MLP_HALO - 1 + tm, :]
                + cw[2:3] * u_ref[MLP_HALO:MLP_HALO + tm, :] + cb_ref[...])

    a = conv(wa_ref, cwa_ref, cba_ref, ua_ref)
    gt = conv(wg_ref, cwg_ref, cbg_ref, ug_ref)
    act = (jax.nn.silu(gt) * a).astype(BF16)
    acc_ref[...] += jnp.dot(act, wd_ref[...], preferred_element_type=F32)

    @pl.when(j == pl.num_programs(1) - 1)
    def _():
        y = x1_ref[MLP_HALO:, :] + acc_ref[...]
        y_ref[...] = _rms(y, fg_ref[...]) if final_norm else y


def _mixer_out_mlp(x, o, w_out, g, w_up, conv_w, conv_b, w_down, final_g=None):
    t, d = x.shape
    f = w_down.shape[0]
    q = o.shape[1]
    tm = _pick_tile(t, (512, 256, 128))
    tf = _pick_tile(f, (1408, 256, 128))
    nf = f // tf
    hb = tm // MLP_HALO
    halo_map = lambda i, j: (jnp.maximum(i * hb - 1, 0), 0)
    fg = g if final_g is None else final_g
    return pl.pallas_call(
        functools.partial(_mixer_out_mlp_kernel, final_g is not None),
        grid=(t // tm, nf),
        in_specs=[pl.BlockSpec((tm, d), lambda i, j: (i, 0)),
                  pl.BlockSpec((MLP_HALO, d), halo_map),
                  pl.BlockSpec((tm, q), lambda i, j: (i, 0)),
                  pl.BlockSpec((MLP_HALO, q), halo_map),
                  pl.BlockSpec((q, d), lambda i, j: (0, 0)),
                  pl.BlockSpec((1, d), lambda i, j: (0, 0)),
                  pl.BlockSpec((1, d), lambda i, j: (0, 0)),
                  pl.BlockSpec((d, tf), lambda i, j: (0, j)),
                  pl.BlockSpec((d, tf), lambda i, j: (0, nf + j)),
                  pl.BlockSpec((3, tf), lambda i, j: (0, j)),
                  pl.BlockSpec((3, tf), lambda i, j: (0, nf + j)),
                  pl.BlockSpec((1, tf), lambda i, j: (0, j)),
                  pl.BlockSpec((1, tf), lambda i, j: (0, nf + j)),
                  pl.BlockSpec((tf, d), lambda i, j: (j, 0))],
        out_specs=pl.BlockSpec((tm, d), lambda i, j: (i, 0)),
        out_shape=jax.ShapeDtypeStruct((t, d), F32),
        scratch_shapes=[pltpu.VMEM((tm + MLP_HALO, d), F32), pltpu.VMEM((tm + MLP_HALO, d), BF16),
                        pltpu.VMEM((tm, d), F32),
                        pltpu.VMEM((tm + MLP_HALO, tf), F32), pltpu.VMEM((tm + MLP_HALO, tf), F32)],
        compiler_params=_cparams(("parallel", "arbitrary")),
        name="mixer_out_mlp",
    )(x, x, o, o, w_out, g.reshape(1, d), fg.reshape(1, d), w_up, w_up, conv_w, conv_w, conv_b, conv_b, w_down)


def _swap_halves(q_pair):
    return pltpu.roll(q_pair.astype(F32), HEAD_DIM, 1).astype(BF16)


def _lane_half(shape):
    return lax.broadcasted_iota(jnp.int32, shape, 1) // HEAD_DIM


def _rows_to_heads(o_t, n_heads, tq):
    stacked = jnp.concatenate([o_t[:, r * tq:(r + 1) * tq] for r in range(n_heads)], axis=0)
    return stacked.T


def _swa_kernel(q_ref, kp_ref, kc_ref, vp_ref, vc_ref, bias_ref, sink_ref, o_ref):
    tq = q_ref.shape[0]
    k_ext = jnp.concatenate([kp_ref[...], kc_ref[...]], axis=0)
    v_ext = jnp.concatenate([vp_ref[0], vc_ref[0]], axis=1)
    half = _lane_half((tq, LANES))
    outs = []
    for g in range(SWA_KV_HEADS):
        parts = []
        for r in range(SWA_GROUP):
            hd = g * SWA_GROUP + r
            q_pair = q_ref[:, (hd // 2) * LANES:(hd // 2 + 1) * LANES]
            src = q_pair if hd % 2 == g else _swap_halves(q_pair)
            parts.append(jnp.where(half == g, src, jnp.zeros_like(src)))
        qp = jnp.concatenate(parts, axis=0)
        s = lax.dot_general(k_ext, qp, (((1,), (1,)), ((), ())), preferred_element_type=F32)
        s = s * (HEAD_DIM ** -0.5) + bias_ref[0, g]
        sink = sink_ref[g]
        m = jnp.maximum(jnp.max(s, axis=0, keepdims=True), sink)
        e = jnp.exp(s - m)
        l = jnp.sum(e, axis=0, keepdims=True) + jnp.exp(sink - m)
        o_t = jnp.dot(v_ext[g * HEAD_DIM:(g + 1) * HEAD_DIM], e.astype(BF16), preferred_element_type=F32)
        outs.append(_rows_to_heads(o_t / l, SWA_GROUP, tq))
    o_ref[...] = jnp.concatenate(outs, axis=1).astype(o_ref.dtype)


def _swa_bias():
    tq = SWA_TQ
    slopes = _alibi_slopes().reshape(SWA_KV_HEADS, SWA_GROUP)
    x = np.arange(2 * tq)[:, None] - tq
    q = np.arange(tq)[None, :]
    dist = q - x
    valid = (dist >= 0) & (dist < SWA_WINDOW)
    out = np.empty((2, SWA_KV_HEADS, 2 * tq, SWA_GROUP * tq), np.float32)
    for var in range(2):
        v = valid & ((x >= 0) | (var == 1))
        for g in range(SWA_KV_HEADS):
            for r in range(SWA_GROUP):
                out[var, g, :, r * tq:(r + 1) * tq] = np.where(v, -slopes[g, r] * dist, MASKED)
    return out


def _swa_attention(qkv, v_t, sinks):
    t = qkv.shape[0]
    tq = SWA_TQ
    kcol = Q_DIM // LANES
    bias = jnp.asarray(_swa_bias())
    sink_rows = jnp.repeat(sinks.astype(F32).reshape(SWA_KV_HEADS, 1, SWA_GROUP), tq, axis=2)
    prev = lambda i: jnp.maximum(i - 1, 0)
    return pl.pallas_call(
        _swa_kernel,
        grid=(t // tq,),
        in_specs=[pl.BlockSpec((tq, Q_DIM), lambda i: (i, 0)),
                  pl.BlockSpec((tq, LANES), lambda i: (prev(i), kcol)),
                  pl.BlockSpec((tq, LANES), lambda i: (i, kcol)),
                  pl.BlockSpec((1, LANES, tq), lambda i: (prev(i), 0, 0)),
                  pl.BlockSpec((1, LANES, tq), lambda i: (i, 0, 0)),
                  pl.BlockSpec((1,) + bias.shape[1:], lambda i: (jnp.minimum(i, 1), 0, 0, 0)),
                  pl.BlockSpec(sink_rows.shape, lambda i: (0, 0, 0))],
        out_specs=pl.BlockSpec((tq, Q_DIM), lambda i: (i, 0)),
        out_shape=jax.ShapeDtypeStruct((t, Q_DIM), BF16),
        compiler_params=_cparams(("parallel",)),
        name="swa_attention",
    )(qkv, qkv, qkv, v_t, v_t, bias, sink_rows)


def _swa_heads(x, norm_g, w_in, sinks):
    w = w_in.astype(BF16)
    n_qk = Q_DIM + SWA_KV_HEADS * HEAD_DIM
    qk, v_t = _norm_proj(x, norm_g, [w[:, :n_qk], w[:, n_qk:]], [BF16, BF16], chunks=[None, SWA_TQ])
    return _swa_attention(qk, v_t, sinks)


FOX_PARTS = 3


def _split_bf16(v):
    parts, rest = [], v
    for _ in range(FOX_PARTS):
        p = rest.astype(BF16)
        parts.append(p)
        rest = rest - p.astype(F32)
    return parts


def _fox_prep_kernel(f_ref, b_ref, aug_ref, cref_ref, carry_ref):
    tm = f_ref.shape[0]

    @pl.when(pl.program_id(0) == 0)
    def _():
        carry_ref[...] = jnp.zeros_like(carry_ref)

    logf = jax.nn.log_sigmoid(f_ref[...] + b_ref[...])
    row = lax.broadcasted_iota(jnp.int32, (tm, tm), 0)
    col = lax.broadcasted_iota(jnp.int32, (tm, tm), 1)
    tri = jnp.where(row >= col, 1.0, 0.0).astype(BF16)
    local = sum(jnp.dot(tri, p, preferred_element_type=F32) for p in _split_bf16(logf))
    cum = local + carry_ref[...]
    carry_ref[...] = cum[tm - 1:tm, :]
    first = cum[0:1, :]
    cref_ref[0] = jnp.broadcast_to(first, cref_ref.shape[1:])
    hi, mid, lo = _split_bf16(first - cum)
    lane = lax.broadcasted_iota(jnp.int32, (tm, LANES), 1)
    zero = jnp.zeros_like(hi)
    aug_ref[...] = jnp.where(lane < N_HEADS, hi, jnp.where(lane < 2 * N_HEADS, mid,
                                                            jnp.where(lane < 3 * N_HEADS, lo, zero)))


def _fox_prep(f_logit3, f_bias3):
    t = f_logit3.shape[0]
    nt = t // FOX_T
    return pl.pallas_call(
        _fox_prep_kernel,
        grid=(nt,),
        in_specs=[pl.BlockSpec((FOX_T, LANES), lambda i: (i, 0)), pl.BlockSpec((1, LANES), lambda i: (0, 0))],
        out_specs=[pl.BlockSpec((FOX_T, LANES), lambda i: (i, 0)), pl.BlockSpec((1, 8, LANES), lambda i: (i, 0, 0))],
        out_shape=[jax.ShapeDtypeStruct((t, LANES), BF16), jax.ShapeDtypeStruct((nt, 8, LANES), F32)],
        scratch_shapes=[pltpu.VMEM((1, LANES), F32)],
        compiler_params=_cparams(("arbitrary",)),
        name="fox_prep",
    )(f_logit3, f_bias3)


def _fox_kernel(cref_ref, q_ref, k_ref, aug_ref, vt_ref, o_ref, m_ref, l_ref, acc_ref, qt_ref, s_ref):
    pair, qi = pl.program_id(0), pl.program_id(1)
    tq = q_ref.shape[0]
    tk = FOX_T
    lane = lax.broadcasted_iota(jnp.int32, (tq, LANES), 1)
    q2 = q_ref[...] * jnp.asarray(HEAD_DIM ** -0.5, BF16)
    for hh in range(2):
        head = 2 * pair + hh
        pick = (lane % N_HEADS == head) & (lane < FOX_PARTS * N_HEADS)
        q_ext = jnp.concatenate([jnp.where(lane // HEAD_DIM == hh, q2, jnp.zeros_like(q2)).astype(F32),
                                 jnp.where(pick, 1.0, 0.0)], axis=1)
        qt_ref[hh] = q_ext.T.astype(BF16)
    m_ref[...] = jnp.full_like(m_ref, M_INIT)
    l_ref[...] = jnp.zeros_like(l_ref)
    acc_ref[...] = jnp.zeros_like(acc_ref)
    ratio = tq // tk

    def scores(kj, hh):
        start = pl.multiple_of(kj * tk, tk)
        k_ext = jnp.concatenate([k_ref[pl.ds(start, tk), :], aug_ref[pl.ds(start, tk), :]], axis=1)
        s_ref[hh] = jnp.dot(k_ext, qt_ref[hh], preferred_element_type=F32)

    def consume(kj, hh, key_offset):
        head = 2 * pair + hh
        s = s_ref[hh]
        if key_offset is not None:
            key = lax.broadcasted_iota(jnp.int32, (tk, tq), 0) + key_offset
            qry = lax.broadcasted_iota(jnp.int32, (tk, tq), 1)
            s = jnp.where(key <= qry, s, MASKED)
        off = cref_ref[qi * ratio, head] - cref_ref[kj, head]
        m_old = m_ref[hh]
        m_new = jnp.maximum(m_old, jnp.max(s, axis=0, keepdims=True) + off)
        e = jnp.exp(s - (m_new - off))
        alpha = jnp.exp(m_old - m_new)
        m_ref[hh] = m_new
        l_ref[hh] = alpha * l_ref[hh] + jnp.sum(e, axis=0, keepdims=True)
        v_t = vt_ref[kj, hh * HEAD_DIM:(hh + 1) * HEAD_DIM, :]
        acc_ref[hh] = alpha * acc_ref[hh] + jnp.dot(v_t, e.astype(BF16), preferred_element_type=F32)

    n_full = qi * ratio
    scores(0, 0)

    def body(j, carry):
        for d in range(ratio):
            kj = j * ratio + d
            scores(kj, 1)
            consume(kj, 0, None)
            scores(kj + 1, 0)
            consume(kj, 1, None)
        return carry

    lax.fori_loop(0, qi, body, 0)
    for d in range(ratio):
        kj = n_full + d
        scores(kj, 1)
        consume(kj, 0, d * tk)
        if d + 1 < ratio:
            scores(kj + 1, 0)
        consume(kj, 1, d * tk)
    o_t = jnp.concatenate([acc_ref[hh] / l_ref[hh] for hh in range(2)], axis=0)
    o_ref[...] = o_t.T.astype(o_ref.dtype)


def _fox_attention(qkv, aug, v_t3, cref):
    t = qkv.shape[0]
    tq = FOX_TQ
    nk = t // FOX_T
    kcol = Q_DIM // LANES
    return pl.pallas_call(
        _fox_kernel,
        grid=(N_HEADS // 2, t // tq),
        in_specs=[pl.BlockSpec(memory_space=pltpu.SMEM),
                  pl.BlockSpec((tq, LANES), lambda p, i: (i, p)),
                  pl.BlockSpec((t, LANES), lambda p, i: (0, kcol + p)),
                  pl.BlockSpec((t, LANES), lambda p, i: (0, 0)),
                  pl.BlockSpec((nk, LANES, FOX_T), lambda p, i: (0, p, 0))],
        out_specs=pl.BlockSpec((tq, LANES), lambda p, i: (i, p)),
        out_shape=jax.ShapeDtypeStruct((t, Q_DIM), BF16),
        scratch_shapes=[pltpu.VMEM((2, 1, tq), F32), pltpu.VMEM((2, 1, tq), F32),
                        pltpu.VMEM((2, HEAD_DIM, tq), F32), pltpu.VMEM((2, 2 * LANES, tq), BF16), pltpu.VMEM((2, FOX_T, tq), F32)],
        compiler_params=_cparams(("parallel", "arbitrary")),
        name="fox_attention",
    )(cref, qkv, qkv, aug, v_t3)


def _fox_heads(x, norm_g, w_in, f_bias):
    t = x.shape[0]
    w_f = w_in[:, 3 * Q_DIM:]
    pad = jnp.zeros((D_MODEL, LANES - FOX_PARTS * N_HEADS), w_f.dtype)
    w_f3 = jnp.concatenate([w_f] * FOX_PARTS + [pad], axis=1).astype(BF16)
    b3 = jnp.concatenate([f_bias.astype(F32)] * FOX_PARTS + [jnp.zeros((LANES - FOX_PARTS * N_HEADS,), F32)])
    w = w_in.astype(BF16)
    qk, v_t3, f_logit3 = _norm_proj(x, norm_g, [w[:, :2 * Q_DIM], w[:, 2 * Q_DIM:3 * Q_DIM], w_f3], [BF16, BF16, F32],
                                    chunks=[None, FOX_T, None])
    aug, cref = _fox_prep(f_logit3, b3.reshape(1, LANES))
    return _fox_attention(qk, aug, v_t3, cref[:, 0, :N_HEADS])


def _nsa_compress_kernel(a_ref, pos_ref, w1_ref, w2_ref, o_ref):
    tc = a_ref.shape[2]
    half = CMP_STRIDE * HEAD_DIM
    pos = pos_ref[0]
    out = jnp.zeros(o_ref.shape[1:], F32)
    for gg in range(2):
        a = a_ref[0, gg]
        top = jnp.dot((a + pos[:, :half]).astype(BF16), w1_ref[0, :half], preferred_element_type=F32)
        bot = jnp.dot((a + pos[:, half:]).astype(BF16), w1_ref[0, half:], preferred_element_type=F32)
        hid = top + pltpu.roll(bot, tc - 1, 0)
        out = out + jnp.dot(jax.nn.gelu(hid).astype(BF16), w2_ref[0, gg], preferred_element_type=F32)
    o_ref[0] = out


def _nsa_compress(a, pos, w1, w2):
    _, g, tc, width = a.shape
    return pl.pallas_call(
        _nsa_compress_kernel,
        grid=(2, g // 2),
        in_specs=[pl.BlockSpec((1, 2, tc, width), lambda kv, gp: (kv, gp, 0, 0)),
                  pl.BlockSpec((1, 1, 2 * width), lambda kv, gp: (kv, 0, 0)),
                  pl.BlockSpec((1, 2 * width, CMP_HIDDEN), lambda kv, gp: (kv, 0, 0)),
                  pl.BlockSpec((1, 2, CMP_HIDDEN, LANES), lambda kv, gp: (kv, 0, 0, 0))],
        out_specs=pl.BlockSpec((1, tc, LANES), lambda kv, gp: (kv, 0, gp)),
        out_shape=jax.ShapeDtypeStruct((2, tc, g * HEAD_DIM), F32),
        compiler_params=_cparams(("parallel", "parallel")),
        name="nsa_compress",
    )(a, pos, w1, w2)


NSA_WIN_TILES = NSA_WINDOW // NSA_TQ + 1


def _nsa_kernel(slopes_ref, q_ref, ks_ref, *refs):
    kw_refs, refs = refs[:NSA_WIN_TILES], refs[NSA_WIN_TILES:]
    vst_ref, refs = refs[0], refs[1:]
    vw_refs, refs = refs[:NSA_WIN_TILES], refs[NSA_WIN_TILES:]
    (kcmp_ref, vov_ref, gates_ref, ksel_ext_ref, kwin_ext_ref, kcmp_ext_ref, qext_ref,
     o_ref, selt_ref, m_ref, l_ref, acc_ref, qt_ref, s_ref, scmp_ref, cm_ref, cl_ref, oc_ref,
     active_ref, list_ref) = refs
    g, qi = pl.program_id(0), pl.program_id(1)
    tq = q_ref.shape[0]
    tc = kcmp_ref.shape[0]
    n_sel = selt_ref.shape[0]
    rr = NSA_GROUP
    t0 = qi * tq
    hg = g % 2
    slope = [slopes_ref[g * rr + r] for r in range(rr)]

    half = _lane_half((tq, LANES))
    parts = []
    for r in range(rr):
        q_pair = q_ref[:, (r // 2) * LANES:(r // 2 + 1) * LANES]
        src = jnp.where(hg == r % 2, q_pair, _swap_halves(q_pair))
        parts.append(jnp.where(half == hg, src, jnp.zeros_like(src)) * jnp.asarray(HEAD_DIM ** -0.5, BF16))
    qp = jnp.concatenate(parts, axis=0)
    qt_ref[0:LANES, :] = qp.astype(F32).T.astype(BF16)

    @pl.when(qi == 0)
    def _():
        qt_ref[LANES:, :] = qext_ref[0]

    def lanes_of(s, r):
        return s[:, r * tq:(r + 1) * tq]

    chunk_span = NSA_CMP_CHUNK * CMP_STRIDE
    last_chunk = ((t0 + tq - CMP_BLOCK) // CMP_STRIDE) // NSA_CMP_CHUNK
    nl16_q = (CMP_STRIDE * lax.broadcasted_iota(jnp.int32, (NSA_CMP_CHUNK, tq), 0)
              - lax.broadcasted_iota(jnp.int32, (NSA_CMP_CHUNK, tq), 1)).astype(F32)
    cm_ref[...] = jnp.full_like(cm_ref, M_INIT)

    def cmp_rows(ch):
        return pl.ds(pl.multiple_of(ch * NSA_CMP_CHUNK, NSA_CMP_CHUNK), NSA_CMP_CHUNK)

    def cmp_offset(ch, r):
        return slope[r] * (ch * chunk_span).astype(F32)

    def cmp_scores(ch, masked):
        k_full = jnp.concatenate([kcmp_ref[cmp_rows(ch), :], kcmp_ext_ref[...]], axis=1)
        s = jnp.dot(k_full, qt_ref[...], preferred_element_type=F32)
        if masked:
            visible = nl16_q <= (t0 - (CMP_BLOCK - 1) - ch * chunk_span).astype(F32)
            s = jnp.concatenate([jnp.where(visible, lanes_of(s, r), MASKED) for r in range(rr)], axis=1)
        scmp_ref[cmp_rows(ch), :] = s
        for r in range(rr):
            cm_ref[r] = jnp.maximum(cm_ref[r], jnp.max(lanes_of(s, r), axis=0, keepdims=True) + cmp_offset(ch, r))

    def cmp_plain(ch, carry):
        cmp_scores(ch, False)
        return carry

    lax.fori_loop(0, jnp.maximum(last_chunk - 1, 0), cmp_plain, 0)

    @pl.when(last_chunk >= 1)
    def _():
        cmp_scores(last_chunk - 1, True)

    cmp_scores(last_chunk, True)

    cl_ref[...] = jnp.zeros_like(cl_ref)
    oc_ref[...] = jnp.zeros_like(oc_ref)

    def cmp_exp(ch, carry):
        s = scmp_ref[cmp_rows(ch), :]
        e_all = []
        for r in range(rr):
            e = jnp.exp(lanes_of(s, r) - (cm_ref[r] - cmp_offset(ch, r)))
            cl_ref[r] = cl_ref[r] + jnp.sum(e, axis=0, keepdims=True)
            e_all.append(e.astype(BF16))
        oc_ref[...] += jnp.dot(vov_ref[ch, 0], jnp.concatenate(e_all, axis=1), preferred_element_type=F32)
        return carry

    lax.fori_loop(0, last_chunk + 1, cmp_exp, 0)
    inv_l = [jnp.where(cl_ref[r] > 0.0, 1.0 / cl_ref[r], 0.0) for r in range(rr)]
    o_cmp = [oc_ref[0:HEAD_DIM, r * tq:(r + 1) * tq] * inv_l[r] for r in range(rr)]

    imp = functools.reduce(jnp.add, [oc_ref[HEAD_DIM:, r * tq:(r + 1) * tq] * inv_l[r] for r in range(rr)])
    blk = lax.broadcasted_iota(jnp.int32, (n_sel, tq), 0)
    cur = (t0 + lax.broadcasted_iota(jnp.int32, (n_sel, tq), 1)) // SEL_BLOCK
    blk_f = blk.astype(F32)
    forced = jnp.where(blk == 0, 1.0, jnp.where(blk == cur, 1.0, jnp.where(blk == cur - 1, 1.0, 0.0)))
    score = jnp.where(forced > 0.0, TAKEN, jnp.where(blk <= cur, imp, -1.0))

    selt_ref[...] = score
    row_step = min(n_sel, NSA_TOPK_ROW_STEP)
    variant = ((t0 + tq - 1) // SEL_BLOCK) // row_step

    for v in range(n_sel // row_step):
        rows = row_step * (v + 1)

        @pl.when(variant == v)
        def _(rows=rows):
            ids = lax.broadcasted_iota(jnp.int32, (rows, tq), 0).astype(F32)

            def pick(_, sc):
                best = jnp.max(sc, axis=0, keepdims=True)
                first = jnp.min(jnp.where(sc == best, ids, float(n_sel)), axis=0, keepdims=True)
                return jnp.where(ids == first, TAKEN, sc)

            selt_ref[0:rows, :] = lax.fori_loop(0, N_SELECT - 3, pick, selt_ref[0:rows, :])

    selt_ref[...] = jnp.where(blk <= cur, jnp.where(selt_ref[...] == TAKEN, 1.0, 0.0), 0.0)

    m_ref[...] = jnp.full_like(m_ref, M_INIT)
    l_ref[...] = jnp.zeros_like(l_ref)
    acc_ref[...] = jnp.zeros_like(acc_ref)
    blocks_per_chunk = NSA_TK // SEL_BLOCK

    for c in range(n_sel // blocks_per_chunk):
        active_ref[c] = jnp.max(selt_ref[c * blocks_per_chunk:(c + 1) * blocks_per_chunk, :])

    def compact(c, n):
        list_ref[n] = c
        return n + (active_ref[c] > 0.0).astype(jnp.int32)

    n_active = lax.fori_loop(0, t0 // NSA_TK + 1, compact, 0)

    def sel_scores(c, buf):
        start = pl.multiple_of(c * NSA_TK, NSA_TK)
        rows = selt_ref[pl.ds(pl.multiple_of(c * blocks_per_chunk, blocks_per_chunk), blocks_per_chunk), :]
        mask_rows = jnp.concatenate([jnp.where(rows > 0.5, 0.0, MASKED), jnp.zeros_like(rows)], axis=0)
        first = LANES + buf * NSA_EXT_MASK_ROWS
        qt_ref[first:first + NSA_EXT_MASK_ROWS, :] = jnp.concatenate([mask_rows] * rr, axis=1).astype(BF16)
        k_full = jnp.concatenate([ks_ref[pl.ds(start, NSA_TK), :], ksel_ext_ref[buf]], axis=1)
        s_ref[buf] = jnp.dot(k_full, qt_ref[...], preferred_element_type=F32)

    def sel_consume(c, buf, diagonal):
        shift = (c * NSA_TK - t0).astype(F32)
        if diagonal:
            key = lax.broadcasted_iota(jnp.int32, (NSA_TK, tq), 0) + c * NSA_TK
            visible = key <= lax.broadcasted_iota(jnp.int32, (NSA_TK, tq), 1) + t0
        e_all, alpha_all = [], []
        for r in range(rr):
            off = slope[r] * shift
            s = s_ref[buf][:, r * tq:(r + 1) * tq]
            if diagonal:
                s = jnp.where(visible, s, MASKED)
            m_old = m_ref[r]
            m_new = jnp.maximum(m_old, jnp.max(s, axis=0, keepdims=True) + off)
            e = jnp.exp(s - (m_new - off))
            alpha = jnp.exp(m_old - m_new)
            m_ref[r] = m_new
            l_ref[r] = alpha * l_ref[r] + jnp.sum(e, axis=0, keepdims=True)
            e_all.append(e.astype(BF16))
            alpha_all.append(alpha)
        pv = jnp.dot(vst_ref[c], jnp.concatenate(e_all, axis=1), preferred_element_type=F32)
        acc_ref[...] = jnp.concatenate(alpha_all, axis=1) * acc_ref[...] + pv

    n_before = n_active - 1
    sel_scores(list_ref[0], 0)

    def sel_pair(j, carry):
        c0, c1, c2 = list_ref[2 * j], list_ref[2 * j + 1], list_ref[2 * j + 2]
        sel_scores(c1, 1)
        sel_consume(c0, 0, False)
        sel_scores(c2, 0)
        sel_consume(c1, 1, False)
        return carry

    lax.fori_loop(0, n_before // 2, sel_pair, 0)

    @pl.when(n_before % 2 == 1)
    def _():
        c0, c_last = list_ref[n_before - 1], list_ref[n_before]
        sel_scores(c_last, 1)
        sel_consume(c0, 0, False)
        sel_consume(c_last, 1, True)

    @pl.when(n_before % 2 == 0)
    def _():
        sel_consume(list_ref[n_before], 0, True)

    xw = lax.broadcasted_iota(jnp.int32, (tq, tq), 0)
    qw = lax.broadcasted_iota(jnp.int32, (tq, tq), 1)
    s_win, off_win = [], []
    for idx in range(NSA_WIN_TILES):
        back = NSA_WIN_TILES - 1 - idx
        k_full = jnp.concatenate([kw_refs[idx][...], kwin_ext_ref[...]], axis=1)
        s_t = jnp.dot(k_full, qt_ref[...], preferred_element_type=F32)
        per_head = [lanes_of(s_t, r) for r in range(rr)]
        if back == NSA_WIN_TILES - 1:
            per_head = [jnp.where(xw > qw, s, MASKED) for s in per_head]
        if back == 0:
            per_head = [jnp.where(xw <= qw, s, MASKED) for s in per_head]
        s_win.append(per_head)
        off_win.append([jnp.where(qi >= back, -slope[r] * float(back * tq), MASKED) for r in range(rr)])
    e_win, l_win = [], []
    for r in range(rr):
        m = functools.reduce(jnp.maximum, [jnp.max(s_win[idx][r], axis=0, keepdims=True) + off_win[idx][r]
                                           for idx in range(NSA_WIN_TILES)])
        e_tiles = [jnp.exp(s_win[idx][r] - (m - off_win[idx][r])) for idx in range(NSA_WIN_TILES)]
        l_win.append(functools.reduce(jnp.add, [jnp.sum(e, axis=0, keepdims=True) for e in e_tiles]))
        e_win.append(jnp.concatenate([e.astype(BF16) for e in e_tiles], axis=0))
    v_win = jnp.concatenate([vw_refs[idx][0] for idx in range(NSA_WIN_TILES)], axis=1)
    o_win_all = jnp.dot(v_win, jnp.concatenate(e_win, axis=1), preferred_element_type=F32)
    o_win = [lanes_of(o_win_all, r) / l_win[r] for r in range(rr)]

    gate = jax.nn.sigmoid(gates_ref[0])
    merged = []
    for r in range(rr):
        o_sel = lanes_of(acc_ref[...], r) / l_ref[r]
        merged.append(gate[3 * r:3 * r + 1] * o_cmp[r] + gate[3 * r + 1:3 * r + 2] * o_sel
                      + gate[3 * r + 2:3 * r + 3] * o_win[r])
    o_ref[...] = jnp.concatenate(merged, axis=0).T.astype(o_ref.dtype)


def _nsa_overlap_t(n_sel, tc):
    cmp_start = np.arange(tc - 1) * CMP_STRIDE
    sel_start = np.arange(n_sel) * SEL_BLOCK
    ov = np.clip(np.minimum(cmp_start[None, :] + CMP_BLOCK, sel_start[:, None] + SEL_BLOCK)
                 - np.maximum(cmp_start[None, :], sel_start[:, None]), 0, None) / CMP_BLOCK
    return np.concatenate([ov, np.zeros((n_sel, 1))], axis=1).astype(np.float32)


NSA_EXT_MASK_ROWS = 16
NSA_EXT_SEL = 32
NSA_EXT_WIN = 48
NSA_EXT_CMP = 64
NSA_CMP_CHUNK = 256
NSA_TOPK_ROW_STEP = 64
SLOPE_PIECES = 3


def _bf16_pieces(v):
    out, rest = [], np.asarray(v, np.float32)
    for _ in range(SLOPE_PIECES):
        p = rest.astype(BF16).astype(np.float32)
        out.append(p)
        rest = rest - p
    return out


def _nsa_extensions(tq):
    ksel = np.zeros((2, NSA_TK, LANES), np.float32)
    x = np.arange(NSA_TK)
    for buf in range(2):
        ksel[buf, x, buf * NSA_EXT_MASK_ROWS + x // SEL_BLOCK] = 1.0
    for j in range(SLOPE_PIECES):
        ksel[:, :, NSA_EXT_SEL + j] = 256 * (x // 256)
        ksel[:, :, NSA_EXT_SEL + SLOPE_PIECES + j] = x % 256
    kwin = np.zeros((tq, LANES), np.float32)
    xw = np.arange(tq)
    for j in range(SLOPE_PIECES):
        kwin[:, NSA_EXT_WIN + j] = 256 * (xw // 256)
        kwin[:, NSA_EXT_WIN + SLOPE_PIECES + j] = xw % 256
    kcmp = np.zeros((NSA_CMP_CHUNK, LANES), np.float32)
    for j in range(SLOPE_PIECES):
        kcmp[:, NSA_EXT_CMP + j] = np.arange(NSA_CMP_CHUNK)
    pieces = _bf16_pieces(_alibi_slopes())
    qext = np.zeros((NSA_KV_HEADS, LANES, NSA_GROUP * tq), np.float32)
    for g in range(NSA_KV_HEADS):
        for r in range(NSA_GROUP):
            cols = slice(r * tq, (r + 1) * tq)
            for j in range(SLOPE_PIECES):
                p = pieces[j][g * NSA_GROUP + r]
                qext[g, NSA_EXT_SEL + j, cols] = p
                qext[g, NSA_EXT_SEL + SLOPE_PIECES + j, cols] = p
                qext[g, NSA_EXT_WIN + j, cols] = p
                qext[g, NSA_EXT_WIN + SLOPE_PIECES + j, cols] = p
                qext[g, NSA_EXT_CMP + j, cols] = CMP_STRIDE * p
    return jnp.asarray(ksel, BF16), jnp.asarray(kwin, BF16), jnp.asarray(kcmp, BF16), jnp.asarray(qext, BF16)


def _nsa_attention(main, vs_t3, vw_t, k_cmp, v_cmp_t, gates_t):
    t = main.shape[0]
    tq = NSA_TQ
    tc = k_cmp.shape[0]
    n_sel = t // SEL_BLOCK
    nc = t // NSA_TK
    ks_col = Q_DIM // LANES
    kw_col = ks_col + NSA_KV_DIM // LANES
    n_ch = tc // NSA_CMP_CHUNK
    ov_t = jnp.asarray(_nsa_overlap_t(n_sel, tc).reshape(n_sel, n_ch, NSA_CMP_CHUNK).transpose(1, 0, 2), BF16)
    v_cmp_t = v_cmp_t.reshape(NSA_KV_HEADS, HEAD_DIM, n_ch, NSA_CMP_CHUNK).transpose(2, 0, 1, 3)
    vov = jnp.concatenate([v_cmp_t, jnp.broadcast_to(ov_t[:, None], (n_ch, NSA_KV_HEADS, n_sel, NSA_CMP_CHUNK))],
                          axis=2)
    slopes = jnp.asarray(_alibi_slopes())
    ksel_ext, kwin_ext, kcmp_ext, qext = _nsa_extensions(tq)

    def kw_spec(back):
        return pl.BlockSpec((tq, LANES), lambda g, i: (jnp.maximum(i - back, 0), kw_col + g // 2))

    def vw_spec(back):
        return pl.BlockSpec((1, HEAD_DIM, tq), lambda g, i: (jnp.maximum(i - back, 0), g, 0))

    backs = list(range(NSA_WIN_TILES - 1, -1, -1))
    return pl.pallas_call(
        _nsa_kernel,
        grid=(NSA_KV_HEADS, t // tq),
        in_specs=[pl.BlockSpec(memory_space=pltpu.SMEM),
                  pl.BlockSpec((tq, NSA_GROUP * HEAD_DIM), lambda g, i: (i, g)),
                  pl.BlockSpec((t, LANES), lambda g, i: (0, ks_col + g // 2))]
                 + [kw_spec(b) for b in backs]
                 + [pl.BlockSpec((nc, HEAD_DIM, NSA_TK), lambda g, i: (0, g, 0))]
                 + [vw_spec(b) for b in backs]
                 + [pl.BlockSpec((tc, LANES), lambda g, i: (0, g // 2)),
                    pl.BlockSpec((n_ch, 1, HEAD_DIM + n_sel, NSA_CMP_CHUNK), lambda g, i: (0, g, 0, 0)),
                    pl.BlockSpec((1, 16, tq), lambda g, i: (g, 0, i)),
                    pl.BlockSpec(ksel_ext.shape, lambda g, i: (0, 0, 0)),
                    pl.BlockSpec(kwin_ext.shape, lambda g, i: (0, 0)),
                    pl.BlockSpec(kcmp_ext.shape, lambda g, i: (0, 0)),
                    pl.BlockSpec((1,) + qext.shape[1:], lambda g, i: (g, 0, 0))],
        out_specs=pl.BlockSpec((tq, NSA_GROUP * HEAD_DIM), lambda g, i: (i, g)),
        out_shape=jax.ShapeDtypeStruct((t, Q_DIM), BF16),
        scratch_shapes=[pltpu.VMEM((n_sel, tq), F32), pltpu.VMEM((NSA_GROUP, 1, tq), F32),
                        pltpu.VMEM((NSA_GROUP, 1, tq), F32), pltpu.VMEM((HEAD_DIM, NSA_GROUP * tq), F32),
                        pltpu.VMEM((2 * LANES, NSA_GROUP * tq), BF16),
                        pltpu.VMEM((2, NSA_TK, NSA_GROUP * tq), F32),
                        pltpu.VMEM((tc, NSA_GROUP * tq), F32), pltpu.VMEM((NSA_GROUP, 1, tq), F32),
                        pltpu.VMEM((NSA_GROUP, 1, tq), F32), pltpu.VMEM((HEAD_DIM + n_sel, NSA_GROUP * tq), F32),
                        pltpu.SMEM((nc,), F32), pltpu.SMEM((nc,), jnp.int32)],
        compiler_params=_cparams(("parallel", "arbitrary")),
        name="nsa_attention",
    )(slopes, main, main, *([main] * NSA_WIN_TILES), vs_t3, *([vw_t] * NSA_WIN_TILES), k_cmp, vov, gates_t,
      ksel_ext, kwin_ext, kcmp_ext, qext)


def _nsa_heads(x, norm_g, w_in, cmp_pos, cmp_w1, cmp_w2):
    t = x.shape[0]
    kv = NSA_KV_DIM
    col = lambda i: w_in[:, Q_DIM + i * kv:Q_DIM + (i + 1) * kv]
    w_main = jnp.concatenate([w_in[:, :Q_DIM], col(2), col(4)], axis=1).astype(BF16)
    w_cmp = jnp.concatenate([col(0), col(1)], axis=1).astype(BF16)
    n_gate = 3 * N_HEADS
    w_gate = jnp.concatenate([w_in[:, Q_DIM + 6 * kv:], jnp.zeros((D_MODEL, LANES - n_gate), w_in.dtype)],
                             axis=1).astype(BF16)
    main, vs_t3, vw_t3, kcvc, gate_logit = _norm_proj(
        x, norm_g, [w_main, col(3).astype(BF16), col(5).astype(BF16), w_cmp, w_gate], [BF16, BF16, BF16, F32, F32],
        chunks=[None, NSA_TK, NSA_TQ, None, None])

    tc = t // CMP_STRIDE
    a = kcvc.reshape(tc, CMP_STRIDE, 2, NSA_KV_HEADS, HEAD_DIM).transpose(2, 3, 0, 1, 4)
    a = a.reshape(2, NSA_KV_HEADS, tc, CMP_STRIDE * HEAD_DIM)
    zeros = jnp.zeros_like(cmp_w2)
    w2 = jnp.stack([jnp.concatenate([cmp_w2, zeros], axis=2), jnp.concatenate([zeros, cmp_w2], axis=2)], axis=1)
    cmp = _nsa_compress(a, cmp_pos.reshape(2, 1, CMP_BLOCK * HEAD_DIM), cmp_w1.astype(BF16), w2.astype(BF16))
    k_cmp = cmp[0].astype(BF16)
    v_cmp_t = cmp[1].T.astype(BF16)

    gates_t =gate_logit[:, :n_gate].reshape(t, NSA_KV_HEADS, 3 * NSA_GROUP).transpose(1, 2, 0)
    gates_t = jnp.pad(gates_t, ((0, 0), (0, 16 - 3 * NSA_GROUP), (0, 0)))
    return _nsa_attention(main, vs_t3, vw_t3, k_cmp, v_cmp_t, gates_t)


def kernel(x, attn_norm, mlp_norm, final_norm, nsa_w_in, nsa_cmp_pos, nsa_cmp_w1, nsa_cmp_w2, nsa_w_out,
           swa_w_in, swa_sinks, swa_w_out, fox_w_in, fox_f_bias, fox_w_out,
           mlp_w_up, mlp_conv_w, mlp_conv_b, mlp_w_down):
    assert x.shape[0] == 1, "the trunk is written for batch 1"
    h = x[0]
    for i in range(DEPTH):
        kind, j = i % N_MIXERS, i // N_MIXERS
        if kind == 0:
            o = _nsa_heads(h, attn_norm[i], nsa_w_in[j], nsa_cmp_pos[j], nsa_cmp_w1[j], nsa_cmp_w2[j])
            w_out = nsa_w_out[j]
        elif kind == 1:
            o = _swa_heads(h, attn_norm[i], swa_w_in[j], swa_sinks[j])
            w_out = swa_w_out[j]
        else:
            o = _fox_heads(h, attn_norm[i], fox_w_in[j], fox_f_bias[j])
            w_out = fox_w_out[j]
        h = _mixer_out_mlp(h, o, w_out.astype(BF16), mlp_norm[i], mlp_w_up[i].astype(BF16), mlp_conv_w[i],
                           mlp_conv_b[i].reshape(1, -1), mlp_w_down[i].astype(BF16),
                           final_g=final_norm if i == DEPTH - 1 else None)
    return h[None]
```

```python
import functools

import numpy as np
import jax
import jax.numpy as jnp
from jax import lax
from jax.experimental import pallas as pl
from jax.experimental.pallas import tpu as pltpu

F32 = jnp.float32
BF16 = jnp.bfloat16

D_MODEL = 1024
DEPTH = 4
N_MIXERS = 3
HEAD_DIM = 64
N_HEADS = 16
Q_DIM = N_HEADS * HEAD_DIM
ALIBI_MAX = 8.0
NORM_EPS = 1e-6
MASKED = -2e30
M_INIT = -1e30
TAKEN = -(2.0 ** 127)

NSA_KV_HEADS = 4
NSA_GROUP = 4
NSA_KV_DIM = NSA_KV_HEADS * HEAD_DIM
CMP_BLOCK = 32
CMP_STRIDE = 16
CMP_HIDDEN = 256
SEL_BLOCK = 64
N_SELECT = 16
NSA_WINDOW = 512
NSA_TQ = 512
NSA_TK = 512

SWA_KV_HEADS = 2
SWA_GROUP = 8
SWA_WINDOW = 128
SWA_TQ = 128

FOX_TQ = 1024
FOX_T = 512

LANES = 128
VMEM_LIMIT = 56 * 1024 * 1024


def _cparams(semantics, vmem=VMEM_LIMIT):
    return pltpu.CompilerParams(dimension_semantics=semantics, vmem_limit_bytes=vmem)


def _alibi_slopes():
    return np.asarray(2.0 ** (-ALIBI_MAX * np.arange(1, N_HEADS + 1) / N_HEADS), dtype=np.float32)


def _rms(x, g):
    ms = jnp.mean(x * x, axis=-1, keepdims=True)
    return x * lax.rsqrt(ms + NORM_EPS) * g


def _norm_proj_kernel(chunks, x_ref, g_ref, *refs):
    n_out = len(chunks)
    h = _rms(x_ref[...], g_ref[...]).astype(BF16)
    for w_ref, o_ref, chunk in zip(refs[:n_out], refs[n_out:], chunks):
        res = jnp.dot(h, w_ref[...], preferred_element_type=F32)
        if chunk is None:
            o_ref[...] = res.astype(o_ref.dtype)
        elif chunk == "cols":
            for c in range(o_ref.shape[0]):
                o_ref[c] = res[:, c * LANES:(c + 1) * LANES].astype(o_ref.dtype)
        else:
            for c in range(res.shape[0] // chunk):
                o_ref[c] = res[c * chunk:(c + 1) * chunk].T.astype(o_ref.dtype)


def _pick_tile(n, candidates):
    for c in candidates:
        if n % c == 0:
            return c
    raise ValueError(f"no tile for {n}")


def _norm_proj(x, g, weights, out_dtypes, chunks=None):
    t, d = x.shape
    tm = _pick_tile(t, (1024, 512, 256, 128))
    chunks = tuple(chunks) if chunks else (None,) * len(weights)
    out_specs, out_shape = [], []
    for w, dt, c in zip(weights, out_dtypes, chunks):
        n = w.shape[1]
        if c is None:
            out_specs.append(pl.BlockSpec((tm, n), lambda i: (i, 0)))
            out_shape.append(jax.ShapeDtypeStruct((t, n), dt))
        elif c == "cols":
            out_specs.append(pl.BlockSpec((n // LANES, tm, LANES), lambda i: (0, i, 0)))
            out_shape.append(jax.ShapeDtypeStruct((n // LANES, t, LANES), dt))
        else:
            out_specs.append(pl.BlockSpec((tm // c, n, c), lambda i: (i, 0, 0)))
            out_shape.append(jax.ShapeDtypeStruct((t // c, n, c), dt))
    return pl.pallas_call(
        functools.partial(_norm_proj_kernel, chunks),
        grid=(t // tm,),
        in_specs=[pl.BlockSpec((tm, d), lambda i: (i, 0)), pl.BlockSpec((1, d), lambda i: (0, 0))]
                 + [pl.BlockSpec(w.shape, lambda i: (0, 0)) for w in weights],
        out_specs=out_specs,
        out_shape=out_shape,
        compiler_params=_cparams(("parallel",)),
        name="norm_proj",
    )(x, g.reshape(1, d), *weights)


MLP_HALO = 16


def _mixer_out_mlp_kernel(final_norm, x_ref, xh_ref, o_ref, oh_ref, wo_ref, g_ref, fg_ref, wa_ref, wg_ref,
                          cwa_ref, cwg_ref, cba_ref, cbg_ref, wd_ref, y_ref, x1_ref, h_ref, acc_ref, ua_ref, ug_ref):
    i, j = pl.program_id(0), pl.program_id(1)
    tm = x_ref.shape[0]

    @pl.when(j == 0)
    def _():
        wo = wo_ref[...]
        x1_ref[0:MLP_HALO, :] = xh_ref[...] + jnp.dot(oh_ref[...], wo, preferred_element_type=F32)
        x1_ref[MLP_HALO:, :] = x_ref[...] + jnp.dot(o_ref[...], wo, preferred_element_type=F32)
        halo = _rms(x1_ref[0:MLP_HALO, :], g_ref[...])
        h_ref[0:MLP_HALO, :] = jnp.where(i > 0, halo, 0.0).astype(BF16)
        h_ref[MLP_HALO:, :] = _rms(x1_ref[MLP_HALO:, :], g_ref[...]).astype(BF16)
        acc_ref[...] = jnp.zeros_like(acc_ref)

    h = h_ref[...]

    def conv(w_ref, cw_ref, cb_ref, u_ref):
        u_ref[...] = jnp.dot(h, w_ref[...], preferred_element_type=F32)
        cw = cw_ref[...]
        return (cw[0:1] * u_ref[MLP_HALO - 2:MLP_HALO - 2 + tm, :] + cw[1:2] * u_ref[MLP_HALO - 1:MLP_HALO - 1 + tm, :]
                + cw[2:3] * u_ref[MLP_HALO:MLP_HALO + tm, :] + cb_ref[...])

    a = conv(wa_ref, cwa_ref, cba_ref, ua_ref)
    gt = conv(wg_ref, cwg_ref, cbg_ref, ug_ref)
    act = (jax.nn.silu(gt) * a).astype(BF16)
    acc_ref[...] += jnp.dot(act, wd_ref[...], preferred_element_type=F32)

    @pl.when(j == pl.num_programs(1) - 1)
    def _():
        y = x1_ref[MLP_HALO:, :] + acc_ref[...]
        y_ref[...] = _rms(y, fg_ref[...]) if final_norm else y


def _mixer_out_mlp(x, o, w_out, g, w_up, conv_w, conv_b, w_down, final_g=None):
    t, d = x.shape
    f = w_down.shape[0]
    q = o.shape[1]
    tm = _pick_tile(t, (512, 256, 128))
    tf = _pick_tile(f, (1408, 256, 128))
    nf = f // tf
    hb = tm // MLP_HALO
    halo_map = lambda i, j: (jnp.maximum(i * hb - 1, 0), 0)
    fg = g if final_g is None else final_g
    return pl.pallas_call(
        functools.partial(_mixer_out_mlp_kernel, final_g is not None),
        grid=(t // tm, nf),
        in_specs=[pl.BlockSpec((tm, d), lambda i, j: (i, 0)),
                  pl.BlockSpec((MLP_HALO, d), halo_map),
                  pl.BlockSpec((tm, q), lambda i, j: (i, 0)),
                  pl.BlockSpec((MLP_HALO, q), halo_map),
                  pl.BlockSpec((q, d), lambda i, j: (0, 0)),
                  pl.BlockSpec((1, d), lambda i, j: (0, 0)),
                  pl.BlockSpec((1, d), lambda i, j: (0, 0)),
                  pl.BlockSpec((d, tf), lambda i, j: (0, j)),
                  pl.BlockSpec((d, tf), lambda i, j: (0, nf + j)),
                  pl.BlockSpec((3, tf), lambda i, j: (0, j)),
                  pl.BlockSpec((3, tf), lambda i, j: (0, nf + j)),
                  pl.BlockSpec((1, tf), lambda i, j: (0, j)),
                  pl.BlockSpec((1, tf), lambda i, j: (0, nf + j)),
                  pl.BlockSpec((tf, d), lambda i, j: (j, 0))],
        out_specs=pl.BlockSpec((tm, d), lambda i, j: (i, 0)),
        out_shape=jax.ShapeDtypeStruct((t, d), F32),
        scratch_shapes=[pltpu.VMEM((tm + MLP_HALO, d), F32), pltpu.VMEM((tm + MLP_HALO, d), BF16),
                        pltpu.VMEM((tm, d), F32),
                        pltpu.VMEM((tm + MLP_HALO, tf), F32), pltpu.VMEM((tm + MLP_HALO, tf), F32)],
        compiler_params=_cparams(("parallel", "arbitrary")),
        name="mixer_out_mlp",
    )(x, x, o, o, w_out, g.reshape(1, d), fg.reshape(1, d), w_up, w_up, conv_w, conv_w, conv_b, conv_b, w_down)


def _swap_halves(q_pair):
    return pltpu.roll(q_pair.astype(F32), HEAD_DIM, 1).astype(BF16)


def _lane_half(shape):
    return lax.broadcasted_iota(jnp.int32, shape, 1) // HEAD_DIM


def _rows_to_heads(o_t, n_heads, tq):
    stacked = jnp.concatenate([o_t[:, r * tq:(r + 1) * tq] for r in range(n_heads)], axis=0)
    return stacked.T


def _swa_kernel(q_ref, kp_ref, kc_ref, vp_ref, vc_ref, bias_ref, sink_ref, o_ref):
    tq = q_ref.shape[0]
    k_ext = jnp.concatenate([kp_ref[...], kc_ref[...]], axis=0)
    v_ext = jnp.concatenate([vp_ref[0], vc_ref[0]], axis=1)
    half = _lane_half((tq, LANES))
    outs = []
    for g in range(SWA_KV_HEADS):
        parts = []
        for r in range(SWA_GROUP):
            hd = g * SWA_GROUP + r
            q_pair = q_ref[:, (hd // 2) * LANES:(hd // 2 + 1) * LANES]
            src = q_pair if hd % 2 == g else _swap_halves(q_pair)
            parts.append(jnp.where(half == g, src, jnp.zeros_like(src)))
        qp = jnp.concatenate(parts, axis=0)
        s = lax.dot_general(k_ext, qp, (((1,), (1,)), ((), ())), preferred_element_type=F32)
        s = s * (HEAD_DIM ** -0.5) + bias_ref[0, g]
        sink = sink_ref[g]
        m = jnp.maximum(jnp.max(s, axis=0, keepdims=True), sink)
        e = jnp.exp(s - m)
        l = jnp.sum(e, axis=0, keepdims=True) + jnp.exp(sink - m)
        o_t = jnp.dot(v_ext[g * HEAD_DIM:(g + 1) * HEAD_DIM], e.astype(BF16), preferred_element_type=F32)
        outs.append(_rows_to_heads(o_t / l, SWA_GROUP, tq))
    o_ref[...] = jnp.concatenate(outs, axis=1).astype(o_ref.dtype)


def _swa_bias():
    tq = SWA_TQ
    slopes = _alibi_slopes().reshape(SWA_KV_HEADS, SWA_GROUP)
    x = np.arange(2 * tq)[:, None] - tq
    q = np.arange(tq)[None, :]
    dist = q - x
    valid = (dist >= 0) & (dist < SWA_WINDOW)
    out = np.empty((2, SWA_KV_HEADS, 2 * tq, SWA_GROUP * tq), np.float32)
    for var in range(2):
        v = valid & ((x >= 0) | (var == 1))
        for g in range(SWA_KV_HEADS):
            for r in range(SWA_GROUP):
                out[var, g, :, r * tq:(r + 1) * tq] = np.where(v, -slopes[g, r] * dist, MASKED)
    return out


def _swa_attention(qkv, v_t, sinks):
    t = qkv.shape[0]
    tq = SWA_TQ
    kcol = Q_DIM // LANES
    bias = jnp.asarray(_swa_bias())
    sink_rows = jnp.repeat(sinks.astype(F32).reshape(SWA_KV_HEADS, 1, SWA_GROUP), tq, axis=2)
    prev = lambda i: jnp.maximum(i - 1, 0)
    return pl.pallas_call(
        _swa_kernel,
        grid=(t // tq,),
        in_specs=[pl.BlockSpec((tq, Q_DIM), lambda i: (i, 0)),
                  pl.BlockSpec((tq, LANES), lambda i: (prev(i), kcol)),
                  pl.BlockSpec((tq, LANES), lambda i: (i, kcol)),
                  pl.BlockSpec((1, LANES, tq), lambda i: (prev(i), 0, 0)),
                  pl.BlockSpec((1, LANES, tq), lambda i: (i, 0, 0)),
                  pl.BlockSpec((1,) + bias.shape[1:], lambda i: (jnp.minimum(i, 1), 0, 0, 0)),
                  pl.BlockSpec(sink_rows.shape, lambda i: (0, 0, 0))],
        out_specs=pl.BlockSpec((tq, Q_DIM), lambda i: (i, 0)),
        out_shape=jax.ShapeDtypeStruct((t, Q_DIM), BF16),
        compiler_params=_cparams(("parallel",)),
        name="swa_attention",
    )(qkv, qkv, qkv, v_t, v_t, bias, sink_rows)


def _swa_heads(x, norm_g, w_in, sinks):
    w = w_in.astype(BF16)
    n_qk = Q_DIM + SWA_KV_HEADS * HEAD_DIM
    qk, v_t = _norm_proj(x, norm_g, [w[:, :n_qk], w[:, n_qk:]], [BF16, BF16], chunks=[None, SWA_TQ])
    return _swa_attention(qk, v_t, sinks)


FOX_PARTS = 3


def _split_bf16(v):
    parts, rest = [], v
    for _ in range(FOX_PARTS):
        p = rest.astype(BF16)
        parts.append(p)
        rest = rest - p.astype(F32)
    return parts


def _fox_prep_kernel(f_ref, b_ref, aug_ref, cref_ref, carry_ref):
    tm = f_ref.shape[0]

    @pl.when(pl.program_id(0) == 0)
    def _():
        carry_ref[...] = jnp.zeros_like(carry_ref)

    logf = jax.nn.log_sigmoid(f_ref[...] + b_ref[...])
    row = lax.broadcasted_iota(jnp.int32, (tm, tm), 0)
    col = lax.broadcasted_iota(jnp.int32, (tm, tm), 1)
    tri = jnp.where(row >= col, 1.0, 0.0).astype(BF16)
    local = sum(jnp.dot(tri, p, preferred_element_type=F32) for p in _split_bf16(logf))
    cum = local + carry_ref[...]
    carry_ref[...] = cum[tm - 1:tm, :]
    first = cum[0:1, :]
    cref_ref[0] = jnp.broadcast_to(first, cref_ref.shape[1:])
    hi, mid, lo = _split_bf16(first - cum)
    lane = lax.broadcasted_iota(jnp.int32, (tm, LANES), 1)
    zero = jnp.zeros_like(hi)
    aug_ref[...] = jnp.where(lane < N_HEADS, hi, jnp.where(lane < 2 * N_HEADS, mid,
                                                            jnp.where(lane < 3 * N_HEADS, lo, zero)))


def _fox_prep(f_logit3, f_bias3):
    t = f_logit3.shape[0]
    nt = t // FOX_T
    return pl.pallas_call(
        _fox_prep_kernel,
        grid=(nt,),
        in_specs=[pl.BlockSpec((FOX_T, LANES), lambda i: (i, 0)), pl.BlockSpec((1, LANES), lambda i: (0, 0))],
        out_specs=[pl.BlockSpec((FOX_T, LANES), lambda i: (i, 0)), pl.BlockSpec((1, 8, LANES), lambda i: (i, 0, 0))],
        out_shape=[jax.ShapeDtypeStruct((t, LANES), BF16), jax.ShapeDtypeStruct((nt, 8, LANES), F32)],
        scratch_shapes=[pltpu.VMEM((1, LANES), F32)],
        compiler_params=_cparams(("arbitrary",)),
        name="fox_prep",
    )(f_logit3, f_bias3)


def _fox_kernel(cref_ref, q_ref, k_ref, aug_ref, vt_ref, o_ref, m_ref, l_ref, acc_ref, qt_ref, s_ref):
    pair, qi = pl.program_id(0), pl.program_id(1)
    tq = q_ref.shape[0]
    tk = FOX_T
    lane = lax.broadcasted_iota(jnp.int32, (tq, LANES), 1)
    q2 = q_ref[...] * jnp.asarray(HEAD_DIM ** -0.5, BF16)
    for hh in range(2):
        head = 2 * pair + hh
        pick = (lane % N_HEADS == head) & (lane < FOX_PARTS * N_HEADS)
        q_ext = jnp.concatenate([jnp.where(lane // HEAD_DIM == hh, q2, jnp.zeros_like(q2)).astype(F32),
                                 jnp.where(pick, 1.0, 0.0)], axis=1)
        qt_ref[hh] = q_ext.T.astype(BF16)
    m_ref[...] = jnp.full_like(m_ref, M_INIT)
    l_ref[...] = jnp.zeros_like(l_ref)
    acc_ref[...] = jnp.zeros_like(acc_ref)
    ratio = tq // tk

    def scores(kj, hh, q_from=0):
        start = pl.multiple_of(kj * tk, tk)
        k_ext = jnp.concatenate([k_ref[pl.ds(start, tk), :], aug_ref[pl.ds(start, tk), :]], axis=1)
        s_ref[hh, :, q_from:] = jnp.dot(k_ext, qt_ref[hh, :, q_from:], preferred_element_type=F32)

    def consume(kj, hh, key_offset):
        head = 2 * pair + hh
        q_from = 0 if key_offset is None else key_offset
        s = s_ref[hh, :, q_from:]
        if key_offset is not None:
            key = lax.broadcasted_iota(jnp.int32, s.shape, 0)
            qry = lax.broadcasted_iota(jnp.int32, s.shape, 1)
            s = jnp.where(key <= qry, s, MASKED)
        off = cref_ref[qi * ratio, head] - cref_ref[kj, head]
        m_old = m_ref[hh, :, q_from:]
        m_new = jnp.maximum(m_old, jnp.max(s, axis=0, keepdims=True) + off)
        e = jnp.exp(s - (m_new - off))
        alpha = jnp.exp(m_old - m_new)
        m_ref[hh, :, q_from:] = m_new
        l_ref[hh, :, q_from:] = alpha * l_ref[hh, :, q_from:] + jnp.sum(e, axis=0, keepdims=True)
        v_t = vt_ref[kj, hh * HEAD_DIM:(hh + 1) * HEAD_DIM, :]
        acc_ref[hh, :, q_from:] = (alpha * acc_ref[hh, :, q_from:]
                                   + jnp.dot(v_t, e.astype(BF16), preferred_element_type=F32))

    n_full = qi * ratio
    scores(0, 0)

    def body(j, carry):
        for d in range(ratio):
            kj = j * ratio + d
            scores(kj, 1)
            consume(kj, 0, None)
            scores(kj + 1, 0)
            consume(kj, 1, None)
        return carry

    lax.fori_loop(0, qi, body, 0)
    for d in range(ratio):
        kj = n_full + d
        scores(kj, 1, d * tk)
        consume(kj, 0, d * tk)
        if d + 1 < ratio:
            scores(kj + 1, 0, (d + 1) * tk)
        consume(kj, 1, d * tk)
    o_t = jnp.concatenate([acc_ref[hh] / l_ref[hh] for hh in range(2)], axis=0)
    o_ref[...] = o_t.T.astype(o_ref.dtype)


def _fox_attention(qkv, aug, v_t3, cref):
    t = qkv.shape[0]
    tq = FOX_TQ
    nk = t // FOX_T
    kcol = Q_DIM // LANES
    return pl.pallas_call(
        _fox_kernel,
        grid=(N_HEADS // 2, t // tq),
        in_specs=[pl.BlockSpec(memory_space=pltpu.SMEM),
                  pl.BlockSpec((tq, LANES), lambda p, i: (i, p)),
                  pl.BlockSpec((t, LANES), lambda p, i: (0, kcol + p)),
                  pl.BlockSpec((t, LANES), lambda p, i: (0, 0)),
                  pl.BlockSpec((nk, LANES, FOX_T), lambda p, i: (0, p, 0))],
        out_specs=pl.BlockSpec((tq, LANES), lambda p, i: (i, p)),
        out_shape=jax.ShapeDtypeStruct((t, Q_DIM), BF16),
        scratch_shapes=[pltpu.VMEM((2, 1, tq), F32), pltpu.VMEM((2, 1, tq), F32),
                        pltpu.VMEM((2, HEAD_DIM, tq), F32), pltpu.VMEM((2, 2 * LANES, tq), BF16), pltpu.VMEM((2, FOX_T, tq), F32)],
        compiler_params=_cparams(("parallel", "arbitrary")),
        name="fox_attention",
    )(cref, qkv, qkv, aug, v_t3)


def _fox_heads(x, norm_g, w_in, f_bias):
    t = x.shape[0]
    w_f = w_in[:, 3 * Q_DIM:]
    pad = jnp.zeros((D_MODEL, LANES - FOX_PARTS * N_HEADS), w_f.dtype)
    w_f3 = jnp.concatenate([w_f] * FOX_PARTS + [pad], axis=1).astype(BF16)
    b3 = jnp.concatenate([f_bias.astype(F32)] * FOX_PARTS + [jnp.zeros((LANES - FOX_PARTS * N_HEADS,), F32)])
    w = w_in.astype(BF16)
    qk, v_t3, f_logit3 = _norm_proj(x, norm_g, [w[:, :2 * Q_DIM], w[:, 2 * Q_DIM:3 * Q_DIM], w_f3], [BF16, BF16, F32],
                                    chunks=[None, FOX_T, None])
    aug, cref = _fox_prep(f_logit3, b3.reshape(1, LANES))
    return _fox_attention(qk, aug, v_t3, cref[:, 0, :N_HEADS])


def _nsa_compress_kernel(a_ref, pos_ref, w1_ref, w2_ref, o_ref):
    tc = a_ref.shape[1]
    half = a_ref.shape[2]
    a = a_ref[0]
    pos = pos_ref[0]
    top = jnp.dot((a + pos[:, :half]).astype(BF16), w1_ref[0, :half], preferred_element_type=F32)
    bot = jnp.dot((a + pos[:, half:]).astype(BF16), w1_ref[0, half:], preferred_element_type=F32)
    hid = top + pltpu.roll(bot, tc - 1, 0)
    o_ref[0] = jnp.dot(jax.nn.gelu(hid).astype(BF16), w2_ref[0], preferred_element_type=F32)


def _nsa_compress(a, pos, w1, w2):
    _, tc, width = a.shape
    return pl.pallas_call(
        _nsa_compress_kernel,
        grid=(2, NSA_KV_HEADS // 2),
        in_specs=[pl.BlockSpec((1, tc, width), lambda kv, gp: (kv * (NSA_KV_HEADS // 2) + gp, 0, 0)),
                  pl.BlockSpec((1, 1, 2 * width), lambda kv, gp: (kv, 0, 0)),
                  pl.BlockSpec((1, 2 * width, 2 * CMP_HIDDEN), lambda kv, gp: (kv, 0, 0)),
                  pl.BlockSpec((1, 2 * CMP_HIDDEN, LANES), lambda kv, gp: (kv, 0, 0))],
        out_specs=pl.BlockSpec((1, tc, LANES), lambda kv, gp: (kv, 0, gp)),
        out_shape=jax.ShapeDtypeStruct((2, tc, NSA_KV_DIM), F32),
        compiler_params=_cparams(("parallel", "parallel")),
        name="nsa_compress",
    )(a, pos, w1, w2)


NSA_WIN_TILES = NSA_WINDOW // NSA_TQ + 1


def _nsa_kernel(slopes_ref, q_ref, ks_ref, *refs):
    kw_refs, refs = refs[:NSA_WIN_TILES], refs[NSA_WIN_TILES:]
    vst_ref, refs = refs[0], refs[1:]
    vw_refs, refs = refs[:NSA_WIN_TILES], refs[NSA_WIN_TILES:]
    (kcmp_ref, vov_ref, gates_ref, ksel_ext_ref, kwin_ext_ref, kcmp_ext_ref, qext_ref,
     o_ref, selt_ref, m_ref, l_ref, acc_ref, qt_ref, s_ref, scmp_ref, cm_ref, cl_ref, oc_ref,
     active_ref, list_ref) = refs
    g, qi = pl.program_id(0), pl.program_id(1)
    tq = q_ref.shape[0]
    tc = kcmp_ref.shape[0]
    n_sel = selt_ref.shape[0]
    rr = NSA_GROUP
    t0 = qi * tq
    hg = g % 2
    slope = [slopes_ref[g * rr + r] for r in range(rr)]

    half = _lane_half((tq, LANES))
    parts = []
    for r in range(rr):
        q_pair = q_ref[:, (r // 2) * LANES:(r // 2 + 1) * LANES]
        src = jnp.where(hg == r % 2, q_pair, _swap_halves(q_pair))
        parts.append(jnp.where(half == hg, src, jnp.zeros_like(src)) * jnp.asarray(HEAD_DIM ** -0.5, BF16))
    qp = jnp.concatenate(parts, axis=0)
    qt_ref[0:LANES, :] = qp.astype(F32).T.astype(BF16)

    @pl.when(qi == 0)
    def _():
        qt_ref[LANES:, :] = qext_ref[0]

    def lanes_of(s, r):
        return s[:, r * tq:(r + 1) * tq]

    chunk_span = NSA_CMP_CHUNK * CMP_STRIDE
    last_chunk = ((t0 + tq - CMP_BLOCK) // CMP_STRIDE) // NSA_CMP_CHUNK
    nl16_q = (CMP_STRIDE * lax.broadcasted_iota(jnp.int32, (NSA_CMP_CHUNK, tq), 0)
              - lax.broadcasted_iota(jnp.int32, (NSA_CMP_CHUNK, tq), 1)).astype(F32)
    cm_ref[...] = jnp.full_like(cm_ref, M_INIT)

    def cmp_rows(ch):
        return pl.ds(pl.multiple_of(ch * NSA_CMP_CHUNK, NSA_CMP_CHUNK), NSA_CMP_CHUNK)

    def cmp_offset(ch, r):
        return slope[r] * (ch * chunk_span).astype(F32)

    def cmp_scores(ch, masked):
        k_full = jnp.concatenate([kcmp_ref[cmp_rows(ch), :], kcmp_ext_ref[...]], axis=1)
        s = jnp.dot(k_full, qt_ref[...], preferred_element_type=F32)
        if masked:
            visible = nl16_q <= (t0 - (CMP_BLOCK - 1) - ch * chunk_span).astype(F32)
            s = jnp.concatenate([jnp.where(visible, lanes_of(s, r), MASKED) for r in range(rr)], axis=1)
        scmp_ref[cmp_rows(ch), :] = s
        for r in range(rr):
            cm_ref[r] = jnp.maximum(cm_ref[r], jnp.max(lanes_of(s, r), axis=0, keepdims=True) + cmp_offset(ch, r))

    def cmp_plain(ch, carry):
        cmp_scores(ch, False)
        return carry

    lax.fori_loop(0, jnp.maximum(last_chunk - 1, 0), cmp_plain, 0)

    @pl.when(last_chunk >= 1)
    def _():
        cmp_scores(last_chunk - 1, True)

    cmp_scores(last_chunk, True)

    cl_ref[...] = jnp.zeros_like(cl_ref)
    oc_ref[...] = jnp.zeros_like(oc_ref)

    def cmp_exp(ch, carry):
        s = scmp_ref[cmp_rows(ch), :]
        e_all = []
        for r in range(rr):
            e = jnp.exp(lanes_of(s, r) - (cm_ref[r] - cmp_offset(ch, r)))
            cl_ref[r] = cl_ref[r] + jnp.sum(e, axis=0, keepdims=True)
            e_all.append(e.astype(BF16))
        oc_ref[...] += jnp.dot(vov_ref[ch, 0], jnp.concatenate(e_all, axis=1), preferred_element_type=F32)
        return carry

    lax.fori_loop(0, last_chunk + 1, cmp_exp, 0)
    inv_l = [jnp.where(cl_ref[r] > 0.0, 1.0 / cl_ref[r], 0.0) for r in range(rr)]
    o_cmp = [oc_ref[0:HEAD_DIM, r * tq:(r + 1) * tq] * inv_l[r] for r in range(rr)]

    imp = functools.reduce(jnp.add, [oc_ref[HEAD_DIM:, r * tq:(r + 1) * tq] * inv_l[r] for r in range(rr)])
    blk = lax.broadcasted_iota(jnp.int32, (n_sel, tq), 0)
    cur = (t0 + lax.broadcasted_iota(jnp.int32, (n_sel, tq), 1)) // SEL_BLOCK
    blk_f = blk.astype(F32)
    forced = jnp.where(blk == 0, 1.0, jnp.where(blk == cur, 1.0, jnp.where(blk == cur - 1, 1.0, 0.0)))
    score = jnp.where(forced > 0.0, TAKEN, jnp.where(blk <= cur, imp, -1.0))

    selt_ref[...] = score
    row_step = min(n_sel, NSA_TOPK_ROW_STEP)
    variant = ((t0 + tq - 1) // SEL_BLOCK) // row_step

    for v in range(n_sel // row_step):
        rows = row_step * (v + 1)

        @pl.when(variant == v)
        def _(rows=rows):
            ids = lax.broadcasted_iota(jnp.int32, (rows, tq), 0).astype(F32)

            def pick(_, sc):
                best = jnp.max(sc, axis=0, keepdims=True)
                first = jnp.min(jnp.where(sc == best, ids, float(n_sel)), axis=0, keepdims=True)
                return jnp.where(ids == first, TAKEN, sc)

            selt_ref[0:rows, :] = lax.fori_loop(0, N_SELECT - 3, pick, selt_ref[0:rows, :])

    selt_ref[...] = jnp.where(blk <= cur, jnp.where(selt_ref[...] == TAKEN, 1.0, 0.0), 0.0)

    m_ref[...] = jnp.full_like(m_ref, M_INIT)
    l_ref[...] = jnp.zeros_like(l_ref)
    acc_ref[...] = jnp.zeros_like(acc_ref)
    blocks_per_chunk = NSA_TK // SEL_BLOCK

    for c in range(n_sel // blocks_per_chunk):
        active_ref[c] = jnp.max(selt_ref[c * blocks_per_chunk:(c + 1) * blocks_per_chunk, :])

    def compact(c, n):
        list_ref[n] = c
        return n + (active_ref[c] > 0.0).astype(jnp.int32)

    n_active = lax.fori_loop(0, t0 // NSA_TK + 1, compact, 0)

    def sel_scores(c, buf):
        start = pl.multiple_of(c * NSA_TK, NSA_TK)
        rows = selt_ref[pl.ds(pl.multiple_of(c * blocks_per_chunk, blocks_per_chunk), blocks_per_chunk), :]
        mask_rows = jnp.concatenate([jnp.where(rows > 0.5, 0.0, MASKED), jnp.zeros_like(rows)], axis=0)
        first = LANES + buf * NSA_EXT_MASK_ROWS
        qt_ref[first:first + NSA_EXT_MASK_ROWS, :] = jnp.concatenate([mask_rows] * rr, axis=1).astype(BF16)
        k_full = jnp.concatenate([ks_ref[pl.ds(start, NSA_TK), :], ksel_ext_ref[buf]], axis=1)
        s_ref[buf] = jnp.dot(k_full, qt_ref[...], preferred_element_type=F32)

    def sel_consume(c, buf, diagonal):
        shift = (c * NSA_TK - t0).astype(F32)
        if diagonal:
            key = lax.broadcasted_iota(jnp.int32, (NSA_TK, tq), 0) + c * NSA_TK
            visible = key <= lax.broadcasted_iota(jnp.int32, (NSA_TK, tq), 1) + t0
        e_all, alpha_all = [], []
        for r in range(rr):
            off = slope[r] * shift
            s = s_ref[buf][:, r * tq:(r + 1) * tq]
            if diagonal:
                s = jnp.where(visible, s, MASKED)
            m_old = m_ref[r]
            m_new = jnp.maximum(m_old, jnp.max(s, axis=0, keepdims=True) + off)
            e = jnp.exp(s - (m_new - off))
            alpha = jnp.exp(m_old - m_new)
            m_ref[r] = m_new
            l_ref[r] = alpha * l_ref[r] + jnp.sum(e, axis=0, keepdims=True)
            e_all.append(e.astype(BF16))
            alpha_all.append(alpha)
        pv = jnp.dot(vst_ref[c], jnp.concatenate(e_all, axis=1), preferred_element_type=F32)
        acc_ref[...] = jnp.concatenate(alpha_all, axis=1) * acc_ref[...] + pv

    n_before = n_active - 1
    sel_scores(list_ref[0], 0)

    def sel_pair(j, carry):
        c0, c1, c2 = list_ref[2 * j], list_ref[2 * j + 1], list_ref[2 * j + 2]
        sel_scores(c1, 1)
        sel_consume(c0, 0, False)
        sel_scores(c2, 0)
        sel_consume(c1, 1, False)
        return carry

    lax.fori_loop(0, n_before // 2, sel_pair, 0)

    @pl.when(n_before % 2 == 1)
    def _():
        c0, c_last = list_ref[n_before - 1], list_ref[n_before]
        sel_scores(c_last, 1)
        sel_consume(c0, 0, False)
        sel_consume(c_last, 1, True)

    @pl.when(n_before % 2 == 0)
    def _():
        sel_consume(list_ref[n_before], 0, True)

    xw = lax.broadcasted_iota(jnp.int32, (tq, tq), 0)
    qw = lax.broadcasted_iota(jnp.int32, (tq, tq), 1)
    s_win, off_win = [], []
    for idx in range(NSA_WIN_TILES):
        back = NSA_WIN_TILES - 1 - idx
        k_full = jnp.concatenate([kw_refs[idx][...], kwin_ext_ref[...]], axis=1)
        s_t = jnp.dot(k_full, qt_ref[...], preferred_element_type=F32)
        per_head = [lanes_of(s_t, r) for r in range(rr)]
        if back == NSA_WIN_TILES - 1:
            per_head = [jnp.where(xw > qw, s, MASKED) for s in per_head]
        if back == 0:
            per_head = [jnp.where(xw <= qw, s, MASKED) for s in per_head]
        s_win.append(per_head)
        off_win.append([jnp.where(qi >= back, -slope[r] * float(back * tq), MASKED) for r in range(rr)])
    e_win, l_win = [], []
    for r in range(rr):
        m = functools.reduce(jnp.maximum, [jnp.max(s_win[idx][r], axis=0, keepdims=True) + off_win[idx][r]
                                           for idx in range(NSA_WIN_TILES)])
        e_tiles = [jnp.exp(s_win[idx][r] - (m - off_win[idx][r])) for idx in range(NSA_WIN_TILES)]
        l_win.append(functools.reduce(jnp.add, [jnp.sum(e, axis=0, keepdims=True) for e in e_tiles]))
        e_win.append(jnp.concatenate([e.astype(BF16) for e in e_tiles], axis=0))
    v_win = jnp.concatenate([vw_refs[idx][0] for idx in range(NSA_WIN_TILES)], axis=1)
    o_win_all = jnp.dot(v_win, jnp.concatenate(e_win, axis=1), preferred_element_type=F32)
    o_win = [lanes_of(o_win_all, r) / l_win[r] for r in range(rr)]

    gate = jax.nn.sigmoid(gates_ref[0])
    merged = []
    for r in range(rr):
        o_sel = lanes_of(acc_ref[...], r) / l_ref[r]
        merged.append(gate[3 * r:3 * r + 1] * o_cmp[r] + gate[3 * r + 1:3 * r + 2] * o_sel
                      + gate[3 * r + 2:3 * r + 3] * o_win[r])
    o_ref[...] = jnp.concatenate(merged, axis=0).T.astype(o_ref.dtype)


def _nsa_overlap_t(n_sel, tc):
    cmp_start = np.arange(tc - 1) * CMP_STRIDE
    sel_start = np.arange(n_sel) * SEL_BLOCK
    ov = np.clip(np.minimum(cmp_start[None, :] + CMP_BLOCK, sel_start[:, None] + SEL_BLOCK)
                 - np.maximum(cmp_start[None, :], sel_start[:, None]), 0, None) / CMP_BLOCK
    return np.concatenate([ov, np.zeros((n_sel, 1))], axis=1).astype(np.float32)


NSA_EXT_MASK_ROWS = 16
NSA_EXT_SEL = 32
NSA_EXT_WIN = 48
NSA_EXT_CMP = 64
NSA_CMP_CHUNK = 256
NSA_TOPK_ROW_STEP = 64
SLOPE_PIECES = 3


def _bf16_pieces(v):
    out, rest = [], np.asarray(v, np.float32)
    for _ in range(SLOPE_PIECES):
        p = rest.astype(BF16).astype(np.float32)
        out.append(p)
        rest = rest - p
    return out


def _nsa_extensions(tq):
    ksel = np.zeros((2, NSA_TK, LANES), np.float32)
    x = np.arange(NSA_TK)
    for buf in range(2):
        ksel[buf, x, buf * NSA_EXT_MASK_ROWS + x // SEL_BLOCK] = 1.0
    for j in range(SLOPE_PIECES):
        ksel[:, :, NSA_EXT_SEL + j] = 256 * (x // 256)
        ksel[:, :, NSA_EXT_SEL + SLOPE_PIECES + j] = x % 256
    kwin = np.zeros((tq, LANES), np.float32)
    xw = np.arange(tq)
    for j in range(SLOPE_PIECES):
        kwin[:, NSA_EXT_WIN + j] = 256 * (xw // 256)
        kwin[:, NSA_EXT_WIN + SLOPE_PIECES + j] = xw % 256
    kcmp = np.zeros((NSA_CMP_CHUNK, LANES), np.float32)
    for j in range(SLOPE_PIECES):
        kcmp[:, NSA_EXT_CMP + j] = np.arange(NSA_CMP_CHUNK)
    pieces = _bf16_pieces(_alibi_slopes())
    qext = np.zeros((NSA_KV_HEADS, LANES, NSA_GROUP * tq), np.float32)
    for g in range(NSA_KV_HEADS):
        for r in range(NSA_GROUP):
            cols = slice(r * tq, (r + 1) * tq)
            for j in range(SLOPE_PIECES):
                p = pieces[j][g * NSA_GROUP + r]
                qext[g, NSA_EXT_SEL + j, cols] = p
                qext[g, NSA_EXT_SEL + SLOPE_PIECES + j, cols] = p
                qext[g, NSA_EXT_WIN + j, cols] = p
                qext[g, NSA_EXT_WIN + SLOPE_PIECES + j, cols] = p
                qext[g, NSA_EXT_CMP + j, cols] = CMP_STRIDE * p
    return jnp.asarray(ksel, BF16), jnp.asarray(kwin, BF16), jnp.asarray(kcmp, BF16), jnp.asarray(qext, BF16)


def _nsa_attention(main, vs_t3, vw_t, k_cmp, v_cmp_t, gates_t):
    t = main.shape[0]
    tq = NSA_TQ
    tc = k_cmp.shape[0]
    n_sel = t // SEL_BLOCK
    nc = t // NSA_TK
    ks_col = Q_DIM // LANES
    kw_col = ks_col + NSA_KV_DIM // LANES
    n_ch = tc // NSA_CMP_CHUNK
    ov_t = jnp.asarray(_nsa_overlap_t(n_sel, tc).reshape(n_sel, n_ch, NSA_CMP_CHUNK).transpose(1, 0, 2), BF16)
    v_cmp_t = v_cmp_t.reshape(NSA_KV_HEADS, HEAD_DIM, n_ch, NSA_CMP_CHUNK).transpose(2, 0, 1, 3)
    vov = jnp.concatenate([v_cmp_t, jnp.broadcast_to(ov_t[:, None], (n_ch, NSA_KV_HEADS, n_sel, NSA_CMP_CHUNK))],
                          axis=2)
    slopes = jnp.asarray(_alibi_slopes())
    ksel_ext, kwin_ext, kcmp_ext, qext = _nsa_extensions(tq)

    def kw_spec(back):
        return pl.BlockSpec((tq, LANES), lambda g, i: (jnp.maximum(i - back, 0), kw_col + g // 2))

    def vw_spec(back):
        return pl.BlockSpec((1, HEAD_DIM, tq), lambda g, i: (jnp.maximum(i - back, 0), g, 0))

    backs = list(range(NSA_WIN_TILES - 1, -1, -1))
    return pl.pallas_call(
        _nsa_kernel,
        grid=(NSA_KV_HEADS, t // tq),
        in_specs=[pl.BlockSpec(memory_space=pltpu.SMEM),
                  pl.BlockSpec((tq, NSA_GROUP * HEAD_DIM), lambda g, i: (i, g)),
                  pl.BlockSpec((t, LANES), lambda g, i: (0, ks_col + g // 2))]
                 + [kw_spec(b) for b in backs]
                 + [pl.BlockSpec((nc, HEAD_DIM, NSA_TK), lambda g, i: (0, g, 0))]
                 + [vw_spec(b) for b in backs]
                 + [pl.BlockSpec((tc, LANES), lambda g, i: (0, g // 2)),
                    pl.BlockSpec((n_ch, 1, HEAD_DIM + n_sel, NSA_CMP_CHUNK), lambda g, i: (0, g, 0, 0)),
                    pl.BlockSpec((1, 16, tq), lambda g, i: (g, 0, i)),
                    pl.BlockSpec(ksel_ext.shape, lambda g, i: (0, 0, 0)),
                    pl.BlockSpec(kwin_ext.shape, lambda g, i: (0, 0)),
                    pl.BlockSpec(kcmp_ext.shape, lambda g, i: (0, 0)),
                    pl.BlockSpec((1,) + qext.shape[1:], lambda g, i: (g, 0, 0))],
        out_specs=pl.BlockSpec((tq, NSA_GROUP * HEAD_DIM), lambda g, i: (i, g)),
        out_shape=jax.ShapeDtypeStruct((t, Q_DIM), BF16),
        scratch_shapes=[pltpu.VMEM((n_sel, tq), F32), pltpu.VMEM((NSA_GROUP, 1, tq), F32),
                        pltpu.VMEM((NSA_GROUP, 1, tq), F32), pltpu.VMEM((HEAD_DIM, NSA_GROUP * tq), F32),
                        pltpu.VMEM((2 * LANES, NSA_GROUP * tq), BF16),
                        pltpu.VMEM((2, NSA_TK, NSA_GROUP * tq), F32),
                        pltpu.VMEM((tc, NSA_GROUP * tq), F32), pltpu.VMEM((NSA_GROUP, 1, tq), F32),
                        pltpu.VMEM((NSA_GROUP, 1, tq), F32), pltpu.VMEM((HEAD_DIM + n_sel, NSA_GROUP * tq), F32),
                        pltpu.SMEM((nc,), F32), pltpu.SMEM((nc,), jnp.int32)],
        compiler_params=_cparams(("parallel", "arbitrary")),
        name="nsa_attention",
    )(slopes, main, main, *([main] * NSA_WIN_TILES), vs_t3, *([vw_t] * NSA_WIN_TILES), k_cmp, vov, gates_t,
      ksel_ext, kwin_ext, kcmp_ext, qext)


def _nsa_heads(x, norm_g, w_in, cmp_pos, cmp_w1, cmp_w2):
    t = x.shape[0]
    kv = NSA_KV_DIM
    col = lambda i: w_in[:, Q_DIM + i * kv:Q_DIM + (i + 1) * kv]
    w_main = jnp.concatenate([w_in[:, :Q_DIM], col(2), col(4)], axis=1).astype(BF16)
    w_cmp = jnp.concatenate([col(0), col(1)], axis=1).astype(BF16)
    n_gate = 3 * N_HEADS
    w_gate = jnp.concatenate([w_in[:, Q_DIM + 6 * kv:], jnp.zeros((D_MODEL, LANES - n_gate), w_in.dtype)],
                             axis=1).astype(BF16)
    main, vs_t3, vw_t3, kcvc, gate_logit = _norm_proj(
        x, norm_g, [w_main, col(3).astype(BF16), col(5).astype(BF16), w_cmp, w_gate], [BF16, BF16, BF16, F32, F32],
        chunks=[None, NSA_TK, NSA_TQ, "cols", None])

    tc = t // CMP_STRIDE
    a = kcvc.reshape(kcvc.shape[0], tc, CMP_STRIDE * LANES)
    pair = jnp.eye(2, dtype=cmp_w1.dtype)
    pos = jnp.broadcast_to(cmp_pos[:, :, None, :], (2, CMP_BLOCK, 2, HEAD_DIM)).reshape(2, 1, 2 * CMP_BLOCK * HEAD_DIM)
    w1 = (cmp_w1.reshape(2, CMP_BLOCK, 1, HEAD_DIM, 1, CMP_HIDDEN) * pair.reshape(1, 1, 2, 1, 2, 1))
    w1 = w1.reshape(2, 2 * CMP_BLOCK * HEAD_DIM, 2 * CMP_HIDDEN)
    w2 = (cmp_w2.reshape(2, 1, CMP_HIDDEN, 1, HEAD_DIM) * pair.reshape(1, 2, 1, 2, 1))
    w2 = w2.reshape(2, 2 * CMP_HIDDEN, 2 * HEAD_DIM)
    cmp = _nsa_compress(a, pos, w1.astype(BF16), w2.astype(BF16))
    k_cmp = cmp[0].astype(BF16)
    v_cmp_t = cmp[1].T.astype(BF16)

    gates_t = gate_logit[:, :n_gate].reshape(t, NSA_KV_HEADS, 3 * NSA_GROUP).transpose(1, 2, 0)
    gates_t = jnp.pad(gates_t, ((0, 0), (0, 16 - 3 * NSA_GROUP), (0, 0)))
    return _nsa_attention(main, vs_t3, vw_t3, k_cmp, v_cmp_t, gates_t)


def kernel(x, attn_norm, mlp_norm, final_norm, nsa_w_in, nsa_cmp_pos, nsa_cmp_w1, nsa_cmp_w2, nsa_w_out,
           swa_w_in, swa_sinks, swa_w_out, fox_w_in, fox_f_bias, fox_w_out,
           mlp_w_up, mlp_conv_w, mlp_conv_b, mlp_w_down):
    assert x.shape[0] == 1, "the trunk is written for batch 1"
    h = x[0]
    for i in range(DEPTH):
        kind, j = i % N_MIXERS, i // N_MIXERS
        if kind == 0:
            o = _nsa_heads(h, attn_norm[i], nsa_w_in[j], nsa_cmp_pos[j], nsa_cmp_w1[j], nsa_cmp_w2[j])
            w_out = nsa_w_out[j]
        elif kind == 1:
            o = _swa_heads(h, attn_norm[i], swa_w_in[j], swa_sinks[j])
            w_out = swa_w_out[j]
        else:
            o = _fox_heads(h, attn_norm[i], fox_w_in[j], fox_f_bias[j])
            w_out = fox_w_out[j]
        h = _mixer_out_mlp(h, o, w_out.astype(BF16), mlp_norm[i], mlp_w_up[i].astype(BF16), mlp_conv_w[i],
                           mlp_conv_b[i].reshape(1, -1), mlp_w_down[i].astype(BF16),
                           final_g=final_norm if i == DEPTH - 1 else None)
    return h[None]
```

```python
import functools

import numpy as np
import jax
import jax.numpy as jnp
from jax import lax
from jax.experimental import pallas as pl
from jax.experimental.pallas import tpu as pltpu

F32 = jnp.float32
BF16 = jnp.bfloat16

D_MODEL = 1024
DEPTH = 4
N_MIXERS = 3
HEAD_DIM = 64
N_HEADS = 16
Q_DIM = N_HEADS * HEAD_DIM
ALIBI_MAX = 8.0
NORM_EPS = 1e-6
MASKED = -2e30
M_INIT = -1e30
TAKEN = -(2.0 ** 127)

NSA_KV_HEADS = 4
NSA_GROUP = 4
NSA_KV_DIM = NSA_KV_HEADS * HEAD_DIM
CMP_BLOCK = 32
CMP_STRIDE = 16
CMP_HIDDEN = 256
SEL_BLOCK = 64
N_SELECT = 16
NSA_WINDOW = 512
NSA_TQ = 512
NSA_TK = 512

SWA_KV_HEADS = 2
SWA_GROUP = 8
SWA_WINDOW = 128
SWA_TQ = 128

FOX_TQ = 1024
FOX_T = 512

LANES = 128
VMEM_LIMIT = 56 * 1024 * 1024


def _cparams(semantics, vmem=VMEM_LIMIT):
    return pltpu.CompilerParams(dimension_semantics=semantics, vmem_limit_bytes=vmem)


def _alibi_slopes():
    return np.asarray(2.0 ** (-ALIBI_MAX * np.arange(1, N_HEADS + 1) / N_HEADS), dtype=np.float32)


def _rms(x, g):
    ms = jnp.mean(x * x, axis=-1, keepdims=True)
    return x * lax.rsqrt(ms + NORM_EPS) * g


def _norm_proj_kernel(chunks, x_ref, g_ref, *refs):
    n_out = len(chunks)
    h = _rms(x_ref[...], g_ref[...]).astype(BF16)
    for w_ref, o_ref, chunk in zip(refs[:n_out], refs[n_out:], chunks):
        res = jnp.dot(h, w_ref[...], preferred_element_type=F32)
        if chunk is None:
            o_ref[...] = res.astype(o_ref.dtype)
        elif chunk == "cols":
            for c in range(o_ref.shape[0]):
                o_ref[c] = res[:, c * LANES:(c + 1) * LANES].astype(o_ref.dtype)
        else:
            for c in range(res.shape[0] // chunk):
                o_ref[c] = res[c * chunk:(c + 1) * chunk].T.astype(o_ref.dtype)


def _pick_tile(n, candidates):
    for c in candidates:
        if n % c == 0:
            return c
    raise ValueError(f"no tile for {n}")


def _norm_proj(x, g, weights, out_dtypes, chunks=None):
    t, d = x.shape
    tm = _pick_tile(t, (1024, 512, 256, 128))
    chunks = tuple(chunks) if chunks else (None,) * len(weights)
    out_specs, out_shape = [], []
    for w, dt, c in zip(weights, out_dtypes, chunks):
        n = w.shape[1]
        if c is None:
            out_specs.append(pl.BlockSpec((tm, n), lambda i: (i, 0)))
            out_shape.append(jax.ShapeDtypeStruct((t, n), dt))
        elif c == "cols":
            out_specs.append(pl.BlockSpec((n // LANES, tm, LANES), lambda i: (0, i, 0)))
            out_shape.append(jax.ShapeDtypeStruct((n // LANES, t, LANES), dt))
        else:
            out_specs.append(pl.BlockSpec((tm // c, n, c), lambda i: (i, 0, 0)))
            out_shape.append(jax.ShapeDtypeStruct((t // c, n, c), dt))
    return pl.pallas_call(
        functools.partial(_norm_proj_kernel, chunks),
        grid=(t // tm,),
        in_specs=[pl.BlockSpec((tm, d), lambda i: (i, 0)), pl.BlockSpec((1, d), lambda i: (0, 0))]
                 + [pl.BlockSpec(w.shape, lambda i: (0, 0)) for w in weights],
        out_specs=out_specs,
        out_shape=out_shape,
        compiler_params=_cparams(("parallel",)),
        name="norm_proj",
    )(x, g.reshape(1, d), *weights)


MLP_HALO = 16


def _mixer_out_mlp_kernel(final_norm, x_ref, xh_ref, o_ref, oh_ref, wo_ref, g_ref, fg_ref, wa_ref, wg_ref,
                          cwa_ref, cwg_ref, cba_ref, cbg_ref, wd_ref, y_ref, x1_ref, h_ref, acc_ref, ua_ref, ug_ref):
    i, j = pl.program_id(0), pl.program_id(1)
    tm = x_ref.shape[0]

    @pl.when(j == 0)
    def _():
        wo = wo_ref[...]
        x1_ref[0:MLP_HALO, :] = xh_ref[...] + jnp.dot(oh_ref[...], wo, preferred_element_type=F32)
        x1_ref[MLP_HALO:, :] = x_ref[...] + jnp.dot(o_ref[...], wo, preferred_element_type=F32)
        halo = _rms(x1_ref[0:MLP_HALO, :], g_ref[...])
        h_ref[0:MLP_HALO, :] = jnp.where(i > 0, halo, 0.0).astype(BF16)
        h_ref[MLP_HALO:, :] = _rms(x1_ref[MLP_HALO:, :], g_ref[...]).astype(BF16)
        acc_ref[...] = jnp.zeros_like(acc_ref)

    h = h_ref[...]

    def conv(w_ref, cw_ref, cb_ref, u_ref):
        u_ref[...] = jnp.dot(h, w_ref[...], preferred_element_type=F32)
        cw = cw_ref[...]
        return (cw[0:1] * u_ref[MLP_HALO - 2:MLP_HALO - 2 + tm, :] + cw[1:2] * u_ref[MLP_HALO - 1:MLP_HALO - 1 + tm, :]
                + cw[2:3] * u_ref[MLP_HALO:MLP_HALO + tm, :] + cb_ref[...])

    a = conv(wa_ref, cwa_ref, cba_ref, ua_ref)
    gt = conv(wg_ref, cwg_ref, cbg_ref, ug_ref)
    act = (jax.nn.silu(gt) * a).astype(BF16)
    acc_ref[...] += jnp.dot(act, wd_ref[...], preferred_element_type=F32)

    @pl.when(j == pl.num_programs(1) - 1)
    def _():
        y = x1_ref[MLP_HALO:, :] + acc_ref[...]
        y_ref[...] = _rms(y, fg_ref[...]) if final_norm else y


def _mixer_out_mlp(x, o, w_out, g, w_up, conv_w, conv_b, w_down, final_g=None):
    t, d = x.shape
    f = w_down.shape[0]
    q = o.shape[1]
    tm = _pick_tile(t, (512, 256, 128))
    tf = _pick_tile(f, (1408, 256, 128))
    nf = f // tf
    hb = tm // MLP_HALO
    halo_map = lambda i, j: (jnp.maximum(i * hb - 1, 0), 0)
    fg = g if final_g is None else final_g
    return pl.pallas_call(
        functools.partial(_mixer_out_mlp_kernel, final_g is not None),
        grid=(t // tm, nf),
        in_specs=[pl.BlockSpec((tm, d), lambda i, j: (i, 0)),
                  pl.BlockSpec((MLP_HALO, d), halo_map),
                  pl.BlockSpec((tm, q), lambda i, j: (i, 0)),
                  pl.BlockSpec((MLP_HALO, q), halo_map),
                  pl.BlockSpec((q, d), lambda i, j: (0, 0)),
                  pl.BlockSpec((1, d), lambda i, j: (0, 0)),
                  pl.BlockSpec((1, d), lambda i, j: (0, 0)),
                  pl.BlockSpec((d, tf), lambda i, j: (0, j)),
                  pl.BlockSpec((d, tf), lambda i, j: (0, nf + j)),
                  pl.BlockSpec((3, tf), lambda i, j: (0, j)),
                  pl.BlockSpec((3, tf), lambda i, j: (0, nf + j)),
                  pl.BlockSpec((1, tf), lambda i, j: (0, j)),
                  pl.BlockSpec((1, tf), lambda i, j: (0, nf + j)),
                  pl.BlockSpec((tf, d), lambda i, j: (j, 0))],
        out_specs=pl.BlockSpec((tm, d), lambda i, j: (i, 0)),
        out_shape=jax.ShapeDtypeStruct((t, d), F32),
        scratch_shapes=[pltpu.VMEM((tm + MLP_HALO, d), F32), pltpu.VMEM((tm + MLP_HALO, d), BF16),
                        pltpu.VMEM((tm, d), F32),
                        pltpu.VMEM((tm + MLP_HALO, tf), F32), pltpu.VMEM((tm + MLP_HALO, tf), F32)],
        compiler_params=_cparams(("parallel", "arbitrary")),
        name="mixer_out_mlp",
    )(x, x, o, o, w_out, g.reshape(1, d), fg.reshape(1, d), w_up, w_up, conv_w, conv_w, conv_b, conv_b, w_down)


def _swap_halves(q_pair):
    return pltpu.roll(q_pair.astype(F32), HEAD_DIM, 1).astype(BF16)


def _lane_half(shape):
    return lax.broadcasted_iota(jnp.int32, shape, 1) // HEAD_DIM


def _rows_to_heads(o_t, n_heads, tq):
    stacked = jnp.concatenate([o_t[:, r * tq:(r + 1) * tq] for r in range(n_heads)], axis=0)
    return stacked.T


def _swa_kernel(q_ref, kp_ref, kc_ref, vp_ref, vc_ref, bias_ref, sink_ref, o_ref):
    tq = q_ref.shape[0]
    k_ext = jnp.concatenate([kp_ref[...], kc_ref[...]], axis=0)
    v_ext = jnp.concatenate([vp_ref[0], vc_ref[0]], axis=1)
    half = _lane_half((tq, LANES))
    outs = []
    for g in range(SWA_KV_HEADS):
        parts = []
        for r in range(SWA_GROUP):
            hd = g * SWA_GROUP + r
            q_pair = q_ref[:, (hd // 2) * LANES:(hd // 2 + 1) * LANES]
            src = q_pair if hd % 2 == g else _swap_halves(q_pair)
            parts.append(jnp.where(half == g, src, jnp.zeros_like(src)))
        qp = jnp.concatenate(parts, axis=0)
        s = lax.dot_general(k_ext, qp, (((1,), (1,)), ((), ())), preferred_element_type=F32)
        s = s * (HEAD_DIM ** -0.5) + bias_ref[0, g]
        sink = sink_ref[g]
        m = jnp.maximum(jnp.max(s, axis=0, keepdims=True), sink)
        e = jnp.exp(s - m)
        l = jnp.sum(e, axis=0, keepdims=True) + jnp.exp(sink - m)
        o_t = jnp.dot(v_ext[g * HEAD_DIM:(g + 1) * HEAD_DIM], e.astype(BF16), preferred_element_type=F32)
        outs.append(_rows_to_heads(o_t / l, SWA_GROUP, tq))
    o_ref[...] = jnp.concatenate(outs, axis=1).astype(o_ref.dtype)


def _swa_bias():
    tq = SWA_TQ
    slopes = _alibi_slopes().reshape(SWA_KV_HEADS, SWA_GROUP)
    x = np.arange(2 * tq)[:, None] - tq
    q = np.arange(tq)[None, :]
    dist = q - x
    valid = (dist >= 0) & (dist < SWA_WINDOW)
    out = np.empty((2, SWA_KV_HEADS, 2 * tq, SWA_GROUP * tq), np.float32)
    for var in range(2):
        v = valid & ((x >= 0) | (var == 1))
        for g in range(SWA_KV_HEADS):
            for r in range(SWA_GROUP):
                out[var, g, :, r * tq:(r + 1) * tq] = np.where(v, -slopes[g, r] * dist, MASKED)
    return out


def _swa_attention(qkv, v_t, sinks):
    t = qkv.shape[0]
    tq = SWA_TQ
    kcol = Q_DIM // LANES
    bias = jnp.asarray(_swa_bias())
    sink_rows = jnp.repeat(sinks.astype(F32).reshape(SWA_KV_HEADS, 1, SWA_GROUP), tq, axis=2)
    prev = lambda i: jnp.maximum(i - 1, 0)
    return pl.pallas_call(
        _swa_kernel,
        grid=(t // tq,),
        in_specs=[pl.BlockSpec((tq, Q_DIM), lambda i: (i, 0)),
                  pl.BlockSpec((tq, LANES), lambda i: (prev(i), kcol)),
                  pl.BlockSpec((tq, LANES), lambda i: (i, kcol)),
                  pl.BlockSpec((1, LANES, tq), lambda i: (prev(i), 0, 0)),
                  pl.BlockSpec((1, LANES, tq), lambda i: (i, 0, 0)),
                  pl.BlockSpec((1,) + bias.shape[1:], lambda i: (jnp.minimum(i, 1), 0, 0, 0)),
                  pl.BlockSpec(sink_rows.shape, lambda i: (0, 0, 0))],
        out_specs=pl.BlockSpec((tq, Q_DIM), lambda i: (i, 0)),
        out_shape=jax.ShapeDtypeStruct((t, Q_DIM), BF16),
        compiler_params=_cparams(("parallel",)),
        name="swa_attention",
    )(qkv, qkv, qkv, v_t, v_t, bias, sink_rows)


def _swa_heads(x, norm_g, w_in, sinks):
    w = w_in.astype(BF16)
    n_qk = Q_DIM + SWA_KV_HEADS * HEAD_DIM
    qk, v_t = _norm_proj(x, norm_g, [w[:, :n_qk], w[:, n_qk:]], [BF16, BF16], chunks=[None, SWA_TQ])
    return _swa_attention(qk, v_t, sinks)


FOX_PARTS = 3


def _split_bf16(v):
    parts, rest = [], v
    for _ in range(FOX_PARTS):
        p = rest.astype(BF16)
        parts.append(p)
        rest = rest - p.astype(F32)
    return parts


def _fox_prep_kernel(f_ref, b_ref, aug_ref, cref_ref, carry_ref):
    tm = f_ref.shape[0]

    @pl.when(pl.program_id(0) == 0)
    def _():
        carry_ref[...] = jnp.zeros_like(carry_ref)

    logf = jax.nn.log_sigmoid(f_ref[...] + b_ref[...])
    row = lax.broadcasted_iota(jnp.int32, (tm, tm), 0)
    col = lax.broadcasted_iota(jnp.int32, (tm, tm), 1)
    tri = jnp.where(row >= col, 1.0, 0.0).astype(BF16)
    local = sum(jnp.dot(tri, p, preferred_element_type=F32) for p in _split_bf16(logf))
    cum = local + carry_ref[...]
    carry_ref[...] = cum[tm - 1:tm, :]
    first = cum[0:1, :]
    cref_ref[0] = jnp.broadcast_to(first, cref_ref.shape[1:])
    hi, mid, lo = _split_bf16(first - cum)
    lane = lax.broadcasted_iota(jnp.int32, (tm, LANES), 1)
    zero = jnp.zeros_like(hi)
    aug_ref[...] = jnp.where(lane < N_HEADS, hi, jnp.where(lane < 2 * N_HEADS, mid,
                                                            jnp.where(lane < 3 * N_HEADS, lo, zero)))


def _fox_prep(f_logit3, f_bias3):
    t = f_logit3.shape[0]
    nt = t // FOX_T
    return pl.pallas_call(
        _fox_prep_kernel,
        grid=(nt,),
        in_specs=[pl.BlockSpec((FOX_T, LANES), lambda i: (i, 0)), pl.BlockSpec((1, LANES), lambda i: (0, 0))],
        out_specs=[pl.BlockSpec((FOX_T, LANES), lambda i: (i, 0)), pl.BlockSpec((1, 8, LANES), lambda i: (i, 0, 0))],
        out_shape=[jax.ShapeDtypeStruct((t, LANES), BF16), jax.ShapeDtypeStruct((nt, 8, LANES), F32)],
        scratch_shapes=[pltpu.VMEM((1, LANES), F32)],
        compiler_params=_cparams(("arbitrary",)),
        name="fox_prep",
    )(f_logit3, f_bias3)


def _fox_kernel(cref_ref, q_ref, k_ref, aug_ref, vt_ref, o_ref, m_ref, l_ref, acc_ref, qt_ref, s_ref):
    pair, qi = pl.program_id(0), pl.program_id(1)
    tq = q_ref.shape[0]
    tk = FOX_T
    lane = lax.broadcasted_iota(jnp.int32, (tq, LANES), 1)
    q2 = q_ref[...] * jnp.asarray(HEAD_DIM ** -0.5, BF16)
    for hh in range(2):
        head = 2 * pair + hh
        pick = (lane % N_HEADS == head) & (lane < FOX_PARTS * N_HEADS)
        q_ext = jnp.concatenate([jnp.where(lane // HEAD_DIM == hh, q2, jnp.zeros_like(q2)).astype(F32),
                                 jnp.where(pick, 1.0, 0.0)], axis=1)
        qt_ref[hh] = q_ext.T.astype(BF16)
    m_ref[...] = jnp.full_like(m_ref, M_INIT)
    l_ref[...] = jnp.zeros_like(l_ref)
    acc_ref[...] = jnp.zeros_like(acc_ref)
    ratio = tq // tk

    def scores(kj, hh, q_from=0):
        start = pl.multiple_of(kj * tk, tk)
        k_ext = jnp.concatenate([k_ref[pl.ds(start, tk), :], aug_ref[pl.ds(start, tk), :]], axis=1)
        s_ref[hh, :, q_from:] = jnp.dot(k_ext, qt_ref[hh, :, q_from:], preferred_element_type=F32)

    def consume(kj, hh, key_offset):
        head = 2 * pair + hh
        q_from = 0 if key_offset is None else key_offset
        s = s_ref[hh, :, q_from:]
        if key_offset is not None:
            key = lax.broadcasted_iota(jnp.int32, s.shape, 0)
            qry = lax.broadcasted_iota(jnp.int32, s.shape, 1)
            s = jnp.where(key <= qry, s, MASKED)
        off = cref_ref[qi * ratio, head] - cref_ref[kj, head]
        m_old = m_ref[hh, :, q_from:]
        m_new = jnp.maximum(m_old, jnp.max(s, axis=0, keepdims=True) + off)
        e = jnp.exp(s - (m_new - off))
        alpha = jnp.exp(m_old - m_new)
        m_ref[hh, :, q_from:] = m_new
        l_ref[hh, :, q_from:] = alpha * l_ref[hh, :, q_from:] + jnp.sum(e, axis=0, keepdims=True)
        v_t = vt_ref[kj, hh * HEAD_DIM:(hh + 1) * HEAD_DIM, :]
        acc_ref[hh, :, q_from:] = (alpha * acc_ref[hh, :, q_from:]
                                   + jnp.dot(v_t, e.astype(BF16), preferred_element_type=F32))

    n_full = qi * ratio
    scores(0, 0)

    def body(j, carry):
        for d in range(ratio):
            kj = j * ratio + d
            scores(kj, 1)
            consume(kj, 0, None)
            scores(kj + 1, 0)
            consume(kj, 1, None)
        return carry

    lax.fori_loop(0, qi, body, 0)
    for d in range(ratio):
        kj = n_full + d
        scores(kj, 1, d * tk)
        consume(kj, 0, d * tk)
        if d + 1 < ratio:
            scores(kj + 1, 0, (d + 1) * tk)
        consume(kj, 1, d * tk)
    o_t = jnp.concatenate([acc_ref[hh] / l_ref[hh] for hh in range(2)], axis=0)
    o_ref[...] = o_t.T.astype(o_ref.dtype)


def _fox_attention(qkv, aug, v_t3, cref):
    t = qkv.shape[0]
    tq = FOX_TQ
    nk = t // FOX_T
    kcol = Q_DIM // LANES
    return pl.pallas_call(
        _fox_kernel,
        grid=(N_HEADS // 2, t // tq),
        in_specs=[pl.BlockSpec(memory_space=pltpu.SMEM),
                  pl.BlockSpec((tq, LANES), lambda p, i: (i, p)),
                  pl.BlockSpec((t, LANES), lambda p, i: (0, kcol + p)),
                  pl.BlockSpec((t, LANES), lambda p, i: (0, 0)),
                  pl.BlockSpec((nk, LANES, FOX_T), lambda p, i: (0, p, 0))],
        out_specs=pl.BlockSpec((tq, LANES), lambda p, i: (i, p)),
        out_shape=jax.ShapeDtypeStruct((t, Q_DIM), BF16),
        scratch_shapes=[pltpu.VMEM((2, 1, tq), F32), pltpu.VMEM((2, 1, tq), F32),
                        pltpu.VMEM((2, HEAD_DIM, tq), F32), pltpu.VMEM((2, 2 * LANES, tq), BF16), pltpu.VMEM((2, FOX_T, tq), F32)],
        compiler_params=_cparams(("parallel", "arbitrary")),
        name="fox_attention",
    )(cref, qkv, qkv, aug, v_t3)


def _fox_heads(x, norm_g, w_in, f_bias):
    t = x.shape[0]
    w_f = w_in[:, 3 * Q_DIM:]
    pad = jnp.zeros((D_MODEL, LANES - FOX_PARTS * N_HEADS), w_f.dtype)
    w_f3 = jnp.concatenate([w_f] * FOX_PARTS + [pad], axis=1).astype(BF16)
    b3 = jnp.concatenate([f_bias.astype(F32)] * FOX_PARTS + [jnp.zeros((LANES - FOX_PARTS * N_HEADS,), F32)])
    w = w_in.astype(BF16)
    qk, v_t3, f_logit3 = _norm_proj(x, norm_g, [w[:, :2 * Q_DIM], w[:, 2 * Q_DIM:3 * Q_DIM], w_f3], [BF16, BF16, F32],
                                    chunks=[None, FOX_T, None])
    aug, cref = _fox_prep(f_logit3, b3.reshape(1, LANES))
    return _fox_attention(qk, aug, v_t3, cref[:, 0, :N_HEADS])


def _nsa_compress_kernel(a_ref, pos_ref, w1_ref, w2_ref, o_ref):
    tc = a_ref.shape[1]
    half = a_ref.shape[2]
    a = a_ref[0]
    pos = pos_ref[0]
    top = jnp.dot((a + pos[:, :half]).astype(BF16), w1_ref[0, :half], preferred_element_type=F32)
    bot = jnp.dot((a + pos[:, half:]).astype(BF16), w1_ref[0, half:], preferred_element_type=F32)
    hid = top + pltpu.roll(bot, tc - 1, 0)
    o_ref[0] = jnp.dot(jax.nn.gelu(hid).astype(BF16), w2_ref[0], preferred_element_type=F32)


def _nsa_compress(a, pos, w1, w2):
    _, tc, width = a.shape
    return pl.pallas_call(
        _nsa_compress_kernel,
        grid=(2, NSA_KV_HEADS // 2),
        in_specs=[pl.BlockSpec((1, tc, width), lambda kv, gp: (kv * (NSA_KV_HEADS // 2) + gp, 0, 0)),
                  pl.BlockSpec((1, 1, 2 * width), lambda kv, gp: (kv, 0, 0)),
                  pl.BlockSpec((1, 2 * width, 2 * CMP_HIDDEN), lambda kv, gp: (kv, 0, 0)),
                  pl.BlockSpec((1, 2 * CMP_HIDDEN, LANES), lambda kv, gp: (kv, 0, 0))],
        out_specs=pl.BlockSpec((1, tc, LANES), lambda kv, gp: (kv, 0, gp)),
        out_shape=jax.ShapeDtypeStruct((2, tc, NSA_KV_DIM), F32),
        compiler_params=_cparams(("parallel", "parallel")),
        name="nsa_compress",
    )(a, pos, w1, w2)


NSA_WQ = 256
NSA_WIN_TILES = (NSA_WINDOW + NSA_TQ) // NSA_WQ


def _nsa_kernel(slopes_ref, q_ref, ks_ref, *refs):
    kw_refs, refs = refs[:NSA_WIN_TILES], refs[NSA_WIN_TILES:]
    vst_ref, refs = refs[0], refs[1:]
    vw_refs, refs = refs[:NSA_WIN_TILES], refs[NSA_WIN_TILES:]
    (kcmp_ref, vov_ref, gates_ref, ksel_ext_ref, kwin_ext_ref, kcmp_ext_ref, qext_ref,
     o_ref, selt_ref, m_ref, l_ref, acc_ref, qt_ref, s_ref, scmp_ref, cm_ref, cl_ref, oc_ref,
     active_ref, list_ref) = refs
    g, qi = pl.program_id(0), pl.program_id(1)
    tq = q_ref.shape[0]
    tc = kcmp_ref.shape[0]
    n_sel = selt_ref.shape[0]
    rr = NSA_GROUP
    t0 = qi * tq
    hg = g % 2
    slope = [slopes_ref[g * rr + r] for r in range(rr)]

    half = _lane_half((tq, LANES))
    parts = []
    for r in range(rr):
        q_pair = q_ref[:, (r // 2) * LANES:(r // 2 + 1) * LANES]
        src = jnp.where(hg == r % 2, q_pair, _swap_halves(q_pair))
        parts.append(jnp.where(half == hg, src, jnp.zeros_like(src)) * jnp.asarray(HEAD_DIM ** -0.5, BF16))
    qp = jnp.concatenate(parts, axis=0)
    qt_ref[0:LANES, :] = qp.astype(F32).T.astype(BF16)

    @pl.when(qi == 0)
    def _():
        qt_ref[LANES:, :] = qext_ref[0]

    def lanes_of(s, r):
        return s[:, r * tq:(r + 1) * tq]

    chunk_span = NSA_CMP_CHUNK * CMP_STRIDE
    last_chunk = ((t0 + tq - CMP_BLOCK) // CMP_STRIDE) // NSA_CMP_CHUNK
    nl16_q = (CMP_STRIDE * lax.broadcasted_iota(jnp.int32, (NSA_CMP_CHUNK, tq), 0)
              - lax.broadcasted_iota(jnp.int32, (NSA_CMP_CHUNK, tq), 1)).astype(F32)
    cm_ref[...] = jnp.full_like(cm_ref, M_INIT)

    def cmp_rows(ch):
        return pl.ds(pl.multiple_of(ch * NSA_CMP_CHUNK, NSA_CMP_CHUNK), NSA_CMP_CHUNK)

    def cmp_offset(ch, r):
        return slope[r] * (ch * chunk_span).astype(F32)

    def cmp_scores(ch, masked):
        k_full = jnp.concatenate([kcmp_ref[cmp_rows(ch), :], kcmp_ext_ref[...]], axis=1)
        s = jnp.dot(k_full, qt_ref[...], preferred_element_type=F32)
        if masked:
            visible = nl16_q <= (t0 - (CMP_BLOCK - 1) - ch * chunk_span).astype(F32)
            s = jnp.concatenate([jnp.where(visible, lanes_of(s, r), MASKED) for r in range(rr)], axis=1)
        scmp_ref[cmp_rows(ch), :] = s
        for r in range(rr):
            cm_ref[r] = jnp.maximum(cm_ref[r], jnp.max(lanes_of(s, r), axis=0, keepdims=True) + cmp_offset(ch, r))

    def cmp_plain(ch, carry):
        cmp_scores(ch, False)
        return carry

    lax.fori_loop(0, jnp.maximum(last_chunk - 1, 0), cmp_plain, 0)

    @pl.when(last_chunk >= 1)
    def _():
        cmp_scores(last_chunk - 1, True)

    cmp_scores(last_chunk, True)

    cl_ref[...] = jnp.zeros_like(cl_ref)
    oc_ref[...] = jnp.zeros_like(oc_ref)

    def cmp_exp(ch, carry):
        s = scmp_ref[cmp_rows(ch), :]
        e_all = []
        for r in range(rr):
            e = jnp.exp(lanes_of(s, r) - (cm_ref[r] - cmp_offset(ch, r)))
            cl_ref[r] = cl_ref[r] + jnp.sum(e, axis=0, keepdims=True)
            e_all.append(e.astype(BF16))
        oc_ref[...] += jnp.dot(vov_ref[ch, 0], jnp.concatenate(e_all, axis=1), preferred_element_type=F32)
        return carry

    lax.fori_loop(0, last_chunk + 1, cmp_exp, 0)
    inv_l = [jnp.where(cl_ref[r] > 0.0, 1.0 / cl_ref[r], 0.0) for r in range(rr)]
    o_cmp = [oc_ref[0:HEAD_DIM, r * tq:(r + 1) * tq] * inv_l[r] for r in range(rr)]

    imp = functools.reduce(jnp.add, [oc_ref[HEAD_DIM:, r * tq:(r + 1) * tq] * inv_l[r] for r in range(rr)])
    blk = lax.broadcasted_iota(jnp.int32, (n_sel, tq), 0)
    cur = (t0 + lax.broadcasted_iota(jnp.int32, (n_sel, tq), 1)) // SEL_BLOCK
    blk_f = blk.astype(F32)
    forced = jnp.where(blk == 0, 1.0, jnp.where(blk == cur, 1.0, jnp.where(blk == cur - 1, 1.0, 0.0)))
    score = jnp.where(forced > 0.0, TAKEN, jnp.where(blk <= cur, imp, -1.0))

    selt_ref[...] = score
    row_step = min(n_sel, NSA_TOPK_ROW_STEP)
    variant = ((t0 + tq - 1) // SEL_BLOCK) // row_step

    for v in range(n_sel // row_step):
        rows = row_step * (v + 1)

        @pl.when(variant == v)
        def _(rows=rows):
            ids = lax.broadcasted_iota(jnp.int32, (rows, tq), 0).astype(F32)

            def pick(_, sc):
                best = jnp.max(sc, axis=0, keepdims=True)
                first = jnp.min(jnp.where(sc == best, ids, float(n_sel)), axis=0, keepdims=True)
                return jnp.where(ids == first, TAKEN, sc)

            selt_ref[0:rows, :] = lax.fori_loop(0, N_SELECT - 3, pick, selt_ref[0:rows, :])

    selt_ref[...] = jnp.where(blk <= cur, jnp.where(selt_ref[...] == TAKEN, 1.0, 0.0), 0.0)

    m_ref[...] = jnp.full_like(m_ref, M_INIT)
    l_ref[...] = jnp.zeros_like(l_ref)
    acc_ref[...] = jnp.zeros_like(acc_ref)
    blocks_per_chunk = NSA_TK // SEL_BLOCK

    for c in range(n_sel // blocks_per_chunk):
        active_ref[c] = jnp.max(selt_ref[c * blocks_per_chunk:(c + 1) * blocks_per_chunk, :])

    def compact(c, n):
        list_ref[n] = c
        return n + (active_ref[c] > 0.0).astype(jnp.int32)

    n_active = lax.fori_loop(0, t0 // NSA_TK + 1, compact, 0)

    def sel_scores(c, buf):
        start = pl.multiple_of(c * NSA_TK, NSA_TK)
        rows = selt_ref[pl.ds(pl.multiple_of(c * blocks_per_chunk, blocks_per_chunk), blocks_per_chunk), :]
        mask_rows = jnp.concatenate([jnp.where(rows > 0.5, 0.0, MASKED), jnp.zeros_like(rows)], axis=0)
        first = LANES + buf * NSA_EXT_MASK_ROWS
        qt_ref[first:first + NSA_EXT_MASK_ROWS, :] = jnp.concatenate([mask_rows] * rr, axis=1).astype(BF16)
        k_full = jnp.concatenate([ks_ref[pl.ds(start, NSA_TK), :], ksel_ext_ref[buf]], axis=1)
        s_ref[buf] = jnp.dot(k_full, qt_ref[...], preferred_element_type=F32)

    def sel_consume(c, buf, diagonal):
        shift = (c * NSA_TK - t0).astype(F32)
        if diagonal:
            key = lax.broadcasted_iota(jnp.int32, (NSA_TK, tq), 0) + c * NSA_TK
            visible = key <= lax.broadcasted_iota(jnp.int32, (NSA_TK, tq), 1) + t0
        e_all, alpha_all = [], []
        for r in range(rr):
            off = slope[r] * shift
            s = s_ref[buf][:, r * tq:(r + 1) * tq]
            if diagonal:
                s = jnp.where(visible, s, MASKED)
            m_old = m_ref[r]
            m_new = jnp.maximum(m_old, jnp.max(s, axis=0, keepdims=True) + off)
            e = jnp.exp(s - (m_new - off))
            alpha = jnp.exp(m_old - m_new)
            m_ref[r] = m_new
            l_ref[r] = alpha * l_ref[r] + jnp.sum(e, axis=0, keepdims=True)
            e_all.append(e.astype(BF16))
            alpha_all.append(alpha)
        pv = jnp.dot(vst_ref[c], jnp.concatenate(e_all, axis=1), preferred_element_type=F32)
        acc_ref[...] = jnp.concatenate(alpha_all, axis=1) * acc_ref[...] + pv

    n_before = n_active - 1
    sel_scores(list_ref[0], 0)

    def sel_pair(j, carry):
        c0, c1, c2 = list_ref[2 * j], list_ref[2 * j + 1], list_ref[2 * j + 2]
        sel_scores(c1, 1)
        sel_consume(c0, 0, False)
        sel_scores(c2, 0)
        sel_consume(c1, 1, False)
        return carry

    lax.fori_loop(0, n_before // 2, sel_pair, 0)

    @pl.when(n_before % 2 == 1)
    def _():
        c0, c_last = list_ref[n_before - 1], list_ref[n_before]
        sel_scores(c_last, 1)
        sel_consume(c0, 0, False)
        sel_consume(c_last, 1, True)

    @pl.when(n_before % 2 == 0)
    def _():
        sel_consume(list_ref[n_before], 0, True)

    wq = NSA_WQ
    n_back = NSA_WINDOW // wq
    xw = lax.broadcasted_iota(jnp.int32, (wq, wq), 0)
    qw = lax.broadcasted_iota(jnp.int32, (wq, wq), 1)
    k_tiles = [jnp.concatenate([kw_refs[j][...], kwin_ext_ref[...]], axis=1) for j in range(NSA_WIN_TILES)]
    o_win = [[] for _ in range(rr)]
    for sub in range(tq // wq):
        q_cols = jnp.concatenate([qt_ref[:, r * tq + sub * wq:r * tq + (sub + 1) * wq] for r in range(rr)], axis=1)
        s_win, off_win = [], []
        for b in range(n_back + 1):
            back = n_back - b
            s_t = jnp.dot(k_tiles[sub + b], q_cols, preferred_element_type=F32)
            per_head = [s_t[:, r * wq:(r + 1) * wq] for r in range(rr)]
            if back == n_back:
                per_head = [jnp.where(xw > qw, s, MASKED) for s in per_head]
            if back == 0:
                per_head = [jnp.where(xw <= qw, s, MASKED) for s in per_head]
            s_win.append(per_head)
            in_range = qi * (tq // wq) + sub >= back
            off_win.append([jnp.where(in_range, -slope[r] * float(back * wq), MASKED) for r in range(rr)])
        e_win, l_win = [], []
        for r in range(rr):
            m = functools.reduce(jnp.maximum, [jnp.max(s_win[b][r], axis=0, keepdims=True) + off_win[b][r]
                                               for b in range(n_back + 1)])
            e_tiles = [jnp.exp(s_win[b][r] - (m - off_win[b][r])) for b in range(n_back + 1)]
            l_win.append(functools.reduce(jnp.add, [jnp.sum(e, axis=0, keepdims=True) for e in e_tiles]))
            e_win.append(jnp.concatenate([e.astype(BF16) for e in e_tiles], axis=0))
        v_win = jnp.concatenate([vw_refs[sub + b][0] for b in range(n_back + 1)], axis=1)
        o_sub = jnp.dot(v_win, jnp.concatenate(e_win, axis=1), preferred_element_type=F32)
        for r in range(rr):
            o_win[r].append(o_sub[:, r * wq:(r + 1) * wq] / l_win[r])
    o_win = [jnp.concatenate(parts, axis=1) for parts in o_win]

    gate = jax.nn.sigmoid(gates_ref[0])
    merged = []
    for r in range(rr):
        o_sel = lanes_of(acc_ref[...], r) / l_ref[r]
        merged.append(gate[3 * r:3 * r + 1] * o_cmp[r] + gate[3 * r + 1:3 * r + 2] * o_sel
                      + gate[3 * r + 2:3 * r + 3] * o_win[r])
    o_ref[...] = jnp.concatenate(merged, axis=0).T.astype(o_ref.dtype)


def _nsa_overlap_t(n_sel, tc):
    cmp_start = np.arange(tc - 1) * CMP_STRIDE
    sel_start = np.arange(n_sel) * SEL_BLOCK
    ov = np.clip(np.minimum(cmp_start[None, :] + CMP_BLOCK, sel_start[:, None] + SEL_BLOCK)
                 - np.maximum(cmp_start[None, :], sel_start[:, None]), 0, None) / CMP_BLOCK
    return np.concatenate([ov, np.zeros((n_sel, 1))], axis=1).astype(np.float32)


NSA_EXT_MASK_ROWS = 16
NSA_EXT_SEL = 32
NSA_EXT_WIN = 48
NSA_EXT_CMP = 64
NSA_CMP_CHUNK = 256
NSA_TOPK_ROW_STEP = 64
SLOPE_PIECES = 3


def _bf16_pieces(v):
    out, rest = [], np.asarray(v, np.float32)
    for _ in range(SLOPE_PIECES):
        p = rest.astype(BF16).astype(np.float32)
        out.append(p)
        rest = rest - p
    return out


def _nsa_extensions(tq):
    ksel = np.zeros((2, NSA_TK, LANES), np.float32)
    x = np.arange(NSA_TK)
    for buf in range(2):
        ksel[buf, x, buf * NSA_EXT_MASK_ROWS + x // SEL_BLOCK] = 1.0
    for j in range(SLOPE_PIECES):
        ksel[:, :, NSA_EXT_SEL + j] = 256 * (x // 256)
        ksel[:, :, NSA_EXT_SEL + SLOPE_PIECES + j] = x % 256
    kwin = np.zeros((NSA_WQ, LANES), np.float32)
    xw = np.arange(NSA_WQ)
    for j in range(SLOPE_PIECES):
        kwin[:, NSA_EXT_WIN + j] = 256 * (xw // 256)
        kwin[:, NSA_EXT_WIN + SLOPE_PIECES + j] = xw % 256
    kcmp = np.zeros((NSA_CMP_CHUNK, LANES), np.float32)
    for j in range(SLOPE_PIECES):
        kcmp[:, NSA_EXT_CMP + j] = np.arange(NSA_CMP_CHUNK)
    pieces = _bf16_pieces(_alibi_slopes())
    qext = np.zeros((NSA_KV_HEADS, LANES, NSA_GROUP * tq), np.float32)
    for g in range(NSA_KV_HEADS):
        for r in range(NSA_GROUP):
            cols = slice(r * tq, (r + 1) * tq)
            for j in range(SLOPE_PIECES):
                p = pieces[j][g * NSA_GROUP + r]
                qext[g, NSA_EXT_SEL + j, cols] = p
                qext[g, NSA_EXT_SEL + SLOPE_PIECES + j, cols] = p
                qext[g, NSA_EXT_WIN + j, cols] = p
                qext[g, NSA_EXT_WIN + SLOPE_PIECES + j, cols] = p
                qext[g, NSA_EXT_CMP + j, cols] = CMP_STRIDE * p
    return jnp.asarray(ksel, BF16), jnp.asarray(kwin, BF16), jnp.asarray(kcmp, BF16), jnp.asarray(qext, BF16)


def _nsa_attention(main, vs_t3, vw_t, k_cmp, v_cmp_t, gates_t):
    t = main.shape[0]
    tq = NSA_TQ
    tc = k_cmp.shape[0]
    n_sel = t // SEL_BLOCK
    nc = t // NSA_TK
    ks_col = Q_DIM // LANES
    kw_col = ks_col + NSA_KV_DIM // LANES
    n_ch = tc // NSA_CMP_CHUNK
    ov_t = jnp.asarray(_nsa_overlap_t(n_sel, tc).reshape(n_sel, n_ch, NSA_CMP_CHUNK).transpose(1, 0, 2), BF16)
    v_cmp_t = v_cmp_t.reshape(NSA_KV_HEADS, HEAD_DIM, n_ch, NSA_CMP_CHUNK).transpose(2, 0, 1, 3)
    vov = jnp.concatenate([v_cmp_t, jnp.broadcast_to(ov_t[:, None], (n_ch, NSA_KV_HEADS, n_sel, NSA_CMP_CHUNK))],
                          axis=2)
    slopes = jnp.asarray(_alibi_slopes())
    ksel_ext, kwin_ext, kcmp_ext, qext = _nsa_extensions(tq)

    def win_tile(i, j):
        return jnp.maximum(i * (tq // NSA_WQ) - NSA_WINDOW // NSA_WQ + j, 0)

    def kw_spec(j):
        return pl.BlockSpec((NSA_WQ, LANES), lambda g, i: (win_tile(i, j), kw_col + g // 2))

    def vw_spec(j):
        return pl.BlockSpec((1, HEAD_DIM, NSA_WQ), lambda g, i: (win_tile(i, j), g, 0))

    backs = list(range(NSA_WIN_TILES))
    return pl.pallas_call(
        _nsa_kernel,
        grid=(NSA_KV_HEADS, t // tq),
        in_specs=[pl.BlockSpec(memory_space=pltpu.SMEM),
                  pl.BlockSpec((tq, NSA_GROUP * HEAD_DIM), lambda g, i: (i, g)),
                  pl.BlockSpec((t, LANES), lambda g, i: (0, ks_col + g // 2))]
                 + [kw_spec(b) for b in backs]
                 + [pl.BlockSpec((nc, HEAD_DIM, NSA_TK), lambda g, i: (0, g, 0))]
                 + [vw_spec(b) for b in backs]
                 + [pl.BlockSpec((tc, LANES), lambda g, i: (0, g // 2)),
                    pl.BlockSpec((n_ch, 1, HEAD_DIM + n_sel, NSA_CMP_CHUNK), lambda g, i: (0, g, 0, 0)),
                    pl.BlockSpec((1, 16, tq), lambda g, i: (g, 0, i)),
                    pl.BlockSpec(ksel_ext.shape, lambda g, i: (0, 0, 0)),
                    pl.BlockSpec(kwin_ext.shape, lambda g, i: (0, 0)),
                    pl.BlockSpec(kcmp_ext.shape, lambda g, i: (0, 0)),
                    pl.BlockSpec((1,) + qext.shape[1:], lambda g, i: (g, 0, 0))],
        out_specs=pl.BlockSpec((tq, NSA_GROUP * HEAD_DIM), lambda g, i: (i, g)),
        out_shape=jax.ShapeDtypeStruct((t, Q_DIM), BF16),
        scratch_shapes=[pltpu.VMEM((n_sel, tq), F32), pltpu.VMEM((NSA_GROUP, 1, tq), F32),
                        pltpu.VMEM((NSA_GROUP, 1, tq), F32), pltpu.VMEM((HEAD_DIM, NSA_GROUP * tq), F32),
                        pltpu.VMEM((2 * LANES, NSA_GROUP * tq), BF16),
                        pltpu.VMEM((2, NSA_TK, NSA_GROUP * tq), F32),
                        pltpu.VMEM((tc, NSA_GROUP * tq), F32), pltpu.VMEM((NSA_GROUP, 1, tq), F32),
                        pltpu.VMEM((NSA_GROUP, 1, tq), F32), pltpu.VMEM((HEAD_DIM + n_sel, NSA_GROUP * tq), F32),
                        pltpu.SMEM((nc,), F32), pltpu.SMEM((nc,), jnp.int32)],
        compiler_params=_cparams(("parallel", "arbitrary")),
        name="nsa_attention",
    )(slopes, main, main, *([main] * NSA_WIN_TILES), vs_t3, *([vw_t] * NSA_WIN_TILES), k_cmp, vov, gates_t,
      ksel_ext, kwin_ext, kcmp_ext, qext)


def _nsa_heads(x, norm_g, w_in, cmp_pos, cmp_w1, cmp_w2):
    t = x.shape[0]
    kv = NSA_KV_DIM
    col = lambda i: w_in[:, Q_DIM + i * kv:Q_DIM + (i + 1) * kv]
    w_main = jnp.concatenate([w_in[:, :Q_DIM], col(2), col(4)], axis=1).astype(BF16)
    w_cmp = jnp.concatenate([col(0), col(1)], axis=1).astype(BF16)
    n_gate = 3 * N_HEADS
    w_gate = jnp.concatenate([w_in[:, Q_DIM + 6 * kv:], jnp.zeros((D_MODEL, LANES - n_gate), w_in.dtype)],
                             axis=1).astype(BF16)
    main, vs_t3, vw_t3, kcvc, gate_logit = _norm_proj(
        x, norm_g, [w_main, col(3).astype(BF16), col(5).astype(BF16), w_cmp, w_gate], [BF16, BF16, BF16, F32, F32],
        chunks=[None, NSA_TK, NSA_WQ, "cols", None])

    tc = t // CMP_STRIDE
    a = kcvc.reshape(kcvc.shape[0], tc, CMP_STRIDE * LANES)
    pair = jnp.eye(2, dtype=cmp_w1.dtype)
    pos = jnp.broadcast_to(cmp_pos[:, :, None, :], (2, CMP_BLOCK, 2, HEAD_DIM)).reshape(2, 1, 2 * CMP_BLOCK * HEAD_DIM)
    w1 = (cmp_w1.reshape(2, CMP_BLOCK, 1, HEAD_DIM, 1, CMP_HIDDEN) * pair.reshape(1, 1, 2, 1, 2, 1))
    w1 = w1.reshape(2, 2 * CMP_BLOCK * HEAD_DIM, 2 * CMP_HIDDEN)
    w2 = (cmp_w2.reshape(2, 1, CMP_HIDDEN, 1, HEAD_DIM) * pair.reshape(1, 2, 1, 2, 1))
    w2 = w2.reshape(2, 2 * CMP_HIDDEN, 2 * HEAD_DIM)
    cmp = _nsa_compress(a, pos, w1.astype(BF16), w2.astype(BF16))
    k_cmp = cmp[0].astype(BF16)
    v_cmp_t = cmp[1].T.astype(BF16)

    gates_t = gate_logit[:, :n_gate].reshape(t, NSA_KV_HEADS, 3 * NSA_GROUP).transpose(1, 2, 0)
    gates_t = jnp.pad(gates_t, ((0, 0), (0, 16 - 3 * NSA_GROUP), (0, 0)))
    return _nsa_attention(main, vs_t3, vw_t3, k_cmp, v_cmp_t, gates_t)


def kernel(x, attn_norm, mlp_norm, final_norm, nsa_w_in, nsa_cmp_pos, nsa_cmp_w1, nsa_cmp_w2, nsa_w_out,
           swa_w_in, swa_sinks, swa_w_out, fox_w_in, fox_f_bias, fox_w_out,
           mlp_w_up, mlp_conv_w, mlp_conv_b, mlp_w_down):
    assert x.shape[0] == 1, "the trunk is written for batch 1"
    h = x[0]
    for i in range(DEPTH):
        kind, j = i % N_MIXERS, i // N_MIXERS
        if kind == 0:
            o = _nsa_heads(h, attn_norm[i], nsa_w_in[j], nsa_cmp_pos[j], nsa_cmp_w1[j], nsa_cmp_w2[j])
            w_out = nsa_w_out[j]
        elif kind == 1:
            o = _swa_heads(h, attn_norm[i], swa_w_in[j], swa_sinks[j])
            w_out = swa_w_out[j]
        else:
            o = _fox_heads(h, attn_norm[i], fox_w_in[j], fox_f_bias[j])
            w_out = fox_w_out[j]
        h = _mixer_out_mlp(h, o, w_out.astype(BF16), mlp_norm[i], mlp_w_up[i].astype(BF16), mlp_conv_w[i],
                           mlp_conv_b[i].reshape(1, -1), mlp_w_down[i].astype(BF16),
                           final_g=final_norm if i == DEPTH - 1 else None)
    return h[None]
```

```python
import functools

import numpy as np
import jax
import jax.numpy as jnp
from jax import lax
from jax.experimental import pallas as pl
from jax.experimental.pallas import tpu as pltpu

F32 = jnp.float32
BF16 = jnp.bfloat16

D_MODEL = 1024
DEPTH = 4
N_MIXERS = 3
HEAD_DIM = 64
N_HEADS = 16
Q_DIM = N_HEADS * HEAD_DIM
ALIBI_MAX = 8.0
NORM_EPS = 1e-6
MASKED = -2e30
M_INIT = -1e30
TAKEN = -(2.0 ** 127)

NSA_KV_HEADS = 4
NSA_GROUP = 4
NSA_KV_DIM = NSA_KV_HEADS * HEAD_DIM
CMP_BLOCK = 32
CMP_STRIDE = 16
CMP_HIDDEN = 256
SEL_BLOCK = 64
N_SELECT = 16
NSA_WINDOW = 512
NSA_TQ = 512
NSA_TK = 512

SWA_KV_HEADS = 2
SWA_GROUP = 8
SWA_WINDOW = 128
SWA_TQ = 128

FOX_TQ = 1024
FOX_T = 512

LANES = 128
VMEM_LIMIT = 56 * 1024 * 1024


def _cparams(semantics, vmem=VMEM_LIMIT):
    return pltpu.CompilerParams(dimension_semantics=semantics, vmem_limit_bytes=vmem)


def _alibi_slopes():
    return np.asarray(2.0 ** (-ALIBI_MAX * np.arange(1, N_HEADS + 1) / N_HEADS), dtype=np.float32)


def _rms(x, g):
    ms = jnp.mean(x * x, axis=-1, keepdims=True)
    return x * lax.rsqrt(ms + NORM_EPS) * g


def _norm_proj_kernel(chunks, x_ref, g_ref, *refs):
    n_out = len(chunks)
    h = _rms(x_ref[...], g_ref[...]).astype(BF16)
    for w_ref, o_ref, chunk in zip(refs[:n_out], refs[n_out:], chunks):
        res = jnp.dot(h, w_ref[...], preferred_element_type=F32)
        if chunk is None:
            o_ref[...] = res.astype(o_ref.dtype)
        elif chunk == "cols":
            for c in range(o_ref.shape[0]):
                o_ref[c] = res[:, c * LANES:(c + 1) * LANES].astype(o_ref.dtype)
        else:
            for c in range(res.shape[0] // chunk):
                o_ref[c] = res[c * chunk:(c + 1) * chunk].T.astype(o_ref.dtype)


def _pick_tile(n, candidates):
    for c in candidates:
        if n % c == 0:
            return c
    raise ValueError(f"no tile for {n}")


def _norm_proj(x, g, weights, out_dtypes, chunks=None):
    t, d = x.shape
    tm = _pick_tile(t, (1024, 512, 256, 128))
    chunks = tuple(chunks) if chunks else (None,) * len(weights)
    out_specs, out_shape = [], []
    for w, dt, c in zip(weights, out_dtypes, chunks):
        n = w.shape[1]
        if c is None:
            out_specs.append(pl.BlockSpec((tm, n), lambda i: (i, 0)))
            out_shape.append(jax.ShapeDtypeStruct((t, n), dt))
        elif c == "cols":
            out_specs.append(pl.BlockSpec((n // LANES, tm, LANES), lambda i: (0, i, 0)))
            out_shape.append(jax.ShapeDtypeStruct((n // LANES, t, LANES), dt))
        else:
            out_specs.append(pl.BlockSpec((tm // c, n, c), lambda i: (i, 0, 0)))
            out_shape.append(jax.ShapeDtypeStruct((t // c, n, c), dt))
    return pl.pallas_call(
        functools.partial(_norm_proj_kernel, chunks),
        grid=(t // tm,),
        in_specs=[pl.BlockSpec((tm, d), lambda i: (i, 0)), pl.BlockSpec((1, d), lambda i: (0, 0))]
                 + [pl.BlockSpec(w.shape, lambda i: (0, 0)) for w in weights],
        out_specs=out_specs,
        out_shape=out_shape,
        compiler_params=_cparams(("parallel",)),
        name="norm_proj",
    )(x, g.reshape(1, d), *weights)


MLP_HALO = 16


def _mixer_out_mlp_kernel(final_norm, x_ref, xh_ref, o_ref, oh_ref, wo_ref, g_ref, fg_ref, wa_ref, wg_ref,
                          cwa_ref, cwg_ref, cba_ref, cbg_ref, wd_ref, y_ref, x1_ref, h_ref, acc_ref, ua_ref, ug_ref):
    i, j = pl.program_id(0), pl.program_id(1)
    tm = x_ref.shape[0]

    @pl.when(j == 0)
    def _():
        wo = wo_ref[...]
        x1_ref[0:MLP_HALO, :] = xh_ref[...] + jnp.dot(oh_ref[...], wo, preferred_element_type=F32)
        x1_ref[MLP_HALO:, :] = x_ref[...] + jnp.dot(o_ref[...], wo, preferred_element_type=F32)
        halo = _rms(x1_ref[0:MLP_HALO, :], g_ref[...])
        h_ref[0:MLP_HALO, :] = jnp.where(i > 0, halo, 0.0).astype(BF16)
        h_ref[MLP_HALO:, :] = _rms(x1_ref[MLP_HALO:, :], g_ref[...]).astype(BF16)
        acc_ref[...] = jnp.zeros_like(acc_ref)

    h = h_ref[...]

    def conv(w_ref, cw_ref, cb_ref, u_ref):
        u_ref[...] = jnp.dot(h, w_ref[...], preferred_element_type=F32)
        cw = cw_ref[...]
        return (cw[0:1] * u_ref[MLP_HALO - 2:MLP_HALO - 2 + tm, :] + cw[1:2] * u_ref[MLP_HALO - 1:MLP_HALO - 1 + tm, :]
                + cw[2:3] * u_ref[MLP_HALO:MLP_HALO + tm, :] + cb_ref[...])

    a = conv(wa_ref, cwa_ref, cba_ref, ua_ref)
    gt = conv(wg_ref, cwg_ref, cbg_ref, ug_ref)
    act = (jax.nn.silu(gt) * a).astype(BF16)
    acc_ref[...] += jnp.dot(act, wd_ref[...], preferred_element_type=F32)

    @pl.when(j == pl.num_programs(1) - 1)
    def _():
        y = x1_ref[MLP_HALO:, :] + acc_ref[...]
        y_ref[...] = _rms(y, fg_ref[...]) if final_norm else y


def _mixer_out_mlp(x, o, w_out, g, w_up, conv_w, conv_b, w_down, final_g=None):
    t, d = x.shape
    f = w_down.shape[0]
    q = o.shape[1]
    tm = _pick_tile(t, (512, 256, 128))
    tf = _pick_tile(f, (1408, 256, 128))
    nf = f // tf
    hb = tm // MLP_HALO
    halo_map = lambda i, j: (jnp.maximum(i * hb - 1, 0), 0)
    fg = g if final_g is None else final_g
    return pl.pallas_call(
        functools.partial(_mixer_out_mlp_kernel, final_g is not None),
        grid=(t // tm, nf),
        in_specs=[pl.BlockSpec((tm, d), lambda i, j: (i, 0)),
                  pl.BlockSpec((MLP_HALO, d), halo_map),
                  pl.BlockSpec((tm, q), lambda i, j: (i, 0)),
                  pl.BlockSpec((MLP_HALO, q), halo_map),
                  pl.BlockSpec((q, d), lambda i, j: (0, 0)),
                  pl.BlockSpec((1, d), lambda i, j: (0, 0)),
                  pl.BlockSpec((1, d), lambda i, j: (0, 0)),
                  pl.BlockSpec((d, tf), lambda i, j: (0, j)),
                  pl.BlockSpec((d, tf), lambda i, j: (0, nf + j)),
                  pl.BlockSpec((3, tf), lambda i, j: (0, j)),
                  pl.BlockSpec((3, tf), lambda i, j: (0, nf + j)),
                  pl.BlockSpec((1, tf), lambda i, j: (0, j)),
                  pl.BlockSpec((1, tf), lambda i, j: (0, nf + j)),
                  pl.BlockSpec((tf, d), lambda i, j: (j, 0))],
        out_specs=pl.BlockSpec((tm, d), lambda i, j: (i, 0)),
        out_shape=jax.ShapeDtypeStruct((t, d), F32),
        scratch_shapes=[pltpu.VMEM((tm + MLP_HALO, d), F32), pltpu.VMEM((tm + MLP_HALO, d), BF16),
                        pltpu.VMEM((tm, d), F32),
                        pltpu.VMEM((tm + MLP_HALO, tf), F32), pltpu.VMEM((tm + MLP_HALO, tf), F32)],
        compiler_params=_cparams(("parallel", "arbitrary")),
        name="mixer_out_mlp",
    )(x, x, o, o, w_out, g.reshape(1, d), fg.reshape(1, d), w_up, w_up, conv_w, conv_w, conv_b, conv_b, w_down)


def _swap_halves(q_pair):
    return pltpu.roll(q_pair.astype(F32), HEAD_DIM, 1).astype(BF16)


def _lane_half(shape):
    return lax.broadcasted_iota(jnp.int32, shape, 1) // HEAD_DIM


def _rows_to_heads(o_t, n_heads, tq):
    stacked = jnp.concatenate([o_t[:, r * tq:(r + 1) * tq] for r in range(n_heads)], axis=0)
    return stacked.T


def _swa_kernel(q_ref, kp_ref, kc_ref, vp_ref, vc_ref, bias_ref, sink_ref, o_ref):
    tq = q_ref.shape[0]
    k_ext = jnp.concatenate([kp_ref[...], kc_ref[...]], axis=0)
    v_ext = jnp.concatenate([vp_ref[0], vc_ref[0]], axis=1)
    half = _lane_half((tq, LANES))
    outs = []
    for g in range(SWA_KV_HEADS):
        parts = []
        for r in range(SWA_GROUP):
            hd = g * SWA_GROUP + r
            q_pair = q_ref[:, (hd // 2) * LANES:(hd // 2 + 1) * LANES]
            src = q_pair if hd % 2 == g else _swap_halves(q_pair)
            parts.append(jnp.where(half == g, src, jnp.zeros_like(src)))
        qp = jnp.concatenate(parts, axis=0)
        s = lax.dot_general(k_ext, qp, (((1,), (1,)), ((), ())), preferred_element_type=F32)
        s = s * (HEAD_DIM ** -0.5) + bias_ref[0, g]
        sink = sink_ref[g]
        m = jnp.maximum(jnp.max(s, axis=0, keepdims=True), sink)
        e = jnp.exp(s - m)
        l = jnp.sum(e, axis=0, keepdims=True) + jnp.exp(sink - m)
        o_t = jnp.dot(v_ext[g * HEAD_DIM:(g + 1) * HEAD_DIM], e.astype(BF16), preferred_element_type=F32)
        outs.append(_rows_to_heads(o_t / l, SWA_GROUP, tq))
    o_ref[...] = jnp.concatenate(outs, axis=1).astype(o_ref.dtype)


def _swa_bias():
    tq = SWA_TQ
    slopes = _alibi_slopes().reshape(SWA_KV_HEADS, SWA_GROUP)
    x = np.arange(2 * tq)[:, None] - tq
    q = np.arange(tq)[None, :]
    dist = q - x
    valid = (dist >= 0) & (dist < SWA_WINDOW)
    out = np.empty((2, SWA_KV_HEADS, 2 * tq, SWA_GROUP * tq), np.float32)
    for var in range(2):
        v = valid & ((x >= 0) | (var == 1))
        for g in range(SWA_KV_HEADS):
            for r in range(SWA_GROUP):
                out[var, g, :, r * tq:(r + 1) * tq] = np.where(v, -slopes[g, r] * dist, MASKED)
    return out


def _swa_attention(qkv, v_t, sinks):
    t = qkv.shape[0]
    tq = SWA_TQ
    kcol = Q_DIM // LANES
    bias = jnp.asarray(_swa_bias())
    sink_rows = jnp.repeat(sinks.astype(F32).reshape(SWA_KV_HEADS, 1, SWA_GROUP), tq, axis=2)
    prev = lambda i: jnp.maximum(i - 1, 0)
    return pl.pallas_call(
        _swa_kernel,
        grid=(t // tq,),
        in_specs=[pl.BlockSpec((tq, Q_DIM), lambda i: (i, 0)),
                  pl.BlockSpec((tq, LANES), lambda i: (prev(i), kcol)),
                  pl.BlockSpec((tq, LANES), lambda i: (i, kcol)),
                  pl.BlockSpec((1, LANES, tq), lambda i: (prev(i), 0, 0)),
                  pl.BlockSpec((1, LANES, tq), lambda i: (i, 0, 0)),
                  pl.BlockSpec((1,) + bias.shape[1:], lambda i: (jnp.minimum(i, 1), 0, 0, 0)),
                  pl.BlockSpec(sink_rows.shape, lambda i: (0, 0, 0))],
        out_specs=pl.BlockSpec((tq, Q_DIM), lambda i: (i, 0)),
        out_shape=jax.ShapeDtypeStruct((t, Q_DIM), BF16),
        compiler_params=_cparams(("parallel",)),
        name="swa_attention",
    )(qkv, qkv, qkv, v_t, v_t, bias, sink_rows)


def _swa_heads(x, norm_g, w_in, sinks):
    w = w_in.astype(BF16)
    n_qk = Q_DIM + SWA_KV_HEADS * HEAD_DIM
    qk, v_t = _norm_proj(x, norm_g, [w[:, :n_qk], w[:, n_qk:]], [BF16, BF16], chunks=[None, SWA_TQ])
    return _swa_attention(qk, v_t, sinks)


FOX_PARTS = 3


def _split_bf16(v):
    parts, rest = [], v
    for _ in range(FOX_PARTS):
        p = rest.astype(BF16)
        parts.append(p)
        rest = rest - p.astype(F32)
    return parts


def _fox_prep_kernel(f_ref, b_ref, aug_ref, cref_ref, carry_ref):
    tm = f_ref.shape[0]

    @pl.when(pl.program_id(0) == 0)
    def _():
        carry_ref[...] = jnp.zeros_like(carry_ref)

    logf = jax.nn.log_sigmoid(f_ref[...] + b_ref[...])
    row = lax.broadcasted_iota(jnp.int32, (tm, tm), 0)
    col = lax.broadcasted_iota(jnp.int32, (tm, tm), 1)
    tri = jnp.where(row >= col, 1.0, 0.0).astype(BF16)
    local = sum(jnp.dot(tri, p, preferred_element_type=F32) for p in _split_bf16(logf))
    cum = local + carry_ref[...]
    carry_ref[...] = cum[tm - 1:tm, :]
    first = cum[0:1, :]
    cref_ref[0] = jnp.broadcast_to(first, cref_ref.shape[1:])
    hi, mid, lo = _split_bf16(first - cum)
    lane = lax.broadcasted_iota(jnp.int32, (tm, LANES), 1)
    zero = jnp.zeros_like(hi)
    aug_ref[...] = jnp.where(lane < N_HEADS, hi, jnp.where(lane < 2 * N_HEADS, mid,
                                                            jnp.where(lane < 3 * N_HEADS, lo, zero)))


def _fox_prep(f_logit3, f_bias3):
    t = f_logit3.shape[0]
    nt = t // FOX_T
    return pl.pallas_call(
        _fox_prep_kernel,
        grid=(nt,),
        in_specs=[pl.BlockSpec((FOX_T, LANES), lambda i: (i, 0)), pl.BlockSpec((1, LANES), lambda i: (0, 0))],
        out_specs=[pl.BlockSpec((FOX_T, LANES), lambda i: (i, 0)), pl.BlockSpec((1, 8, LANES), lambda i: (i, 0, 0))],
        out_shape=[jax.ShapeDtypeStruct((t, LANES), BF16), jax.ShapeDtypeStruct((nt, 8, LANES), F32)],
        scratch_shapes=[pltpu.VMEM((1, LANES), F32)],
        compiler_params=_cparams(("arbitrary",)),
        name="fox_prep",
    )(f_logit3, f_bias3)


def _fox_kernel(cref_ref, q_ref, k_ref, aug_ref, vt_ref, o_ref, m_ref, l_ref, acc_ref, qt_ref, s_ref):
    pair, qi = pl.program_id(0), pl.program_id(1)
    tq = q_ref.shape[0]
    tk = FOX_T
    lane = lax.broadcasted_iota(jnp.int32, (tq, LANES), 1)
    q2 = q_ref[...] * jnp.asarray(HEAD_DIM ** -0.5, BF16)
    for hh in range(2):
        head = 2 * pair + hh
        pick = (lane % N_HEADS == head) & (lane < FOX_PARTS * N_HEADS)
        q_ext = jnp.concatenate([jnp.where(lane // HEAD_DIM == hh, q2, jnp.zeros_like(q2)).astype(F32),
                                 jnp.where(pick, 1.0, 0.0)], axis=1)
        qt_ref[hh] = q_ext.T.astype(BF16)
    m_ref[...] = jnp.full_like(m_ref, M_INIT)
    l_ref[...] = jnp.zeros_like(l_ref)
    acc_ref[...] = jnp.zeros_like(acc_ref)
    ratio = tq // tk

    def scores(kj, hh, q_from=0):
        start = pl.multiple_of(kj * tk, tk)
        k_ext = jnp.concatenate([k_ref[pl.ds(start, tk), :], aug_ref[pl.ds(start, tk), :]], axis=1)
        s_ref[hh, :, q_from:] = jnp.dot(k_ext, qt_ref[hh, :, q_from:], preferred_element_type=F32)

    def consume(kj, hh, key_offset):
        head = 2 * pair + hh
        q_from = 0 if key_offset is None else key_offset
        s = s_ref[hh, :, q_from:]
        if key_offset is not None:
            key = lax.broadcasted_iota(jnp.int32, s.shape, 0)
            qry = lax.broadcasted_iota(jnp.int32, s.shape, 1)
            s = jnp.where(key <= qry, s, MASKED)
        off = cref_ref[qi * ratio, head] - cref_ref[kj, head]
        m_old = m_ref[hh, :, q_from:]
        m_new = jnp.maximum(m_old, jnp.max(s, axis=0, keepdims=True) + off)
        e = jnp.exp(s - (m_new - off))
        alpha = jnp.exp(m_old - m_new)
        m_ref[hh, :, q_from:] = m_new
        l_ref[hh, :, q_from:] = alpha * l_ref[hh, :, q_from:] + jnp.sum(e, axis=0, keepdims=True)
        v_t = vt_ref[kj, hh * HEAD_DIM:(hh + 1) * HEAD_DIM, :]
        acc_ref[hh, :, q_from:] = (alpha * acc_ref[hh, :, q_from:]
                                   + jnp.dot(v_t, e.astype(BF16), preferred_element_type=F32))

    n_full = qi * ratio
    scores(0, 0)

    def body(j, carry):
        for d in range(ratio):
            kj = j * ratio + d
            scores(kj, 1)
            consume(kj, 0, None)
            scores(kj + 1, 0)
            consume(kj, 1, None)
        return carry

    lax.fori_loop(0, qi, body, 0)
    for d in range(ratio):
        kj = n_full + d
        scores(kj, 1, d * tk)
        consume(kj, 0, d * tk)
        if d + 1 < ratio:
            scores(kj + 1, 0, (d + 1) * tk)
        consume(kj, 1, d * tk)
    o_t = jnp.concatenate([acc_ref[hh] / l_ref[hh] for hh in range(2)], axis=0)
    o_ref[...] = o_t.T.astype(o_ref.dtype)


def _fox_attention(qkv, aug, v_t3, cref):
    t = qkv.shape[0]
    tq = FOX_TQ
    nk = t // FOX_T
    kcol = Q_DIM // LANES
    return pl.pallas_call(
        _fox_kernel,
        grid=(N_HEADS // 2, t // tq),
        in_specs=[pl.BlockSpec(memory_space=pltpu.SMEM),
                  pl.BlockSpec((tq, LANES), lambda p, i: (i, p)),
                  pl.BlockSpec((t, LANES), lambda p, i: (0, kcol + p)),
                  pl.BlockSpec((t, LANES), lambda p, i: (0, 0)),
                  pl.BlockSpec((nk, LANES, FOX_T), lambda p, i: (0, p, 0))],
        out_specs=pl.BlockSpec((tq, LANES), lambda p, i: (i, p)),
        out_shape=jax.ShapeDtypeStruct((t, Q_DIM), BF16),
        scratch_shapes=[pltpu.VMEM((2, 1, tq), F32), pltpu.VMEM((2, 1, tq), F32),
                        pltpu.VMEM((2, HEAD_DIM, tq), F32), pltpu.VMEM((2, 2 * LANES, tq), BF16), pltpu.VMEM((2, FOX_T, tq), F32)],
        compiler_params=_cparams(("parallel", "arbitrary")),
        name="fox_attention",
    )(cref, qkv, qkv, aug, v_t3)


def _fox_heads(x, norm_g, w_in, f_bias):
    t = x.shape[0]
    w_f = w_in[:, 3 * Q_DIM:]
    pad = jnp.zeros((D_MODEL, LANES - FOX_PARTS * N_HEADS), w_f.dtype)
    w_f3 = jnp.concatenate([w_f] * FOX_PARTS + [pad], axis=1).astype(BF16)
    b3 = jnp.concatenate([f_bias.astype(F32)] * FOX_PARTS + [jnp.zeros((LANES - FOX_PARTS * N_HEADS,), F32)])
    w = w_in.astype(BF16)
    qk, v_t3, f_logit3 = _norm_proj(x, norm_g, [w[:, :2 * Q_DIM], w[:, 2 * Q_DIM:3 * Q_DIM], w_f3], [BF16, BF16, F32],
                                    chunks=[None, FOX_T, None])
    aug, cref = _fox_prep(f_logit3, b3.reshape(1, LANES))
    return _fox_attention(qk, aug, v_t3, cref[:, 0, :N_HEADS])


def _nsa_compress_kernel(a_ref, pos_ref, w1_ref, w2_ref, o_ref):
    tc = a_ref.shape[1]
    half = a_ref.shape[2]
    a = a_ref[0]
    pos = pos_ref[0]
    top = jnp.dot((a + pos[:, :half]).astype(BF16), w1_ref[0, :half], preferred_element_type=F32)
    bot = jnp.dot((a + pos[:, half:]).astype(BF16), w1_ref[0, half:], preferred_element_type=F32)
    hid = top + pltpu.roll(bot, tc - 1, 0)
    o_ref[0] = jnp.dot(jax.nn.gelu(hid).astype(BF16), w2_ref[0], preferred_element_type=F32)


def _nsa_compress(a, pos, w1, w2):
    _, tc, width = a.shape
    return pl.pallas_call(
        _nsa_compress_kernel,
        grid=(2, NSA_KV_HEADS // 2),
        in_specs=[pl.BlockSpec((1, tc, width), lambda kv, gp: (kv * (NSA_KV_HEADS // 2) + gp, 0, 0)),
                  pl.BlockSpec((1, 1, 2 * width), lambda kv, gp: (kv, 0, 0)),
                  pl.BlockSpec((1, 2 * width, 2 * CMP_HIDDEN), lambda kv, gp: (kv, 0, 0)),
                  pl.BlockSpec((1, 2 * CMP_HIDDEN, LANES), lambda kv, gp: (kv, 0, 0))],
        out_specs=pl.BlockSpec((1, tc, LANES), lambda kv, gp: (kv, 0, gp)),
        out_shape=jax.ShapeDtypeStruct((2, tc, NSA_KV_DIM), F32),
        compiler_params=_cparams(("parallel", "parallel")),
        name="nsa_compress",
    )(a, pos, w1, w2)


NSA_WQ = 256
NSA_WIN_TILES = (NSA_WINDOW + NSA_TQ) // NSA_WQ


def _nsa_kernel(slopes_ref, q_ref, ks_ref, *refs):
    kw_refs, refs = refs[:NSA_WIN_TILES], refs[NSA_WIN_TILES:]
    vst_ref, refs = refs[0], refs[1:]
    vw_refs, refs = refs[:NSA_WIN_TILES], refs[NSA_WIN_TILES:]
    (kcmp_ref, vov_ref, gates_ref, ksel_ext_ref, kwin_ext_ref, kcmp_ext_ref, qext_ref,
     o_ref, selt_ref, m_ref, l_ref, acc_ref, qt_ref, s_ref, scmp_ref, cm_ref, cl_ref, oc_ref, sw_ref,
     active_ref, list_ref) = refs
    g, qi = pl.program_id(0), pl.program_id(1)
    tq = q_ref.shape[0]
    tc = kcmp_ref.shape[0]
    n_sel = selt_ref.shape[0]
    rr = NSA_GROUP
    t0 = qi * tq
    hg = g % 2
    slope = [slopes_ref[g * rr + r] for r in range(rr)]

    half = _lane_half((tq, LANES))
    parts = []
    for r in range(rr):
        q_pair = q_ref[:, (r // 2) * LANES:(r // 2 + 1) * LANES]
        src = jnp.where(hg == r % 2, q_pair, _swap_halves(q_pair))
        parts.append(jnp.where(half == hg, src, jnp.zeros_like(src)) * jnp.asarray(HEAD_DIM ** -0.5, BF16))
    qp = jnp.concatenate(parts, axis=0)
    qt_ref[0:LANES, :] = qp.astype(F32).T.astype(BF16)

    @pl.when(qi == 0)
    def _():
        qt_ref[LANES:, :] = qext_ref[0]

    def lanes_of(s, r):
        return s[:, r * tq:(r + 1) * tq]

    chunk_span = NSA_CMP_CHUNK * CMP_STRIDE
    last_chunk = ((t0 + tq - CMP_BLOCK) // CMP_STRIDE) // NSA_CMP_CHUNK
    nl16_q = (CMP_STRIDE * lax.broadcasted_iota(jnp.int32, (NSA_CMP_CHUNK, tq), 0)
              - lax.broadcasted_iota(jnp.int32, (NSA_CMP_CHUNK, tq), 1)).astype(F32)
    cm_ref[...] = jnp.full_like(cm_ref, M_INIT)

    def cmp_rows(ch):
        return pl.ds(pl.multiple_of(ch * NSA_CMP_CHUNK, NSA_CMP_CHUNK), NSA_CMP_CHUNK)

    def cmp_offset(ch, r):
        return slope[r] * (ch * chunk_span).astype(F32)

    def cmp_scores(ch, masked):
        k_full = jnp.concatenate([kcmp_ref[cmp_rows(ch), :], kcmp_ext_ref[...]], axis=1)
        s = jnp.dot(k_full, qt_ref[...], preferred_element_type=F32)
        if masked:
            visible = nl16_q <= (t0 - (CMP_BLOCK - 1) - ch * chunk_span).astype(F32)
            s = jnp.concatenate([jnp.where(visible, lanes_of(s, r), MASKED) for r in range(rr)], axis=1)
        scmp_ref[cmp_rows(ch), :] = s
        for r in range(rr):
            cm_ref[r] = jnp.maximum(cm_ref[r], jnp.max(lanes_of(s, r), axis=0, keepdims=True) + cmp_offset(ch, r))

    def cmp_plain(ch, carry):
        cmp_scores(ch, False)
        return carry

    lax.fori_loop(0, jnp.maximum(last_chunk - 1, 0), cmp_plain, 0)

    @pl.when(last_chunk >= 1)
    def _():
        cmp_scores(last_chunk - 1, True)

    cmp_scores(last_chunk, True)

    cl_ref[...] = jnp.zeros_like(cl_ref)
    oc_ref[...] = jnp.zeros_like(oc_ref)

    def cmp_exp(ch, carry):
        s = scmp_ref[cmp_rows(ch), :]
        e_all = []
        for r in range(rr):
            e = jnp.exp(lanes_of(s, r) - (cm_ref[r] - cmp_offset(ch, r)))
            cl_ref[r] = cl_ref[r] + jnp.sum(e, axis=0, keepdims=True)
            e_all.append(e.astype(BF16))
        oc_ref[...] += jnp.dot(vov_ref[ch, 0], jnp.concatenate(e_all, axis=1), preferred_element_type=F32)
        return carry

    lax.fori_loop(0, last_chunk + 1, cmp_exp, 0)
    inv_l = [jnp.where(cl_ref[r] > 0.0, 1.0 / cl_ref[r], 0.0) for r in range(rr)]
    o_cmp = [oc_ref[0:HEAD_DIM, r * tq:(r + 1) * tq] * inv_l[r] for r in range(rr)]

    imp = functools.reduce(jnp.add, [oc_ref[HEAD_DIM:, r * tq:(r + 1) * tq] * inv_l[r] for r in range(rr)])
    blk = lax.broadcasted_iota(jnp.int32, (n_sel, tq), 0)
    cur = (t0 + lax.broadcasted_iota(jnp.int32, (n_sel, tq), 1)) // SEL_BLOCK
    blk_f = blk.astype(F32)
    forced = jnp.where(blk == 0, 1.0, jnp.where(blk == cur, 1.0, jnp.where(blk == cur - 1, 1.0, 0.0)))
    score = jnp.where(forced > 0.0, TAKEN, jnp.where(blk <= cur, imp, -1.0))

    selt_ref[...] = score
    row_step = min(n_sel, NSA_TOPK_ROW_STEP)
    variant = ((t0 + tq - 1) // SEL_BLOCK) // row_step

    wq = NSA_WQ
    n_back = NSA_WINDOW // wq
    win_jobs = [(sub, b) for sub in range(tq // wq) for b in range(n_back + 1)]
    assert len(win_jobs) <= N_SELECT - 3

    def win_scores(job):
        sub, b = win_jobs[job]
        k_full = jnp.concatenate([kw_refs[sub + b][...], kwin_ext_ref[...]], axis=1)
        q_cols = jnp.concatenate([qt_ref[:, r * tq + sub * wq:r * tq + (sub + 1) * wq] for r in range(rr)], axis=1)
        sw_ref[job] = jnp.dot(k_full, q_cols, preferred_element_type=F32)

    for v in range(n_sel // row_step):
        rows = row_step * (v + 1)

        @pl.when(variant == v)
        def _(rows=rows):
            ids = lax.broadcasted_iota(jnp.int32, (rows, tq), 0).astype(F32)
            sc = selt_ref[0:rows, :]
            for rnd in range(N_SELECT - 3):
                if rnd < len(win_jobs):
                    win_scores(rnd)
                best = jnp.max(sc, axis=0, keepdims=True)
                first = jnp.min(jnp.where(sc == best, ids, float(n_sel)), axis=0, keepdims=True)
                sc = jnp.where(ids == first, TAKEN, sc)
            selt_ref[0:rows, :] = sc

    selt_ref[...] = jnp.where(blk <= cur, jnp.where(selt_ref[...] == TAKEN, 1.0, 0.0), 0.0)

    m_ref[...] = jnp.full_like(m_ref, M_INIT)
    l_ref[...] = jnp.zeros_like(l_ref)
    acc_ref[...] = jnp.zeros_like(acc_ref)
    blocks_per_chunk = NSA_TK // SEL_BLOCK

    for c in range(n_sel // blocks_per_chunk):
        active_ref[c] = jnp.max(selt_ref[c * blocks_per_chunk:(c + 1) * blocks_per_chunk, :])

    def compact(c, n):
        list_ref[n] = c
        return n + (active_ref[c] > 0.0).astype(jnp.int32)

    n_active = lax.fori_loop(0, t0 // NSA_TK + 1, compact, 0)

    def sel_scores(c, buf):
        start = pl.multiple_of(c * NSA_TK, NSA_TK)
        rows = selt_ref[pl.ds(pl.multiple_of(c * blocks_per_chunk, blocks_per_chunk), blocks_per_chunk), :]
        mask_rows = jnp.concatenate([jnp.where(rows > 0.5, 0.0, MASKED), jnp.zeros_like(rows)], axis=0)
        first = LANES + buf * NSA_EXT_MASK_ROWS
        qt_ref[first:first + NSA_EXT_MASK_ROWS, :] = jnp.concatenate([mask_rows] * rr, axis=1).astype(BF16)
        k_full = jnp.concatenate([ks_ref[pl.ds(start, NSA_TK), :], ksel_ext_ref[buf]], axis=1)
        s_ref[buf] = jnp.dot(k_full, qt_ref[...], preferred_element_type=F32)

    def sel_consume(c, buf, diagonal):
        shift = (c * NSA_TK - t0).astype(F32)
        if diagonal:
            key = lax.broadcasted_iota(jnp.int32, (NSA_TK, tq), 0) + c * NSA_TK
            visible = key <= lax.broadcasted_iota(jnp.int32, (NSA_TK, tq), 1) + t0
        e_all, alpha_all = [], []
        for r in range(rr):
            off = slope[r] * shift
            s = s_ref[buf][:, r * tq:(r + 1) * tq]
            if diagonal:
                s = jnp.where(visible, s, MASKED)
            m_old = m_ref[r]
            m_new = jnp.maximum(m_old, jnp.max(s, axis=0, keepdims=True) + off)
            e = jnp.exp(s - (m_new - off))
            alpha = jnp.exp(m_old - m_new)
            m_ref[r] = m_new
            l_ref[r] = alpha * l_ref[r] + jnp.sum(e, axis=0, keepdims=True)
            e_all.append(e.astype(BF16))
            alpha_all.append(alpha)
        pv = jnp.dot(vst_ref[c], jnp.concatenate(e_all, axis=1), preferred_element_type=F32)
        acc_ref[...] = jnp.concatenate(alpha_all, axis=1) * acc_ref[...] + pv

    n_before = n_active - 1
    sel_scores(list_ref[0], 0)

    def sel_pair(j, carry):
        c0, c1, c2 = list_ref[2 * j], list_ref[2 * j + 1], list_ref[2 * j + 2]
        sel_scores(c1, 1)
        sel_consume(c0, 0, False)
        sel_scores(c2, 0)
        sel_consume(c1, 1, False)
        return carry

    lax.fori_loop(0, n_before // 2, sel_pair, 0)

    @pl.when(n_before % 2 == 1)
    def _():
        c0, c_last = list_ref[n_before - 1], list_ref[n_before]
        sel_scores(c_last, 1)
        sel_consume(c0, 0, False)
        sel_consume(c_last, 1, True)

    @pl.when(n_before % 2 == 0)
    def _():
        sel_consume(list_ref[n_before], 0, True)

    xw = lax.broadcasted_iota(jnp.int32, (wq, wq), 0)
    qw = lax.broadcasted_iota(jnp.int32, (wq, wq), 1)
    o_win = [[] for _ in range(rr)]
    for sub in range(tq // wq):
        s_win, off_win = [], []
        for b in range(n_back + 1):
            back = n_back - b
            s_t = sw_ref[sub * (n_back + 1) + b]
            per_head = [s_t[:, r * wq:(r + 1) * wq] for r in range(rr)]
            if back == n_back:
                per_head = [jnp.where(xw > qw, s, MASKED) for s in per_head]
            if back == 0:
                per_head = [jnp.where(xw <= qw, s, MASKED) for s in per_head]
            s_win.append(per_head)
            in_range = qi * (tq // wq) + sub >= back
            off_win.append([jnp.where(in_range, -slope[r] * float(back * wq), MASKED) for r in range(rr)])
        e_win, l_win = [], []
        for r in range(rr):
            m = functools.reduce(jnp.maximum, [jnp.max(s_win[b][r], axis=0, keepdims=True) + off_win[b][r]
                                               for b in range(n_back + 1)])
            e_tiles = [jnp.exp(s_win[b][r] - (m - off_win[b][r])) for b in range(n_back + 1)]
            l_win.append(functools.reduce(jnp.add, [jnp.sum(e, axis=0, keepdims=True) for e in e_tiles]))
            e_win.append(jnp.concatenate([e.astype(BF16) for e in e_tiles], axis=0))
        v_win = jnp.concatenate([vw_refs[sub + b][0] for b in range(n_back + 1)], axis=1)
        o_sub = jnp.dot(v_win, jnp.concatenate(e_win, axis=1), preferred_element_type=F32)
        for r in range(rr):
            o_win[r].append(o_sub[:, r * wq:(r + 1) * wq] / l_win[r])
    o_win = [jnp.concatenate(parts, axis=1) for parts in o_win]

    gate = jax.nn.sigmoid(gates_ref[0])
    merged = []
    for r in range(rr):
        o_sel = lanes_of(acc_ref[...], r) / l_ref[r]
        merged.append(gate[3 * r:3 * r + 1] * o_cmp[r] + gate[3 * r + 1:3 * r + 2] * o_sel
                      + gate[3 * r + 2:3 * r + 3] * o_win[r])
    o_ref[...] = jnp.concatenate(merged, axis=0).T.astype(o_ref.dtype)


def _nsa_overlap_t(n_sel, tc):
    cmp_start = np.arange(tc - 1) * CMP_STRIDE
    sel_start = np.arange(n_sel) * SEL_BLOCK
    ov = np.clip(np.minimum(cmp_start[None, :] + CMP_BLOCK, sel_start[:, None] + SEL_BLOCK)
                 - np.maximum(cmp_start[None, :], sel_start[:, None]), 0, None) / CMP_BLOCK
    return np.concatenate([ov, np.zeros((n_sel, 1))], axis=1).astype(np.float32)


NSA_EXT_MASK_ROWS = 16
NSA_EXT_SEL = 32
NSA_EXT_WIN = 48
NSA_EXT_CMP = 64
NSA_CMP_CHUNK = 256
NSA_TOPK_ROW_STEP = 64
SLOPE_PIECES = 3


def _bf16_pieces(v):
    out, rest = [], np.asarray(v, np.float32)
    for _ in range(SLOPE_PIECES):
        p = rest.astype(BF16).astype(np.float32)
        out.append(p)
        rest = rest - p
    return out


def _nsa_extensions(tq):
    ksel = np.zeros((2, NSA_TK, LANES), np.float32)
    x = np.arange(NSA_TK)
    for buf in range(2):
        ksel[buf, x, buf * NSA_EXT_MASK_ROWS + x // SEL_BLOCK] = 1.0
    for j in range(SLOPE_PIECES):
        ksel[:, :, NSA_EXT_SEL + j] = 256 * (x // 256)
        ksel[:, :, NSA_EXT_SEL + SLOPE_PIECES + j] = x % 256
    kwin = np.zeros((NSA_WQ, LANES), np.float32)
    xw = np.arange(NSA_WQ)
    for j in range(SLOPE_PIECES):
        kwin[:, NSA_EXT_WIN + j] = 256 * (xw // 256)
        kwin[:, NSA_EXT_WIN + SLOPE_PIECES + j] = xw % 256
    kcmp = np.zeros((NSA_CMP_CHUNK, LANES), np.float32)
    for j in range(SLOPE_PIECES):
        kcmp[:, NSA_EXT_CMP + j] = np.arange(NSA_CMP_CHUNK)
    pieces = _bf16_pieces(_alibi_slopes())
    qext = np.zeros((NSA_KV_HEADS, LANES, NSA_GROUP * tq), np.float32)
    for g in range(NSA_KV_HEADS):
        for r in range(NSA_GROUP):
            cols = slice(r * tq, (r + 1) * tq)
            for j in range(SLOPE_PIECES):
                p = pieces[j][g * NSA_GROUP + r]
                qext[g, NSA_EXT_SEL + j, cols] = p
                qext[g, NSA_EXT_SEL + SLOPE_PIECES + j, cols] = p
                qext[g, NSA_EXT_WIN + j, cols] = p
                qext[g, NSA_EXT_WIN + SLOPE_PIECES + j, cols] = p
                qext[g, NSA_EXT_CMP + j, cols] = CMP_STRIDE * p
    return jnp.asarray(ksel, BF16), jnp.asarray(kwin, BF16), jnp.asarray(kcmp, BF16), jnp.asarray(qext, BF16)


def _nsa_attention(main, vs_t3, vw_t, k_cmp, v_cmp_t, gates_t):
    t = main.shape[0]
    tq = NSA_TQ
    tc = k_cmp.shape[0]
    n_sel = t // SEL_BLOCK
    nc = t // NSA_TK
    ks_col = Q_DIM // LANES
    kw_col = ks_col + NSA_KV_DIM // LANES
    n_ch = tc // NSA_CMP_CHUNK
    ov_t = jnp.asarray(_nsa_overlap_t(n_sel, tc).reshape(n_sel, n_ch, NSA_CMP_CHUNK).transpose(1, 0, 2), BF16)
    v_cmp_t = v_cmp_t.reshape(NSA_KV_HEADS, HEAD_DIM, n_ch, NSA_CMP_CHUNK).transpose(2, 0, 1, 3)
    vov = jnp.concatenate([v_cmp_t, jnp.broadcast_to(ov_t[:, None], (n_ch, NSA_KV_HEADS, n_sel, NSA_CMP_CHUNK))],
                          axis=2)
    slopes = jnp.asarray(_alibi_slopes())
    ksel_ext, kwin_ext, kcmp_ext, qext = _nsa_extensions(tq)

    def win_tile(i, j):
        return jnp.maximum(i * (tq // NSA_WQ) - NSA_WINDOW // NSA_WQ + j, 0)

    def kw_spec(j):
        return pl.BlockSpec((NSA_WQ, LANES), lambda g, i: (win_tile(i, j), kw_col + g // 2))

    def vw_spec(j):
        return pl.BlockSpec((1, HEAD_DIM, NSA_WQ), lambda g, i: (win_tile(i, j), g, 0))

    backs = list(range(NSA_WIN_TILES))
    return pl.pallas_call(
        _nsa_kernel,
        grid=(NSA_KV_HEADS, t // tq),
        in_specs=[pl.BlockSpec(memory_space=pltpu.SMEM),
                  pl.BlockSpec((tq, NSA_GROUP * HEAD_DIM), lambda g, i: (i, g)),
                  pl.BlockSpec((t, LANES), lambda g, i: (0, ks_col + g // 2))]
                 + [kw_spec(b) for b in backs]
                 + [pl.BlockSpec((nc, HEAD_DIM, NSA_TK), lambda g, i: (0, g, 0))]
                 + [vw_spec(b) for b in backs]
                 + [pl.BlockSpec((tc, LANES), lambda g, i: (0, g // 2)),
                    pl.BlockSpec((n_ch, 1, HEAD_DIM + n_sel, NSA_CMP_CHUNK), lambda g, i: (0, g, 0, 0)),
                    pl.BlockSpec((1, 16, tq), lambda g, i: (g, 0, i)),
                    pl.BlockSpec(ksel_ext.shape, lambda g, i: (0, 0, 0)),
                    pl.BlockSpec(kwin_ext.shape, lambda g, i: (0, 0)),
                    pl.BlockSpec(kcmp_ext.shape, lambda g, i: (0, 0)),
                    pl.BlockSpec((1,) + qext.shape[1:], lambda g, i: (g, 0, 0))],
        out_specs=pl.BlockSpec((tq, NSA_GROUP * HEAD_DIM), lambda g, i: (i, g)),
        out_shape=jax.ShapeDtypeStruct((t, Q_DIM), BF16),
        scratch_shapes=[pltpu.VMEM((n_sel, tq), F32), pltpu.VMEM((NSA_GROUP, 1, tq), F32),
                        pltpu.VMEM((NSA_GROUP, 1, tq), F32), pltpu.VMEM((HEAD_DIM, NSA_GROUP * tq), F32),
                        pltpu.VMEM((2 * LANES, NSA_GROUP * tq), BF16),
                        pltpu.VMEM((2, NSA_TK, NSA_GROUP * tq), F32),
                        pltpu.VMEM((tc, NSA_GROUP * tq), F32), pltpu.VMEM((NSA_GROUP, 1, tq), F32),
                        pltpu.VMEM((NSA_GROUP, 1, tq), F32), pltpu.VMEM((HEAD_DIM + n_sel, NSA_GROUP * tq), F32),
                        pltpu.VMEM((tq // NSA_WQ * (NSA_WINDOW // NSA_WQ + 1), NSA_WQ, NSA_GROUP * NSA_WQ), F32),
                        pltpu.SMEM((nc,), F32), pltpu.SMEM((nc,), jnp.int32)],
        compiler_params=_cparams(("parallel", "arbitrary")),
        name="nsa_attention",
    )(slopes, main, main, *([main] * NSA_WIN_TILES), vs_t3, *([vw_t] * NSA_WIN_TILES), k_cmp, vov, gates_t,
      ksel_ext, kwin_ext, kcmp_ext, qext)


def _nsa_heads(x, norm_g, w_in, cmp_pos, cmp_w1, cmp_w2):
    t = x.shape[0]
    kv = NSA_KV_DIM
    col = lambda i: w_in[:, Q_DIM + i * kv:Q_DIM + (i + 1) * kv]
    w_main = jnp.concatenate([w_in[:, :Q_DIM], col(2), col(4)], axis=1).astype(BF16)
    w_cmp = jnp.concatenate([col(0), col(1)], axis=1).astype(BF16)
    n_gate = 3 * N_HEADS
    w_gate = jnp.concatenate([w_in[:, Q_DIM + 6 * kv:], jnp.zeros((D_MODEL, LANES - n_gate), w_in.dtype)],
                             axis=1).astype(BF16)
    main, vs_t3, vw_t3, kcvc, gate_logit = _norm_proj(
        x, norm_g, [w_main, col(3).astype(BF16), col(5).astype(BF16), w_cmp, w_gate], [BF16, BF16, BF16, F32, F32],
        chunks=[None, NSA_TK, NSA_WQ, "cols", None])

    tc = t // CMP_STRIDE
    a = kcvc.reshape(kcvc.shape[0], tc, CMP_STRIDE * LANES)
    pair = jnp.eye(2, dtype=cmp_w1.dtype)
    pos = jnp.broadcast_to(cmp_pos[:, :, None, :], (2, CMP_BLOCK, 2, HEAD_DIM)).reshape(2, 1, 2 * CMP_BLOCK * HEAD_DIM)
    w1 = (cmp_w1.reshape(2, CMP_BLOCK, 1, HEAD_DIM, 1, CMP_HIDDEN) * pair.reshape(1, 1, 2, 1, 2, 1))
    w1 = w1.reshape(2, 2 * CMP_BLOCK * HEAD_DIM, 2 * CMP_HIDDEN)
    w2 = (cmp_w2.reshape(2, 1, CMP_HIDDEN, 1, HEAD_DIM) * pair.reshape(1, 2, 1, 2, 1))
    w2 = w2.reshape(2, 2 * CMP_HIDDEN, 2 * HEAD_DIM)
    cmp = _nsa_compress(a, pos, w1.astype(BF16), w2.astype(BF16))
    k_cmp = cmp[0].astype(BF16)
    v_cmp_t = cmp[1].T.astype(BF16)

    gates_t = gate_logit[:, :n_gate].reshape(t, NSA_KV_HEADS, 3 * NSA_GROUP).transpose(1, 2, 0)
    gates_t = jnp.pad(gates_t, ((0, 0), (0, 16 - 3 * NSA_GROUP), (0, 0)))
    return _nsa_attention(main, vs_t3, vw_t3, k_cmp, v_cmp_t, gates_t)


def kernel(x, attn_norm, mlp_norm, final_norm, nsa_w_in, nsa_cmp_pos, nsa_cmp_w1, nsa_cmp_w2, nsa_w_out,
           swa_w_in, swa_sinks, swa_w_out, fox_w_in, fox_f_bias, fox_w_out,
           mlp_w_up, mlp_conv_w, mlp_conv_b, mlp_w_down):
    assert x.shape[0] == 1, "the trunk is written for batch 1"
    h = x[0]
    for i in range(DEPTH):
        kind, j = i % N_MIXERS, i // N_MIXERS
        if kind == 0:
            o = _nsa_heads(h, attn_norm[i], nsa_w_in[j], nsa_cmp_pos[j], nsa_cmp_w1[j], nsa_cmp_w2[j])
            w_out = nsa_w_out[j]
        elif kind == 1:
            o = _swa_heads(h, attn_norm[i], swa_w_in[j], swa_sinks[j])
            w_out = swa_w_out[j]
        else:
            o = _fox_heads(h, attn_norm[i], fox_w_in[j], fox_f_bias[j])
            w_out = fox_w_out[j]
        h = _mixer_out_mlp(h, o, w_out.astype(BF16), mlp_norm[i], mlp_w_up[i].astype(BF16), mlp_conv_w[i],
                           mlp_conv_b[i].reshape(1, -1), mlp_w_down[i].astype(BF16),
                           final_g=final_norm if i == DEPTH - 1 else None)
    return h[None]
```

```python
import functools

import numpy as np
import jax
import jax.numpy as jnp
from jax import lax
from jax.experimental import pallas as pl
from jax.experimental.pallas import tpu as pltpu

F32 = jnp.float32
BF16 = jnp.bfloat16

D_MODEL = 1024
DEPTH = 4
N_MIXERS = 3
HEAD_DIM = 64
N_HEADS = 16
Q_DIM = N_HEADS * HEAD_DIM
ALIBI_MAX = 8.0
NORM_EPS = 1e-6
MASKED = -2e30
M_INIT = -1e30
TAKEN = -(2.0 ** 127)

NSA_KV_HEADS = 4
NSA_GROUP = 4
NSA_KV_DIM = NSA_KV_HEADS * HEAD_DIM
CMP_BLOCK = 32
CMP_STRIDE = 16
CMP_HIDDEN = 256
SEL_BLOCK = 64
N_SELECT = 16
NSA_WINDOW = 512
NSA_TQ = 512
NSA_TK = 512

SWA_KV_HEADS = 2
SWA_GROUP = 8
SWA_WINDOW = 128
SWA_TQ = 128

FOX_TQ = 1024
FOX_T = 512

LANES = 128
VMEM_LIMIT = 56 * 1024 * 1024


def _cparams(semantics, vmem=VMEM_LIMIT):
    return pltpu.CompilerParams(dimension_semantics=semantics, vmem_limit_bytes=vmem)


def _alibi_slopes():
    return np.asarray(2.0 ** (-ALIBI_MAX * np.arange(1, N_HEADS + 1) / N_HEADS), dtype=np.float32)


def _rms(x, g):
    ms = jnp.mean(x * x, axis=-1, keepdims=True)
    return x * lax.rsqrt(ms + NORM_EPS) * g


def _norm_proj_kernel(chunks, x_ref, g_ref, *refs):
    n_out = len(chunks)
    h = _rms(x_ref[...], g_ref[...]).astype(BF16)
    for w_ref, o_ref, chunk in zip(refs[:n_out], refs[n_out:], chunks):
        res = jnp.dot(h, w_ref[...], preferred_element_type=F32)
        if chunk is None:
            o_ref[...] = res.astype(o_ref.dtype)
        elif chunk == "cols":
            for c in range(o_ref.shape[0]):
                o_ref[c] = res[:, c * LANES:(c + 1) * LANES].astype(o_ref.dtype)
        else:
            for c in range(res.shape[0] // chunk):
                o_ref[c] = res[c * chunk:(c + 1) * chunk].T.astype(o_ref.dtype)


def _pick_tile(n, candidates):
    for c in candidates:
        if n % c == 0:
            return c
    raise ValueError(f"no tile for {n}")


def _norm_proj(x, g, weights, out_dtypes, chunks=None):
    t, d = x.shape
    tm = _pick_tile(t, (1024, 512, 256, 128))
    chunks = tuple(chunks) if chunks else (None,) * len(weights)
    out_specs, out_shape = [], []
    for w, dt, c in zip(weights, out_dtypes, chunks):
        n = w.shape[1]
        if c is None:
            out_specs.append(pl.BlockSpec((tm, n), lambda i: (i, 0)))
            out_shape.append(jax.ShapeDtypeStruct((t, n), dt))
        elif c == "cols":
            out_specs.append(pl.BlockSpec((n // LANES, tm, LANES), lambda i: (0, i, 0)))
            out_shape.append(jax.ShapeDtypeStruct((n // LANES, t, LANES), dt))
        else:
            out_specs.append(pl.BlockSpec((tm // c, n, c), lambda i: (i, 0, 0)))
            out_shape.append(jax.ShapeDtypeStruct((t // c, n, c), dt))
    return pl.pallas_call(
        functools.partial(_norm_proj_kernel, chunks),
        grid=(t // tm,),
        in_specs=[pl.BlockSpec((tm, d), lambda i: (i, 0)), pl.BlockSpec((1, d), lambda i: (0, 0))]
                 + [pl.BlockSpec(w.shape, lambda i: (0, 0)) for w in weights],
        out_specs=out_specs,
        out_shape=out_shape,
        compiler_params=_cparams(("parallel",)),
        name="norm_proj",
    )(x, g.reshape(1, d), *weights)


MLP_HALO = 16


def _mixer_out_mlp_kernel(final_norm, x_ref, xh_ref, o_ref, oh_ref, wo_ref, g_ref, fg_ref, wa_ref, wg_ref,
                          cwa_ref, cwg_ref, cba_ref, cbg_ref, wd_ref, y_ref, x1_ref, h_ref, acc_ref, ua_ref, ug_ref):
    i, j = pl.program_id(0), pl.program_id(1)
    tm = x_ref.shape[0]

    @pl.when(j == 0)
    def _():
        wo = wo_ref[...]
        x1_ref[0:MLP_HALO, :] = xh_ref[...] + jnp.dot(oh_ref[...], wo, preferred_element_type=F32)
        x1_ref[MLP_HALO:, :] = x_ref[...] + jnp.dot(o_ref[...], wo, preferred_element_type=F32)
        halo = _rms(x1_ref[0:MLP_HALO, :], g_ref[...])
        h_ref[0:MLP_HALO, :] = jnp.where(i > 0, halo, 0.0).astype(BF16)
        h_ref[MLP_HALO:, :] = _rms(x1_ref[MLP_HALO:, :], g_ref[...]).astype(BF16)
        acc_ref[...] = jnp.zeros_like(acc_ref)

    h = h_ref[...]

    def conv(w_ref, cw_ref, cb_ref, u_ref):
        u_ref[...] = jnp.dot(h, w_ref[...], preferred_element_type=F32)
        cw = cw_ref[...]
        return (cw[0:1] * u_ref[MLP_HALO - 2:MLP_HALO - 2 + tm, :] + cw[1:2] * u_ref[MLP_HALO - 1:MLP_HALO - 1 + tm, :]
                + cw[2:3] * u_ref[MLP_HALO:MLP_HALO + tm, :] + cb_ref[...])

    a = conv(wa_ref, cwa_ref, cba_ref, ua_ref)
    gt = conv(wg_ref, cwg_ref, cbg_ref, ug_ref)
    act = (jax.nn.silu(gt) * a).astype(BF16)
    acc_ref[...] += jnp.dot(act, wd_ref[...], preferred_element_type=F32)

    @pl.when(j == pl.num_programs(1) - 1)
    def _():
        y = x1_ref[MLP_HALO:, :] + acc_ref[...]
        y_ref[...] = _rms(y, fg_ref[...]) if final_norm else y


def _mixer_out_mlp(x, o, w_out, g, w_up, conv_w, conv_b, w_down, final_g=None):
    t, d = x.shape
    f = w_down.shape[0]
    q = o.shape[1]
    tm = _pick_tile(t, (512, 256, 128))
    tf = _pick_tile(f, (1408, 256, 128))
    nf = f // tf
    hb = tm // MLP_HALO
    halo_map = lambda i, j: (jnp.maximum(i * hb - 1, 0), 0)
    fg = g if final_g is None else final_g
    return pl.pallas_call(
        functools.partial(_mixer_out_mlp_kernel, final_g is not None),
        grid=(t // tm, nf),
        in_specs=[pl.BlockSpec((tm, d), lambda i, j: (i, 0)),
                  pl.BlockSpec((MLP_HALO, d), halo_map),
                  pl.BlockSpec((tm, q), lambda i, j: (i, 0)),
                  pl.BlockSpec((MLP_HALO, q), halo_map),
                  pl.BlockSpec((q, d), lambda i, j: (0, 0)),
                  pl.BlockSpec((1, d), lambda i, j: (0, 0)),
                  pl.BlockSpec((1, d), lambda i, j: (0, 0)),
                  pl.BlockSpec((d, tf), lambda i, j: (0, j)),
                  pl.BlockSpec((d, tf), lambda i, j: (0, nf + j)),
                  pl.BlockSpec((3, tf), lambda i, j: (0, j)),
                  pl.BlockSpec((3, tf), lambda i, j: (0, nf + j)),
                  pl.BlockSpec((1, tf), lambda i, j: (0, j)),
                  pl.BlockSpec((1, tf), lambda i, j: (0, nf + j)),
                  pl.BlockSpec((tf, d), lambda i, j: (j, 0))],
        out_specs=pl.BlockSpec((tm, d), lambda i, j: (i, 0)),
        out_shape=jax.ShapeDtypeStruct((t, d), F32),
        scratch_shapes=[pltpu.VMEM((tm + MLP_HALO, d), F32), pltpu.VMEM((tm + MLP_HALO, d), BF16),
                        pltpu.VMEM((tm, d), F32),
                        pltpu.VMEM((tm + MLP_HALO, tf), F32), pltpu.VMEM((tm + MLP_HALO, tf), F32)],
        compiler_params=_cparams(("parallel", "arbitrary")),
        name="mixer_out_mlp",
    )(x, x, o, o, w_out, g.reshape(1, d), fg.reshape(1, d), w_up, w_up, conv_w, conv_w, conv_b, conv_b, w_down)


def _swap_halves(q_pair):
    return pltpu.roll(q_pair.astype(F32), HEAD_DIM, 1).astype(BF16)


def _lane_half(shape):
    return lax.broadcasted_iota(jnp.int32, shape, 1) // HEAD_DIM


def _rows_to_heads(o_t, n_heads, tq):
    stacked = jnp.concatenate([o_t[:, r * tq:(r + 1) * tq] for r in range(n_heads)], axis=0)
    return stacked.T


def _swa_kernel(q_ref, kp_ref, kc_ref, vp_ref, vc_ref, bias_ref, sink_ref, o_ref):
    tq = q_ref.shape[0]
    k_ext = jnp.concatenate([kp_ref[...], kc_ref[...]], axis=0)
    v_ext = jnp.concatenate([vp_ref[0], vc_ref[0]], axis=1)
    half = _lane_half((tq, LANES))
    outs = []
    for g in range(SWA_KV_HEADS):
        parts = []
        for r in range(SWA_GROUP):
            hd = g * SWA_GROUP + r
            q_pair = q_ref[:, (hd // 2) * LANES:(hd // 2 + 1) * LANES]
            src = q_pair if hd % 2 == g else _swap_halves(q_pair)
            parts.append(jnp.where(half == g, src, jnp.zeros_like(src)))
        qp = jnp.concatenate(parts, axis=0)
        s = lax.dot_general(k_ext, qp, (((1,), (1,)), ((), ())), preferred_element_type=F32)
        s = s * (HEAD_DIM ** -0.5) + bias_ref[0, g]
        sink = sink_ref[g]
        m = jnp.maximum(jnp.max(s, axis=0, keepdims=True), sink)
        e = jnp.exp(s - m)
        l = jnp.sum(e, axis=0, keepdims=True) + jnp.exp(sink - m)
        o_t = jnp.dot(v_ext[g * HEAD_DIM:(g + 1) * HEAD_DIM], e.astype(BF16), preferred_element_type=F32)
        outs.append(_rows_to_heads(o_t / l, SWA_GROUP, tq))
    o_ref[...] = jnp.concatenate(outs, axis=1).astype(o_ref.dtype)


def _swa_bias():
    tq = SWA_TQ
    slopes = _alibi_slopes().reshape(SWA_KV_HEADS, SWA_GROUP)
    x = np.arange(2 * tq)[:, None] - tq
    q = np.arange(tq)[None, :]
    dist = q - x
    valid = (dist >= 0) & (dist < SWA_WINDOW)
    out = np.empty((2, SWA_KV_HEADS, 2 * tq, SWA_GROUP * tq), np.float32)
    for var in range(2):
        v = valid & ((x >= 0) | (var == 1))
        for g in range(SWA_KV_HEADS):
            for r in range(SWA_GROUP):
                out[var, g, :, r * tq:(r + 1) * tq] = np.where(v, -slopes[g, r] * dist, MASKED)
    return out


def _swa_attention(qkv, v_t, sinks):
    t = qkv.shape[0]
    tq = SWA_TQ
    kcol = Q_DIM // LANES
    bias = jnp.asarray(_swa_bias())
    sink_rows = jnp.repeat(sinks.astype(F32).reshape(SWA_KV_HEADS, 1, SWA_GROUP), tq, axis=2)
    prev = lambda i: jnp.maximum(i - 1, 0)
    return pl.pallas_call(
        _swa_kernel,
        grid=(t // tq,),
        in_specs=[pl.BlockSpec((tq, Q_DIM), lambda i: (i, 0)),
                  pl.BlockSpec((tq, LANES), lambda i: (prev(i), kcol)),
                  pl.BlockSpec((tq, LANES), lambda i: (i, kcol)),
                  pl.BlockSpec((1, LANES, tq), lambda i: (prev(i), 0, 0)),
                  pl.BlockSpec((1, LANES, tq), lambda i: (i, 0, 0)),
                  pl.BlockSpec((1,) + bias.shape[1:], lambda i: (jnp.minimum(i, 1), 0, 0, 0)),
                  pl.BlockSpec(sink_rows.shape, lambda i: (0, 0, 0))],
        out_specs=pl.BlockSpec((tq, Q_DIM), lambda i: (i, 0)),
        out_shape=jax.ShapeDtypeStruct((t, Q_DIM), BF16),
        compiler_params=_cparams(("parallel",)),
        name="swa_attention",
    )(qkv, qkv, qkv, v_t, v_t, bias, sink_rows)


def _swa_heads(x, norm_g, w_in, sinks):
    w = w_in.astype(BF16)
    n_qk = Q_DIM + SWA_KV_HEADS * HEAD_DIM
    qk, v_t = _norm_proj(x, norm_g, [w[:, :n_qk], w[:, n_qk:]], [BF16, BF16], chunks=[None, SWA_TQ])
    return _swa_attention(qk, v_t, sinks)


FOX_PARTS = 3


def _split_bf16(v):
    parts, rest = [], v
    for _ in range(FOX_PARTS):
        p = rest.astype(BF16)
        parts.append(p)
        rest = rest - p.astype(F32)
    return parts


def _fox_prep_kernel(f_ref, b_ref, aug_ref, cref_ref, carry_ref):
    tm = f_ref.shape[0]

    @pl.when(pl.program_id(0) == 0)
    def _():
        carry_ref[...] = jnp.zeros_like(carry_ref)

    logf = jax.nn.log_sigmoid(f_ref[...] + b_ref[...])
    row = lax.broadcasted_iota(jnp.int32, (tm, tm), 0)
    col = lax.broadcasted_iota(jnp.int32, (tm, tm), 1)
    tri = jnp.where(row >= col, 1.0, 0.0).astype(BF16)
    local = sum(jnp.dot(tri, p, preferred_element_type=F32) for p in _split_bf16(logf))
    cum = local + carry_ref[...]
    carry_ref[...] = cum[tm - 1:tm, :]
    first = cum[0:1, :]
    cref_ref[0] = jnp.broadcast_to(first, cref_ref.shape[1:])
    hi, mid, lo = _split_bf16(first - cum)
    lane = lax.broadcasted_iota(jnp.int32, (tm, LANES), 1)
    zero = jnp.zeros_like(hi)
    aug_ref[...] = jnp.where(lane < N_HEADS, hi, jnp.where(lane < 2 * N_HEADS, mid,
                                                            jnp.where(lane < 3 * N_HEADS, lo, zero)))


def _fox_prep(f_logit3, f_bias3):
    t = f_logit3.shape[0]
    nt = t // FOX_T
    return pl.pallas_call(
        _fox_prep_kernel,
        grid=(nt,),
        in_specs=[pl.BlockSpec((FOX_T, LANES), lambda i: (i, 0)), pl.BlockSpec((1, LANES), lambda i: (0, 0))],
        out_specs=[pl.BlockSpec((FOX_T, LANES), lambda i: (i, 0)), pl.BlockSpec((1, 8, LANES), lambda i: (i, 0, 0))],
        out_shape=[jax.ShapeDtypeStruct((t, LANES), BF16), jax.ShapeDtypeStruct((nt, 8, LANES), F32)],
        scratch_shapes=[pltpu.VMEM((1, LANES), F32)],
        compiler_params=_cparams(("arbitrary",)),
        name="fox_prep",
    )(f_logit3, f_bias3)


def _fox_kernel(cref_ref, q_ref, k_ref, aug_ref, vt_ref, o_ref, m_ref, l_ref, acc_ref, qt_ref, s_ref):
    pair, qi = pl.program_id(0), pl.program_id(1)
    tq = q_ref.shape[0]
    tk = FOX_T
    lane = lax.broadcasted_iota(jnp.int32, (tq, LANES), 1)
    q2 = q_ref[...] * jnp.asarray(HEAD_DIM ** -0.5, BF16)
    for hh in range(2):
        head = 2 * pair + hh
        pick = (lane % N_HEADS == head) & (lane < FOX_PARTS * N_HEADS)
        q_ext = jnp.concatenate([jnp.where(lane // HEAD_DIM == hh, q2, jnp.zeros_like(q2)).astype(F32),
                                 jnp.where(pick, 1.0, 0.0)], axis=1)
        qt_ref[hh] = q_ext.T.astype(BF16)
    m_ref[...] = jnp.full_like(m_ref, M_INIT)
    l_ref[...] = jnp.zeros_like(l_ref)
    acc_ref[...] = jnp.zeros_like(acc_ref)
    ratio = tq // tk

    def scores(kj, hh, q_from=0):
        start = pl.multiple_of(kj * tk, tk)
        k_ext = jnp.concatenate([k_ref[pl.ds(start, tk), :], aug_ref[pl.ds(start, tk), :]], axis=1)
        s_ref[hh, :, q_from:] = jnp.dot(k_ext, qt_ref[hh, :, q_from:], preferred_element_type=F32)

    def consume(kj, hh, key_offset):
        head = 2 * pair + hh
        q_from = 0 if key_offset is None else key_offset
        s = s_ref[hh, :, q_from:]
        if key_offset is not None:
            key = lax.broadcasted_iota(jnp.int32, s.shape, 0)
            qry = lax.broadcasted_iota(jnp.int32, s.shape, 1)
            s = jnp.where(key <= qry, s, MASKED)
        off = cref_ref[qi * ratio, head] - cref_ref[kj, head]
        m_old = m_ref[hh, :, q_from:]
        m_new = jnp.maximum(m_old, jnp.max(s, axis=0, keepdims=True) + off)
        e = jnp.exp(s - (m_new - off))
        alpha = jnp.exp(m_old - m_new)
        m_ref[hh, :, q_from:] = m_new
        l_ref[hh, :, q_from:] = alpha * l_ref[hh, :, q_from:] + jnp.sum(e, axis=0, keepdims=True)
        v_t = vt_ref[kj, hh * HEAD_DIM:(hh + 1) * HEAD_DIM, :]
        acc_ref[hh, :, q_from:] = (alpha * acc_ref[hh, :, q_from:]
                                   + jnp.dot(v_t, e.astype(BF16), preferred_element_type=F32))

    n_full = qi * ratio
    scores(0, 0)

    def body(j, carry):
        for d in range(ratio):
            kj = j * ratio + d
            scores(kj, 1)
            consume(kj, 0, None)
            scores(kj + 1, 0)
            consume(kj, 1, None)
        return carry

    lax.fori_loop(0, qi, body, 0)
    for d in range(ratio):
        kj = n_full + d
        scores(kj, 1, d * tk)
        consume(kj, 0, d * tk)
        if d + 1 < ratio:
            scores(kj + 1, 0, (d + 1) * tk)
        consume(kj, 1, d * tk)
    o_t = jnp.concatenate([acc_ref[hh] / l_ref[hh] for hh in range(2)], axis=0)
    o_ref[...] = o_t.T.astype(o_ref.dtype)


def _fox_attention(qkv, aug, v_t3, cref):
    t = qkv.shape[0]
    tq = FOX_TQ
    nk = t // FOX_T
    kcol = Q_DIM // LANES
    return pl.pallas_call(
        _fox_kernel,
        grid=(N_HEADS // 2, t // tq),
        in_specs=[pl.BlockSpec(memory_space=pltpu.SMEM),
                  pl.BlockSpec((tq, LANES), lambda p, i: (i, p)),
                  pl.BlockSpec((t, LANES), lambda p, i: (0, kcol + p)),
                  pl.BlockSpec((t, LANES), lambda p, i: (0, 0)),
                  pl.BlockSpec((nk, LANES, FOX_T), lambda p, i: (0, p, 0))],
        out_specs=pl.BlockSpec((tq, LANES), lambda p, i: (i, p)),
        out_shape=jax.ShapeDtypeStruct((t, Q_DIM), BF16),
        scratch_shapes=[pltpu.VMEM((2, 1, tq), F32), pltpu.VMEM((2, 1, tq), F32),
                        pltpu.VMEM((2, HEAD_DIM, tq), F32), pltpu.VMEM((2, 2 * LANES, tq), BF16), pltpu.VMEM((2, FOX_T, tq), F32)],
        compiler_params=_cparams(("parallel", "arbitrary")),
        name="fox_attention",
    )(cref, qkv, qkv, aug, v_t3)


def _fox_heads(x, norm_g, w_in, f_bias):
    t = x.shape[0]
    w_f = w_in[:, 3 * Q_DIM:]
    pad = jnp.zeros((D_MODEL, LANES - FOX_PARTS * N_HEADS), w_f.dtype)
    w_f3 = jnp.concatenate([w_f] * FOX_PARTS + [pad], axis=1).astype(BF16)
    b3 = jnp.concatenate([f_bias.astype(F32)] * FOX_PARTS + [jnp.zeros((LANES - FOX_PARTS * N_HEADS,), F32)])
    w = w_in.astype(BF16)
    qk, v_t3, f_logit3 = _norm_proj(x, norm_g, [w[:, :2 * Q_DIM], w[:, 2 * Q_DIM:3 * Q_DIM], w_f3], [BF16, BF16, F32],
                                    chunks=[None, FOX_T, None])
    aug, cref = _fox_prep(f_logit3, b3.reshape(1, LANES))
    return _fox_attention(qk, aug, v_t3, cref[:, 0, :N_HEADS])


def _nsa_compress_kernel(a_ref, pos_ref, w1_ref, w2_ref, o_ref):
    tc = a_ref.shape[1]
    half = a_ref.shape[2]
    a = a_ref[0]
    pos = pos_ref[0]
    top = jnp.dot((a + pos[:, :half]).astype(BF16), w1_ref[0, :half], preferred_element_type=F32)
    bot = jnp.dot((a + pos[:, half:]).astype(BF16), w1_ref[0, half:], preferred_element_type=F32)
    hid = top + pltpu.roll(bot, tc - 1, 0)
    o_ref[0] = jnp.dot(jax.nn.gelu(hid).astype(BF16), w2_ref[0], preferred_element_type=F32)


def _nsa_compress(a, pos, w1, w2):
    _, tc, width = a.shape
    return pl.pallas_call(
        _nsa_compress_kernel,
        grid=(2, NSA_KV_HEADS // 2),
        in_specs=[pl.BlockSpec((1, tc, width), lambda kv, gp: (kv * (NSA_KV_HEADS // 2) + gp, 0, 0)),
                  pl.BlockSpec((1, 1, 2 * width), lambda kv, gp: (kv, 0, 0)),
                  pl.BlockSpec((1, 2 * width, 2 * CMP_HIDDEN), lambda kv, gp: (kv, 0, 0)),
                  pl.BlockSpec((1, 2 * CMP_HIDDEN, LANES), lambda kv, gp: (kv, 0, 0))],
        out_specs=pl.BlockSpec((1, tc, LANES), lambda kv, gp: (kv, 0, gp)),
        out_shape=jax.ShapeDtypeStruct((2, tc, NSA_KV_DIM), F32),
        compiler_params=_cparams(("parallel", "parallel")),
        name="nsa_compress",
    )(a, pos, w1, w2)


NSA_WQ = 256
NSA_WIN_TILES = (NSA_WINDOW + NSA_TQ) // NSA_WQ


def _nsa_kernel(slopes_ref, q_ref, ks_ref, *refs):
    kw_refs, refs = refs[:NSA_WIN_TILES], refs[NSA_WIN_TILES:]
    vst_ref, refs = refs[0], refs[1:]
    vw_refs, refs = refs[:NSA_WIN_TILES], refs[NSA_WIN_TILES:]
    (kcmp_ref, vov_ref, gates_ref, ksel_ext_ref, kwin_ext_ref, kcmp_ext_ref, qext_ref,
     o_ref, selt_ref, m_ref, l_ref, acc_ref, qt_ref, s_ref, scmp_ref, cm_ref, cl_ref, oc_ref, sw_ref,
     active_ref, list_ref) = refs
    g, qi = pl.program_id(0), pl.program_id(1)
    tq = q_ref.shape[0]
    tc = kcmp_ref.shape[0]
    n_sel = selt_ref.shape[0]
    rr = NSA_GROUP
    t0 = qi * tq
    hg = g % 2
    slope = [slopes_ref[g * rr + r] for r in range(rr)]

    half = _lane_half((tq, LANES))
    parts = []
    for r in range(rr):
        q_pair = q_ref[:, (r // 2) * LANES:(r // 2 + 1) * LANES]
        src = jnp.where(hg == r % 2, q_pair, _swap_halves(q_pair))
        parts.append(jnp.where(half == hg, src, jnp.zeros_like(src)) * jnp.asarray(HEAD_DIM ** -0.5, BF16))
    qp = jnp.concatenate(parts, axis=0)
    qt_ref[0:LANES, :] = qp.astype(F32).T.astype(BF16)

    @pl.when(qi == 0)
    def _():
        qt_ref[LANES:, :] = qext_ref[0]

    def lanes_of(s, r):
        return s[:, r * tq:(r + 1) * tq]

    chunk_span = NSA_CMP_CHUNK * CMP_STRIDE
    last_chunk = ((t0 + tq - CMP_BLOCK) // CMP_STRIDE) // NSA_CMP_CHUNK
    nl16_q = (CMP_STRIDE * lax.broadcasted_iota(jnp.int32, (NSA_CMP_CHUNK, tq), 0)
              - lax.broadcasted_iota(jnp.int32, (NSA_CMP_CHUNK, tq), 1)).astype(F32)
    cm_ref[...] = jnp.full_like(cm_ref, M_INIT)

    def cmp_rows(ch):
        return pl.ds(ch * NSA_CMP_CHUNK, NSA_CMP_CHUNK)

    def cmp_offset(ch, r):
        return slope[r] * float(ch * chunk_span)

    def cmp_scores(ch, masked):
        k_full = jnp.concatenate([kcmp_ref[cmp_rows(ch), :], kcmp_ext_ref[...]], axis=1)
        s = jnp.dot(k_full, qt_ref[...], preferred_element_type=F32)
        if masked:
            visible = nl16_q <= (t0 - (CMP_BLOCK - 1) - ch * chunk_span).astype(F32)
            s = jnp.concatenate([jnp.where(visible, lanes_of(s, r), MASKED) for r in range(rr)], axis=1)
        scmp_ref[cmp_rows(ch), :] = s
        return [jnp.max(lanes_of(s, r), axis=0, keepdims=True) + cmp_offset(ch, r) for r in range(rr)]

    for last in range(tc // NSA_CMP_CHUNK):
        @pl.when(last_chunk == last)
        def _(last=last):
            maxima = [cmp_scores(ch, ch >= last - 1) for ch in range(last + 1)]
            pv = None
            for r in range(rr):
                cm_ref[r] = functools.reduce(jnp.maximum, [m[r] for m in maxima] + [cm_ref[r]])
            sums = [jnp.zeros((1, tq), F32) for _ in range(rr)]
            for ch in range(last + 1):
                s = scmp_ref[cmp_rows(ch), :]
                e_all = []
                for r in range(rr):
                    e = jnp.exp(lanes_of(s, r) - (cm_ref[r] - cmp_offset(ch, r)))
                    sums[r] = sums[r] + jnp.sum(e, axis=0, keepdims=True)
                    e_all.append(e.astype(BF16))
                part = jnp.dot(vov_ref[ch, 0], jnp.concatenate(e_all, axis=1), preferred_element_type=F32)
                pv = part if pv is None else pv + part
            for r in range(rr):
                cl_ref[r] = sums[r]
            oc_ref[...] = pv
    inv_l = [jnp.where(cl_ref[r] > 0.0, 1.0 / cl_ref[r], 0.0) for r in range(rr)]
    o_cmp = [oc_ref[0:HEAD_DIM, r * tq:(r + 1) * tq] * inv_l[r] for r in range(rr)]

    imp = functools.reduce(jnp.add, [oc_ref[HEAD_DIM:, r * tq:(r + 1) * tq] * inv_l[r] for r in range(rr)])
    blk = lax.broadcasted_iota(jnp.int32, (n_sel, tq), 0)
    cur = (t0 + lax.broadcasted_iota(jnp.int32, (n_sel, tq), 1)) // SEL_BLOCK
    blk_f = blk.astype(F32)
    forced = jnp.where(blk == 0, 1.0, jnp.where(blk == cur, 1.0, jnp.where(blk == cur - 1, 1.0, 0.0)))
    score = jnp.where(forced > 0.0, TAKEN, jnp.where(blk <= cur, imp, -1.0))

    selt_ref[...] = score
    row_step = min(n_sel, NSA_TOPK_ROW_STEP)
    variant = ((t0 + tq - 1) // SEL_BLOCK) // row_step

    wq = NSA_WQ
    n_back = NSA_WINDOW // wq
    win_jobs = [(sub, b) for sub in range(tq // wq) for b in range(n_back + 1)]
    assert len(win_jobs) <= N_SELECT - 3

    def win_scores(job):
        sub, b = win_jobs[job]
        k_full = jnp.concatenate([kw_refs[sub + b][...], kwin_ext_ref[...]], axis=1)
        q_cols = jnp.concatenate([qt_ref[:, r * tq + sub * wq:r * tq + (sub + 1) * wq] for r in range(rr)], axis=1)
        sw_ref[job] = jnp.dot(k_full, q_cols, preferred_element_type=F32)

    for v in range(n_sel // row_step):
        rows = row_step * (v + 1)

        @pl.when(variant == v)
        def _(rows=rows):
            ids = lax.broadcasted_iota(jnp.int32, (rows, tq), 0).astype(F32)
            sc = selt_ref[0:rows, :]
            for rnd in range(N_SELECT - 3):
                if rnd < len(win_jobs):
                    win_scores(rnd)
                best = jnp.max(sc, axis=0, keepdims=True)
                first = jnp.min(jnp.where(sc == best, ids, float(n_sel)), axis=0, keepdims=True)
                sc = jnp.where(ids == first, TAKEN, sc)
            selt_ref[0:rows, :] = sc

    selt_ref[...] = jnp.where(blk <= cur, jnp.where(selt_ref[...] == TAKEN, 1.0, 0.0), 0.0)

    m_ref[...] = jnp.full_like(m_ref, M_INIT)
    l_ref[...] = jnp.zeros_like(l_ref)
    acc_ref[...] = jnp.zeros_like(acc_ref)
    blocks_per_chunk = NSA_TK // SEL_BLOCK

    for c in range(n_sel // blocks_per_chunk):
        active_ref[c] = jnp.max(selt_ref[c * blocks_per_chunk:(c + 1) * blocks_per_chunk, :])

    def compact(c, n):
        list_ref[n] = c
        return n + (active_ref[c] > 0.0).astype(jnp.int32)

    n_active = lax.fori_loop(0, t0 // NSA_TK + 1, compact, 0)

    def sel_scores(c, buf):
        start = pl.multiple_of(c * NSA_TK, NSA_TK)
        rows = selt_ref[pl.ds(pl.multiple_of(c * blocks_per_chunk, blocks_per_chunk), blocks_per_chunk), :]
        mask_rows = jnp.concatenate([jnp.where(rows > 0.5, 0.0, MASKED), jnp.zeros_like(rows)], axis=0)
        first = LANES + buf * NSA_EXT_MASK_ROWS
        qt_ref[first:first + NSA_EXT_MASK_ROWS, :] = jnp.concatenate([mask_rows] * rr, axis=1).astype(BF16)
        k_full = jnp.concatenate([ks_ref[pl.ds(start, NSA_TK), :], ksel_ext_ref[buf]], axis=1)
        s_ref[buf] = jnp.dot(k_full, qt_ref[...], preferred_element_type=F32)

    def sel_consume(c, buf, diagonal):
        shift = (c * NSA_TK - t0).astype(F32)
        if diagonal:
            key = lax.broadcasted_iota(jnp.int32, (NSA_TK, tq), 0) + c * NSA_TK
            visible = key <= lax.broadcasted_iota(jnp.int32, (NSA_TK, tq), 1) + t0
        e_all, alpha_all = [], []
        for r in range(rr):
            off = slope[r] * shift
            s = s_ref[buf][:, r * tq:(r + 1) * tq]
            if diagonal:
                s = jnp.where(visible, s, MASKED)
            m_old = m_ref[r]
            m_new = jnp.maximum(m_old, jnp.max(s, axis=0, keepdims=True) + off)
            e = jnp.exp(s - (m_new - off))
            alpha = jnp.exp(m_old - m_new)
            m_ref[r] = m_new
            l_ref[r] = alpha * l_ref[r] + jnp.sum(e, axis=0, keepdims=True)
            e_all.append(e.astype(BF16))
            alpha_all.append(alpha)
        pv = jnp.dot(vst_ref[c], jnp.concatenate(e_all, axis=1), preferred_element_type=F32)
        acc_ref[...] = jnp.concatenate(alpha_all, axis=1) * acc_ref[...] + pv

    n_before = n_active - 1
    sel_scores(list_ref[0], 0)

    def sel_pair(j, carry):
        c0, c1, c2 = list_ref[2 * j], list_ref[2 * j + 1], list_ref[2 * j + 2]
        sel_scores(c1, 1)
        sel_consume(c0, 0, False)
        sel_scores(c2, 0)
        sel_consume(c1, 1, False)
        return carry

    lax.fori_loop(0, n_before // 2, sel_pair, 0)

    @pl.when(n_before % 2 == 1)
    def _():
        c0, c_last = list_ref[n_before - 1], list_ref[n_before]
        sel_scores(c_last, 1)
        sel_consume(c0, 0, False)
        sel_consume(c_last, 1, True)

    @pl.when(n_before % 2 == 0)
    def _():
        sel_consume(list_ref[n_before], 0, True)

    xw = lax.broadcasted_iota(jnp.int32, (wq, wq), 0)
    qw = lax.broadcasted_iota(jnp.int32, (wq, wq), 1)
    o_win = [[] for _ in range(rr)]
    for sub in range(tq // wq):
        s_win, off_win = [], []
        for b in range(n_back + 1):
            back = n_back - b
            s_t = sw_ref[sub * (n_back + 1) + b]
            per_head = [s_t[:, r * wq:(r + 1) * wq] for r in range(rr)]
            if back == n_back:
                per_head = [jnp.where(xw > qw, s, MASKED) for s in per_head]
            if back == 0:
                per_head = [jnp.where(xw <= qw, s, MASKED) for s in per_head]
            s_win.append(per_head)
            in_range = qi * (tq // wq) + sub >= back
            off_win.append([jnp.where(in_range, -slope[r] * float(back * wq), MASKED) for r in range(rr)])
        e_win, l_win = [], []
        for r in range(rr):
            m = functools.reduce(jnp.maximum, [jnp.max(s_win[b][r], axis=0, keepdims=True) + off_win[b][r]
                                               for b in range(n_back + 1)])
            e_tiles = [jnp.exp(s_win[b][r] - (m - off_win[b][r])) for b in range(n_back + 1)]
            l_win.append(functools.reduce(jnp.add, [jnp.sum(e, axis=0, keepdims=True) for e in e_tiles]))
            e_win.append(jnp.concatenate([e.astype(BF16) for e in e_tiles], axis=0))
        v_win = jnp.concatenate([vw_refs[sub + b][0] for b in range(n_back + 1)], axis=1)
        o_sub = jnp.dot(v_win, jnp.concatenate(e_win, axis=1), preferred_element_type=F32)
        for r in range(rr):
            o_win[r].append(o_sub[:, r * wq:(r + 1) * wq] / l_win[r])
    o_win = [jnp.concatenate(parts, axis=1) for parts in o_win]

    gate = jax.nn.sigmoid(gates_ref[0])
    merged = []
    for r in range(rr):
        o_sel = lanes_of(acc_ref[...], r) / l_ref[r]
        merged.append(gate[3 * r:3 * r + 1] * o_cmp[r] + gate[3 * r + 1:3 * r + 2] * o_sel
                      + gate[3 * r + 2:3 * r + 3] * o_win[r])
    o_ref[...] = jnp.concatenate(merged, axis=0).T.astype(o_ref.dtype)


def _nsa_overlap_t(n_sel, tc):
    cmp_start = np.arange(tc - 1) * CMP_STRIDE
    sel_start = np.arange(n_sel) * SEL_BLOCK
    ov = np.clip(np.minimum(cmp_start[None, :] + CMP_BLOCK, sel_start[:, None] + SEL_BLOCK)
                 - np.maximum(cmp_start[None, :], sel_start[:, None]), 0, None) / CMP_BLOCK
    return np.concatenate([ov, np.zeros((n_sel, 1))], axis=1).astype(np.float32)


NSA_EXT_MASK_ROWS = 16
NSA_EXT_SEL = 32
NSA_EXT_WIN = 48
NSA_EXT_CMP = 64
NSA_CMP_CHUNK = 256
NSA_TOPK_ROW_STEP = 64
SLOPE_PIECES = 3


def _bf16_pieces(v):
    out, rest = [], np.asarray(v, np.float32)
    for _ in range(SLOPE_PIECES):
        p = rest.astype(BF16).astype(np.float32)
        out.append(p)
        rest = rest - p
    return out


def _nsa_extensions(tq):
    ksel = np.zeros((2, NSA_TK, LANES), np.float32)
    x = np.arange(NSA_TK)
    for buf in range(2):
        ksel[buf, x, buf * NSA_EXT_MASK_ROWS + x // SEL_BLOCK] = 1.0
    for j in range(SLOPE_PIECES):
        ksel[:, :, NSA_EXT_SEL + j] = 256 * (x // 256)
        ksel[:, :, NSA_EXT_SEL + SLOPE_PIECES + j] = x % 256
    kwin = np.zeros((NSA_WQ, LANES), np.float32)
    xw = np.arange(NSA_WQ)
    for j in range(SLOPE_PIECES):
        kwin[:, NSA_EXT_WIN + j] = 256 * (xw // 256)
        kwin[:, NSA_EXT_WIN + SLOPE_PIECES + j] = xw % 256
    kcmp = np.zeros((NSA_CMP_CHUNK, LANES), np.float32)
    for j in range(SLOPE_PIECES):
        kcmp[:, NSA_EXT_CMP + j] = np.arange(NSA_CMP_CHUNK)
    pieces = _bf16_pieces(_alibi_slopes())
    qext = np.zeros((NSA_KV_HEADS, LANES, NSA_GROUP * tq), np.float32)
    for g in range(NSA_KV_HEADS):
        for r in range(NSA_GROUP):
            cols = slice(r * tq, (r + 1) * tq)
            for j in range(SLOPE_PIECES):
                p = pieces[j][g * NSA_GROUP + r]
                qext[g, NSA_EXT_SEL + j, cols] = p
                qext[g, NSA_EXT_SEL + SLOPE_PIECES + j, cols] = p
                qext[g, NSA_EXT_WIN + j, cols] = p
                qext[g, NSA_EXT_WIN + SLOPE_PIECES + j, cols] = p
                qext[g, NSA_EXT_CMP + j, cols] = CMP_STRIDE * p
    return jnp.asarray(ksel, BF16), jnp.asarray(kwin, BF16), jnp.asarray(kcmp, BF16), jnp.asarray(qext, BF16)


def _nsa_attention(main, vs_t3, vw_t, k_cmp, v_cmp_t, gates_t):
    t = main.shape[0]
    tq = NSA_TQ
    tc = k_cmp.shape[0]
    n_sel = t // SEL_BLOCK
    nc = t // NSA_TK
    ks_col = Q_DIM // LANES
    kw_col = ks_col + NSA_KV_DIM // LANES
    n_ch = tc // NSA_CMP_CHUNK
    ov_t = jnp.asarray(_nsa_overlap_t(n_sel, tc).reshape(n_sel, n_ch, NSA_CMP_CHUNK).transpose(1, 0, 2), BF16)
    v_cmp_t = v_cmp_t.reshape(NSA_KV_HEADS, HEAD_DIM, n_ch, NSA_CMP_CHUNK).transpose(2, 0, 1, 3)
    vov = jnp.concatenate([v_cmp_t, jnp.broadcast_to(ov_t[:, None], (n_ch, NSA_KV_HEADS, n_sel, NSA_CMP_CHUNK))],
                          axis=2)
    slopes = jnp.asarray(_alibi_slopes())
    ksel_ext, kwin_ext, kcmp_ext, qext = _nsa_extensions(tq)

    def win_tile(i, j):
        return jnp.maximum(i * (tq // NSA_WQ) - NSA_WINDOW // NSA_WQ + j, 0)

    def kw_spec(j):
        return pl.BlockSpec((NSA_WQ, LANES), lambda g, i: (win_tile(i, j), kw_col + g // 2))

    def vw_spec(j):
        return pl.BlockSpec((1, HEAD_DIM, NSA_WQ), lambda g, i: (win_tile(i, j), g, 0))

    backs = list(range(NSA_WIN_TILES))
    return pl.pallas_call(
        _nsa_kernel,
        grid=(NSA_KV_HEADS, t // tq),
        in_specs=[pl.BlockSpec(memory_space=pltpu.SMEM),
                  pl.BlockSpec((tq, NSA_GROUP * HEAD_DIM), lambda g, i: (i, g)),
                  pl.BlockSpec((t, LANES), lambda g, i: (0, ks_col + g // 2))]
                 + [kw_spec(b) for b in backs]
                 + [pl.BlockSpec((nc, HEAD_DIM, NSA_TK), lambda g, i: (0, g, 0))]
                 + [vw_spec(b) for b in backs]
                 + [pl.BlockSpec((tc, LANES), lambda g, i: (0, g // 2)),
                    pl.BlockSpec((n_ch, 1, HEAD_DIM + n_sel, NSA_CMP_CHUNK), lambda g, i: (0, g, 0, 0)),
                    pl.BlockSpec((1, 16, tq), lambda g, i: (g, 0, i)),
                    pl.BlockSpec(ksel_ext.shape, lambda g, i: (0, 0, 0)),
                    pl.BlockSpec(kwin_ext.shape, lambda g, i: (0, 0)),
                    pl.BlockSpec(kcmp_ext.shape, lambda g, i: (0, 0)),
                    pl.BlockSpec((1,) + qext.shape[1:], lambda g, i: (g, 0, 0))],
        out_specs=pl.BlockSpec((tq, NSA_GROUP * HEAD_DIM), lambda g, i: (i, g)),
        out_shape=jax.ShapeDtypeStruct((t, Q_DIM), BF16),
        scratch_shapes=[pltpu.VMEM((n_sel, tq), F32), pltpu.VMEM((NSA_GROUP, 1, tq), F32),
                        pltpu.VMEM((NSA_GROUP, 1, tq), F32), pltpu.VMEM((HEAD_DIM, NSA_GROUP * tq), F32),
                        pltpu.VMEM((2 * LANES, NSA_GROUP * tq), BF16),
                        pltpu.VMEM((2, NSA_TK, NSA_GROUP * tq), F32),
                        pltpu.VMEM((tc, NSA_GROUP * tq), F32), pltpu.VMEM((NSA_GROUP, 1, tq), F32),
                        pltpu.VMEM((NSA_GROUP, 1, tq), F32), pltpu.VMEM((HEAD_DIM + n_sel, NSA_GROUP * tq), F32),
                        pltpu.VMEM((tq // NSA_WQ * (NSA_WINDOW // NSA_WQ + 1), NSA_WQ, NSA_GROUP * NSA_WQ), F32),
                        pltpu.SMEM((nc,), F32), pltpu.SMEM((nc,), jnp.int32)],
        compiler_params=_cparams(("parallel", "arbitrary")),
        name="nsa_attention",
    )(slopes, main, main, *([main] * NSA_WIN_TILES), vs_t3, *([vw_t] * NSA_WIN_TILES), k_cmp, vov, gates_t,
      ksel_ext, kwin_ext, kcmp_ext, qext)


def _nsa_heads(x, norm_g, w_in, cmp_pos, cmp_w1, cmp_w2):
    t = x.shape[0]
    kv = NSA_KV_DIM
    col = lambda i: w_in[:, Q_DIM + i * kv:Q_DIM + (i + 1) * kv]
    w_main = jnp.concatenate([w_in[:, :Q_DIM], col(2), col(4)], axis=1).astype(BF16)
    w_cmp = jnp.concatenate([col(0), col(1)], axis=1).astype(BF16)
    n_gate = 3 * N_HEADS
    w_gate = jnp.concatenate([w_in[:, Q_DIM + 6 * kv:], jnp.zeros((D_MODEL, LANES - n_gate), w_in.dtype)],
                             axis=1).astype(BF16)
    main, vs_t3, vw_t3, kcvc, gate_logit = _norm_proj(
        x, norm_g, [w_main, col(3).astype(BF16), col(5).astype(BF16), w_cmp, w_gate], [BF16, BF16, BF16, F32, F32],
        chunks=[None, NSA_TK, NSA_WQ, "cols", None])

    tc = t // CMP_STRIDE
    a = kcvc.reshape(kcvc.shape[0], tc, CMP_STRIDE * LANES)
    pair = jnp.eye(2, dtype=cmp_w1.dtype)
    pos = jnp.broadcast_to(cmp_pos[:, :, None, :], (2, CMP_BLOCK, 2, HEAD_DIM)).reshape(2, 1, 2 * CMP_BLOCK * HEAD_DIM)
    w1 = (cmp_w1.reshape(2, CMP_BLOCK, 1, HEAD_DIM, 1, CMP_HIDDEN) * pair.reshape(1, 1, 2, 1, 2, 1))
    w1 = w1.reshape(2, 2 * CMP_BLOCK * HEAD_DIM, 2 * CMP_HIDDEN)
    w2 = (cmp_w2.reshape(2, 1, CMP_HIDDEN, 1, HEAD_DIM) * pair.reshape(1, 2, 1, 2, 1))
    w2 = w2.reshape(2, 2 * CMP_HIDDEN, 2 * HEAD_DIM)
    cmp = _nsa_compress(a, pos, w1.astype(BF16), w2.astype(BF16))
    k_cmp = cmp[0].astype(BF16)
    v_cmp_t = cmp[1].T.astype(BF16)

    gates_t = gate_logit[:, :n_gate].reshape(t, NSA_KV_HEADS, 3 * NSA_GROUP).transpose(1, 2, 0)
    gates_t = jnp.pad(gates_t, ((0, 0), (0, 16 - 3 * NSA_GROUP), (0, 0)))
    return _nsa_attention(main, vs_t3, vw_t3, k_cmp, v_cmp_t, gates_t)


def kernel(x, attn_norm, mlp_norm, final_norm, nsa_w_in, nsa_cmp_pos, nsa_cmp_w1, nsa_cmp_w2, nsa_w_out,
           swa_w_in, swa_sinks, swa_w_out, fox_w_in, fox_f_bias, fox_w_out,
           mlp_w_up, mlp_conv_w, mlp_conv_b, mlp_w_down):
    assert x.shape[0] == 1, "the trunk is written for batch 1"
    h = x[0]
    for i in range(DEPTH):
        kind, j = i % N_MIXERS, i // N_MIXERS
        if kind == 0:
            o = _nsa_heads(h, attn_norm[i], nsa_w_in[j], nsa_cmp_pos[j], nsa_cmp_w1[j], nsa_cmp_w2[j])
            w_out = nsa_w_out[j]
        elif kind == 1:
            o = _swa_heads(h, attn_norm[i], swa_w_in[j], swa_sinks[j])
            w_out = swa_w_out[j]
        else:
            o = _fox_heads(h, attn_norm[i], fox_w_in[j], fox_f_bias[j])
            w_out = fox_w_out[j]
        h = _mixer_out_mlp(h, o, w_out.astype(BF16), mlp_norm[i], mlp_w_up[i].astype(BF16), mlp_conv_w[i],
                           mlp_conv_b[i].reshape(1, -1), mlp_w_down[i].astype(BF16),
                           final_g=final_norm if i == DEPTH - 1 else None)
    return h[None]
```

```python
import functools

import numpy as np
import jax
import jax.numpy as jnp
from jax import lax
from jax.experimental import pallas as pl
from jax.experimental.pallas import tpu as pltpu

F32 = jnp.float32
BF16 = jnp.bfloat16

D_MODEL = 1024
DEPTH = 4
N_MIXERS = 3
HEAD_DIM = 64
N_HEADS = 16
Q_DIM = N_HEADS * HEAD_DIM
ALIBI_MAX = 8.0
NORM_EPS = 1e-6
MASKED = -2e30
M_INIT = -1e30
TAKEN = -(2.0 ** 127)

NSA_KV_HEADS = 4
NSA_GROUP = 4
NSA_KV_DIM = NSA_KV_HEADS * HEAD_DIM
CMP_BLOCK = 32
CMP_STRIDE = 16
CMP_HIDDEN = 256
SEL_BLOCK = 64
N_SELECT = 16
NSA_WINDOW = 512
NSA_TQ = 512
NSA_TK = 512

SWA_KV_HEADS = 2
SWA_GROUP = 8
SWA_WINDOW = 128
SWA_TQ = 128

FOX_TQ = 1024
FOX_T = 512

LANES = 128
VMEM_LIMIT = 56 * 1024 * 1024


def _cparams(semantics, vmem=VMEM_LIMIT):
    return pltpu.CompilerParams(dimension_semantics=semantics, vmem_limit_bytes=vmem)


def _alibi_slopes():
    return np.asarray(2.0 ** (-ALIBI_MAX * np.arange(1, N_HEADS + 1) / N_HEADS), dtype=np.float32)


def _rms(x, g):
    ms = jnp.mean(x * x, axis=-1, keepdims=True)
    return x * lax.rsqrt(ms + NORM_EPS) * g


def _norm_proj_kernel(chunks, x_ref, g_ref, *refs):
    n_out = len(chunks)
    h = _rms(x_ref[...], g_ref[...]).astype(BF16)
    for w_ref, o_ref, chunk in zip(refs[:n_out], refs[n_out:], chunks):
        res = jnp.dot(h, w_ref[...], preferred_element_type=F32)
        if chunk is None:
            o_ref[...] = res.astype(o_ref.dtype)
        elif chunk == "cols":
            for c in range(o_ref.shape[0]):
                o_ref[c] = res[:, c * LANES:(c + 1) * LANES].astype(o_ref.dtype)
        else:
            for c in range(res.shape[0] // chunk):
                o_ref[c] = res[c * chunk:(c + 1) * chunk].T.astype(o_ref.dtype)


def _pick_tile(n, candidates):
    for c in candidates:
        if n % c == 0:
            return c
    raise ValueError(f"no tile for {n}")


def _norm_proj(x, g, weights, out_dtypes, chunks=None):
    t, d = x.shape
    tm = _pick_tile(t, (1024, 512, 256, 128))
    chunks = tuple(chunks) if chunks else (None,) * len(weights)
    out_specs, out_shape = [], []
    for w, dt, c in zip(weights, out_dtypes, chunks):
        n = w.shape[1]
        if c is None:
            out_specs.append(pl.BlockSpec((tm, n), lambda i: (i, 0)))
            out_shape.append(jax.ShapeDtypeStruct((t, n), dt))
        elif c == "cols":
            out_specs.append(pl.BlockSpec((n // LANES, tm, LANES), lambda i: (0, i, 0)))
            out_shape.append(jax.ShapeDtypeStruct((n // LANES, t, LANES), dt))
        else:
            out_specs.append(pl.BlockSpec((tm // c, n, c), lambda i: (i, 0, 0)))
            out_shape.append(jax.ShapeDtypeStruct((t // c, n, c), dt))
    return pl.pallas_call(
        functools.partial(_norm_proj_kernel, chunks),
        grid=(t // tm,),
        in_specs=[pl.BlockSpec((tm, d), lambda i: (i, 0)), pl.BlockSpec((1, d), lambda i: (0, 0))]
                 + [pl.BlockSpec(w.shape, lambda i: (0, 0)) for w in weights],
        out_specs=out_specs,
        out_shape=out_shape,
        compiler_params=_cparams(("parallel",)),
        name="norm_proj",
    )(x, g.reshape(1, d), *weights)


MLP_HALO = 16


def _mixer_out_mlp_kernel(final_norm, x_ref, xh_ref, o_ref, oh_ref, wo_ref, g_ref, fg_ref, wa_ref, wg_ref,
                          cwa_ref, cwg_ref, cba_ref, cbg_ref, wd_ref, y_ref, x1_ref, h_ref, acc_ref, ua_ref, ug_ref):
    i, j = pl.program_id(0), pl.program_id(1)
    tm = x_ref.shape[0]

    @pl.when(j == 0)
    def _():
        wo = wo_ref[...]
        x1_ref[0:MLP_HALO, :] = xh_ref[...] + jnp.dot(oh_ref[...], wo, preferred_element_type=F32)
        x1_ref[MLP_HALO:, :] = x_ref[...] + jnp.dot(o_ref[...], wo, preferred_element_type=F32)
        halo = _rms(x1_ref[0:MLP_HALO, :], g_ref[...])
        h_ref[0:MLP_HALO, :] = jnp.where(i > 0, halo, 0.0).astype(BF16)
        h_ref[MLP_HALO:, :] = _rms(x1_ref[MLP_HALO:, :], g_ref[...]).astype(BF16)
        acc_ref[...] = jnp.zeros_like(acc_ref)

    h = h_ref[...]

    def conv(w_ref, cw_ref, cb_ref, u_ref):
        u_ref[...] = jnp.dot(h, w_ref[...], preferred_element_type=F32)
        cw = cw_ref[...]
        return (cw[0:1] * u_ref[MLP_HALO - 2:MLP_HALO - 2 + tm, :] + cw[1:2] * u_ref[MLP_HALO - 1:MLP_HALO - 1 + tm, :]
                + cw[2:3] * u_ref[MLP_HALO:MLP_HALO + tm, :] + cb_ref[...])

    a = conv(wa_ref, cwa_ref, cba_ref, ua_ref)
    gt = conv(wg_ref, cwg_ref, cbg_ref, ug_ref)
    act = (jax.nn.silu(gt) * a).astype(BF16)
    acc_ref[...] += jnp.dot(act, wd_ref[...], preferred_element_type=F32)

    @pl.when(j == pl.num_programs(1) - 1)
    def _():
        y = x1_ref[MLP_HALO:, :] + acc_ref[...]
        y_ref[...] = _rms(y, fg_ref[...]) if final_norm else y


def _mixer_out_mlp(x, o, w_out, g, w_up, conv_w, conv_b, w_down, final_g=None):
    t, d = x.shape
    f = w_down.shape[0]
    q = o.shape[1]
    tm = _pick_tile(t, (512, 256, 128))
    tf = _pick_tile(f, (1408, 256, 128))
    nf = f // tf
    hb = tm // MLP_HALO
    halo_map = lambda i, j: (jnp.maximum(i * hb - 1, 0), 0)
    fg = g if final_g is None else final_g
    return pl.pallas_call(
        functools.partial(_mixer_out_mlp_kernel, final_g is not None),
        grid=(t // tm, nf),
        in_specs=[pl.BlockSpec((tm, d), lambda i, j: (i, 0)),
                  pl.BlockSpec((MLP_HALO, d), halo_map),
                  pl.BlockSpec((tm, q), lambda i, j: (i, 0)),
                  pl.BlockSpec((MLP_HALO, q), halo_map),
                  pl.BlockSpec((q, d), lambda i, j: (0, 0)),
                  pl.BlockSpec((1, d), lambda i, j: (0, 0)),
                  pl.BlockSpec((1, d), lambda i, j: (0, 0)),
                  pl.BlockSpec((d, tf), lambda i, j: (0, j)),
                  pl.BlockSpec((d, tf), lambda i, j: (0, nf + j)),
                  pl.BlockSpec((3, tf), lambda i, j: (0, j)),
                  pl.BlockSpec((3, tf), lambda i, j: (0, nf + j)),
                  pl.BlockSpec((1, tf), lambda i, j: (0, j)),
                  pl.BlockSpec((1, tf), lambda i, j: (0, nf + j)),
                  pl.BlockSpec((tf, d), lambda i, j: (j, 0))],
        out_specs=pl.BlockSpec((tm, d), lambda i, j: (i, 0)),
        out_shape=jax.ShapeDtypeStruct((t, d), F32),
        scratch_shapes=[pltpu.VMEM((tm + MLP_HALO, d), F32), pltpu.VMEM((tm + MLP_HALO, d), BF16),
                        pltpu.VMEM((tm, d), F32),
                        pltpu.VMEM((tm + MLP_HALO, tf), F32), pltpu.VMEM((tm + MLP_HALO, tf), F32)],
        compiler_params=_cparams(("parallel", "arbitrary")),
        name="mixer_out_mlp",
    )(x, x, o, o, w_out, g.reshape(1, d), fg.reshape(1, d), w_up, w_up, conv_w, conv_w, conv_b, conv_b, w_down)


def _swap_halves(q_pair):
    return pltpu.roll(q_pair.astype(F32), HEAD_DIM, 1).astype(BF16)


def _lane_half(shape):
    return lax.broadcasted_iota(jnp.int32, shape, 1) // HEAD_DIM


def _rows_to_heads(o_t, n_heads, tq):
    stacked = jnp.concatenate([o_t[:, r * tq:(r + 1) * tq] for r in range(n_heads)], axis=0)
    return stacked.T


def _swa_kernel(q_ref, kp_ref, kc_ref, vp_ref, vc_ref, bias_ref, sink_ref, o_ref):
    tq = q_ref.shape[0]
    k_ext = jnp.concatenate([kp_ref[...], kc_ref[...]], axis=0)
    v_ext = jnp.concatenate([vp_ref[0], vc_ref[0]], axis=1)
    half = _lane_half((tq, LANES))
    outs = []
    for g in range(SWA_KV_HEADS):
        parts = []
        for r in range(SWA_GROUP):
            hd = g * SWA_GROUP + r
            q_pair = q_ref[:, (hd // 2) * LANES:(hd // 2 + 1) * LANES]
            src = q_pair if hd % 2 == g else _swap_halves(q_pair)
            parts.append(jnp.where(half == g, src, jnp.zeros_like(src)))
        qp = jnp.concatenate(parts, axis=0)
        s = lax.dot_general(k_ext, qp, (((1,), (1,)), ((), ())), preferred_element_type=F32)
        s = s * (HEAD_DIM ** -0.5) + bias_ref[0, g]
        sink = sink_ref[g]
        m = jnp.maximum(jnp.max(s, axis=0, keepdims=True), sink)
        e = jnp.exp(s - m)
        l = jnp.sum(e, axis=0, keepdims=True) + jnp.exp(sink - m)
        o_t = jnp.dot(v_ext[g * HEAD_DIM:(g + 1) * HEAD_DIM], e.astype(BF16), preferred_element_type=F32)
        outs.append(_rows_to_heads(o_t / l, SWA_GROUP, tq))
    o_ref[...] = jnp.concatenate(outs, axis=1).astype(o_ref.dtype)


def _swa_bias():
    tq = SWA_TQ
    slopes = _alibi_slopes().reshape(SWA_KV_HEADS, SWA_GROUP)
    x = np.arange(2 * tq)[:, None] - tq
    q = np.arange(tq)[None, :]
    dist = q - x
    valid = (dist >= 0) & (dist < SWA_WINDOW)
    out = np.empty((2, SWA_KV_HEADS, 2 * tq, SWA_GROUP * tq), np.float32)
    for var in range(2):
        v = valid & ((x >= 0) | (var == 1))
        for g in range(SWA_KV_HEADS):
            for r in range(SWA_GROUP):
                out[var, g, :, r * tq:(r + 1) * tq] = np.where(v, -slopes[g, r] * dist, MASKED)
    return out


def _swa_attention(qkv, v_t, sinks):
    t = qkv.shape[0]
    tq = SWA_TQ
    kcol = Q_DIM // LANES
    bias = jnp.asarray(_swa_bias())
    sink_rows = jnp.repeat(sinks.astype(F32).reshape(SWA_KV_HEADS, 1, SWA_GROUP), tq, axis=2)
    prev = lambda i: jnp.maximum(i - 1, 0)
    return pl.pallas_call(
        _swa_kernel,
        grid=(t // tq,),
        in_specs=[pl.BlockSpec((tq, Q_DIM), lambda i: (i, 0)),
                  pl.BlockSpec((tq, LANES), lambda i: (prev(i), kcol)),
                  pl.BlockSpec((tq, LANES), lambda i: (i, kcol)),
                  pl.BlockSpec((1, LANES, tq), lambda i: (prev(i), 0, 0)),
                  pl.BlockSpec((1, LANES, tq), lambda i: (i, 0, 0)),
                  pl.BlockSpec((1,) + bias.shape[1:], lambda i: (jnp.minimum(i, 1), 0, 0, 0)),
                  pl.BlockSpec(sink_rows.shape, lambda i: (0, 0, 0))],
        out_specs=pl.BlockSpec((tq, Q_DIM), lambda i: (i, 0)),
        out_shape=jax.ShapeDtypeStruct((t, Q_DIM), BF16),
        compiler_params=_cparams(("parallel",)),
        name="swa_attention",
    )(qkv, qkv, qkv, v_t, v_t, bias, sink_rows)


def _swa_heads(x, norm_g, w_in, sinks):
    w = w_in.astype(BF16)
    n_qk = Q_DIM + SWA_KV_HEADS * HEAD_DIM
    qk, v_t = _norm_proj(x, norm_g, [w[:, :n_qk], w[:, n_qk:]], [BF16, BF16], chunks=[None, SWA_TQ])
    return _swa_attention(qk, v_t, sinks)


FOX_PARTS = 3


def _split_bf16(v):
    parts, rest = [], v
    for _ in range(FOX_PARTS):
        p = rest.astype(BF16)
        parts.append(p)
        rest = rest - p.astype(F32)
    return parts


def _fox_prep_kernel(f_ref, b_ref, aug_ref, cref_ref, carry_ref):
    tm = f_ref.shape[0]

    @pl.when(pl.program_id(0) == 0)
    def _():
        carry_ref[...] = jnp.zeros_like(carry_ref)

    logf = jax.nn.log_sigmoid(f_ref[...] + b_ref[...])
    row = lax.broadcasted_iota(jnp.int32, (tm, tm), 0)
    col = lax.broadcasted_iota(jnp.int32, (tm, tm), 1)
    tri = jnp.where(row >= col, 1.0, 0.0).astype(BF16)
    local = sum(jnp.dot(tri, p, preferred_element_type=F32) for p in _split_bf16(logf))
    cum = local + carry_ref[...]
    carry_ref[...] = cum[tm - 1:tm, :]
    first = cum[0:1, :]
    cref_ref[0] = jnp.broadcast_to(first, cref_ref.shape[1:])
    hi, mid, lo = _split_bf16(first - cum)
    lane = lax.broadcasted_iota(jnp.int32, (tm, LANES), 1)
    zero = jnp.zeros_like(hi)
    aug_ref[...] = jnp.where(lane < N_HEADS, hi, jnp.where(lane < 2 * N_HEADS, mid,
                                                            jnp.where(lane < 3 * N_HEADS, lo, zero)))


def _fox_prep(f_logit3, f_bias3):
    t = f_logit3.shape[0]
    nt = t // FOX_T
    return pl.pallas_call(
        _fox_prep_kernel,
        grid=(nt,),
        in_specs=[pl.BlockSpec((FOX_T, LANES), lambda i: (i, 0)), pl.BlockSpec((1, LANES), lambda i: (0, 0))],
        out_specs=[pl.BlockSpec((FOX_T, LANES), lambda i: (i, 0)), pl.BlockSpec((1, 8, LANES), lambda i: (i, 0, 0))],
        out_shape=[jax.ShapeDtypeStruct((t, LANES), BF16), jax.ShapeDtypeStruct((nt, 8, LANES), F32)],
        scratch_shapes=[pltpu.VMEM((1, LANES), F32)],
        compiler_params=_cparams(("arbitrary",)),
        name="fox_prep",
    )(f_logit3, f_bias3)


def _fox_kernel(cref_ref, q_ref, k_ref, aug_ref, vt_ref, o_ref, m_ref, l_ref, acc_ref, qt_ref, s_ref):
    pair, qi = pl.program_id(0), pl.program_id(1)
    tq = q_ref.shape[0]
    tk = FOX_T
    lane = lax.broadcasted_iota(jnp.int32, (tq, LANES), 1)
    q2 = q_ref[...] * jnp.asarray(HEAD_DIM ** -0.5, BF16)
    for hh in range(2):
        head = 2 * pair + hh
        pick = (lane % N_HEADS == head) & (lane < FOX_PARTS * N_HEADS)
        q_ext = jnp.concatenate([jnp.where(lane // HEAD_DIM == hh, q2, jnp.zeros_like(q2)).astype(F32),
                                 jnp.where(pick, 1.0, 0.0)], axis=1)
        qt_ref[hh] = q_ext.T.astype(BF16)
    m_ref[...] = jnp.full_like(m_ref, M_INIT)
    l_ref[...] = jnp.zeros_like(l_ref)
    acc_ref[...] = jnp.zeros_like(acc_ref)
    ratio = tq // tk

    def scores(kj, hh, q_from=0):
        start = pl.multiple_of(kj * tk, tk)
        k_ext = jnp.concatenate([k_ref[pl.ds(start, tk), :], aug_ref[pl.ds(start, tk), :]], axis=1)
        s_ref[hh, :, q_from:] = jnp.dot(k_ext, qt_ref[hh, :, q_from:], preferred_element_type=F32)

    def consume(kj, hh, key_offset):
        head = 2 * pair + hh
        q_from = 0 if key_offset is None else key_offset
        s = s_ref[hh, :, q_from:]
        if key_offset is not None:
            key = lax.broadcasted_iota(jnp.int32, s.shape, 0)
            qry = lax.broadcasted_iota(jnp.int32, s.shape, 1)
            s = jnp.where(key <= qry, s, MASKED)
        off = cref_ref[qi * ratio, head] - cref_ref[kj, head]
        m_old = m_ref[hh, :, q_from:]
        m_new = jnp.maximum(m_old, jnp.max(s, axis=0, keepdims=True) + off)
        e = jnp.exp(s - (m_new - off))
        alpha = jnp.exp(m_old - m_new)
        m_ref[hh, :, q_from:] = m_new
        l_ref[hh, :, q_from:] = alpha * l_ref[hh, :, q_from:] + jnp.sum(e, axis=0, keepdims=True)
        v_t = vt_ref[kj, hh * HEAD_DIM:(hh + 1) * HEAD_DIM, :]
        acc_ref[hh, :, q_from:] = (alpha * acc_ref[hh, :, q_from:]
                                   + jnp.dot(v_t, e.astype(BF16), preferred_element_type=F32))

    n_full = qi * ratio
    scores(0, 0)

    def body(j, carry):
        for d in range(ratio):
            kj = j * ratio + d
            scores(kj, 1)
            consume(kj, 0, None)
            scores(kj + 1, 0)
            consume(kj, 1, None)
        return carry

    lax.fori_loop(0, qi, body, 0)
    for d in range(ratio):
        kj = n_full + d
        scores(kj, 1, d * tk)
        consume(kj, 0, d * tk)
        if d + 1 < ratio:
            scores(kj + 1, 0, (d + 1) * tk)
        consume(kj, 1, d * tk)
    o_t = jnp.concatenate([acc_ref[hh] / l_ref[hh] for hh in range(2)], axis=0)
    o_ref[...] = o_t.T.astype(o_ref.dtype)


def _fox_attention(qkv, aug, v_t3, cref):
    t = qkv.shape[0]
    tq = FOX_TQ
    nk = t // FOX_T
    kcol = Q_DIM // LANES
    return pl.pallas_call(
        _fox_kernel,
        grid=(N_HEADS // 2, t // tq),
        in_specs=[pl.BlockSpec(memory_space=pltpu.SMEM),
                  pl.BlockSpec((tq, LANES), lambda p, i: (i, p)),
                  pl.BlockSpec((t, LANES), lambda p, i: (0, kcol + p)),
                  pl.BlockSpec((t, LANES), lambda p, i: (0, 0)),
                  pl.BlockSpec((nk, LANES, FOX_T), lambda p, i: (0, p, 0))],
        out_specs=pl.BlockSpec((tq, LANES), lambda p, i: (i, p)),
        out_shape=jax.ShapeDtypeStruct((t, Q_DIM), BF16),
        scratch_shapes=[pltpu.VMEM((2, 1, tq), F32), pltpu.VMEM((2, 1, tq), F32),
                        pltpu.VMEM((2, HEAD_DIM, tq), F32), pltpu.VMEM((2, 2 * LANES, tq), BF16), pltpu.VMEM((2, FOX_T, tq), F32)],
        compiler_params=_cparams(("parallel", "arbitrary")),
        name="fox_attention",
    )(cref, qkv, qkv, aug, v_t3)


def _fox_heads(x, norm_g, w_in, f_bias):
    t = x.shape[0]
    w_f = w_in[:, 3 * Q_DIM:]
    pad = jnp.zeros((D_MODEL, LANES - FOX_PARTS * N_HEADS), w_f.dtype)
    w_f3 = jnp.concatenate([w_f] * FOX_PARTS + [pad], axis=1).astype(BF16)
    b3 = jnp.concatenate([f_bias.astype(F32)] * FOX_PARTS + [jnp.zeros((LANES - FOX_PARTS * N_HEADS,), F32)])
    w = w_in.astype(BF16)
    qk, v_t3, f_logit3 = _norm_proj(x, norm_g, [w[:, :2 * Q_DIM], w[:, 2 * Q_DIM:3 * Q_DIM], w_f3], [BF16, BF16, F32],
                                    chunks=[None, FOX_T, None])
    aug, cref = _fox_prep(f_logit3, b3.reshape(1, LANES))
    return _fox_attention(qk, aug, v_t3, cref[:, 0, :N_HEADS])


def _nsa_compress_kernel(a_ref, pos_ref, w1_ref, w2_ref, o_ref):
    tc = a_ref.shape[1]
    half = a_ref.shape[2]
    a = a_ref[0]
    pos = pos_ref[0]
    top = jnp.dot((a + pos[:, :half]).astype(BF16), w1_ref[0, :half], preferred_element_type=F32)
    bot = jnp.dot((a + pos[:, half:]).astype(BF16), w1_ref[0, half:], preferred_element_type=F32)
    hid = top + pltpu.roll(bot, tc - 1, 0)
    o_ref[0] = jnp.dot(jax.nn.gelu(hid).astype(BF16), w2_ref[0], preferred_element_type=F32)


def _nsa_compress(a, pos, w1, w2):
    _, tc, width = a.shape
    return pl.pallas_call(
        _nsa_compress_kernel,
        grid=(2, NSA_KV_HEADS // 2),
        in_specs=[pl.BlockSpec((1, tc, width), lambda kv, gp: (kv * (NSA_KV_HEADS // 2) + gp, 0, 0)),
                  pl.BlockSpec((1, 1, 2 * width), lambda kv, gp: (kv, 0, 0)),
                  pl.BlockSpec((1, 2 * width, 2 * CMP_HIDDEN), lambda kv, gp: (kv, 0, 0)),
                  pl.BlockSpec((1, 2 * CMP_HIDDEN, LANES), lambda kv, gp: (kv, 0, 0))],
        out_specs=pl.BlockSpec((1, tc, LANES), lambda kv, gp: (kv, 0, gp)),
        out_shape=jax.ShapeDtypeStruct((2, tc, NSA_KV_DIM), F32),
        compiler_params=_cparams(("parallel", "parallel")),
        name="nsa_compress",
    )(a, pos, w1, w2)


NSA_WQ = 256
NSA_WIN_TILES = (NSA_WINDOW + NSA_TQ) // NSA_WQ


def _nsa_kernel(slopes_ref, q_ref, ks_ref, *refs):
    kw_refs, refs = refs[:NSA_WIN_TILES], refs[NSA_WIN_TILES:]
    vst_ref, refs = refs[0], refs[1:]
    vw_refs, refs = refs[:NSA_WIN_TILES], refs[NSA_WIN_TILES:]
    (kcmp_ref, vov_ref, gates_ref, ksel_ext_ref, kwin_ext_ref, kcmp_ext_ref, qext_ref,
     o_ref, selt_ref, m_ref, l_ref, acc_ref, qt_ref, s_ref, scmp_ref, cm_ref, cl_ref, oc_ref, sw_ref,
     active_ref, list_ref) = refs
    g, qi = pl.program_id(0), pl.program_id(1)
    tq = q_ref.shape[0]
    tc = kcmp_ref.shape[0]
    n_sel = selt_ref.shape[0]
    rr = NSA_GROUP
    t0 = qi * tq
    hg = g % 2
    slope = [slopes_ref[g * rr + r] for r in range(rr)]

    half = _lane_half((tq, LANES))
    parts = []
    for r in range(rr):
        q_pair = q_ref[:, (r // 2) * LANES:(r // 2 + 1) * LANES]
        src = jnp.where(hg == r % 2, q_pair, _swap_halves(q_pair))
        parts.append(jnp.where(half == hg, src, jnp.zeros_like(src)) * jnp.asarray(HEAD_DIM ** -0.5, BF16))
    qp = jnp.concatenate(parts, axis=0)
    qt_ref[0:LANES, :] = qp.astype(F32).T.astype(BF16)

    @pl.when(qi == 0)
    def _():
        qt_ref[LANES:, :] = qext_ref[0]

    def lanes_of(s, r):
        return s[:, r * tq:(r + 1) * tq]

    chunk_span = NSA_CMP_CHUNK * CMP_STRIDE
    last_chunk = ((t0 + tq - CMP_BLOCK) // CMP_STRIDE) // NSA_CMP_CHUNK
    nl16_q = (CMP_STRIDE * lax.broadcasted_iota(jnp.int32, (NSA_CMP_CHUNK, tq), 0)
              - lax.broadcasted_iota(jnp.int32, (NSA_CMP_CHUNK, tq), 1)).astype(F32)
    cm_ref[...] = jnp.full_like(cm_ref, M_INIT)

    def cmp_rows(ch):
        return pl.ds(ch * NSA_CMP_CHUNK, NSA_CMP_CHUNK)

    def cmp_offset(ch, r):
        return slope[r] * float(ch * chunk_span)

    def cmp_scores(ch, masked):
        k_full = jnp.concatenate([kcmp_ref[cmp_rows(ch), :], kcmp_ext_ref[...]], axis=1)
        s = jnp.dot(k_full, qt_ref[...], preferred_element_type=F32)
        if masked:
            visible = nl16_q <= (t0 - (CMP_BLOCK - 1) - ch * chunk_span).astype(F32)
            s = jnp.concatenate([jnp.where(visible, lanes_of(s, r), MASKED) for r in range(rr)], axis=1)
        scmp_ref[cmp_rows(ch), :] = s
        return [jnp.max(lanes_of(s, r), axis=0, keepdims=True) + cmp_offset(ch, r) for r in range(rr)]

    for last in range(tc // NSA_CMP_CHUNK):
        @pl.when(last_chunk == last)
        def _(last=last):
            maxima = [cmp_scores(ch, ch >= last - 1) for ch in range(last + 1)]
            pv = None
            for r in range(rr):
                cm_ref[r] = functools.reduce(jnp.maximum, [m[r] for m in maxima] + [cm_ref[r]])
            sums = [jnp.zeros((1, tq), F32) for _ in range(rr)]
            for ch in range(last + 1):
                s = scmp_ref[cmp_rows(ch), :]
                e_all = []
                for r in range(rr):
                    e = jnp.exp(lanes_of(s, r) - (cm_ref[r] - cmp_offset(ch, r)))
                    sums[r] = sums[r] + jnp.sum(e, axis=0, keepdims=True)
                    e_all.append(e.astype(BF16))
                part = jnp.dot(vov_ref[ch, 0], jnp.concatenate(e_all, axis=1), preferred_element_type=F32)
                pv = part if pv is None else pv + part
            for r in range(rr):
                cl_ref[r] = sums[r]
            oc_ref[...] = pv
    inv_l = [jnp.where(cl_ref[r] > 0.0, 1.0 / cl_ref[r], 0.0) for r in range(rr)]
    o_cmp = [oc_ref[0:HEAD_DIM, r * tq:(r + 1) * tq] * inv_l[r] for r in range(rr)]

    imp = functools.reduce(jnp.add, [oc_ref[HEAD_DIM:, r * tq:(r + 1) * tq] * inv_l[r] for r in range(rr)])
    blk = lax.broadcasted_iota(jnp.int32, (n_sel, tq), 0)
    cur = (t0 + lax.broadcasted_iota(jnp.int32, (n_sel, tq), 1)) // SEL_BLOCK
    blk_f = blk.astype(F32)
    forced = jnp.where(blk == 0, 1.0, jnp.where(blk == cur, 1.0, jnp.where(blk == cur - 1, 1.0, 0.0)))
    score = jnp.where(forced > 0.0, TAKEN, jnp.where(blk <= cur, imp, -1.0))

    selt_ref[...] = score
    row_step = min(n_sel, NSA_TOPK_ROW_STEP)
    variant = ((t0 + tq - 1) // SEL_BLOCK) // row_step

    wq = NSA_WQ
    n_back = NSA_WINDOW // wq
    win_jobs = [(sub, b) for sub in range(tq // wq) for b in range(n_back + 1)]
    assert len(win_jobs) <= N_SELECT - 3

    def win_scores(job):
        sub, b = win_jobs[job]
        k_full = jnp.concatenate([kw_refs[sub + b][...], kwin_ext_ref[...]], axis=1)
        q_cols = jnp.concatenate([qt_ref[:, r * tq + sub * wq:r * tq + (sub + 1) * wq] for r in range(rr)], axis=1)
        sw_ref[job] = jnp.dot(k_full, q_cols, preferred_element_type=F32)

    for v in range(n_sel // row_step):
        rows = row_step * (v + 1)

        @pl.when(variant == v)
        def _(rows=rows):
            ids = lax.broadcasted_iota(jnp.int32, (rows, tq), 0).astype(F32)
            sc = selt_ref[0:rows, :]
            for rnd in range(N_SELECT - 3):
                if rnd < len(win_jobs):
                    win_scores(rnd)
                best = jnp.max(sc, axis=0, keepdims=True)
                first = jnp.min(jnp.where(sc == best, ids, float(n_sel)), axis=0, keepdims=True)
                sc = jnp.where(ids == first, TAKEN, sc)
            selt_ref[0:rows, :] = sc

    selt_ref[...] = jnp.where(blk <= cur, jnp.where(selt_ref[...] == TAKEN, 1.0, 0.0), 0.0)

    m_ref[...] = jnp.full_like(m_ref, M_INIT)
    l_ref[...] = jnp.zeros_like(l_ref)
    acc_ref[...] = jnp.zeros_like(acc_ref)
    blocks_per_chunk = NSA_TK // SEL_BLOCK

    for c in range(n_sel // blocks_per_chunk):
        active_ref[c] = jnp.max(selt_ref[c * blocks_per_chunk:(c + 1) * blocks_per_chunk, :])

    def compact(c, n):
        list_ref[n] = c
        return n + (active_ref[c] > 0.0).astype(jnp.int32)

    n_active = lax.fori_loop(0, t0 // NSA_TK + 1, compact, 0)

    def sel_scores(c, buf):
        start = pl.multiple_of(c * NSA_TK, NSA_TK)
        rows = selt_ref[pl.ds(pl.multiple_of(c * blocks_per_chunk, blocks_per_chunk), blocks_per_chunk), :]
        mask_rows = jnp.concatenate([jnp.where(rows > 0.5, 0.0, MASKED), jnp.zeros_like(rows)], axis=0)
        first = LANES + buf * NSA_EXT_MASK_ROWS
        qt_ref[first:first + NSA_EXT_MASK_ROWS, :] = jnp.concatenate([mask_rows] * rr, axis=1).astype(BF16)
        k_full = jnp.concatenate([ks_ref[pl.ds(start, NSA_TK), :], ksel_ext_ref[buf]], axis=1)
        s_ref[buf] = jnp.dot(k_full, qt_ref[...], preferred_element_type=F32)

    def sel_consume(c, buf, diagonal):
        shift = (c * NSA_TK - t0).astype(F32)
        if diagonal:
            key = lax.broadcasted_iota(jnp.int32, (NSA_TK, tq), 0) + c * NSA_TK
            visible = key <= lax.broadcasted_iota(jnp.int32, (NSA_TK, tq), 1) + t0
        e_all, alpha_all = [], []
        for r in range(rr):
            off = slope[r] * shift
            s = s_ref[buf][:, r * tq:(r + 1) * tq]
            if diagonal:
                s = jnp.where(visible, s, MASKED)
            m_old = m_ref[r]
            m_new = jnp.maximum(m_old, jnp.max(s, axis=0, keepdims=True) + off)
            e = jnp.exp(s - (m_new - off))
            alpha = jnp.exp(m_old - m_new)
            m_ref[r] = m_new
            l_ref[r] = alpha * l_ref[r] + jnp.sum(e, axis=0, keepdims=True)
            e_all.append(e.astype(BF16))
            alpha_all.append(alpha)
        pv = jnp.dot(vst_ref[c], jnp.concatenate(e_all, axis=1), preferred_element_type=F32)
        acc_ref[...] = jnp.concatenate(alpha_all, axis=1) * acc_ref[...] + pv

    n_before = n_active - 1

    for count in range(1, NSA_SEL_UNROLLED + 1):
        @pl.when(n_active == count)
        def _(count=count):
            sel_scores(list_ref[0], 0)
            for idx in range(count):
                if idx + 1 < count:
                    sel_scores(list_ref[idx + 1], (idx + 1) % 2)
                sel_consume(list_ref[idx], idx % 2, idx == count - 1)

    @pl.when(n_active > NSA_SEL_UNROLLED)
    def _():
        sel_scores(list_ref[0], 0)

        def sel_pair(j, carry):
            c0, c1, c2 = list_ref[2 * j], list_ref[2 * j + 1], list_ref[2 * j + 2]
            sel_scores(c1, 1)
            sel_consume(c0, 0, False)
            sel_scores(c2, 0)
            sel_consume(c1, 1, False)
            return carry

        lax.fori_loop(0, n_before // 2, sel_pair, 0)

        @pl.when(n_before % 2 == 1)
        def _():
            c0, c_last = list_ref[n_before - 1], list_ref[n_before]
            sel_scores(c_last, 1)
            sel_consume(c0, 0, False)
            sel_consume(c_last, 1, True)

        @pl.when(n_before % 2 == 0)
        def _():
            sel_consume(list_ref[n_before], 0, True)

    xw = lax.broadcasted_iota(jnp.int32, (wq, wq), 0)
    qw = lax.broadcasted_iota(jnp.int32, (wq, wq), 1)
    o_win = [[] for _ in range(rr)]
    for sub in range(tq // wq):
        s_win, off_win = [], []
        for b in range(n_back + 1):
            back = n_back - b
            s_t = sw_ref[sub * (n_back + 1) + b]
            per_head = [s_t[:, r * wq:(r + 1) * wq] for r in range(rr)]
            if back == n_back:
                per_head = [jnp.where(xw > qw, s, MASKED) for s in per_head]
            if back == 0:
                per_head = [jnp.where(xw <= qw, s, MASKED) for s in per_head]
            s_win.append(per_head)
            in_range = qi * (tq // wq) + sub >= back
            off_win.append([jnp.where(in_range, -slope[r] * float(back * wq), MASKED) for r in range(rr)])
        e_win, l_win = [], []
        for r in range(rr):
            m = functools.reduce(jnp.maximum, [jnp.max(s_win[b][r], axis=0, keepdims=True) + off_win[b][r]
                                               for b in range(n_back + 1)])
            e_tiles = [jnp.exp(s_win[b][r] - (m - off_win[b][r])) for b in range(n_back + 1)]
            l_win.append(functools.reduce(jnp.add, [jnp.sum(e, axis=0, keepdims=True) for e in e_tiles]))
            e_win.append(jnp.concatenate([e.astype(BF16) for e in e_tiles], axis=0))
        v_win = jnp.concatenate([vw_refs[sub + b][0] for b in range(n_back + 1)], axis=1)
        o_sub = jnp.dot(v_win, jnp.concatenate(e_win, axis=1), preferred_element_type=F32)
        for r in range(rr):
            o_win[r].append(o_sub[:, r * wq:(r + 1) * wq] / l_win[r])
    o_win = [jnp.concatenate(parts, axis=1) for parts in o_win]

    gate = jax.nn.sigmoid(gates_ref[0])
    merged = []
    for r in range(rr):
        o_sel = lanes_of(acc_ref[...], r) / l_ref[r]
        merged.append(gate[3 * r:3 * r + 1] * o_cmp[r] + gate[3 * r + 1:3 * r + 2] * o_sel
                      + gate[3 * r + 2:3 * r + 3] * o_win[r])
    o_ref[...] = jnp.concatenate(merged, axis=0).T.astype(o_ref.dtype)


def _nsa_overlap_t(n_sel, tc):
    cmp_start = np.arange(tc - 1) * CMP_STRIDE
    sel_start = np.arange(n_sel) * SEL_BLOCK
    ov = np.clip(np.minimum(cmp_start[None, :] + CMP_BLOCK, sel_start[:, None] + SEL_BLOCK)
                 - np.maximum(cmp_start[None, :], sel_start[:, None]), 0, None) / CMP_BLOCK
    return np.concatenate([ov, np.zeros((n_sel, 1))], axis=1).astype(np.float32)


NSA_EXT_MASK_ROWS = 16
NSA_EXT_SEL = 32
NSA_EXT_WIN = 48
NSA_EXT_CMP = 64
NSA_CMP_CHUNK = 256
NSA_SEL_UNROLLED = 4
NSA_TOPK_ROW_STEP = 64
SLOPE_PIECES = 3


def _bf16_pieces(v):
    out, rest = [], np.asarray(v, np.float32)
    for _ in range(SLOPE_PIECES):
        p = rest.astype(BF16).astype(np.float32)
        out.append(p)
        rest = rest - p
    return out


def _nsa_extensions(tq):
    ksel = np.zeros((2, NSA_TK, LANES), np.float32)
    x = np.arange(NSA_TK)
    for buf in range(2):
        ksel[buf, x, buf * NSA_EXT_MASK_ROWS + x // SEL_BLOCK] = 1.0
    for j in range(SLOPE_PIECES):
        ksel[:, :, NSA_EXT_SEL + j] = 256 * (x // 256)
        ksel[:, :, NSA_EXT_SEL + SLOPE_PIECES + j] = x % 256
    kwin = np.zeros((NSA_WQ, LANES), np.float32)
    xw = np.arange(NSA_WQ)
    for j in range(SLOPE_PIECES):
        kwin[:, NSA_EXT_WIN + j] = 256 * (xw // 256)
        kwin[:, NSA_EXT_WIN + SLOPE_PIECES + j] = xw % 256
    kcmp = np.zeros((NSA_CMP_CHUNK, LANES), np.float32)
    for j in range(SLOPE_PIECES):
        kcmp[:, NSA_EXT_CMP + j] = np.arange(NSA_CMP_CHUNK)
    pieces = _bf16_pieces(_alibi_slopes())
    qext = np.zeros((NSA_KV_HEADS, LANES, NSA_GROUP * tq), np.float32)
    for g in range(NSA_KV_HEADS):
        for r in range(NSA_GROUP):
            cols = slice(r * tq, (r + 1) * tq)
            for j in range(SLOPE_PIECES):
                p = pieces[j][g * NSA_GROUP + r]
                qext[g, NSA_EXT_SEL + j, cols] = p
                qext[g, NSA_EXT_SEL + SLOPE_PIECES + j, cols] = p
                qext[g, NSA_EXT_WIN + j, cols] = p
                qext[g, NSA_EXT_WIN + SLOPE_PIECES + j, cols] = p
                qext[g, NSA_EXT_CMP + j, cols] = CMP_STRIDE * p
    return jnp.asarray(ksel, BF16), jnp.asarray(kwin, BF16), jnp.asarray(kcmp, BF16), jnp.asarray(qext, BF16)


def _nsa_attention(main, vs_t3, vw_t, k_cmp, v_cmp_t, gates_t):
    t = main.shape[0]
    tq = NSA_TQ
    tc = k_cmp.shape[0]
    n_sel = t // SEL_BLOCK
    nc = t // NSA_TK
    ks_col = Q_DIM // LANES
    kw_col = ks_col + NSA_KV_DIM // LANES
    n_ch = tc // NSA_CMP_CHUNK
    ov_t = jnp.asarray(_nsa_overlap_t(n_sel, tc).reshape(n_sel, n_ch, NSA_CMP_CHUNK).transpose(1, 0, 2), BF16)
    v_cmp_t = v_cmp_t.reshape(NSA_KV_HEADS, HEAD_DIM, n_ch, NSA_CMP_CHUNK).transpose(2, 0, 1, 3)
    vov = jnp.concatenate([v_cmp_t, jnp.broadcast_to(ov_t[:, None], (n_ch, NSA_KV_HEADS, n_sel, NSA_CMP_CHUNK))],
                          axis=2)
    slopes = jnp.asarray(_alibi_slopes())
    ksel_ext, kwin_ext, kcmp_ext, qext = _nsa_extensions(tq)

    def win_tile(i, j):
        return jnp.maximum(i * (tq // NSA_WQ) - NSA_WINDOW // NSA_WQ + j, 0)

    def kw_spec(j):
        return pl.BlockSpec((NSA_WQ, LANES), lambda g, i: (win_tile(i, j), kw_col + g // 2))

    def vw_spec(j):
        return pl.BlockSpec((1, HEAD_DIM, NSA_WQ), lambda g, i: (win_tile(i, j), g, 0))

    backs = list(range(NSA_WIN_TILES))
    return pl.pallas_call(
        _nsa_kernel,
        grid=(NSA_KV_HEADS, t // tq),
        in_specs=[pl.BlockSpec(memory_space=pltpu.SMEM),
                  pl.BlockSpec((tq, NSA_GROUP * HEAD_DIM), lambda g, i: (i, g)),
                  pl.BlockSpec((t, LANES), lambda g, i: (0, ks_col + g // 2))]
                 + [kw_spec(b) for b in backs]
                 + [pl.BlockSpec((nc, HEAD_DIM, NSA_TK), lambda g, i: (0, g, 0))]
                 + [vw_spec(b) for b in backs]
                 + [pl.BlockSpec((tc, LANES), lambda g, i: (0, g // 2)),
                    pl.BlockSpec((n_ch, 1, HEAD_DIM + n_sel, NSA_CMP_CHUNK), lambda g, i: (0, g, 0, 0)),
                    pl.BlockSpec((1, 16, tq), lambda g, i: (g, 0, i)),
                    pl.BlockSpec(ksel_ext.shape, lambda g, i: (0, 0, 0)),
                    pl.BlockSpec(kwin_ext.shape, lambda g, i: (0, 0)),
                    pl.BlockSpec(kcmp_ext.shape, lambda g, i: (0, 0)),
                    pl.BlockSpec((1,) + qext.shape[1:], lambda g, i: (g, 0, 0))],
        out_specs=pl.BlockSpec((tq, NSA_GROUP * HEAD_DIM), lambda g, i: (i, g)),
        out_shape=jax.ShapeDtypeStruct((t, Q_DIM), BF16),
        scratch_shapes=[pltpu.VMEM((n_sel, tq), F32), pltpu.VMEM((NSA_GROUP, 1, tq), F32),
                        pltpu.VMEM((NSA_GROUP, 1, tq), F32), pltpu.VMEM((HEAD_DIM, NSA_GROUP * tq), F32),
                        pltpu.VMEM((2 * LANES, NSA_GROUP * tq), BF16),
                        pltpu.VMEM((2, NSA_TK, NSA_GROUP * tq), F32),
                        pltpu.VMEM((tc, NSA_GROUP * tq), F32), pltpu.VMEM((NSA_GROUP, 1, tq), F32),
                        pltpu.VMEM((NSA_GROUP, 1, tq), F32), pltpu.VMEM((HEAD_DIM + n_sel, NSA_GROUP * tq), F32),
                        pltpu.VMEM((tq // NSA_WQ * (NSA_WINDOW // NSA_WQ + 1), NSA_WQ, NSA_GROUP * NSA_WQ), F32),
                        pltpu.SMEM((nc,), F32), pltpu.SMEM((nc,), jnp.int32)],
        compiler_params=_cparams(("parallel", "arbitrary")),
        name="nsa_attention",
    )(slopes, main, main, *([main] * NSA_WIN_TILES), vs_t3, *([vw_t] * NSA_WIN_TILES), k_cmp, vov, gates_t,
      ksel_ext, kwin_ext, kcmp_ext, qext)


def _nsa_heads(x, norm_g, w_in, cmp_pos, cmp_w1, cmp_w2):
    t = x.shape[0]
    kv = NSA_KV_DIM
    col = lambda i: w_in[:, Q_DIM + i * kv:Q_DIM + (i + 1) * kv]
    w_main = jnp.concatenate([w_in[:, :Q_DIM], col(2), col(4)], axis=1).astype(BF16)
    w_cmp = jnp.concatenate([col(0), col(1)], axis=1).astype(BF16)
    n_gate = 3 * N_HEADS
    w_gate = jnp.concatenate([w_in[:, Q_DIM + 6 * kv:], jnp.zeros((D_MODEL, LANES - n_gate), w_in.dtype)],
                             axis=1).astype(BF16)
    main, vs_t3, vw_t3, kcvc, gate_logit = _norm_proj(
        x, norm_g, [w_main, col(3).astype(BF16), col(5).astype(BF16), w_cmp, w_gate], [BF16, BF16, BF16, F32, F32],
        chunks=[None, NSA_TK, NSA_WQ, "cols", None])

    tc = t // CMP_STRIDE
    a = kcvc.reshape(kcvc.shape[0], tc, CMP_STRIDE * LANES)
    pair = jnp.eye(2, dtype=cmp_w1.dtype)
    pos = jnp.broadcast_to(cmp_pos[:, :, None, :], (2, CMP_BLOCK, 2, HEAD_DIM)).reshape(2, 1, 2 * CMP_BLOCK * HEAD_DIM)
    w1 = (cmp_w1.reshape(2, CMP_BLOCK, 1, HEAD_DIM, 1, CMP_HIDDEN) * pair.reshape(1, 1, 2, 1, 2, 1))
    w1 = w1.reshape(2, 2 * CMP_BLOCK * HEAD_DIM, 2 * CMP_HIDDEN)
    w2 = (cmp_w2.reshape(2, 1, CMP_HIDDEN, 1, HEAD_DIM) * pair.reshape(1, 2, 1, 2, 1))
    w2 = w2.reshape(2, 2 * CMP_HIDDEN, 2 * HEAD_DIM)
    cmp = _nsa_compress(a, pos, w1.astype(BF16), w2.astype(BF16))
    k_cmp = cmp[0].astype(BF16)
    v_cmp_t = cmp[1].T.astype(BF16)

    gates_t = gate_logit[:, :n_gate].reshape(t, NSA_KV_HEADS, 3 * NSA_GROUP).transpose(1, 2, 0)
    gates_t = jnp.pad(gates_t, ((0, 0), (0, 16 - 3 * NSA_GROUP), (0, 0)))
    return _nsa_attention(main, vs_t3, vw_t3, k_cmp, v_cmp_t, gates_t)


def kernel(x, attn_norm, mlp_norm, final_norm, nsa_w_in, nsa_cmp_pos, nsa_cmp_w1, nsa_cmp_w2, nsa_w_out,
           swa_w_in, swa_sinks, swa_w_out, fox_w_in, fox_f_bias, fox_w_out,
           mlp_w_up, mlp_conv_w, mlp_conv_b, mlp_w_down):
    assert x.shape[0] == 1, "the trunk is written for batch 1"
    h = x[0]
    for i in range(DEPTH):
        kind, j = i % N_MIXERS, i // N_MIXERS
        if kind == 0:
            o = _nsa_heads(h, attn_norm[i], nsa_w_in[j], nsa_cmp_pos[j], nsa_cmp_w1[j], nsa_cmp_w2[j])
            w_out = nsa_w_out[j]
        elif kind == 1:
            o = _swa_heads(h, attn_norm[i], swa_w_in[j], swa_sinks[j])
            w_out = swa_w_out[j]
        else:
            o = _fox_heads(h, attn_norm[i], fox_w_in[j], fox_f_bias[j])
            w_out = fox_w_out[j]
        h = _mixer_out_mlp(h, o, w_out.astype(BF16), mlp_norm[i], mlp_w_up[i].astype(BF16), mlp_conv_w[i],
                           mlp_conv_b[i].reshape(1, -1), mlp_w_down[i].astype(BF16),
                           final_g=final_norm if i == DEPTH - 1 else None)
    return h[None]
```
